```python
import math
import jax, jax.numpy as jnp
from jax import lax
import numpy as np

D_MODEL = 2048
BATCH = 4
SEQ = 8192
DEPTH = 2
DEC_BATCH = 16
DEC_SEQ = 16
PAST_LEN = 1024

CHUNK = 64
D_MIX = D_MODEL
D_SSM = D_MIX // 2
SSM_HEAD_DIM = 64
H_SSM = D_SSM // SSM_HEAD_DIM
SSM_GROUPS = 2
SSM_STATE = 128
SSM_CONV = 4
D_XBC = D_SSM + 2 * SSM_GROUPS * SSM_STATE
SSD_BLOCK = CHUNK
D_CONV = D_MIX // 4
CONV_WIDTH = 31
D_ATT = D_MIX - D_SSM - D_CONV
HEAD_DIM = 64
H_ATT = D_ATT // HEAD_DIM
H_IDX = 8
D_IDX = 64
TOPK_MAX = 256
Q_BLOCK = 128
N_BUCKETS = 32
REL_MAX_DIST = 1024
D_FF = 5632
FFN_CONV = 3
EPS = 1e-6

_SPLIT_SIZES = (D_SSM, D_XBC, H_SSM, 2 * D_CONV, D_ATT, D_ATT, D_ATT, H_IDX * D_IDX, D_IDX, H_IDX)
_SPLIT_IDX = tuple(sum(_SPLIT_SIZES[:i + 1]) for i in range(len(_SPLIT_SIZES) - 1))
D_IN = sum(_SPLIT_SIZES)

kernel_name = "hybrid_streaming_encoder_step"


def rmsnorm(x, g):
    xf = x.astype(jnp.float32)
    y = xf * lax.rsqrt(jnp.mean(xf * xf, axis=-1, keepdims=True) + EPS)
    return (y * g.astype(jnp.float32)).astype(x.dtype)


def layernorm(x, g, b):
    xf = x.astype(jnp.float32)
    xc = xf - jnp.mean(xf, axis=-1, keepdims=True)
    y = xc * lax.rsqrt(jnp.mean(xc * xc, axis=-1, keepdims=True) + EPS)
    return (y * g.astype(jnp.float32) + b.astype(jnp.float32)).astype(x.dtype)


def causal_dwconv(x, prev, w, b):
    width, ch = w.shape
    xp = jnp.concatenate([prev.astype(x.dtype), x], axis=1)
    y = lax.conv_general_dilated(xp, w.astype(x.dtype)[:, None, :], window_strides=(1,), padding="VALID",
                                 dimension_numbers=("NWC", "WIO", "NWC"), feature_group_count=ch)
    return y + b.astype(x.dtype), xp[:, xp.shape[1] - (width - 1):]


def ssd_scan(x, dt, a, bm, cm, h0, block):
    f32 = jnp.float32
    bsz, seqlen, nh, hp = x.shape
    ns = bm.shape[-1]
    nc = seqlen // block
    xr = x.astype(f32).reshape(bsz, nc, block, nh, hp)
    dtr = dt.reshape(bsz, nc, block, nh)
    br = bm.astype(f32).reshape(bsz, nc, block, nh, ns)
    cr = cm.astype(f32).reshape(bsz, nc, block, nh, ns)
    cum = jnp.cumsum(dtr * a, axis=2)
    seg = cum[:, :, :, None, :] - cum[:, :, None, :, :]
    causal = jnp.tril(jnp.ones((block, block), dtype=bool))[None, None, :, :, None]
    decay = jnp.exp(jnp.where(causal, seg, -jnp.inf))
    xdt = xr * dtr[..., None]
    scores = jnp.einsum('bclhn,bcshn->bclsh', cr, br) * decay
    y_diag = jnp.einsum('bclsh,bcshp->bclhp', scores, xdt)
    decay_end = jnp.exp(cum[:, :, -1:, :] - cum)
    states = jnp.einsum('bclhn,bclh,bclhp->bchpn', br, decay_end, xdt)
    block_decay = jnp.exp(cum[:, :, -1, :])

    def step(h, inp):
        st, dec = inp
        return h * dec[:, :, None, None] + st, h

    h_final, h_prev = lax.scan(step, h0.astype(f32),
                               (jnp.moveaxis(states, 1, 0), jnp.moveaxis(block_decay, 1, 0)))
    h_prev = jnp.moveaxis(h_prev, 0, 1)
    y_off = jnp.einsum('bclhn,bchpn,bclh->bclhp', cr, h_prev, jnp.exp(cum))
    return (y_diag + y_off).reshape(bsz, seqlen, nh, hp), h_final


def ssd_mixer(z, xbc, dt_raw, conv_prev, h0, conv_w, conv_b, dt_bias, a_log, d_skip, norm_g):
    bsz, t = z.shape[:2]
    xbc, conv_new = causal_dwconv(xbc, conv_prev, conv_w, conv_b)
    xbc = jax.nn.silu(xbc)
    xs, bs, cs = jnp.split(xbc, (D_SSM, D_SSM + SSM_GROUPS * SSM_STATE), axis=-1)
    xs = xs.reshape(bsz, t, H_SSM, SSM_HEAD_DIM)
    rep = H_SSM // SSM_GROUPS
    bs = jnp.repeat(bs.reshape(bsz, t, SSM_GROUPS, SSM_STATE), rep, axis=2)
    cs = jnp.repeat(cs.reshape(bsz, t, SSM_GROUPS, SSM_STATE), rep, axis=2)
    dt = jax.nn.softplus(dt_raw.astype(jnp.float32) + dt_bias.astype(jnp.float32))
    a = -jnp.exp(a_log.astype(jnp.float32))
    block = SSD_BLOCK if t % SSD_BLOCK == 0 else t
    y, h_new = ssd_scan(xs, dt, a, bs, cs, h0, block)
    y = y + d_skip.astype(jnp.float32)[:, None] * xs.astype(jnp.float32)
    y = rmsnorm(y.reshape(bsz, t, D_SSM) * jax.nn.silu(z.astype(jnp.float32)), norm_g)
    return y.astype(z.dtype), conv_new, h_new.astype(h0.dtype)


def conformer_conv(glu_in, prev, w, b, ln_g, ln_b):
    val, gate = jnp.split(glu_in, 2, axis=-1)
    u = val * jax.nn.sigmoid(gate)
    y, new_prev = causal_dwconv(u, prev, w, b)
    return jax.nn.silu(layernorm(y, ln_g, ln_b)), new_prev


def t5_bucket(rel):
    nb = N_BUCKETS // 2
    max_exact = nb // 2
    ret = jnp.where(rel > 0, nb, 0)
    n = jnp.abs(rel)
    nf = jnp.maximum(n, 1).astype(jnp.float32)
    large = max_exact + (jnp.log(nf / max_exact) / math.log(REL_MAX_DIST / max_exact)
                         * (nb - max_exact)).astype(jnp.int32)
    large = jnp.minimum(large, nb - 1)
    return ret + jnp.where(n < max_exact, n, large)


def sparse_attention_block(q, qi, wi, qpos, k, v, ki, rel_bias, k_sel):
    f32 = jnp.float32
    s_len = k.shape[1]
    kpos = jnp.arange(s_len, dtype=jnp.int32)
    chunk_end = (qpos // CHUNK + 1) * CHUNK
    admissible = kpos[None, :] < chunk_end[:, None]
    rel = jax.nn.relu(jnp.einsum('bqhd,bsd->bqhs', qi.astype(f32), ki.astype(f32)) * (D_IDX ** -0.5))
    iscore = jnp.einsum('bqhs,bqh->bqs', rel, wi.astype(f32) * (H_IDX ** -0.5))
    iscore = jnp.where(admissible[None], iscore, -jnp.inf)
    _, sel = lax.top_k(iscore, k_sel)
    valid = sel < chunk_end[None, :, None]
    kg = jax.vmap(lambda kb, ib: kb[ib])(k, sel)
    vg = jax.vmap(lambda vb, ib: vb[ib])(v, sel)
    logits = jnp.einsum('bqhd,bqkhd->bqhk', q.astype(f32), kg.astype(f32)) * (HEAD_DIM ** -0.5)
    bias = rel_bias.astype(f32)[t5_bucket(sel - qpos[None, :, None])]
    logits = logits + jnp.swapaxes(bias, 2, 3)
    logits = jnp.where(valid[:, :, None, :], logits, -jnp.inf)
    p = jax.nn.softmax(logits, axis=-1)
    return jnp.einsum('bqhk,bqkhd->bqhd', p, vg.astype(f32)).astype(q.dtype)


def sparse_attention(q, qi, wi, k, v, ki, rel_bias, q_start):
    bsz, t = q.shape[:2]
    s_len = k.shape[1]
    k_sel = min(TOPK_MAX, s_len // 4)
    qpos = q_start + jnp.arange(t, dtype=jnp.int32)
    if t > Q_BLOCK and t % Q_BLOCK == 0:
        nb = t // Q_BLOCK

        def to_blocks(a):
            return jnp.swapaxes(a.reshape((bsz, nb, Q_BLOCK) + a.shape[2:]), 0, 1)

        out = lax.map(lambda blk: sparse_attention_block(blk[0], blk[1], blk[2], blk[3], k, v, ki, rel_bias, k_sel),
                      (to_blocks(q), to_blocks(qi), to_blocks(wi), qpos.reshape(nb, Q_BLOCK)))
        return jnp.swapaxes(out, 0, 1).reshape(bsz, t, H_ATT, HEAD_DIM)
    return sparse_attention_block(q, qi, wi, qpos, k, v, ki, rel_bias, k_sel)


def trunk_layer(x, past_k, past_v, past_ki, ssm_conv_prev, ssm_h0, cconv_prev, fconv_prev, rel_bias, lw):
    bsz, t = x.shape[:2]
    q_start = past_k.shape[1]
    h = rmsnorm(x, lw['g_mix_pre'])
    u = h @ lw['w_in'].astype(h.dtype)
    z, xbc, dt_raw, glu_in, q, k, v, qi, ki, wi = jnp.split(u, _SPLIT_IDX, axis=-1)
    y_ssm, ssm_conv_new, h_new = ssd_mixer(z, xbc, dt_raw, ssm_conv_prev, ssm_h0, lw['ssm_conv_w'], lw['ssm_conv_b'],
                                           lw['ssm_dt_bias'], lw['ssm_a_log'], lw['ssm_d'], lw['ssm_norm_g'])
    y_conv, cconv_new = conformer_conv(glu_in, cconv_prev, lw['cconv_w'], lw['cconv_b'],
                                       lw['cconv_ln_g'], lw['cconv_ln_b'])
    q = q.reshape(bsz, t, H_ATT, HEAD_DIM)
    k = k.reshape(bsz, t, H_ATT, HEAD_DIM)
    v = v.reshape(bsz, t, H_ATT, HEAD_DIM)
    qi = qi.reshape(bsz, t, H_IDX, D_IDX)
    k_all = jnp.concatenate([past_k.astype(k.dtype), k], axis=1)
    v_all = jnp.concatenate([past_v.astype(v.dtype), v], axis=1)
    ki_all = jnp.concatenate([past_ki.astype(ki.dtype), ki], axis=1)
    y_att = sparse_attention(q, qi, wi, k_all, v_all, ki_all, rel_bias, q_start)
    mix = jnp.concatenate([y_ssm.astype(x.dtype), y_conv.astype(x.dtype),
                           y_att.reshape(bsz, t, D_ATT).astype(x.dtype)], axis=-1)
    x = x + rmsnorm(mix @ lw['w_out'].astype(x.dtype), lw['g_mix_post'])
    h2 = rmsnorm(x, lw['g_ffn_pre'])
    a, fconv_new = causal_dwconv(h2 @ lw['ffn_w_gate'].astype(h2.dtype), fconv_prev, lw['ffn_conv_w'], lw['ffn_conv_b'])
    f = (jax.nn.silu(a) * (h2 @ lw['ffn_w_up'].astype(h2.dtype))) @ lw['ffn_w_down'].astype(h2.dtype)
    x = x + rmsnorm(f, lw['g_ffn_post'])
    return x, (k, v, ki, h_new, ssm_conv_new, cconv_new, fconv_new)


def setup_inputs(seed: int = 0) -> dict:
    key = jax.random.key(seed)
    ks = jax.random.split(key, 32)
    f32 = jnp.float32

    def nrm(i, shape, scale):
        return jax.random.normal(ks[i], shape, f32) * scale

    dt0 = jnp.exp(jax.random.uniform(ks[10], (DEPTH, H_SSM), f32, math.log(1e-3), math.log(1e-1)))
    return {
        "x_prompt": nrm(0, (BATCH, SEQ, D_MODEL), 1.0),
        "x_sample": nrm(1, (DEC_BATCH, DEC_SEQ, D_MODEL), 1.0),
        "cache_k": nrm(2, (DEPTH, DEC_BATCH, PAST_LEN, H_ATT, HEAD_DIM), 1.0),
        "cache_v": nrm(3, (DEPTH, DEC_BATCH, PAST_LEN, H_ATT, HEAD_DIM), 1.0),
        "cache_kidx": nrm(4, (DEPTH, DEC_BATCH, PAST_LEN, D_IDX), 1.0),
        "state_ssm": nrm(5, (DEPTH, DEC_BATCH, H_SSM, SSM_HEAD_DIM, SSM_STATE), 0.1),
        "state_ssm_conv": nrm(6, (DEPTH, DEC_BATCH, SSM_CONV - 1, D_XBC), 1.0),
        "state_cconv": nrm(7, (DEPTH, DEC_BATCH, CONV_WIDTH - 1, D_CONV), 0.5),
        "state_ffn_conv": nrm(8, (DEPTH, DEC_BATCH, FFN_CONV - 1, D_FF), 1.0),
        "rel_bias": nrm(9, (N_BUCKETS, H_ATT), 0.5),
        "g_mix_pre": 1.0 + nrm(11, (DEPTH, D_MODEL), 0.05),
        "w_in": nrm(12, (DEPTH, D_MODEL, D_IN), D_MODEL ** -0.5),
        "ssm_conv_w": nrm(13, (DEPTH, SSM_CONV, D_XBC), SSM_CONV ** -0.5),
        "ssm_conv_b": nrm(14, (DEPTH, D_XBC), 0.01),
        "ssm_dt_bias": dt0 + jnp.log(-jnp.expm1(-dt0)),
        "ssm_a_log": jnp.log(jax.random.uniform(ks[15], (DEPTH, H_SSM), f32, 1.0, 16.0)),
        "ssm_d": 1.0 + nrm(16, (DEPTH, H_SSM), 0.05),
        "ssm_norm_g": 1.0 + nrm(17, (DEPTH, D_SSM), 0.05),
        "cconv_w": nrm(18, (DEPTH, CONV_WIDTH, D_CONV), CONV_WIDTH ** -0.5),
        "cconv_b": nrm(19, (DEPTH, D_CONV), 0.01),
        "cconv_ln_g": 1.0 + nrm(20, (DEPTH, D_CONV), 0.05),
        "cconv_ln_b": nrm(21, (DEPTH, D_CONV), 0.01),
        "w_out": nrm(22, (DEPTH, D_MIX, D_MODEL), D_MIX ** -0.5),
        "g_mix_post": 1.0 + nrm(23, (DEPTH, D_MODEL), 0.05),
        "g_ffn_pre": 1.0 + nrm(24, (DEPTH, D_MODEL), 0.05),
        "ffn_w_gate": nrm(25, (DEPTH, D_MODEL, D_FF), D_MODEL ** -0.5),
        "ffn_w_up": nrm(26, (DEPTH, D_MODEL, D_FF), D_MODEL ** -0.5),
        "ffn_conv_w": nrm(27, (DEPTH, FFN_CONV, D_FF), FFN_CONV ** -0.5),
        "ffn_conv_b": nrm(28, (DEPTH, D_FF), 0.01),
        "ffn_w_down": nrm(29, (DEPTH, D_FF, D_MODEL), D_FF ** -0.5),
        "g_ffn_post": 1.0 + nrm(30, (DEPTH, D_MODEL), 0.05),
    }


def reference(x_prompt, x_sample, cache_k, cache_v, cache_kidx, state_ssm, state_ssm_conv, state_cconv,
              state_ffn_conv, rel_bias, g_mix_pre, w_in, ssm_conv_w, ssm_conv_b, ssm_dt_bias, ssm_a_log, ssm_d,
              ssm_norm_g, cconv_w, cconv_b, cconv_ln_g, cconv_ln_b, w_out, g_mix_post, g_ffn_pre, ffn_w_gate,
              ffn_w_up, ffn_conv_w, ffn_conv_b, ffn_w_down, g_ffn_post):
    bp = x_prompt.shape[0]
    dtp = x_prompt.dtype
    zk = jnp.zeros((bp, 0, H_ATT, HEAD_DIM), dtp)
    zki = jnp.zeros((bp, 0, D_IDX), dtp)
    zsc = jnp.zeros((bp, SSM_CONV - 1, D_XBC), dtp)
    zh = jnp.zeros((bp, H_SSM, SSM_HEAD_DIM, SSM_STATE), dtp)
    zcc = jnp.zeros((bp, CONV_WIDTH - 1, D_CONV), dtp)
    zfc = jnp.zeros((bp, FFN_CONV - 1, D_FF), dtp)
    xp, xs = x_prompt, x_sample
    p_states = [[] for _ in range(7)]
    s_states = [[] for _ in range(7)]
    for l in range(DEPTH):
        lw = dict(g_mix_pre=g_mix_pre[l], w_in=w_in[l], ssm_conv_w=ssm_conv_w[l], ssm_conv_b=ssm_conv_b[l],
                  ssm_dt_bias=ssm_dt_bias[l], ssm_a_log=ssm_a_log[l], ssm_d=ssm_d[l], ssm_norm_g=ssm_norm_g[l],
                  cconv_w=cconv_w[l], cconv_b=cconv_b[l], cconv_ln_g=cconv_ln_g[l], cconv_ln_b=cconv_ln_b[l],
                  w_out=w_out[l], g_mix_post=g_mix_post[l], g_ffn_pre=g_ffn_pre[l], ffn_w_gate=ffn_w_gate[l],
                  ffn_w_up=ffn_w_up[l], ffn_conv_w=ffn_conv_w[l], ffn_conv_b=ffn_conv_b[l],
                  ffn_w_down=ffn_w_down[l], g_ffn_post=g_ffn_post[l])
        xp, st_p = trunk_layer(xp, zk, zk, zki, zsc, zh, zcc, zfc, rel_bias, lw)
        xs, st_s = trunk_layer(xs, cache_k[l], cache_v[l], cache_kidx[l], state_ssm_conv[l], state_ssm[l],
                               state_cconv[l], state_ffn_conv[l], rel_bias, lw)
        for i in range(7):
            p_states[i].append(st_p[i])
            s_states[i].append(st_s[i])
    k_prompt, v_prompt, kidx_prompt, ssm_prompt, ssm_conv_prompt, cconv_prompt, ffn_conv_prompt = [jnp.stack(s) for s in p_states]
    k_sample, v_sample, kidx_sample, ssm_sample, ssm_conv_sample, cconv_sample, ffn_conv_sample = [jnp.stack(s) for s in s_states]
    return (xp, xs, k_prompt, v_prompt, kidx_prompt, ssm_prompt, ssm_conv_prompt, cconv_prompt, ffn_conv_prompt,
            k_sample, v_sample, kidx_sample, ssm_sample, ssm_conv_sample, cconv_sample, ffn_conv_sample)
```

```python
import functools
import math

import numpy as np
import jax
import jax.numpy as jnp
from jax import lax
from jax.experimental import pallas as pl
from jax.experimental.pallas import tpu as pltpu

F32 = jnp.float32
BF16 = jnp.bfloat16
I32 = jnp.int32

D_MODEL = 2048
D_SSM = 1024
SSM_HEAD_DIM = 64
H_SSM = 16
SSM_GROUPS = 2
SSM_STATE = 128
SSM_CONV = 4
D_XBC = D_SSM + 2 * SSM_GROUPS * SSM_STATE
D_CONV = 512
CONV_WIDTH = 31
D_ATT = 512
HEAD_DIM = 64
H_ATT = 8
H_IDX = 8
D_IDX = 64
TOPK = 256
CHUNK = 64
N_BUCKETS = 32
REL_MAX_DIST = 1024
D_FF = 5632
FFN_CONV = 3
EPS = 1e-6

LANES = 128
SUBLANES = 8

COL_Z, COL_XBC, COL_GLU, COL_Q, COL_K, COL_V, COL_QI, COL_SMALL = 0, 1024, 2560, 3584, 4096, 4608, 5120, 5632
D_PROJ = COL_SMALL + LANES
SM_DT, SM_WI, SM_KI = 0, 16, 64

INT_MIN = -(2 ** 31)
INT_MAX = 2 ** 31 - 1
NEG_BIG = -1e30
VMEM_LIMIT = 56 * 1024 * 1024


def _bucket_thresholds():
    nb = N_BUCKETS // 2
    max_exact = nb // 2
    n = np.arange(0, 4 * REL_MAX_DIST, dtype=np.int64)
    nf = np.maximum(n, 1).astype(np.float32)
    large = max_exact + (np.log(nf / np.float32(max_exact)) / np.float32(math.log(REL_MAX_DIST / max_exact))
                         * np.float32(nb - max_exact)).astype(np.int32)
    large = np.minimum(large, nb - 1)
    bucket = np.where(n < max_exact, n, large)
    steps = np.nonzero(np.diff(bucket))[0] + 1
    assert np.all(np.diff(bucket) >= 0) and np.all(np.diff(bucket) <= 1) and bucket[-1] == nb - 1
    return tuple(int(s) for s in steps)


BUCKET_STEPS = _bucket_thresholds()


def _sigmoid(x):
    return 1.0 / (1.0 + jnp.exp(-x))


def _silu(x):
    return x * _sigmoid(x)


def _split3(x):
    hi = x.astype(BF16)
    r1 = x - hi.astype(F32)
    mid = r1.astype(BF16)
    lo = (r1 - mid.astype(F32)).astype(BF16)
    return hi, mid, lo


def _dot(a, b):
    return jnp.dot(a, b, preferred_element_type=F32)


def _dot_nt(a, b):
    return lax.dot_general(a, b, (((1,), (1,)), ((), ())), preferred_element_type=F32)


def _exact_dot(sel_bf16, x_f32):
    hi, mid, lo = _split3(x_f32)
    return _dot(sel_bf16, hi) + _dot(sel_bf16, mid) + _dot(sel_bf16, lo)


def _exact_dot_r(x_f32, sel_bf16):
    hi, mid, lo = _split3(x_f32)
    return _dot(hi, sel_bf16) + _dot(mid, sel_bf16) + _dot(lo, sel_bf16)


def _rms(x, g):
    ms = jnp.mean(x * x, axis=-1, keepdims=True)
    return x * lax.rsqrt(ms + EPS) * g


def _rms_matmul_kernel(x_ref, g_ref, w_ref, o_ref, h_ref):
    @pl.when(pl.program_id(1) == 0)
    def _():
        h_ref[...] = _rms(x_ref[...], g_ref[...]).astype(BF16)

    o_ref[...] = _dot(h_ref[...], w_ref[...])


def _rms_matmul(x, g, w, *, tm, tn):
    m, d = x.shape
    n = w.shape[1]
    assert m % tm == 0 and n % tn == 0
    return pl.pallas_call(
        _rms_matmul_kernel,
        out_shape=jax.ShapeDtypeStruct((m, n), F32),
        grid=(m // tm, n // tn),
        in_specs=[pl.BlockSpec((tm, d), lambda i, j: (i, 0)),
                  pl.BlockSpec((1, d), lambda i, j: (0, 0)),
                  pl.BlockSpec((d, tn), lambda i, j: (0, j))],
        out_specs=pl.BlockSpec((tm, tn), lambda i, j: (i, j)),
        scratch_shapes=[pltpu.VMEM((tm, d), BF16)],
        compiler_params=pltpu.CompilerParams(dimension_semantics=("arbitrary", "arbitrary"),
                                             vmem_limit_bytes=VMEM_LIMIT),
        name="rms_in_proj",
    )(x, g, w)


def _bias_table_kernel(rb_ref, o_ref, *, off0, step):
    d = pl.program_id(0)
    h = pl.program_id(1)
    rows, cols = o_ref.shape[2], o_ref.shape[3]
    rel = (lax.broadcasted_iota(I32, (rows, cols), 1) - lax.broadcasted_iota(I32, (rows, cols), 0)
           + (off0 - d * step))
    n = jnp.abs(rel)
    bucket = jnp.where(rel > 0, N_BUCKETS // 2, 0)
    for s in BUCKET_STEPS:
        bucket = bucket + jnp.where(n >= s, 1, 0)
    acc = jnp.zeros((rows, cols), F32)
    for b in range(N_BUCKETS):
        acc = jnp.where(bucket == b, rb_ref[b, h], acc)
    o_ref[0, 0] = acc


def _bias_table(rel_bias, *, nd, rows, cols, off0, step):
    return pl.pallas_call(
        functools.partial(_bias_table_kernel, off0=off0, step=step),
        out_shape=jax.ShapeDtypeStruct((nd, H_ATT, rows, cols), F32),
        grid=(nd, H_ATT),
        in_specs=[pl.BlockSpec(memory_space=pltpu.SMEM)],
        out_specs=pl.BlockSpec((1, 1, rows, cols), lambda d, h: (d, h, 0, 0)),
        name="bias_table",
    )(rel_bias)


CC_HALO = 32


def _cconv_kernel(val_ref, gate_ref, prev_ref, w_ref, b_ref, lg_ref, lb_ref, y_ref, tail_ref, buf_ref, *, tm):
    @pl.when(pl.program_id(1) == 0)
    def _():
        buf_ref[0:CC_HALO, :] = prev_ref[0]

    buf_ref[CC_HALO:CC_HALO + tm, :] = val_ref[...] * _sigmoid(gate_ref[...])
    first = CC_HALO - (CONV_WIDTH - 1)
    acc = jnp.zeros((tm, D_CONV), F32) + b_ref[...]
    for k in range(CONV_WIDTH):
        acc = acc + w_ref[k:k + 1, :] * buf_ref[first + k:first + k + tm, :]
    mu = jnp.mean(acc, axis=-1, keepdims=True)
    xc = acc - mu
    var = jnp.mean(xc * xc, axis=-1, keepdims=True)
    y = xc * lax.rsqrt(var + EPS) * lg_ref[...] + lb_ref[...]
    y_ref[...] = _silu(y).astype(y_ref.dtype)
    tail = buf_ref[tm:tm + CC_HALO, :]
    tail_ref[0] = tail
    buf_ref[0:CC_HALO, :] = tail


def _cconv(u, prev, w, b, lg, lb, *, bsz, t, tm):
    nt = t // tm
    row = lambda bi, ti: bi * nt + ti
    return pl.pallas_call(
        functools.partial(_cconv_kernel, tm=tm),
        out_shape=(jax.ShapeDtypeStruct((bsz * t, D_CONV), BF16),
                   jax.ShapeDtypeStruct((bsz, CC_HALO, D_CONV), F32)),
        grid=(bsz, nt),
        in_specs=[pl.BlockSpec((tm, D_CONV), lambda bi, ti: (row(bi, ti), COL_GLU // D_CONV)),
                  pl.BlockSpec((tm, D_CONV), lambda bi, ti: (row(bi, ti), COL_GLU // D_CONV + 1)),
                  pl.BlockSpec((1, CC_HALO, D_CONV), lambda bi, ti: (bi, 0, 0)),
                  pl.BlockSpec((CONV_WIDTH, D_CONV), lambda bi, ti: (0, 0)),
                  pl.BlockSpec((1, D_CONV), lambda bi, ti: (0, 0)),
                  pl.BlockSpec((1, D_CONV), lambda bi, ti: (0, 0)),
                  pl.BlockSpec((1, D_CONV), lambda bi, ti: (0, 0))],
        out_specs=(pl.BlockSpec((tm, D_CONV), lambda bi, ti: (row(bi, ti), 0)),
                   pl.BlockSpec((1, CC_HALO, D_CONV), lambda bi, ti: (bi, 0, 0))),
        scratch_shapes=[pltpu.VMEM((CC_HALO + tm, D_CONV), F32)],
        compiler_params=pltpu.CompilerParams(dimension_semantics=("arbitrary", "arbitrary"),
                                             vmem_limit_bytes=VMEM_LIMIT),
        name="conformer_conv",
    )(u, u, prev, w, b, lg, lb)


SSD_HALO = 8


def _ssd_kernel(z_ref, x0_ref, x1_ref, x2_ref, sm_ref, prevc_ref, h0_ref, cw_ref, cb_ref, dtb_ref, alog_ref,
                dx_ref, ng_ref, e_ref, y_ref, hout_ref, ctail_ref, buf_ref, ht_ref, yb_ref, *, L, nv):
    @pl.when(pl.program_id(1) == 0)
    def _():
        buf_ref[0:SSD_HALO, :] = prevc_ref[0]
        ht_ref[...] = h0_ref[0]

    if nv < L:
        buf_ref[SSD_HALO + nv:SSD_HALO + L, :] = jnp.zeros((L - nv, D_XBC), F32)
    for c, r in enumerate((x0_ref, x1_ref, x2_ref)):
        buf_ref[SSD_HALO:SSD_HALO + nv, c * 512:(c + 1) * 512] = r[...]
    first = SSD_HALO - (SSM_CONV - 1)
    acc = jnp.zeros((L, D_XBC), F32) + cb_ref[...]
    for k in range(SSM_CONV):
        acc = acc + cw_ref[k:k + 1, :] * buf_ref[first + k:first + k + L, :]
    xbc = _silu(acc)
    tail = buf_ref[nv:nv + SSD_HALO, :]
    ctail_ref[0] = tail
    buf_ref[0:SSD_HALO, :] = tail

    xs = xbc[:, :D_SSM]
    lane = lax.broadcasted_iota(I32, (L, LANES), 1)
    rowi = lax.broadcasted_iota(I32, (L, LANES), 0)
    sm = sm_ref[...]
    if nv < L:
        sm = jnp.concatenate([sm, jnp.zeros((L - nv, LANES), F32)], axis=0)
    dtr = sm + dtb_ref[...]
    dt = jnp.maximum(dtr, 0.0) + jnp.log(1.0 + jnp.exp(-jnp.abs(dtr)))
    dt = jnp.where((lane < H_SSM) & (rowi < nv), dt, 0.0)
    a = -jnp.exp(alog_ref[...])
    da = dt * a
    ri = lax.broadcasted_iota(I32, (L, L), 0)
    ci = lax.broadcasted_iota(I32, (L, L), 1)
    causal = ri >= ci
    tril = jnp.where(causal, 1.0, 0.0).astype(BF16)
    cum = _exact_dot(tril, da)
    eye = jnp.where(lax.broadcasted_iota(I32, (LANES, LANES), 0) == lax.broadcasted_iota(I32, (LANES, LANES), 1),
                    1.0, 0.0).astype(BF16)
    ch, cm, cl = _split3(cum)
    cum_t = _dot_nt(eye, ch) + _dot_nt(eye, cm) + _dot_nt(eye, cl)
    e = e_ref[...]
    ecx = _exact_dot_r(jnp.exp(cum), e)
    dtx = _exact_dot_r(dt, e)
    xdt = (xs * dtx).astype(BF16)
    edl = ecx[L - 1:L, :]
    dend_t = jnp.exp(cum_t[:, L - 1:L] - cum_t)
    lane_l = lax.broadcasted_iota(I32, (L, LANES), 1)
    lane_n = lax.broadcasted_iota(I32, (SSM_STATE, LANES), 1)
    hpg = H_SSM // SSM_GROUPS
    for g in range(SSM_GROUPS):
        bg = xbc[:, D_SSM + g * SSM_STATE:D_SSM + (g + 1) * SSM_STATE].astype(BF16)
        cg = xbc[:, D_SSM + (SSM_GROUPS + g) * SSM_STATE:D_SSM + (SSM_GROUPS + g + 1) * SSM_STATE].astype(BF16)
        cbt = _dot_nt(cg, bg)
        bg_t = _dot_nt(eye, bg)
        gcols = slice(g * hpg * SSM_HEAD_DIM, (g + 1) * hpg * SSM_HEAD_DIM)
        yoff = _dot(cg, ht_ref[:, gcols].astype(BF16)) * ecx[:, gcols]
        for p in range(hpg // 2):
            h0 = g * hpg + 2 * p
            pcols = slice(h0 * SSM_HEAD_DIM, (h0 + 2) * SSM_HEAD_DIM)
            xpair = xdt[:, pcols]
            res, st = [], []
            for hh in (h0, h0 + 1):
                seg = cum[:, hh:hh + 1] - cum_t[hh:hh + 1, :]
                dec = jnp.where(causal, jnp.exp(seg), 0.0)
                res.append(_dot((cbt * dec).astype(BF16), xpair))
                st.append(_dot((bg_t * dend_t[hh:hh + 1, :]).astype(BF16), xpair))
            yb_ref[:, pcols] = (jnp.where(lane_l < SSM_HEAD_DIM, res[0], res[1])
                                + yoff[:, 2 * p * SSM_HEAD_DIM:(2 * p + 2) * SSM_HEAD_DIM])
            ht_ref[:, pcols] = (ht_ref[:, pcols] * edl[:, pcols]
                                + jnp.where(lane_n < SSM_HEAD_DIM, st[0], st[1]))
    hout_ref[0] = ht_ref[...]
    y = yb_ref[...] + dx_ref[...] * xs
    z = z_ref[...]
    if nv < L:
        z = jnp.concatenate([z, jnp.zeros((L - nv, D_SSM), F32)], axis=0)
    y = _rms(y * _silu(z), ng_ref[...])
    y_ref[...] = y[:nv].astype(y_ref.dtype)


def _ssd(u, prevc, h0t, cw, cb, dtb, alog, dx, ng, emat, *, bsz, t, L, nv):
    nt = t // nv
    row = lambda bi, ti: bi * nt + ti
    c512 = lambda c: (lambda bi, ti: (row(bi, ti), c))
    const2 = lambda bi, ti: (0, 0)
    return pl.pallas_call(
        functools.partial(_ssd_kernel, L=L, nv=nv),
        out_shape=(jax.ShapeDtypeStruct((bsz * t, D_SSM), BF16),
                   jax.ShapeDtypeStruct((bsz, SSM_STATE, D_SSM), F32),
                   jax.ShapeDtypeStruct((bsz, SSD_HALO, D_XBC), F32)),
        grid=(bsz, nt),
        in_specs=[pl.BlockSpec((nv, D_SSM), lambda bi, ti: (row(bi, ti), 0)),
                  pl.BlockSpec((nv, 512), c512(COL_XBC // 512)),
                  pl.BlockSpec((nv, 512), c512(COL_XBC // 512 + 1)),
                  pl.BlockSpec((nv, 512), c512(COL_XBC // 512 + 2)),
                  pl.BlockSpec((nv, LANES), c512(COL_SMALL // LANES)),
                  pl.BlockSpec((1, SSD_HALO, D_XBC), lambda bi, ti: (bi, 0, 0)),
                  pl.BlockSpec((1, SSM_STATE, D_SSM), lambda bi, ti: (bi, 0, 0)),
                  pl.BlockSpec((SSM_CONV, D_XBC), const2),
                  pl.BlockSpec((1, D_XBC), const2),
                  pl.BlockSpec((1, LANES), const2),
                  pl.BlockSpec((1, LANES), const2),
                  pl.BlockSpec((1, D_SSM), const2),
                  pl.BlockSpec((1, D_SSM), const2),
                  pl.BlockSpec((LANES, D_SSM), const2)],
        out_specs=(pl.BlockSpec((nv, D_SSM), lambda bi, ti: (row(bi, ti), 0)),
                   pl.BlockSpec((1, SSM_STATE, D_SSM), lambda bi, ti: (bi, 0, 0)),
                   pl.BlockSpec((1, SSD_HALO, D_XBC), lambda bi, ti: (bi, 0, 0))),
        scratch_shapes=[pltpu.VMEM((SSD_HALO + L, D_XBC), F32),
                        pltpu.VMEM((SSM_STATE, D_SSM), F32),
                        pltpu.VMEM((L, D_SSM), F32)],
        compiler_params=pltpu.CompilerParams(dimension_semantics=("arbitrary", "arbitrary"),
                                             vmem_limit_bytes=VMEM_LIMIT),
        name="ssd_mixer",
    )(u, u, u, u, u, prevc, h0t, cw, cb, dtb, alog, dx, ng, emat)


def _mono_key(x):
    b = lax.bitcast_convert_type(x, I32)
    return b ^ ((b >> 31) & INT_MAX)


def _attn_body(*, tq, kt, nkt, topk, qf, qif, sm, kt_tile, v_tile, ki_tile, adm_fn, bias_fn,
               sc_ref, m_ref, l_ref, acc_ref, y_ref):
    nsl = kt // LANES
    lane = lax.broadcasted_iota(I32, (tq, LANES), 1)
    low = lane < HEAD_DIM
    qb = (qf * (HEAD_DIM ** -0.5)).astype(BF16)
    qib = qif.astype(BF16)
    wi = sm[:, SM_WI:SM_WI + H_IDX] * ((D_IDX ** -0.5) * (H_IDX ** -0.5))
    zero_b = jnp.zeros((tq, LANES), BF16)

    def head_window(x, h):
        win = x[:, (h // 2) * LANES:(h // 2 + 1) * LANES]
        return jnp.where(low if h % 2 == 0 else jnp.logical_not(low), win, zero_b)

    qim = [head_window(qib, h) for h in range(H_IDX)]
    wcol = [wi[:, h:h + 1] for h in range(H_IDX)]

    def p1(j, carry):
        ki = ki_tile(j)
        acc = jnp.zeros((tq, kt), F32)
        for h in range(H_IDX):
            acc = acc + jnp.maximum(_dot(qim[h], ki), 0.0) * wcol[h]
        for s in range(nsl):
            key = jnp.where(adm_fn(j, s), _mono_key(acc[:, s * LANES:(s + 1) * LANES]), INT_MIN)
            sc_ref[j, :, s * LANES:(s + 1) * LANES] = key
        return carry

    lax.fori_loop(0, nkt, p1, 0)

    def count(pred):
        def body(j, acc):
            tile = sc_ref[j]
            for s in range(nsl):
                acc = acc + jnp.where(pred(tile[:, s * LANES:(s + 1) * LANES], j, s), 1.0, 0.0)
            return acc
        acc = lax.fori_loop(0, nkt, body, jnp.zeros((tq, LANES), F32))
        return jnp.sum(acc, axis=1, keepdims=True)

    def count_ge(cand):
        cb = jnp.broadcast_to(cand, (tq, LANES))
        return count(lambda t, j, s: t >= cb)

    kf = float(topk)
    prefix = jnp.where(count_ge(jnp.zeros((tq, 1), I32)) >= kf, 0, INT_MIN).astype(I32)

    def bit_step(it, prefix):
        trial = prefix | lax.shift_left(jnp.int32(1), 30 - it)
        return jnp.where(count_ge(trial) >= kf, trial, prefix)

    thr = lax.fori_loop(0, 31, bit_step, prefix)
    thr = jnp.maximum(thr, INT_MIN + 1)
    thr_b = jnp.broadcast_to(thr, (tq, LANES))
    n_gt = count(lambda t, j, s: t > thr_b)
    n_eq = count(lambda t, j, s: t == thr_b)
    take = kf - n_gt

    def kpos(j, s):
        return j * kt + s * LANES + lane

    def tie_cut(_):
        def step(it, cut):
            trial = cut | lax.shift_left(jnp.int32(1), 30 - it)
            tb = jnp.broadcast_to(trial, (tq, LANES))
            c = count(lambda t, j, s: (t == thr_b) & (kpos(j, s) < tb))
            return jnp.where(c <= take, trial, cut)
        return lax.fori_loop(0, 31, step, jnp.zeros((tq, 1), I32))

    has_tie = jnp.max(jnp.where(n_gt + n_eq > kf, 1.0, 0.0)) > 0.0
    cut = lax.cond(has_tie, tie_cut, lambda _: jnp.full((tq, 1), INT_MAX, I32), 0)
    cut_b = jnp.broadcast_to(cut, (tq, LANES))

    qm = [head_window(qb, h) for h in range(H_ATT)]
    m_ref[...] = jnp.full(m_ref.shape, NEG_BIG, F32)
    l_ref[...] = jnp.zeros(l_ref.shape, F32)
    acc_ref[...] = jnp.zeros(acc_ref.shape, F32)

    def p3(j, carry):
        keyt = sc_ref[j]
        sel = []
        for s in range(nsl):
            ks = keyt[:, s * LANES:(s + 1) * LANES]
            sel.append((ks > thr_b) | ((ks == thr_b) & (kpos(j, s) < cut_b)))
        for p in range(H_ATT // 2):
            kp = kt_tile(j, p)
            vp = v_tile(j, p)
            for hsub in range(2):
                h = 2 * p + hsub
                s_all = _dot(qm[h], kp)
                parts = [jnp.where(sel[s], s_all[:, s * LANES:(s + 1) * LANES] + bias_fn(j, s, h), NEG_BIG)
                         for s in range(nsl)]
                mx = parts[0]
                for s in range(1, nsl):
                    mx = jnp.maximum(mx, parts[s])
                m_old = m_ref[h]
                m_new = jnp.maximum(m_old, jnp.max(mx, axis=1, keepdims=True))
                alpha = jnp.exp(m_old - m_new)
                pr = [jnp.exp(part - m_new) for part in parts]
                psum = pr[0]
                for s in range(1, nsl):
                    psum = psum + pr[s]
                l_ref[h] = alpha * l_ref[h] + psum
                pb = jnp.concatenate([x.astype(BF16) for x in pr], axis=1)
                acc_ref[h] = alpha * acc_ref[h] + _dot(pb, vp)
                m_ref[h] = m_new
        return carry

    lax.fori_loop(0, nkt, p3, 0)
    for p in range(H_ATT // 2):
        outs = []
        for hsub in range(2):
            h = 2 * p + hsub
            lsum = jnp.sum(l_ref[h], axis=1, keepdims=True)
            outs.append(acc_ref[h] / lsum)
        y_ref[:, p * LANES:(p + 1) * LANES] = jnp.where(low, outs[0], outs[1]).astype(y_ref.dtype)


N_BIAS_NEAR = 7


def _attn_prompt_kernel(q_ref, qi_ref, sm_ref, kt_ref, v_ref, ki_ref, bt_ref, y_ref, sc_ref, m_ref, l_ref, acc_ref,
                        *, tq, kt, topk):
    i = pl.program_id(1)
    nsl = kt // LANES
    nkt = lax.div((i + 1) * tq + (kt - 1), kt)
    lane = lax.broadcasted_iota(I32, (tq, LANES), 1)
    rowi = lax.broadcasted_iota(I32, (tq, LANES), 0)
    qpos = i * tq + rowi
    cend = (lax.shift_right_logical(qpos, int(math.log2(CHUNK))) + 1) * CHUNK

    def adm_fn(j, s):
        return (j * kt + s * LANES + lane) < cend

    def bias_fn(j, s, h):
        d = jnp.clip(i - (j * nsl + s), 0, N_BIAS_NEAR - 1)
        return bt_ref[d, h]

    _attn_body(tq=tq, kt=kt, nkt=nkt, topk=topk, qf=q_ref[...], qif=qi_ref[...], sm=sm_ref[...],
               kt_tile=lambda j, p: kt_ref[0, j, p * LANES:(p + 1) * LANES, :],
               v_tile=lambda j, p: v_ref[0, j, :, p * LANES:(p + 1) * LANES],
               ki_tile=lambda j: ki_ref[0, j],
               adm_fn=adm_fn, bias_fn=bias_fn,
               sc_ref=sc_ref, m_ref=m_ref, l_ref=l_ref, acc_ref=acc_ref, y_ref=y_ref)


def _attn_prompt(u, kt_tiles, v_tiles, ki_tiles, btab, *, bsz, t, tq, kt):
    assert tq == LANES and t % kt == 0 and t % tq == 0 and CHUNK * 2 == tq
    nq = t // tq
    nk = t // kt
    row = lambda bi, qi: bi * nq + qi
    return pl.pallas_call(
        functools.partial(_attn_prompt_kernel, tq=tq, kt=kt, topk=min(TOPK, t // 4)),
        out_shape=jax.ShapeDtypeStruct((bsz * t, D_ATT), BF16),
        grid=(bsz, nq),
        in_specs=[pl.BlockSpec((tq, D_ATT), lambda bi, qi: (row(bi, qi), COL_Q // D_ATT)),
                  pl.BlockSpec((tq, D_ATT), lambda bi, qi: (row(bi, qi), COL_QI // D_ATT)),
                  pl.BlockSpec((tq, LANES), lambda bi, qi: (row(bi, qi), COL_SMALL // LANES)),
                  pl.BlockSpec((1, nk, D_ATT, kt), lambda bi, qi: (bi, 0, 0, 0)),
                  pl.BlockSpec((1, nk, kt, D_ATT), lambda bi, qi: (bi, 0, 0, 0)),
                  pl.BlockSpec((1, nk, LANES, kt), lambda bi, qi: (bi, 0, 0, 0)),
                  pl.BlockSpec((N_BIAS_NEAR, H_ATT, tq, LANES), lambda bi, qi: (0, 0, 0, 0))],
        out_specs=pl.BlockSpec((tq, D_ATT), lambda bi, qi: (row(bi, qi), 0)),
        scratch_shapes=[pltpu.VMEM((nk, tq, kt), I32),
                        pltpu.VMEM((H_ATT, tq, LANES), F32),
                        pltpu.VMEM((H_ATT, tq, LANES), F32),
                        pltpu.VMEM((H_ATT, tq, LANES), F32)],
        compiler_params=pltpu.CompilerParams(dimension_semantics=("arbitrary", "arbitrary"),
                                             vmem_limit_bytes=VMEM_LIMIT),
        name="sparse_attn_prompt",
    )(u, u, u, kt_tiles, v_tiles, ki_tiles, btab)


def _attn_sample_kernel(q_ref, qi_ref, sm_ref, kt_ref, v_ref, ki_ref, bt_ref, y_ref, sc_ref, m_ref, l_ref, acc_ref,
                        *, tq, kt, n_keys, topk):
    lane = lax.broadcasted_iota(I32, (tq, LANES), 1)

    def adm_fn(j, s):
        return (j * kt + s * LANES + lane) < n_keys

    def bias_fn(j, s, h):
        return bt_ref[0, h, :, s * LANES:(s + 1) * LANES]

    _attn_body(tq=tq, kt=kt, nkt=1, topk=topk, qf=q_ref[...], qif=qi_ref[...], sm=sm_ref[...],
               kt_tile=lambda j, p: kt_ref[0, p * LANES:(p + 1) * LANES, :],
               v_tile=lambda j, p: v_ref[0, :, p * LANES:(p + 1) * LANES],
               ki_tile=lambda j: ki_ref[0],
               adm_fn=adm_fn, bias_fn=bias_fn,
               sc_ref=sc_ref, m_ref=m_ref, l_ref=l_ref, acc_ref=acc_ref, y_ref=y_ref)


def _attn_sample(u, kt_all, v_all, ki_all, btab, *, bsz, tq, kt, n_keys):
    return pl.pallas_call(
        functools.partial(_attn_sample_kernel, tq=tq, kt=kt, n_keys=n_keys, topk=min(TOPK, n_keys // 4)),
        out_shape=jax.ShapeDtypeStruct((bsz * tq, D_ATT), BF16),
        grid=(bsz,),
        in_specs=[pl.BlockSpec((tq, D_ATT), lambda bi: (bi, COL_Q // D_ATT)),
                  pl.BlockSpec((tq, D_ATT), lambda bi: (bi, COL_QI // D_ATT)),
                  pl.BlockSpec((tq, LANES), lambda bi: (bi, COL_SMALL // LANES)),
                  pl.BlockSpec((1, D_ATT, kt), lambda bi: (bi, 0, 0)),
                  pl.BlockSpec((1, kt, D_ATT), lambda bi: (bi, 0, 0)),
                  pl.BlockSpec((1, LANES, kt), lambda bi: (bi, 0, 0)),
                  pl.BlockSpec((1, H_ATT, tq, kt), lambda bi: (0, 0, 0, 0))],
        out_specs=pl.BlockSpec((tq, D_ATT), lambda bi: (bi, 0)),
        scratch_shapes=[pltpu.VMEM((1, tq, kt), I32),
                        pltpu.VMEM((H_ATT, tq, LANES), F32),
                        pltpu.VMEM((H_ATT, tq, LANES), F32),
                        pltpu.VMEM((H_ATT, tq, LANES), F32)],
        compiler_params=pltpu.CompilerParams(dimension_semantics=("arbitrary",),
                                             vmem_limit_bytes=VMEM_LIMIT),
        name="sparse_attn_sample",
    )(u, u, u, kt_all, v_all, ki_all, btab)


def _out_proj_kernel(x_ref, ys_ref, yc_ref, ya_ref, w_ref, g_ref, o_ref):
    acc = _dot(ys_ref[...], w_ref[0:D_SSM, :])
    acc = acc + _dot(yc_ref[...], w_ref[D_SSM:D_SSM + D_CONV, :])
    acc = acc + _dot(ya_ref[...], w_ref[D_SSM + D_CONV:D_MODEL, :])
    o_ref[...] = x_ref[...] + _rms(acc, g_ref[...])


def _out_proj(x, ys, yc, ya, w, g, *, tm):
    m = x.shape[0]
    return pl.pallas_call(
        _out_proj_kernel,
        out_shape=jax.ShapeDtypeStruct((m, D_MODEL), F32),
        grid=(m // tm,),
        in_specs=[pl.BlockSpec((tm, D_MODEL), lambda i: (i, 0)),
                  pl.BlockSpec((tm, D_SSM), lambda i: (i, 0)),
                  pl.BlockSpec((tm, D_CONV), lambda i: (i, 0)),
                  pl.BlockSpec((tm, D_ATT), lambda i: (i, 0)),
                  pl.BlockSpec((D_MODEL, D_MODEL), lambda i: (0, 0)),
                  pl.BlockSpec((1, D_MODEL), lambda i: (0, 0))],
        out_specs=pl.BlockSpec((tm, D_MODEL), lambda i: (i, 0)),
        compiler_params=pltpu.CompilerParams(dimension_semantics=("arbitrary",),
                                             vmem_limit_bytes=VMEM_LIMIT),
        name="out_proj",
    )(x, ys, yc, ya, w, g)


FFN_HALO = 8


def _ffn_kernel(x_ref, gpre_ref, wg_ref, wu_ref, wd_ref, cw_ref, cb_ref, gpost_ref, p1_ref, p2_ref,
                o_ref, aux_ref, h_ref, acc_ref, buf_ref, tail_ref, *, tm, tps, seq_len, chained):
    i = pl.program_id(0)
    j = pl.program_id(1)

    @pl.when(j == 0)
    def _():
        h_ref[...] = _rms(x_ref[...], gpre_ref[...]).astype(BF16)
        acc_ref[...] = jnp.zeros(acc_ref.shape, F32)

    h = h_ref[...]
    a_pre = _dot(h, wg_ref[...])
    buf_ref[FFN_HALO:FFN_HALO + tm, :] = a_pre
    if chained:
        seq_start = lax.rem(i, tps) == 0
        buf_ref[0:FFN_HALO, :] = jnp.where(seq_start, p1_ref[0], tail_ref[j])
        prev1 = buf_ref[FFN_HALO - 1:FFN_HALO - 1 + tm, :]
        prev2 = buf_ref[FFN_HALO - 2:FFN_HALO - 2 + tm, :]
        last = a_pre[tm - FFN_HALO:tm, :]
        tail_ref[j] = last
        aux_ref[0] = last
    else:
        buf_ref[0:FFN_HALO, :] = jnp.zeros((FFN_HALO, a_pre.shape[1]), F32)
        tpos = lax.rem(lax.broadcasted_iota(I32, a_pre.shape, 0), seq_len)
        prev1 = jnp.where(tpos >= 1, buf_ref[FFN_HALO - 1:FFN_HALO - 1 + tm, :], p1_ref[...])
        prev2 = jnp.where(tpos >= 2, buf_ref[FFN_HALO - 2:FFN_HALO - 2 + tm, :], p2_ref[...])
        aux_ref[...] = a_pre
    a = cw_ref[0:1, :] * prev2 + cw_ref[1:2, :] * prev1 + cw_ref[2:3, :] * a_pre + cb_ref[...]
    f = (_silu(a) * _dot(h, wu_ref[...])).astype(BF16)
    acc_ref[...] += _dot(f, wd_ref[...])

    @pl.when(j == pl.num_programs(1) - 1)
    def _():
        o_ref[...] = x_ref[...] + _rms(acc_ref[...], gpost_ref[...])


def _ffn(x, gpre, wg, wu, wd, cw, cb, gpost, p1, p2, *, tm, tf, seq_len, chained):
    m = x.shape[0]
    nf = D_FF // tf
    tps = max(seq_len // tm, 1)
    if chained:
        nseq = m // seq_len
        p_specs = [pl.BlockSpec((1, FFN_HALO, tf), lambda i, j: (i // tps, 0, j)),
                   pl.BlockSpec((1, FFN_HALO, tf), lambda i, j: (i // tps, 0, j))]
        aux_shape = jax.ShapeDtypeStruct((nseq, FFN_HALO, D_FF), F32)
        aux_spec = pl.BlockSpec((1, FFN_HALO, tf), lambda i, j: (i // tps, 0, j))
    else:
        p_specs = [pl.BlockSpec((tm, tf), lambda i, j: (i, j)), pl.BlockSpec((tm, tf), lambda i, j: (i, j))]
        aux_shape = jax.ShapeDtypeStruct((m, D_FF), F32)
        aux_spec = pl.BlockSpec((tm, tf), lambda i, j: (i, j))
    return pl.pallas_call(
        functools.partial(_ffn_kernel, tm=tm, tps=tps, seq_len=seq_len, chained=chained),
        out_shape=(jax.ShapeDtypeStruct((m, D_MODEL), F32), aux_shape),
        grid=(m // tm, nf),
        in_specs=[pl.BlockSpec((tm, D_MODEL), lambda i, j: (i, 0)),
                  pl.BlockSpec((1, D_MODEL), lambda i, j: (0, 0)),
                  pl.BlockSpec((D_MODEL, tf), lambda i, j: (0, j)),
                  pl.BlockSpec((D_MODEL, tf), lambda i, j: (0, j)),
                  pl.BlockSpec((tf, D_MODEL), lambda i, j: (j, 0)),
                  pl.BlockSpec((FFN_CONV, tf), lambda i, j: (0, j)),
                  pl.BlockSpec((1, tf), lambda i, j: (0, j)),
                  pl.BlockSpec((1, D_MODEL), lambda i, j: (0, 0))] + p_specs,
        out_specs=(pl.BlockSpec((tm, D_MODEL), lambda i, j: (i, 0)), aux_spec),
        scratch_shapes=[pltpu.VMEM((tm, D_MODEL), BF16),
                        pltpu.VMEM((tm, D_MODEL), F32),
                        pltpu.VMEM((FFN_HALO + tm, tf), F32),
                        pltpu.VMEM((nf, FFN_HALO, tf), F32)],
        compiler_params=pltpu.CompilerParams(dimension_semantics=("arbitrary", "arbitrary"),
                                             vmem_limit_bytes=VMEM_LIMIT),
        name="conv_ffn",
    )(x, gpre, wg, wu, wd, cw, cb, gpost, p1, p2)


def _prep_layer_weights(w):
    w_in = w["w_in"]
    o_dt = D_SSM + D_XBC
    o_glu = o_dt + H_SSM
    o_ki = o_glu + 2 * D_CONV + 4 * D_ATT
    o_wi = o_ki + D_IDX
    pad = jnp.zeros((D_MODEL, LANES - H_SSM - H_IDX - D_IDX), w_in.dtype)
    w_r = jnp.concatenate([w_in[:, :o_dt], w_in[:, o_glu:o_ki], w_in[:, o_dt:o_glu], w_in[:, o_wi:o_wi + H_IDX],
                           pad, w_in[:, o_ki:o_wi]], axis=1).astype(BF16)
    assert w_r.shape[1] == D_PROJ
    row = lambda v: v.reshape(1, -1).astype(F32)
    padl = lambda v: jnp.pad(v.astype(F32), (0, LANES - v.shape[0])).reshape(1, LANES)
    return dict(
        w_in=w_r, g_mix_pre=row(w["g_mix_pre"]),
        ssm_conv_w=w["ssm_conv_w"].astype(F32), ssm_conv_b=row(w["ssm_conv_b"]),
        dt_bias=padl(w["ssm_dt_bias"]), a_log=padl(w["ssm_a_log"]),
        d_x=row(jnp.repeat(w["ssm_d"], SSM_HEAD_DIM)), ssm_norm_g=row(w["ssm_norm_g"]),
        cconv_w=w["cconv_w"].astype(F32), cconv_b=row(w["cconv_b"]),
        cconv_ln_g=row(w["cconv_ln_g"]), cconv_ln_b=row(w["cconv_ln_b"]),
        w_out=w["w_out"].astype(BF16), g_mix_post=row(w["g_mix_post"]), g_ffn_pre=row(w["g_ffn_pre"]),
        ffn_w_gate=w["ffn_w_gate"].astype(BF16), ffn_w_up=w["ffn_w_up"].astype(BF16),
        ffn_w_down=w["ffn_w_down"].astype(BF16), ffn_conv_w=w["ffn_conv_w"].astype(F32),
        ffn_conv_b=row(w["ffn_conv_b"]), g_ffn_post=row(w["g_ffn_post"]))


def _expand_matrix():
    e = np.zeros((LANES, D_SSM), np.float32)
    for h in range(H_SSM):
        e[h, h * SSM_HEAD_DIM:(h + 1) * SSM_HEAD_DIM] = 1.0
    return jnp.asarray(e, BF16)


def _front_pad(state, halo):
    return jnp.pad(state.astype(F32), ((0, 0), (halo - state.shape[1], 0), (0, 0)))


def _state_t(h):
    b = h.shape[0]
    return jnp.transpose(h.astype(F32), (0, 3, 1, 2)).reshape(b, SSM_STATE, D_SSM)


def _state_from_t(ht):
    b = ht.shape[0]
    return jnp.transpose(ht.reshape(b, SSM_STATE, H_SSM, SSM_HEAD_DIM), (0, 2, 3, 1))


def _mixer_common(x, lw, emat, ssm_conv_prev, ssm_h0, cconv_prev, *, bsz, t, tm_proj, tn_proj, ssd_l, ssd_nv, cc_tm):
    u = _rms_matmul(x, lw["g_mix_pre"], lw["w_in"], tm=tm_proj, tn=tn_proj)
    y_ssm, ht, ctail = _ssd(u, _front_pad(ssm_conv_prev, SSD_HALO), _state_t(ssm_h0), lw["ssm_conv_w"],
                            lw["ssm_conv_b"], lw["dt_bias"], lw["a_log"], lw["d_x"], lw["ssm_norm_g"], emat,
                            bsz=bsz, t=t, L=ssd_l, nv=ssd_nv)
    y_conv, cctail = _cconv(u, _front_pad(cconv_prev, CC_HALO), lw["cconv_w"], lw["cconv_b"], lw["cconv_ln_g"],
                            lw["cconv_ln_b"], bsz=bsz, t=t, tm=cc_tm)
    k = u[:, COL_K:COL_K + D_ATT]
    v = u[:, COL_V:COL_V + D_ATT]
    ki = u[:, COL_SMALL + SM_KI:COL_SMALL + SM_KI + D_IDX]
    states = dict(k=k.reshape(bsz, t, H_ATT, HEAD_DIM), v=v.reshape(bsz, t, H_ATT, HEAD_DIM),
                  ki=ki.reshape(bsz, t, D_IDX), h=_state_from_t(ht),
                  ssm_conv=ctail[:, SSD_HALO - (SSM_CONV - 1):], cconv=cctail[:, CC_HALO - (CONV_WIDTH - 1):])
    return u, y_ssm, y_conv, k, v, ki, states


def _layer_prompt(x, lw, emat, btab, *, bsz, t, cfg):
    zeros = lambda *s: jnp.zeros(s, F32)
    u, y_ssm, y_conv, k, v, ki, st = _mixer_common(
        x, lw, emat, zeros(bsz, SSM_CONV - 1, D_XBC), zeros(bsz, H_SSM, SSM_HEAD_DIM, SSM_STATE),
        zeros(bsz, CONV_WIDTH - 1, D_CONV), bsz=bsz, t=t, tm_proj=cfg["tm_proj"], tn_proj=cfg["tn_proj"],
        ssd_l=cfg["ssd_l"], ssd_nv=cfg["ssd_l"], cc_tm=cfg["cc_tm"])
    kt = cfg["kt"]
    nk = t // kt
    kb = k.astype(BF16).reshape(bsz, nk, kt, D_ATT)
    kt_tiles = jnp.transpose(kb, (0, 1, 3, 2))
    v_tiles = v.astype(BF16).reshape(bsz, nk, kt, D_ATT)
    kit = jnp.transpose(ki.astype(BF16).reshape(bsz, nk, kt, D_IDX), (0, 1, 3, 2))
    ki_tiles = jnp.concatenate([kit, kit], axis=2)
    y_att = _attn_prompt(u, kt_tiles, v_tiles, ki_tiles, btab, bsz=bsz, t=t, tq=LANES, kt=kt)
    x1 = _out_proj(x, y_ssm, y_conv, y_att, lw["w_out"], lw["g_mix_post"], tm=cfg["tm_out"])
    prev = zeros(bsz, FFN_HALO, D_FF)
    x2, ftail = _ffn(x1, lw["g_ffn_pre"], lw["ffn_w_gate"], lw["ffn_w_up"], lw["ffn_w_down"], lw["ffn_conv_w"],
                     lw["ffn_conv_b"], lw["g_ffn_post"], prev, prev, tm=cfg["tm_ffn"], tf=cfg["tf"], seq_len=t,
                     chained=True)
    st["ffn_conv"] = ftail[:, FFN_HALO - (FFN_CONV - 1):]
    return x2, st


def _layer_sample(x, lw, emat, btab, past_k, past_v, past_ki, ssm_conv_prev, ssm_h0, cconv_prev, fconv_prev,
                  *, bsz, t, cfg):
    m = bsz * t
    u, y_ssm, y_conv, k, v, ki, st = _mixer_common(
        x, lw, emat, ssm_conv_prev, ssm_h0, cconv_prev, bsz=bsz, t=t, tm_proj=m, tn_proj=cfg["tn_proj"],
        ssd_l=LANES, ssd_nv=t, cc_tm=t)
    past = past_k.shape[1]
    n_keys = past + t
    ktp = cfg["kt_sample"]
    padk = lambda a: jnp.pad(a, ((0, 0), (0, ktp - n_keys), (0, 0)))
    k_all = padk(jnp.concatenate([past_k.reshape(bsz, past, D_ATT), k.reshape(bsz, t, D_ATT)], axis=1).astype(BF16))
    v_all = padk(jnp.concatenate([past_v.reshape(bsz, past, D_ATT), v.reshape(bsz, t, D_ATT)], axis=1).astype(BF16))
    ki_all = padk(jnp.concatenate([past_ki, ki.reshape(bsz, t, D_IDX)], axis=1).astype(BF16))
    kt_all = jnp.transpose(k_all, (0, 2, 1))
    kit = jnp.transpose(ki_all, (0, 2, 1))
    y_att = _attn_sample(u, kt_all, v_all, jnp.concatenate([kit, kit], axis=1), btab, bsz=bsz, tq=t, kt=ktp,
                         n_keys=n_keys)
    x1 = _out_proj(x, y_ssm, y_conv, y_att, lw["w_out"], lw["g_mix_post"], tm=m)
    fprev = fconv_prev.astype(F32)
    zrow = jnp.zeros((bsz, t - 1, D_FF), F32)
    p1 = jnp.concatenate([fprev[:, 1:2], zrow], axis=1).reshape(m, D_FF)
    p2 = jnp.concatenate([fprev, zrow[:, 1:]], axis=1).reshape(m, D_FF)
    x2, a_pre = _ffn(x1, lw["g_ffn_pre"], lw["ffn_w_gate"], lw["ffn_w_up"], lw["ffn_w_down"], lw["ffn_conv_w"],
                     lw["ffn_conv_b"], lw["g_ffn_post"], p1, p2, tm=m, tf=cfg["tf"], seq_len=t, chained=False)
    st["ffn_conv"] = a_pre.reshape(bsz, t, D_FF)[:, t - (FFN_CONV - 1):]
    return x2, st


_STATE_ORDER = ("k", "v", "ki", "h", "ssm_conv", "cconv", "ffn_conv")


def _prompt_cfg(t):
    big = t >= 4096
    return dict(tm_proj=1024 if big else 256, tn_proj=640, ssd_l=256, cc_tm=256, kt=512 if big else 256,
                tm_out=512 if big else 256, tm_ffn=512 if big else 256, tf=512)


def _forward(x_prompt, x_sample, cache_k, cache_v, cache_kidx, state_ssm, state_ssm_conv, state_cconv,
             state_ffn_conv, rel_bias, weights):
    bp, tp, _ = x_prompt.shape
    bs, ts, _ = x_sample.shape
    depth = weights["w_in"].shape[0]
    past = cache_k.shape[2]
    emat = _expand_matrix()
    cfg_p = _prompt_cfg(tp)
    kt_sample = -(-(past + ts) // LANES) * LANES
    cfg_s = dict(tn_proj=640, tf=512, kt_sample=kt_sample)
    rb = rel_bias.astype(F32)
    btab_p = _bias_table(rb, nd=N_BIAS_NEAR, rows=LANES, cols=LANES, off0=0, step=LANES)
    btab_s = _bias_table(rb, nd=1, rows=ts, cols=kt_sample, off0=-past, step=0)
    xp = x_prompt.reshape(bp * tp, D_MODEL)
    xs = x_sample.reshape(bs * ts, D_MODEL)
    p_states = {n: [] for n in _STATE_ORDER}
    s_states = {n: [] for n in _STATE_ORDER}
    for l in range(depth):
        lw = _prep_layer_weights({n: w[l] for n, w in weights.items()})
        xp, st_p = _layer_prompt(xp, lw, emat, btab_p, bsz=bp, t=tp, cfg=cfg_p)
        xs, st_s = _layer_sample(xs, lw, emat, btab_s, cache_k[l], cache_v[l], cache_kidx[l], state_ssm_conv[l],
                                 state_ssm[l], state_cconv[l], state_ffn_conv[l], bsz=bs, t=ts, cfg=cfg_s)
        for n in _STATE_ORDER:
            p_states[n].append(st_p[n])
            s_states[n].append(st_s[n])
    outs = [xp.reshape(bp, tp, D_MODEL), xs.reshape(bs, ts, D_MODEL)]
    outs += [jnp.stack(p_states[n]) for n in _STATE_ORDER]
    outs += [jnp.stack(s_states[n]) for n in _STATE_ORDER]
    return tuple(outs)


def kernel(x_prompt, x_sample, cache_k, cache_v, cache_kidx, state_ssm, state_ssm_conv, state_cconv, state_ffn_conv, rel_bias, g_mix_pre, w_in, ssm_conv_w, ssm_conv_b, ssm_dt_bias, ssm_a_log, ssm_d, ssm_norm_g, cconv_w, cconv_b, cconv_ln_g, cconv_ln_b, w_out, g_mix_post, g_ffn_pre, ffn_w_gate, ffn_w_up, ffn_conv_w, ffn_conv_b, ffn_w_down, g_ffn_post):
    weights = dict(g_mix_pre=g_mix_pre, w_in=w_in, ssm_conv_w=ssm_conv_w, ssm_conv_b=ssm_conv_b,
                   ssm_dt_bias=ssm_dt_bias, ssm_a_log=ssm_a_log, ssm_d=ssm_d, ssm_norm_g=ssm_norm_g,
                   cconv_w=cconv_w, cconv_b=cconv_b, cconv_ln_g=cconv_ln_g, cconv_ln_b=cconv_ln_b, w_out=w_out,
                   g_mix_post=g_mix_post, g_ffn_pre=g_ffn_pre, ffn_w_gate=ffn_w_gate, ffn_w_up=ffn_w_up,
                   ffn_conv_w=ffn_conv_w, ffn_conv_b=ffn_conv_b, ffn_w_down=ffn_w_down, g_ffn_post=g_ffn_post)
    return _forward(x_prompt, x_sample, cache_k, cache_v, cache_kidx, state_ssm, state_ssm_conv, state_cconv,
                    state_ffn_conv, rel_bias, weights)
```

```python
import functools
import math

import numpy as np
import jax
import jax.numpy as jnp
from jax import lax
from jax.experimental import pallas as pl
from jax.experimental.pallas import tpu as pltpu

F32 = jnp.float32
BF16 = jnp.bfloat16
I32 = jnp.int32

D_MODEL = 2048
D_SSM = 1024
SSM_HEAD_DIM = 64
H_SSM = 16
SSM_GROUPS = 2
SSM_STATE = 128
SSM_CONV = 4
D_XBC = D_SSM + 2 * SSM_GROUPS * SSM_STATE
D_CONV = 512
CONV_WIDTH = 31
D_ATT = 512
HEAD_DIM = 64
H_ATT = 8
H_IDX = 8
D_IDX = 64
TOPK = 256
CHUNK = 64
N_BUCKETS = 32
REL_MAX_DIST = 1024
D_FF = 5632
FFN_CONV = 3
EPS = 1e-6

LANES = 128
SUBLANES = 8

COL_Z, COL_XBC, COL_GLU, COL_Q, COL_K, COL_V, COL_QI, COL_SMALL = 0, 1024, 2560, 3584, 4096, 4608, 5120, 5632
D_PROJ = COL_SMALL + LANES
SM_DT, SM_WI, SM_KI = 0, 16, 64

INT_MIN = -(2 ** 31)
INT_MAX = 2 ** 31 - 1
NEG_BIG = -1e30
VMEM_LIMIT = 56 * 1024 * 1024


def _bucket_thresholds():
    nb = N_BUCKETS // 2
    max_exact = nb // 2
    n = np.arange(0, 4 * REL_MAX_DIST, dtype=np.int64)
    nf = np.maximum(n, 1).astype(np.float32)
    large = max_exact + (np.log(nf / np.float32(max_exact)) / np.float32(math.log(REL_MAX_DIST / max_exact))
                         * np.float32(nb - max_exact)).astype(np.int32)
    large = np.minimum(large, nb - 1)
    bucket = np.where(n < max_exact, n, large)
    steps = np.nonzero(np.diff(bucket))[0] + 1
    assert np.all(np.diff(bucket) >= 0) and np.all(np.diff(bucket) <= 1) and bucket[-1] == nb - 1
    return tuple(int(s) for s in steps)


BUCKET_STEPS = _bucket_thresholds()


def _sigmoid(x):
    return 1.0 / (1.0 + jnp.exp(-x))


def _silu(x):
    return x * _sigmoid(x)


def _split3(x):
    hi = x.astype(BF16)
    r1 = x - hi.astype(F32)
    mid = r1.astype(BF16)
    lo = (r1 - mid.astype(F32)).astype(BF16)
    return hi, mid, lo


def _dot(a, b):
    return jnp.dot(a, b, preferred_element_type=F32)


def _dot_nt(a, b):
    return lax.dot_general(a, b, (((1,), (1,)), ((), ())), preferred_element_type=F32)


def _exact_dot(sel_bf16, x_f32):
    hi, mid, lo = _split3(x_f32)
    return _dot(sel_bf16, hi) + _dot(sel_bf16, mid) + _dot(sel_bf16, lo)


def _exact_dot_r(x_f32, sel_bf16):
    hi, mid, lo = _split3(x_f32)
    return _dot(hi, sel_bf16) + _dot(mid, sel_bf16) + _dot(lo, sel_bf16)


def _rms(x, g):
    ms = jnp.mean(x * x, axis=-1, keepdims=True)
    return x * lax.rsqrt(ms + EPS) * g


def _rms_matmul_kernel(x_ref, g_ref, w_ref, o_ref, h_ref):
    @pl.when(pl.program_id(1) == 0)
    def _():
        h_ref[...] = _rms(x_ref[...], g_ref[...]).astype(BF16)

    o_ref[...] = _dot(h_ref[...], w_ref[...])


def _rms_matmul(x, g, w, *, tm, tn):
    m, d = x.shape
    n = w.shape[1]
    assert m % tm == 0 and n % tn == 0
    return pl.pallas_call(
        _rms_matmul_kernel,
        out_shape=jax.ShapeDtypeStruct((m, n), F32),
        grid=(m // tm, n // tn),
        in_specs=[pl.BlockSpec((tm, d), lambda i, j: (i, 0)),
                  pl.BlockSpec((1, d), lambda i, j: (0, 0)),
                  pl.BlockSpec((d, tn), lambda i, j: (0, j))],
        out_specs=pl.BlockSpec((tm, tn), lambda i, j: (i, j)),
        scratch_shapes=[pltpu.VMEM((tm, d), BF16)],
        compiler_params=pltpu.CompilerParams(dimension_semantics=("arbitrary", "arbitrary"),
                                             vmem_limit_bytes=VMEM_LIMIT),
        name="rms_in_proj",
    )(x, g, w)


def _bias_table_kernel(rb_ref, o_ref, *, off0, step, key_axis, scale):
    d = pl.program_id(0)
    h = pl.program_id(1)
    rows, cols = o_ref.shape[2], o_ref.shape[3]
    rel = (lax.broadcasted_iota(I32, (rows, cols), key_axis) - lax.broadcasted_iota(I32, (rows, cols), 1 - key_axis)
           + (off0 - d * step))
    n = jnp.abs(rel)
    bucket = jnp.where(rel > 0, N_BUCKETS // 2, 0)
    for s in BUCKET_STEPS:
        bucket = bucket + jnp.where(n >= s, 1, 0)
    acc = jnp.zeros((rows, cols), F32)
    for b in range(N_BUCKETS):
        acc = jnp.where(bucket == b, rb_ref[b, h], acc)
    o_ref[0, 0] = acc * scale


def _bias_table(rel_bias, *, nd, rows, cols, off0, step, key_axis=1, scale=1.0):
    return pl.pallas_call(
        functools.partial(_bias_table_kernel, off0=off0, step=step, key_axis=key_axis, scale=scale),
        out_shape=jax.ShapeDtypeStruct((nd, H_ATT, rows, cols), F32),
        grid=(nd, H_ATT),
        in_specs=[pl.BlockSpec(memory_space=pltpu.SMEM)],
        out_specs=pl.BlockSpec((1, 1, rows, cols), lambda d, h: (d, h, 0, 0)),
        name="bias_table",
    )(rel_bias)


CC_HALO = 32


def _cconv_kernel(val_ref, gate_ref, prev_ref, w_ref, b_ref, lg_ref, lb_ref, y_ref, tail_ref, buf_ref, *, tm):
    @pl.when(pl.program_id(1) == 0)
    def _():
        buf_ref[0:CC_HALO, :] = prev_ref[0]

    buf_ref[CC_HALO:CC_HALO + tm, :] = val_ref[...] * _sigmoid(gate_ref[...])
    first = CC_HALO - (CONV_WIDTH - 1)
    acc = jnp.zeros((tm, D_CONV), F32) + b_ref[...]
    for k in range(CONV_WIDTH):
        acc = acc + w_ref[k:k + 1, :] * buf_ref[first + k:first + k + tm, :]
    mu = jnp.mean(acc, axis=-1, keepdims=True)
    xc = acc - mu
    var = jnp.mean(xc * xc, axis=-1, keepdims=True)
    y = xc * lax.rsqrt(var + EPS) * lg_ref[...] + lb_ref[...]
    y_ref[...] = _silu(y).astype(y_ref.dtype)
    tail = buf_ref[tm:tm + CC_HALO, :]
    tail_ref[0] = tail
    buf_ref[0:CC_HALO, :] = tail


def _cconv(u, prev, w, b, lg, lb, *, bsz, t, tm):
    nt = t // tm
    row = lambda bi, ti: bi * nt + ti
    return pl.pallas_call(
        functools.partial(_cconv_kernel, tm=tm),
        out_shape=(jax.ShapeDtypeStruct((bsz * t, D_CONV), BF16),
                   jax.ShapeDtypeStruct((bsz, CC_HALO, D_CONV), F32)),
        grid=(bsz, nt),
        in_specs=[pl.BlockSpec((tm, D_CONV), lambda bi, ti: (row(bi, ti), COL_GLU // D_CONV)),
                  pl.BlockSpec((tm, D_CONV), lambda bi, ti: (row(bi, ti), COL_GLU // D_CONV + 1)),
                  pl.BlockSpec((1, CC_HALO, D_CONV), lambda bi, ti: (bi, 0, 0)),
                  pl.BlockSpec((CONV_WIDTH, D_CONV), lambda bi, ti: (0, 0)),
                  pl.BlockSpec((1, D_CONV), lambda bi, ti: (0, 0)),
                  pl.BlockSpec((1, D_CONV), lambda bi, ti: (0, 0)),
                  pl.BlockSpec((1, D_CONV), lambda bi, ti: (0, 0))],
        out_specs=(pl.BlockSpec((tm, D_CONV), lambda bi, ti: (row(bi, ti), 0)),
                   pl.BlockSpec((1, CC_HALO, D_CONV), lambda bi, ti: (bi, 0, 0))),
        scratch_shapes=[pltpu.VMEM((CC_HALO + tm, D_CONV), F32)],
        compiler_params=pltpu.CompilerParams(dimension_semantics=("arbitrary", "arbitrary"),
                                             vmem_limit_bytes=VMEM_LIMIT),
        name="conformer_conv",
    )(u, u, prev, w, b, lg, lb)


SSD_HALO = 8


def _ssd_kernel(z_ref, x0_ref, x1_ref, x2_ref, sm_ref, prevc_ref, h0_ref, cw_ref, cb_ref, dtb_ref, alog_ref,
                dx_ref, ng_ref, e_ref, y_ref, hout_ref, ctail_ref, buf_ref, ht_ref, yb_ref, *, L, nv):
    @pl.when(pl.program_id(1) == 0)
    def _():
        buf_ref[0:SSD_HALO, :] = prevc_ref[0]
        ht_ref[...] = h0_ref[0]

    if nv < L:
        buf_ref[SSD_HALO + nv:SSD_HALO + L, :] = jnp.zeros((L - nv, D_XBC), F32)
    for c, r in enumerate((x0_ref, x1_ref, x2_ref)):
        buf_ref[SSD_HALO:SSD_HALO + nv, c * 512:(c + 1) * 512] = r[...]
    first = SSD_HALO - (SSM_CONV - 1)
    acc = jnp.zeros((L, D_XBC), F32) + cb_ref[...]
    for k in range(SSM_CONV):
        acc = acc + cw_ref[k:k + 1, :] * buf_ref[first + k:first + k + L, :]
    xbc = _silu(acc)
    tail = buf_ref[nv:nv + SSD_HALO, :]
    ctail_ref[0] = tail
    buf_ref[0:SSD_HALO, :] = tail

    xs = xbc[:, :D_SSM]
    lane = lax.broadcasted_iota(I32, (L, LANES), 1)
    rowi = lax.broadcasted_iota(I32, (L, LANES), 0)
    sm = sm_ref[...]
    if nv < L:
        sm = jnp.concatenate([sm, jnp.zeros((L - nv, LANES), F32)], axis=0)
    dtr = sm + dtb_ref[...]
    dt = jnp.maximum(dtr, 0.0) + jnp.log(1.0 + jnp.exp(-jnp.abs(dtr)))
    dt = jnp.where((lane < H_SSM) & (rowi < nv), dt, 0.0)
    a = -jnp.exp(alog_ref[...])
    da = dt * a
    ri = lax.broadcasted_iota(I32, (L, L), 0)
    ci = lax.broadcasted_iota(I32, (L, L), 1)
    causal = ri >= ci
    tril = jnp.where(causal, 1.0, 0.0).astype(BF16)
    cum = _exact_dot(tril, da)
    eye = jnp.where(lax.broadcasted_iota(I32, (LANES, LANES), 0) == lax.broadcasted_iota(I32, (LANES, LANES), 1),
                    1.0, 0.0).astype(BF16)
    ch, cm, cl = _split3(cum)
    cum_t = _dot_nt(eye, ch) + _dot_nt(eye, cm) + _dot_nt(eye, cl)
    e = e_ref[...]
    ecx = _exact_dot_r(jnp.exp(cum), e)
    dtx = _exact_dot_r(dt, e)
    xdt = (xs * dtx).astype(BF16)
    edl = ecx[L - 1:L, :]
    dend_t = jnp.exp(cum_t[:, L - 1:L] - cum_t)
    lane_l = lax.broadcasted_iota(I32, (L, LANES), 1)
    lane_n = lax.broadcasted_iota(I32, (SSM_STATE, LANES), 1)
    hpg = H_SSM // SSM_GROUPS
    for g in range(SSM_GROUPS):
        bg = xbc[:, D_SSM + g * SSM_STATE:D_SSM + (g + 1) * SSM_STATE].astype(BF16)
        cg = xbc[:, D_SSM + (SSM_GROUPS + g) * SSM_STATE:D_SSM + (SSM_GROUPS + g + 1) * SSM_STATE].astype(BF16)
        cbt = _dot_nt(cg, bg)
        bg_t = _dot_nt(eye, bg)
        gcols = slice(g * hpg * SSM_HEAD_DIM, (g + 1) * hpg * SSM_HEAD_DIM)
        yoff = _dot(cg, ht_ref[:, gcols].astype(BF16)) * ecx[:, gcols]
        for p in range(hpg // 2):
            h0 = g * hpg + 2 * p
            pcols = slice(h0 * SSM_HEAD_DIM, (h0 + 2) * SSM_HEAD_DIM)
            xpair = xdt[:, pcols]
            res, st = [], []
            for hh in (h0, h0 + 1):
                seg = cum[:, hh:hh + 1] - cum_t[hh:hh + 1, :]
                dec = jnp.where(causal, jnp.exp(seg), 0.0)
                res.append(_dot((cbt * dec).astype(BF16), xpair))
                st.append(_dot((bg_t * dend_t[hh:hh + 1, :]).astype(BF16), xpair))
            yb_ref[:, pcols] = (jnp.where(lane_l < SSM_HEAD_DIM, res[0], res[1])
                                + yoff[:, 2 * p * SSM_HEAD_DIM:(2 * p + 2) * SSM_HEAD_DIM])
            ht_ref[:, pcols] = (ht_ref[:, pcols] * edl[:, pcols]
                                + jnp.where(lane_n < SSM_HEAD_DIM, st[0], st[1]))
    hout_ref[0] = ht_ref[...]
    y = yb_ref[...] + dx_ref[...] * xs
    z = z_ref[...]
    if nv < L:
        z = jnp.concatenate([z, jnp.zeros((L - nv, D_SSM), F32)], axis=0)
    y = _rms(y * _silu(z), ng_ref[...])
    y_ref[...] = y[:nv].astype(y_ref.dtype)


def _ssd(u, prevc, h0t, cw, cb, dtb, alog, dx, ng, emat, *, bsz, t, L, nv):
    nt = t // nv
    row = lambda bi, ti: bi * nt + ti
    c512 = lambda c: (lambda bi, ti: (row(bi, ti), c))
    const2 = lambda bi, ti: (0, 0)
    return pl.pallas_call(
        functools.partial(_ssd_kernel, L=L, nv=nv),
        out_shape=(jax.ShapeDtypeStruct((bsz * t, D_SSM), BF16),
                   jax.ShapeDtypeStruct((bsz, SSM_STATE, D_SSM), F32),
                   jax.ShapeDtypeStruct((bsz, SSD_HALO, D_XBC), F32)),
        grid=(bsz, nt),
        in_specs=[pl.BlockSpec((nv, D_SSM), lambda bi, ti: (row(bi, ti), 0)),
                  pl.BlockSpec((nv, 512), c512(COL_XBC // 512)),
                  pl.BlockSpec((nv, 512), c512(COL_XBC // 512 + 1)),
                  pl.BlockSpec((nv, 512), c512(COL_XBC // 512 + 2)),
                  pl.BlockSpec((nv, LANES), c512(COL_SMALL // LANES)),
                  pl.BlockSpec((1, SSD_HALO, D_XBC), lambda bi, ti: (bi, 0, 0)),
                  pl.BlockSpec((1, SSM_STATE, D_SSM), lambda bi, ti: (bi, 0, 0)),
                  pl.BlockSpec((SSM_CONV, D_XBC), const2),
                  pl.BlockSpec((1, D_XBC), const2),
                  pl.BlockSpec((1, LANES), const2),
                  pl.BlockSpec((1, LANES), const2),
                  pl.BlockSpec((1, D_SSM), const2),
                  pl.BlockSpec((1, D_SSM), const2),
                  pl.BlockSpec((LANES, D_SSM), const2)],
        out_specs=(pl.BlockSpec((nv, D_SSM), lambda bi, ti: (row(bi, ti), 0)),
                   pl.BlockSpec((1, SSM_STATE, D_SSM), lambda bi, ti: (bi, 0, 0)),
                   pl.BlockSpec((1, SSD_HALO, D_XBC), lambda bi, ti: (bi, 0, 0))),
        scratch_shapes=[pltpu.VMEM((SSD_HALO + L, D_XBC), F32),
                        pltpu.VMEM((SSM_STATE, D_SSM), F32),
                        pltpu.VMEM((L, D_SSM), F32)],
        compiler_params=pltpu.CompilerParams(dimension_semantics=("arbitrary", "arbitrary"),
                                             vmem_limit_bytes=VMEM_LIMIT),
        name="ssd_mixer",
    )(u, u, u, u, u, prevc, h0t, cw, cb, dtb, alog, dx, ng, emat)


def _mono_key(x):
    b = lax.bitcast_convert_type(x, I32)
    return b ^ ((b >> 31) & INT_MAX)


def _attn_body(*, tq, kt, nkt, topk, qf, qif, sm, kt_tile, v_tile, ki_tile, adm_fn, bias_fn,
               sc_ref, m_ref, l_ref, acc_ref, y_ref):
    nsl = kt // LANES
    lane = lax.broadcasted_iota(I32, (tq, LANES), 1)
    low = lane < HEAD_DIM
    qb = (qf * (HEAD_DIM ** -0.5)).astype(BF16)
    qib = qif.astype(BF16)
    wi = sm[:, SM_WI:SM_WI + H_IDX] * ((D_IDX ** -0.5) * (H_IDX ** -0.5))
    zero_b = jnp.zeros((tq, LANES), BF16)

    def head_window(x, h):
        win = x[:, (h // 2) * LANES:(h // 2 + 1) * LANES]
        return jnp.where(low if h % 2 == 0 else jnp.logical_not(low), win, zero_b)

    qim = [head_window(qib, h) for h in range(H_IDX)]
    wcol = [wi[:, h:h + 1] for h in range(H_IDX)]

    def p1(j, carry):
        ki = ki_tile(j)
        acc = jnp.zeros((tq, kt), F32)
        for h in range(H_IDX):
            acc = acc + jnp.maximum(_dot(qim[h], ki), 0.0) * wcol[h]
        for s in range(nsl):
            key = jnp.where(adm_fn(j, s), _mono_key(acc[:, s * LANES:(s + 1) * LANES]), INT_MIN)
            sc_ref[j, :, s * LANES:(s + 1) * LANES] = key
        return carry

    lax.fori_loop(0, nkt, p1, 0)

    def count(pred):
        def body(j, acc):
            tile = sc_ref[j]
            for s in range(nsl):
                acc = acc + jnp.where(pred(tile[:, s * LANES:(s + 1) * LANES], j, s), 1.0, 0.0)
            return acc
        acc = lax.fori_loop(0, nkt, body, jnp.zeros((tq, LANES), F32))
        return jnp.sum(acc, axis=1, keepdims=True)

    def count_ge(cand):
        cb = jnp.broadcast_to(cand, (tq, LANES))
        return count(lambda t, j, s: t >= cb)

    kf = float(topk)
    prefix = jnp.where(count_ge(jnp.zeros((tq, 1), I32)) >= kf, 0, INT_MIN).astype(I32)

    def bit_step(it, prefix):
        trial = prefix | lax.shift_left(jnp.int32(1), 30 - it)
        return jnp.where(count_ge(trial) >= kf, trial, prefix)

    thr = lax.fori_loop(0, 31, bit_step, prefix)
    thr = jnp.maximum(thr, INT_MIN + 1)
    thr_b = jnp.broadcast_to(thr, (tq, LANES))
    n_gt = count(lambda t, j, s: t > thr_b)
    n_eq = count(lambda t, j, s: t == thr_b)
    take = kf - n_gt

    def kpos(j, s):
        return j * kt + s * LANES + lane

    def tie_cut(_):
        def step(it, cut):
            trial = cut | lax.shift_left(jnp.int32(1), 30 - it)
            tb = jnp.broadcast_to(trial, (tq, LANES))
            c = count(lambda t, j, s: (t == thr_b) & (kpos(j, s) < tb))
            return jnp.where(c <= take, trial, cut)
        return lax.fori_loop(0, 31, step, jnp.zeros((tq, 1), I32))

    has_tie = jnp.max(jnp.where(n_gt + n_eq > kf, 1.0, 0.0)) > 0.0
    cut = lax.cond(has_tie, tie_cut, lambda _: jnp.full((tq, 1), INT_MAX, I32), 0)
    cut_b = jnp.broadcast_to(cut, (tq, LANES))

    qm = [head_window(qb, h) for h in range(H_ATT)]
    m_ref[...] = jnp.full(m_ref.shape, NEG_BIG, F32)
    l_ref[...] = jnp.zeros(l_ref.shape, F32)
    acc_ref[...] = jnp.zeros(acc_ref.shape, F32)

    def p3(j, carry):
        keyt = sc_ref[j]
        sel = []
        for s in range(nsl):
            ks = keyt[:, s * LANES:(s + 1) * LANES]
            sel.append((ks > thr_b) | ((ks == thr_b) & (kpos(j, s) < cut_b)))
        for p in range(H_ATT // 2):
            kp = kt_tile(j, p)
            vp = v_tile(j, p)
            for hsub in range(2):
                h = 2 * p + hsub
                s_all = _dot(qm[h], kp)
                parts = [jnp.where(sel[s], s_all[:, s * LANES:(s + 1) * LANES] + bias_fn(j, s, h), NEG_BIG)
                         for s in range(nsl)]
                mx = parts[0]
                for s in range(1, nsl):
                    mx = jnp.maximum(mx, parts[s])
                m_old = m_ref[h]
                m_new = jnp.maximum(m_old, jnp.max(mx, axis=1, keepdims=True))
                alpha = jnp.exp(m_old - m_new)
                pr = [jnp.exp(part - m_new) for part in parts]
                psum = pr[0]
                for s in range(1, nsl):
                    psum = psum + pr[s]
                l_ref[h] = alpha * l_ref[h] + psum
                pb = jnp.concatenate([x.astype(BF16) for x in pr], axis=1)
                acc_ref[h] = alpha * acc_ref[h] + _dot(pb, vp)
                m_ref[h] = m_new
        return carry

    lax.fori_loop(0, nkt, p3, 0)
    for p in range(H_ATT // 2):
        outs = []
        for hsub in range(2):
            h = 2 * p + hsub
            lsum = jnp.sum(l_ref[h], axis=1, keepdims=True)
            outs.append(acc_ref[h] / lsum)
        y_ref[:, p * LANES:(p + 1) * LANES] = jnp.where(low, outs[0], outs[1]).astype(y_ref.dtype)


N_BIAS_NEAR = 7


LOG2E = math.log2(math.e)
PV_ROWS = 256


def _attn_prompt_kernel(q_ref, qi_ref, sm_ref, k_ref, vt_ref, ki_ref, bt_ref, cf_ref, y_ref,
                        sc_ref, qt_ref, zb_ref, nm_ref, m_ref, l_ref, acc_ref, *, tq, kt, topk):
    i = pl.program_id(1)
    nql = tq // LANES
    nsl = kt // LANES
    nkt = lax.div((i + 1) * tq + (kt - 1), kt)

    def fold(x, op):
        x = x.reshape(x.shape[0] // SUBLANES, SUBLANES, tq)
        while x.shape[0] > 1:
            half = x.shape[0] // 2
            x = op(x[:half], x[half:])
        return x[0]

    eye = jnp.where(lax.broadcasted_iota(I32, (LANES, LANES), 0) == lax.broadcasted_iota(I32, (LANES, LANES), 1),
                    1.0, 0.0).astype(BF16)
    lane = lax.broadcasted_iota(I32, (tq, LANES), 1)
    low = lane < HEAD_DIM
    qb = (q_ref[...] * (HEAD_DIM ** -0.5 * LOG2E)).astype(BF16)
    qib = qi_ref[...].astype(BF16)
    zero_b = jnp.zeros((tq, LANES), BF16)
    for src, base in ((qib, 0), (qb, H_IDX)):
        for h in range(H_ATT):
            win = src[:, (h // 2) * LANES:(h // 2 + 1) * LANES]
            win = jnp.where(low if h % 2 == 0 else jnp.logical_not(low), win, zero_b)
            qt_ref[base + h] = _dot_nt(eye, win).astype(BF16)
    sh, smm, sl_ = _split3(sm_ref[...])
    sm_t = _dot_nt(eye, sh) + _dot_nt(eye, smm) + _dot_nt(eye, sl_)
    wrow = [sm_t[SM_WI + h:SM_WI + h + 1, :] * ((D_IDX ** -0.5) * (H_IDX ** -0.5)) for h in range(H_IDX)]

    qpos = i * tq + lax.broadcasted_iota(I32, (1, tq), 1)
    cend = (lax.shift_right_logical(qpos, int(math.log2(CHUNK))) + 1) * CHUNK
    krow = lax.broadcasted_iota(I32, (kt, tq), 0)

    def p1(j, carry):
        ki2 = ki_ref[0, j]
        for h in range(H_IDX):
            t = jnp.maximum(_dot(ki2, qt_ref[h]), 0.0) * wrow[h]
            if h == 0:
                zb_ref[0] = t
            elif h < H_IDX - 1:
                zb_ref[0] += t
            else:
                sc_ref[j] = jnp.where(j * kt + krow < cend, _mono_key(zb_ref[0] + t), INT_MIN)
        return carry

    lax.fori_loop(0, nkt, p1, 0)

    def count(pred):
        def body(j, acc):
            return acc + fold(jnp.where(pred(sc_ref[j], j), 1.0, 0.0), jnp.add)
        acc = lax.fori_loop(0, nkt, body, jnp.zeros((SUBLANES, tq), F32))
        return jnp.sum(acc, axis=0, keepdims=True)

    kf = float(topk)
    prefix = jnp.where(count(lambda t, j: t >= 0) >= kf, 0, INT_MIN).astype(I32)

    def bit_step(it, prefix):
        trial = prefix | lax.shift_left(jnp.int32(1), 30 - it)
        return jnp.where(count(lambda t, j: t >= trial) >= kf, trial, prefix)

    thr = jnp.maximum(lax.fori_loop(0, 31, bit_step, prefix), INT_MIN + 1)
    n_gt = count(lambda t, j: t > thr)
    n_eq = count(lambda t, j: t == thr)
    take = kf - n_gt

    def tie_cut(_):
        def step(it, cut):
            trial = cut | lax.shift_left(jnp.int32(1), 30 - it)
            c = count(lambda t, j: (t == thr) & (j * kt + krow < trial))
            return jnp.where(c <= take, trial, cut)
        return lax.fori_loop(0, 31, step, jnp.zeros((1, tq), I32))

    has_tie = jnp.max(jnp.where(n_gt + n_eq > kf, 1.0, 0.0)) > 0.0
    cut = lax.cond(has_tie, tie_cut, lambda _: jnp.full((1, tq), INT_MAX, I32), 0)

    m_ref[...] = jnp.full(m_ref.shape, NEG_BIG, F32)
    l_ref[...] = jnp.zeros(l_ref.shape, F32)
    acc_ref[...] = jnp.zeros(acc_ref.shape, F32)

    def tile_step(j, near):
        keyt = sc_ref[j]
        sel = (keyt > thr) | ((keyt == thr) & (j * kt + krow < cut))
        nm_ref[...] = jnp.where(sel, 0.0, NEG_BIG)

        def logits(h):
            mx = None
            for sl in range(nsl):
                rows = slice(sl * LANES, (sl + 1) * LANES)
                z = _dot(k_ref[0, j, rows, (h // 2) * LANES:(h // 2 + 1) * LANES], qt_ref[H_IDX + h]) + nm_ref[rows, :]
                if near:
                    z = z + jnp.concatenate(
                        [bt_ref[jnp.clip((i * nql + hf) - (j * nsl + sl), 0, N_BIAS_NEAR - 1), h]
                         for hf in range(nql)], axis=1)
                zb_ref[h % 2, rows, :] = z
                cm = fold(z, jnp.maximum)
                mx = cm if mx is None else jnp.maximum(mx, cm)
            return mx

        def accumulate(h, mx):
            shift = 0.0 if near else cf_ref[h]
            m_old = m_ref[h]
            m_new = jnp.maximum(m_old, jnp.max(mx, axis=0, keepdims=True) + shift)
            alpha = jnp.exp2(m_old - m_new)
            msub = m_new - shift
            lsum, pv = None, None
            for c in range(kt // PV_ROWS):
                rows = slice(c * PV_ROWS, (c + 1) * PV_ROWS)
                p = jnp.exp2(zb_ref[h % 2, rows, :] - msub)
                ls = fold(p, jnp.add)
                pc = _dot(vt_ref[0, j, (h // 2) * LANES:(h // 2 + 1) * LANES, rows], p.astype(BF16))
                lsum = ls if lsum is None else lsum + ls
                pv = pc if pv is None else pv + pc
            l_ref[h] = alpha * l_ref[h] + lsum
            acc_ref[h] = alpha * acc_ref[h] + pv
            m_ref[h] = m_new

        mx_next = logits(0)
        for h in range(H_ATT):
            mx_cur = mx_next
            if h + 1 < H_ATT:
                mx_next = logits(h + 1)
            accumulate(h, mx_cur)

    n_far = jnp.minimum(lax.div(jnp.maximum(nql * i - (N_BIAS_NEAR - 1) - (nsl - 1) + nsl, 0), nsl), nkt)

    def p3_far(j, carry):
        tile_step(j, False)
        return carry

    def p3_near(j, carry):
        tile_step(j, True)
        return carry

    lax.fori_loop(0, n_far, p3_far, 0)
    lax.fori_loop(n_far, nkt, p3_near, 0)

    eye_q = jnp.where(lax.broadcasted_iota(I32, (tq, tq), 0) == lax.broadcasted_iota(I32, (tq, tq), 1),
                      1.0, 0.0).astype(BF16)
    rowd = lax.broadcasted_iota(I32, (LANES, tq), 0)
    for p in range(H_ATT // 2):
        outs = []
        for hsub in range(2):
            h = 2 * p + hsub
            outs.append(acc_ref[h] / jnp.sum(l_ref[h], axis=0, keepdims=True))
        y_t = jnp.where(rowd < HEAD_DIM, outs[0], outs[1]).astype(BF16)
        y_ref[:, p * LANES:(p + 1) * LANES] = _dot_nt(eye_q, y_t).astype(y_ref.dtype)


def _attn_prompt(u, k_tiles, vt_tiles, ki_tiles, btab, cfar, *, bsz, t, tq, kt):
    assert t % kt == 0 and t % tq == 0 and tq % LANES == 0 and kt % LANES == 0
    nq = t // tq
    nk = t // kt
    row = lambda bi, qi: bi * nq + qi
    once = pl.Buffered(1)
    return pl.pallas_call(
        functools.partial(_attn_prompt_kernel, tq=tq, kt=kt, topk=min(TOPK, t // 4)),
        out_shape=jax.ShapeDtypeStruct((bsz * t, D_ATT), BF16),
        grid=(bsz, nq),
        in_specs=[pl.BlockSpec((tq, D_ATT), lambda bi, qi: (row(bi, qi), COL_Q // D_ATT)),
                  pl.BlockSpec((tq, D_ATT), lambda bi, qi: (row(bi, qi), COL_QI // D_ATT)),
                  pl.BlockSpec((tq, LANES), lambda bi, qi: (row(bi, qi), COL_SMALL // LANES)),
                  pl.BlockSpec((1, nk, kt, D_ATT), lambda bi, qi: (bi, 0, 0, 0), pipeline_mode=once),
                  pl.BlockSpec((1, nk, D_ATT, kt), lambda bi, qi: (bi, 0, 0, 0), pipeline_mode=once),
                  pl.BlockSpec((1, nk, kt, LANES), lambda bi, qi: (bi, 0, 0, 0), pipeline_mode=once),
                  pl.BlockSpec((N_BIAS_NEAR, H_ATT, LANES, LANES), lambda bi, qi: (0, 0, 0, 0), pipeline_mode=once),
                  pl.BlockSpec((H_ATT, 1, tq), lambda bi, qi: (0, 0, 0), pipeline_mode=once)],
        out_specs=pl.BlockSpec((tq, D_ATT), lambda bi, qi: (row(bi, qi), 0)),
        scratch_shapes=[pltpu.VMEM((nk, kt, tq), I32),
                        pltpu.VMEM((H_IDX + H_ATT, LANES, tq), BF16),
                        pltpu.VMEM((2, kt, tq), F32),
                        pltpu.VMEM((kt, tq), F32),
                        pltpu.VMEM((H_ATT, 1, tq), F32),
                        pltpu.VMEM((H_ATT, SUBLANES, tq), F32),
                        pltpu.VMEM((H_ATT, LANES, tq), F32)],
        compiler_params=pltpu.CompilerParams(dimension_semantics=("arbitrary", "arbitrary"),
                                             vmem_limit_bytes=VMEM_LIMIT),
        name="sparse_attn_prompt",
    )(u, u, u, k_tiles, vt_tiles, ki_tiles, btab, cfar)


def _attn_sample_kernel(q_ref, qi_ref, sm_ref, kt_ref, v_ref, ki_ref, bt_ref, y_ref, sc_ref, m_ref, l_ref, acc_ref,
                        *, tq, kt, n_keys, topk):
    lane = lax.broadcasted_iota(I32, (tq, LANES), 1)

    def adm_fn(j, s):
        return (j * kt + s * LANES + lane) < n_keys

    def bias_fn(j, s, h):
        return bt_ref[0, h, :, s * LANES:(s + 1) * LANES]

    _attn_body(tq=tq, kt=kt, nkt=1, topk=topk, qf=q_ref[...], qif=qi_ref[...], sm=sm_ref[...],
               kt_tile=lambda j, p: kt_ref[0, p * LANES:(p + 1) * LANES, :],
               v_tile=lambda j, p: v_ref[0, :, p * LANES:(p + 1) * LANES],
               ki_tile=lambda j: ki_ref[0],
               adm_fn=adm_fn, bias_fn=bias_fn,
               sc_ref=sc_ref, m_ref=m_ref, l_ref=l_ref, acc_ref=acc_ref, y_ref=y_ref)


def _attn_sample(u, kt_all, v_all, ki_all, btab, *, bsz, tq, kt, n_keys):
    return pl.pallas_call(
        functools.partial(_attn_sample_kernel, tq=tq, kt=kt, n_keys=n_keys, topk=min(TOPK, n_keys // 4)),
        out_shape=jax.ShapeDtypeStruct((bsz * tq, D_ATT), BF16),
        grid=(bsz,),
        in_specs=[pl.BlockSpec((tq, D_ATT), lambda bi: (bi, COL_Q // D_ATT)),
                  pl.BlockSpec((tq, D_ATT), lambda bi: (bi, COL_QI // D_ATT)),
                  pl.BlockSpec((tq, LANES), lambda bi: (bi, COL_SMALL // LANES)),
                  pl.BlockSpec((1, D_ATT, kt), lambda bi: (bi, 0, 0)),
                  pl.BlockSpec((1, kt, D_ATT), lambda bi: (bi, 0, 0)),
                  pl.BlockSpec((1, LANES, kt), lambda bi: (bi, 0, 0)),
                  pl.BlockSpec((1, H_ATT, tq, kt), lambda bi: (0, 0, 0, 0))],
        out_specs=pl.BlockSpec((tq, D_ATT), lambda bi: (bi, 0)),
        scratch_shapes=[pltpu.VMEM((1, tq, kt), I32),
                        pltpu.VMEM((H_ATT, tq, LANES), F32),
                        pltpu.VMEM((H_ATT, tq, LANES), F32),
                        pltpu.VMEM((H_ATT, tq, LANES), F32)],
        compiler_params=pltpu.CompilerParams(dimension_semantics=("arbitrary",),
                                             vmem_limit_bytes=VMEM_LIMIT),
        name="sparse_attn_sample",
    )(u, u, u, kt_all, v_all, ki_all, btab)


def _out_proj_kernel(x_ref, ys_ref, yc_ref, ya_ref, w_ref, g_ref, o_ref):
    acc = _dot(ys_ref[...], w_ref[0:D_SSM, :])
    acc = acc + _dot(yc_ref[...], w_ref[D_SSM:D_SSM + D_CONV, :])
    acc = acc + _dot(ya_ref[...], w_ref[D_SSM + D_CONV:D_MODEL, :])
    o_ref[...] = x_ref[...] + _rms(acc, g_ref[...])


def _out_proj(x, ys, yc, ya, w, g, *, tm):
    m = x.shape[0]
    return pl.pallas_call(
        _out_proj_kernel,
        out_shape=jax.ShapeDtypeStruct((m, D_MODEL), F32),
        grid=(m // tm,),
        in_specs=[pl.BlockSpec((tm, D_MODEL), lambda i: (i, 0)),
                  pl.BlockSpec((tm, D_SSM), lambda i: (i, 0)),
                  pl.BlockSpec((tm, D_CONV), lambda i: (i, 0)),
                  pl.BlockSpec((tm, D_ATT), lambda i: (i, 0)),
                  pl.BlockSpec((D_MODEL, D_MODEL), lambda i: (0, 0)),
                  pl.BlockSpec((1, D_MODEL), lambda i: (0, 0))],
        out_specs=pl.BlockSpec((tm, D_MODEL), lambda i: (i, 0)),
        compiler_params=pltpu.CompilerParams(dimension_semantics=("arbitrary",),
                                             vmem_limit_bytes=VMEM_LIMIT),
        name="out_proj",
    )(x, ys, yc, ya, w, g)


FFN_HALO = 8


def _ffn_kernel(x_ref, gpre_ref, wg_ref, wu_ref, wd_ref, cw_ref, cb_ref, gpost_ref, p1_ref, p2_ref,
                o_ref, aux_ref, h_ref, acc_ref, buf_ref, tail_ref, *, tm, tps, seq_len, chained):
    i = pl.program_id(0)
    j = pl.program_id(1)

    @pl.when(j == 0)
    def _():
        h_ref[...] = _rms(x_ref[...], gpre_ref[...]).astype(BF16)
        acc_ref[...] = jnp.zeros(acc_ref.shape, F32)

    h = h_ref[...]
    a_pre = _dot(h, wg_ref[...])
    buf_ref[FFN_HALO:FFN_HALO + tm, :] = a_pre
    if chained:
        seq_start = lax.rem(i, tps) == 0
        buf_ref[0:FFN_HALO, :] = jnp.where(seq_start, p1_ref[0], tail_ref[j])
        prev1 = buf_ref[FFN_HALO - 1:FFN_HALO - 1 + tm, :]
        prev2 = buf_ref[FFN_HALO - 2:FFN_HALO - 2 + tm, :]
        last = a_pre[tm - FFN_HALO:tm, :]
        tail_ref[j] = last
        aux_ref[0] = last
    else:
        buf_ref[0:FFN_HALO, :] = jnp.zeros((FFN_HALO, a_pre.shape[1]), F32)
        tpos = lax.rem(lax.broadcasted_iota(I32, a_pre.shape, 0), seq_len)
        prev1 = jnp.where(tpos >= 1, buf_ref[FFN_HALO - 1:FFN_HALO - 1 + tm, :], p1_ref[...])
        prev2 = jnp.where(tpos >= 2, buf_ref[FFN_HALO - 2:FFN_HALO - 2 + tm, :], p2_ref[...])
        aux_ref[...] = a_pre
    a = cw_ref[0:1, :] * prev2 + cw_ref[1:2, :] * prev1 + cw_ref[2:3, :] * a_pre + cb_ref[...]
    f = (_silu(a) * _dot(h, wu_ref[...])).astype(BF16)
    acc_ref[...] += _dot(f, wd_ref[...])

    @pl.when(j == pl.num_programs(1) - 1)
    def _():
        o_ref[...] = x_ref[...] + _rms(acc_ref[...], gpost_ref[...])


def _ffn(x, gpre, wg, wu, wd, cw, cb, gpost, p1, p2, *, tm, tf, seq_len, chained):
    m = x.shape[0]
    nf = D_FF // tf
    tps = max(seq_len // tm, 1)
    if chained:
        nseq = m // seq_len
        p_specs = [pl.BlockSpec((1, FFN_HALO, tf), lambda i, j: (i // tps, 0, j)),
                   pl.BlockSpec((1, FFN_HALO, tf), lambda i, j: (i // tps, 0, j))]
        aux_shape = jax.ShapeDtypeStruct((m // tm, FFN_HALO, D_FF), F32)
        aux_spec = pl.BlockSpec((1, FFN_HALO, tf), lambda i, j: (i, 0, j))
    else:
        p_specs = [pl.BlockSpec((tm, tf), lambda i, j: (i, j)), pl.BlockSpec((tm, tf), lambda i, j: (i, j))]
        aux_shape = jax.ShapeDtypeStruct((m, D_FF), F32)
        aux_spec = pl.BlockSpec((tm, tf), lambda i, j: (i, j))
    return pl.pallas_call(
        functools.partial(_ffn_kernel, tm=tm, tps=tps, seq_len=seq_len, chained=chained),
        out_shape=(jax.ShapeDtypeStruct((m, D_MODEL), F32), aux_shape),
        grid=(m // tm, nf),
        in_specs=[pl.BlockSpec((tm, D_MODEL), lambda i, j: (i, 0)),
                  pl.BlockSpec((1, D_MODEL), lambda i, j: (0, 0)),
                  pl.BlockSpec((D_MODEL, tf), lambda i, j: (0, j)),
                  pl.BlockSpec((D_MODEL, tf), lambda i, j: (0, j)),
                  pl.BlockSpec((tf, D_MODEL), lambda i, j: (j, 0)),
                  pl.BlockSpec((FFN_CONV, tf), lambda i, j: (0, j)),
                  pl.BlockSpec((1, tf), lambda i, j: (0, j)),
                  pl.BlockSpec((1, D_MODEL), lambda i, j: (0, 0))] + p_specs,
        out_specs=(pl.BlockSpec((tm, D_MODEL), lambda i, j: (i, 0)), aux_spec),
        scratch_shapes=[pltpu.VMEM((tm, D_MODEL), BF16),
                        pltpu.VMEM((tm, D_MODEL), F32),
                        pltpu.VMEM((FFN_HALO + tm, tf), F32),
                        pltpu.VMEM((nf, FFN_HALO, tf), F32)],
        compiler_params=pltpu.CompilerParams(dimension_semantics=("arbitrary", "arbitrary"),
                                             vmem_limit_bytes=VMEM_LIMIT),
        name="conv_ffn",
    )(x, gpre, wg, wu, wd, cw, cb, gpost, p1, p2)


def _prep_layer_weights(w):
    w_in = w["w_in"]
    o_dt = D_SSM + D_XBC
    o_glu = o_dt + H_SSM
    o_ki = o_glu + 2 * D_CONV + 4 * D_ATT
    o_wi = o_ki + D_IDX
    pad = jnp.zeros((D_MODEL, LANES - H_SSM - H_IDX - D_IDX), w_in.dtype)
    w_r = jnp.concatenate([w_in[:, :o_dt], w_in[:, o_glu:o_ki], w_in[:, o_dt:o_glu], w_in[:, o_wi:o_wi + H_IDX],
                           pad, w_in[:, o_ki:o_wi]], axis=1).astype(BF16)
    assert w_r.shape[1] == D_PROJ
    row = lambda v: v.reshape(1, -1).astype(F32)
    padl = lambda v: jnp.pad(v.astype(F32), (0, LANES - v.shape[0])).reshape(1, LANES)
    return dict(
        w_in=w_r, g_mix_pre=row(w["g_mix_pre"]),
        ssm_conv_w=w["ssm_conv_w"].astype(F32), ssm_conv_b=row(w["ssm_conv_b"]),
        dt_bias=padl(w["ssm_dt_bias"]), a_log=padl(w["ssm_a_log"]),
        d_x=row(jnp.repeat(w["ssm_d"], SSM_HEAD_DIM)), ssm_norm_g=row(w["ssm_norm_g"]),
        cconv_w=w["cconv_w"].astype(F32), cconv_b=row(w["cconv_b"]),
        cconv_ln_g=row(w["cconv_ln_g"]), cconv_ln_b=row(w["cconv_ln_b"]),
        w_out=w["w_out"].astype(BF16), g_mix_post=row(w["g_mix_post"]), g_ffn_pre=row(w["g_ffn_pre"]),
        ffn_w_gate=w["ffn_w_gate"].astype(BF16), ffn_w_up=w["ffn_w_up"].astype(BF16),
        ffn_w_down=w["ffn_w_down"].astype(BF16), ffn_conv_w=w["ffn_conv_w"].astype(F32),
        ffn_conv_b=row(w["ffn_conv_b"]), g_ffn_post=row(w["g_ffn_post"]))


def _expand_matrix():
    e = np.zeros((LANES, D_SSM), np.float32)
    for h in range(H_SSM):
        e[h, h * SSM_HEAD_DIM:(h + 1) * SSM_HEAD_DIM] = 1.0
    return jnp.asarray(e, BF16)


def _front_pad(state, halo):
    return jnp.pad(state.astype(F32), ((0, 0), (halo - state.shape[1], 0), (0, 0)))


def _state_t(h):
    b = h.shape[0]
    return jnp.transpose(h.astype(F32), (0, 3, 1, 2)).reshape(b, SSM_STATE, D_SSM)


def _state_from_t(ht):
    b = ht.shape[0]
    return jnp.transpose(ht.reshape(b, SSM_STATE, H_SSM, SSM_HEAD_DIM), (0, 2, 3, 1))


def _mixer_common(x, lw, emat, ssm_conv_prev, ssm_h0, cconv_prev, *, bsz, t, tm_proj, tn_proj, ssd_l, ssd_nv, cc_tm):
    u = _rms_matmul(x, lw["g_mix_pre"], lw["w_in"], tm=tm_proj, tn=tn_proj)
    y_ssm, ht, ctail = _ssd(u, _front_pad(ssm_conv_prev, SSD_HALO), _state_t(ssm_h0), lw["ssm_conv_w"],
                            lw["ssm_conv_b"], lw["dt_bias"], lw["a_log"], lw["d_x"], lw["ssm_norm_g"], emat,
                            bsz=bsz, t=t, L=ssd_l, nv=ssd_nv)
    y_conv, cctail = _cconv(u, _front_pad(cconv_prev, CC_HALO), lw["cconv_w"], lw["cconv_b"], lw["cconv_ln_g"],
                            lw["cconv_ln_b"], bsz=bsz, t=t, tm=cc_tm)
    k = u[:, COL_K:COL_K + D_ATT]
    v = u[:, COL_V:COL_V + D_ATT]
    ki = u[:, COL_SMALL + SM_KI:COL_SMALL + SM_KI + D_IDX]
    states = dict(k=k.reshape(bsz, t, H_ATT, HEAD_DIM), v=v.reshape(bsz, t, H_ATT, HEAD_DIM),
                  ki=ki.reshape(bsz, t, D_IDX), h=_state_from_t(ht),
                  ssm_conv=ctail[:, SSD_HALO - (SSM_CONV - 1):], cconv=cctail[:, CC_HALO - (CONV_WIDTH - 1):])
    return u, y_ssm, y_conv, k, v, ki, states


def _layer_prompt(x, lw, emat, btab, *, bsz, t, cfg):
    zeros = lambda *s: jnp.zeros(s, F32)
    u, y_ssm, y_conv, k, v, ki, st = _mixer_common(
        x, lw, emat, zeros(bsz, SSM_CONV - 1, D_XBC), zeros(bsz, H_SSM, SSM_HEAD_DIM, SSM_STATE),
        zeros(bsz, CONV_WIDTH - 1, D_CONV), bsz=bsz, t=t, tm_proj=cfg["tm_proj"], tn_proj=cfg["tn_proj"],
        ssd_l=cfg["ssd_l"], ssd_nv=cfg["ssd_l"], cc_tm=cfg["cc_tm"])
    kt = cfg["kt"]
    nk = t // kt
    k_tiles = k.astype(BF16).reshape(bsz, nk, kt, D_ATT)
    vt_tiles = jnp.transpose(v.astype(BF16).reshape(bsz, nk, kt, D_ATT), (0, 1, 3, 2))
    kib = ki.astype(BF16).reshape(bsz, nk, kt, D_IDX)
    ki_tiles = jnp.concatenate([kib, kib], axis=3)
    tq = cfg["tq"]
    cfar = jnp.tile(btab[N_BIAS_NEAR - 1, :, 0:1, :], (1, 1, tq // LANES))
    y_att = _attn_prompt(u, k_tiles, vt_tiles, ki_tiles, btab, cfar, bsz=bsz, t=t, tq=tq, kt=kt)
    x1 = _out_proj(x, y_ssm, y_conv, y_att, lw["w_out"], lw["g_mix_post"], tm=cfg["tm_out"])
    prev = zeros(bsz, FFN_HALO, D_FF)
    x2, ftail = _ffn(x1, lw["g_ffn_pre"], lw["ffn_w_gate"], lw["ffn_w_up"], lw["ffn_w_down"], lw["ffn_conv_w"],
                     lw["ffn_conv_b"], lw["g_ffn_post"], prev, prev, tm=cfg["tm_ffn"], tf=cfg["tf"], seq_len=t,
                     chained=True)
    ftail = ftail.reshape(bsz, t // cfg["tm_ffn"], FFN_HALO, D_FF)[:, -1]
    st["ffn_conv"] = ftail[:, FFN_HALO - (FFN_CONV - 1):]
    return x2, st


def _layer_sample(x, lw, emat, btab, past_k, past_v, past_ki, ssm_conv_prev, ssm_h0, cconv_prev, fconv_prev,
                  *, bsz, t, cfg):
    m = bsz * t
    u, y_ssm, y_conv, k, v, ki, st = _mixer_common(
        x, lw, emat, ssm_conv_prev, ssm_h0, cconv_prev, bsz=bsz, t=t, tm_proj=m, tn_proj=cfg["tn_proj"],
        ssd_l=LANES, ssd_nv=t, cc_tm=t)
    past = past_k.shape[1]
    n_keys = past + t
    ktp = cfg["kt_sample"]
    padk = lambda a: jnp.pad(a, ((0, 0), (0, ktp - n_keys), (0, 0)))
    k_all = padk(jnp.concatenate([past_k.reshape(bsz, past, D_ATT), k.reshape(bsz, t, D_ATT)], axis=1).astype(BF16))
    v_all = padk(jnp.concatenate([past_v.reshape(bsz, past, D_ATT), v.reshape(bsz, t, D_ATT)], axis=1).astype(BF16))
    ki_all = padk(jnp.concatenate([past_ki, ki.reshape(bsz, t, D_IDX)], axis=1).astype(BF16))
    kt_all = jnp.transpose(k_all, (0, 2, 1))
    kit = jnp.transpose(ki_all, (0, 2, 1))
    y_att = _attn_sample(u, kt_all, v_all, jnp.concatenate([kit, kit], axis=1), btab, bsz=bsz, tq=t, kt=ktp,
                         n_keys=n_keys)
    x1 = _out_proj(x, y_ssm, y_conv, y_att, lw["w_out"], lw["g_mix_post"], tm=m)
    fprev = fconv_prev.astype(F32)
    zrow = jnp.zeros((bsz, t - 1, D_FF), F32)
    p1 = jnp.concatenate([fprev[:, 1:2], zrow], axis=1).reshape(m, D_FF)
    p2 = jnp.concatenate([fprev, zrow[:, 1:]], axis=1).reshape(m, D_FF)
    x2, a_pre = _ffn(x1, lw["g_ffn_pre"], lw["ffn_w_gate"], lw["ffn_w_up"], lw["ffn_w_down"], lw["ffn_conv_w"],
                     lw["ffn_conv_b"], lw["g_ffn_post"], p1, p2, tm=m, tf=cfg["tf"], seq_len=t, chained=False)
    st["ffn_conv"] = a_pre.reshape(bsz, t, D_FF)[:, t - (FFN_CONV - 1):]
    return x2, st


_STATE_ORDER = ("k", "v", "ki", "h", "ssm_conv", "cconv", "ffn_conv")


def _prompt_cfg(t):
    big = t >= 4096
    return dict(tm_proj=1024 if big else 256, tn_proj=640, ssd_l=256, cc_tm=256, kt=512, tq=256,
                tm_out=512 if big else 256, tm_ffn=512 if big else 256, tf=512)


def _forward(x_prompt, x_sample, cache_k, cache_v, cache_kidx, state_ssm, state_ssm_conv, state_cconv,
             state_ffn_conv, rel_bias, weights):
    bp, tp, _ = x_prompt.shape
    bs, ts, _ = x_sample.shape
    depth = weights["w_in"].shape[0]
    past = cache_k.shape[2]
    emat = _expand_matrix()
    cfg_p = _prompt_cfg(tp)
    kt_sample = -(-(past + ts) // LANES) * LANES
    cfg_s = dict(tn_proj=640, tf=512, kt_sample=kt_sample)
    rb = rel_bias.astype(F32)
    btab_p = _bias_table(rb, nd=N_BIAS_NEAR, rows=LANES, cols=LANES, off0=0, step=LANES, key_axis=0, scale=LOG2E)
    btab_s = _bias_table(rb, nd=1, rows=ts, cols=kt_sample, off0=-past, step=0)
    xp = x_prompt.reshape(bp * tp, D_MODEL)
    xs = x_sample.reshape(bs * ts, D_MODEL)
    p_states = {n: [] for n in _STATE_ORDER}
    s_states = {n: [] for n in _STATE_ORDER}
    for l in range(depth):
        lw = _prep_layer_weights({n: w[l] for n, w in weights.items()})
        xp, st_p = _layer_prompt(xp, lw, emat, btab_p, bsz=bp, t=tp, cfg=cfg_p)
        xs, st_s = _layer_sample(xs, lw, emat, btab_s, cache_k[l], cache_v[l], cache_kidx[l], state_ssm_conv[l],
                                 state_ssm[l], state_cconv[l], state_ffn_conv[l], bsz=bs, t=ts, cfg=cfg_s)
        for n in _STATE_ORDER:
            p_states[n].append(st_p[n])
            s_states[n].append(st_s[n])
    outs = [xp.reshape(bp, tp, D_MODEL), xs.reshape(bs, ts, D_MODEL)]
    outs += [jnp.stack(p_states[n]) for n in _STATE_ORDER]
    outs += [jnp.stack(s_states[n]) for n in _STATE_ORDER]
    return tuple(outs)


def kernel(x_prompt, x_sample, cache_k, cache_v, cache_kidx, state_ssm, state_ssm_conv, state_cconv, state_ffn_conv, rel_bias, g_mix_pre, w_in, ssm_conv_w, ssm_conv_b, ssm_dt_bias, ssm_a_log, ssm_d, ssm_norm_g, cconv_w, cconv_b, cconv_ln_g, cconv_ln_b, w_out, g_mix_post, g_ffn_pre, ffn_w_gate, ffn_w_up, ffn_conv_w, ffn_conv_b, ffn_w_down, g_ffn_post):
    weights = dict(g_mix_pre=g_mix_pre, w_in=w_in, ssm_conv_w=ssm_conv_w, ssm_conv_b=ssm_conv_b,
                   ssm_dt_bias=ssm_dt_bias, ssm_a_log=ssm_a_log, ssm_d=ssm_d, ssm_norm_g=ssm_norm_g,
                   cconv_w=cconv_w, cconv_b=cconv_b, cconv_ln_g=cconv_ln_g, cconv_ln_b=cconv_ln_b, w_out=w_out,
                   g_mix_post=g_mix_post, g_ffn_pre=g_ffn_pre, ffn_w_gate=ffn_w_gate, ffn_w_up=ffn_w_up,
                   ffn_conv_w=ffn_conv_w, ffn_conv_b=ffn_conv_b, ffn_w_down=ffn_w_down, g_ffn_post=g_ffn_post)
    return _forward(x_prompt, x_sample, cache_k, cache_v, cache_kidx, state_ssm, state_ssm_conv, state_cconv,
                    state_ffn_conv, rel_bias, weights)
```

```python
import functools
import math

import numpy as np
import jax
import jax.numpy as jnp
from jax import lax
from jax.experimental import pallas as pl
from jax.experimental.pallas import tpu as pltpu

F32 = jnp.float32
BF16 = jnp.bfloat16
I32 = jnp.int32

D_MODEL = 2048
D_SSM = 1024
SSM_HEAD_DIM = 64
H_SSM = 16
SSM_GROUPS = 2
SSM_STATE = 128
SSM_CONV = 4
D_XBC = D_SSM + 2 * SSM_GROUPS * SSM_STATE
D_CONV = 512
CONV_WIDTH = 31
D_ATT = 512
HEAD_DIM = 64
H_ATT = 8
H_IDX = 8
D_IDX = 64
TOPK = 256
CHUNK = 64
N_BUCKETS = 32
REL_MAX_DIST = 1024
D_FF = 5632
FFN_CONV = 3
EPS = 1e-6

LANES = 128
SUBLANES = 8

COL_Z, COL_XBC, COL_GLU, COL_Q, COL_K, COL_V, COL_QI, COL_SMALL = 0, 1024, 2560, 3584, 4096, 4608, 5120, 5632
D_PROJ = COL_SMALL + LANES
SM_DT, SM_WI, SM_KI = 0, 16, 64

INT_MIN = -(2 ** 31)
INT_MAX = 2 ** 31 - 1
NEG_BIG = -1e30
VMEM_LIMIT = 56 * 1024 * 1024


def _bucket_thresholds():
    nb = N_BUCKETS // 2
    max_exact = nb // 2
    n = np.arange(0, 4 * REL_MAX_DIST, dtype=np.int64)
    nf = np.maximum(n, 1).astype(np.float32)
    large = max_exact + (np.log(nf / np.float32(max_exact)) / np.float32(math.log(REL_MAX_DIST / max_exact))
                         * np.float32(nb - max_exact)).astype(np.int32)
    large = np.minimum(large, nb - 1)
    bucket = np.where(n < max_exact, n, large)
    steps = np.nonzero(np.diff(bucket))[0] + 1
    assert np.all(np.diff(bucket) >= 0) and np.all(np.diff(bucket) <= 1) and bucket[-1] == nb - 1
    return tuple(int(s) for s in steps)


BUCKET_STEPS = _bucket_thresholds()


def _sigmoid(x):
    return 1.0 / (1.0 + jnp.exp(-x))


def _silu(x):
    return x * _sigmoid(x)


def _split3(x):
    hi = x.astype(BF16)
    r1 = x - hi.astype(F32)
    mid = r1.astype(BF16)
    lo = (r1 - mid.astype(F32)).astype(BF16)
    return hi, mid, lo


def _dot(a, b):
    return jnp.dot(a, b, preferred_element_type=F32)


def _dot_nt(a, b):
    return lax.dot_general(a, b, (((1,), (1,)), ((), ())), preferred_element_type=F32)


def _exact_dot(sel_bf16, x_f32):
    hi, mid, lo = _split3(x_f32)
    return _dot(sel_bf16, hi) + _dot(sel_bf16, mid) + _dot(sel_bf16, lo)


def _exact_dot_r(x_f32, sel_bf16):
    hi, mid, lo = _split3(x_f32)
    return _dot(hi, sel_bf16) + _dot(mid, sel_bf16) + _dot(lo, sel_bf16)


def _rms(x, g):
    ms = jnp.mean(x * x, axis=-1, keepdims=True)
    return x * lax.rsqrt(ms + EPS) * g


def _rms_matmul_kernel(x_ref, g_ref, w_ref, o_ref, h_ref):
    @pl.when(pl.program_id(1) == 0)
    def _():
        h_ref[...] = _rms(x_ref[...], g_ref[...]).astype(BF16)

    o_ref[...] = _dot(h_ref[...], w_ref[...])


def _rms_matmul(x, g, w, *, tm, tn):
    m, d = x.shape
    n = w.shape[1]
    assert m % tm == 0 and n % tn == 0
    return pl.pallas_call(
        _rms_matmul_kernel,
        out_shape=jax.ShapeDtypeStruct((m, n), F32),
        grid=(m // tm, n // tn),
        in_specs=[pl.BlockSpec((tm, d), lambda i, j: (i, 0)),
                  pl.BlockSpec((1, d), lambda i, j: (0, 0)),
                  pl.BlockSpec((d, tn), lambda i, j: (0, j))],
        out_specs=pl.BlockSpec((tm, tn), lambda i, j: (i, j)),
        scratch_shapes=[pltpu.VMEM((tm, d), BF16)],
        compiler_params=pltpu.CompilerParams(dimension_semantics=("arbitrary", "arbitrary"),
                                             vmem_limit_bytes=VMEM_LIMIT),
        name="rms_in_proj",
    )(x, g, w)


def _bias_table_kernel(rb_ref, o_ref, *, off0, step, key_axis, scale):
    d = pl.program_id(0)
    h = pl.program_id(1)
    rows, cols = o_ref.shape[2], o_ref.shape[3]
    rel = (lax.broadcasted_iota(I32, (rows, cols), key_axis) - lax.broadcasted_iota(I32, (rows, cols), 1 - key_axis)
           + (off0 - d * step))
    n = jnp.abs(rel)
    bucket = jnp.where(rel > 0, N_BUCKETS // 2, 0)
    for s in BUCKET_STEPS:
        bucket = bucket + jnp.where(n >= s, 1, 0)
    acc = jnp.zeros((rows, cols), F32)
    for b in range(N_BUCKETS):
        acc = jnp.where(bucket == b, rb_ref[b, h], acc)
    o_ref[0, 0] = acc * scale


def _bias_table(rel_bias, *, nd, rows, cols, off0, step, key_axis=1, scale=1.0):
    return pl.pallas_call(
        functools.partial(_bias_table_kernel, off0=off0, step=step, key_axis=key_axis, scale=scale),
        out_shape=jax.ShapeDtypeStruct((nd, H_ATT, rows, cols), F32),
        grid=(nd, H_ATT),
        in_specs=[pl.BlockSpec(memory_space=pltpu.SMEM)],
        out_specs=pl.BlockSpec((1, 1, rows, cols), lambda d, h: (d, h, 0, 0)),
        name="bias_table",
    )(rel_bias)


CC_HALO = 32


def _cconv_kernel(val_ref, gate_ref, prev_ref, w_ref, b_ref, lg_ref, lb_ref, y_ref, tail_ref, buf_ref, *, tm):
    @pl.when(pl.program_id(1) == 0)
    def _():
        buf_ref[0:CC_HALO, :] = prev_ref[0]

    buf_ref[CC_HALO:CC_HALO + tm, :] = val_ref[...] * _sigmoid(gate_ref[...])
    first = CC_HALO - (CONV_WIDTH - 1)
    acc = jnp.zeros((tm, D_CONV), F32) + b_ref[...]
    for k in range(CONV_WIDTH):
        acc = acc + w_ref[k:k + 1, :] * buf_ref[first + k:first + k + tm, :]
    mu = jnp.mean(acc, axis=-1, keepdims=True)
    xc = acc - mu
    var = jnp.mean(xc * xc, axis=-1, keepdims=True)
    y = xc * lax.rsqrt(var + EPS) * lg_ref[...] + lb_ref[...]
    y_ref[...] = _silu(y).astype(y_ref.dtype)
    tail = buf_ref[tm:tm + CC_HALO, :]
    tail_ref[0] = tail
    buf_ref[0:CC_HALO, :] = tail


def _cconv(u, prev, w, b, lg, lb, *, bsz, t, tm):
    nt = t // tm
    row = lambda bi, ti: bi * nt + ti
    return pl.pallas_call(
        functools.partial(_cconv_kernel, tm=tm),
        out_shape=(jax.ShapeDtypeStruct((bsz * t, D_CONV), BF16),
                   jax.ShapeDtypeStruct((bsz, CC_HALO, D_CONV), F32)),
        grid=(bsz, nt),
        in_specs=[pl.BlockSpec((tm, D_CONV), lambda bi, ti: (row(bi, ti), COL_GLU // D_CONV)),
                  pl.BlockSpec((tm, D_CONV), lambda bi, ti: (row(bi, ti), COL_GLU // D_CONV + 1)),
                  pl.BlockSpec((1, CC_HALO, D_CONV), lambda bi, ti: (bi, 0, 0)),
                  pl.BlockSpec((CONV_WIDTH, D_CONV), lambda bi, ti: (0, 0)),
                  pl.BlockSpec((1, D_CONV), lambda bi, ti: (0, 0)),
                  pl.BlockSpec((1, D_CONV), lambda bi, ti: (0, 0)),
                  pl.BlockSpec((1, D_CONV), lambda bi, ti: (0, 0))],
        out_specs=(pl.BlockSpec((tm, D_CONV), lambda bi, ti: (row(bi, ti), 0)),
                   pl.BlockSpec((1, CC_HALO, D_CONV), lambda bi, ti: (bi, 0, 0))),
        scratch_shapes=[pltpu.VMEM((CC_HALO + tm, D_CONV), F32)],
        compiler_params=pltpu.CompilerParams(dimension_semantics=("arbitrary", "arbitrary"),
                                             vmem_limit_bytes=VMEM_LIMIT),
        name="conformer_conv",
    )(u, u, prev, w, b, lg, lb)


SSD_HALO = 8


def _ssd_kernel(z_ref, x0_ref, x1_ref, x2_ref, sm_ref, prevc_ref, h0_ref, cw_ref, cb_ref, dtb_ref, alog_ref,
                dx_ref, ng_ref, e_ref, y_ref, hout_ref, ctail_ref, buf_ref, ht_ref, yb_ref, *, L, nv):
    @pl.when(pl.program_id(1) == 0)
    def _():
        buf_ref[0:SSD_HALO, :] = prevc_ref[0]
        ht_ref[...] = h0_ref[0]

    if nv < L:
        buf_ref[SSD_HALO + nv:SSD_HALO + L, :] = jnp.zeros((L - nv, D_XBC), F32)
    for c, r in enumerate((x0_ref, x1_ref, x2_ref)):
        buf_ref[SSD_HALO:SSD_HALO + nv, c * 512:(c + 1) * 512] = r[...]
    first = SSD_HALO - (SSM_CONV - 1)
    acc = jnp.zeros((L, D_XBC), F32) + cb_ref[...]
    for k in range(SSM_CONV):
        acc = acc + cw_ref[k:k + 1, :] * buf_ref[first + k:first + k + L, :]
    xbc = _silu(acc)
    tail = buf_ref[nv:nv + SSD_HALO, :]
    ctail_ref[0] = tail
    buf_ref[0:SSD_HALO, :] = tail

    xs = xbc[:, :D_SSM]
    lane = lax.broadcasted_iota(I32, (L, LANES), 1)
    rowi = lax.broadcasted_iota(I32, (L, LANES), 0)
    sm = sm_ref[...]
    if nv < L:
        sm = jnp.concatenate([sm, jnp.zeros((L - nv, LANES), F32)], axis=0)
    dtr = sm + dtb_ref[...]
    dt = jnp.maximum(dtr, 0.0) + jnp.log(1.0 + jnp.exp(-jnp.abs(dtr)))
    dt = jnp.where((lane < H_SSM) & (rowi < nv), dt, 0.0)
    a = -jnp.exp(alog_ref[...])
    da = dt * a
    ri = lax.broadcasted_iota(I32, (L, L), 0)
    ci = lax.broadcasted_iota(I32, (L, L), 1)
    causal = ri >= ci
    tril = jnp.where(causal, 1.0, 0.0).astype(BF16)
    cum = _exact_dot(tril, da)
    eye = jnp.where(lax.broadcasted_iota(I32, (LANES, LANES), 0) == lax.broadcasted_iota(I32, (LANES, LANES), 1),
                    1.0, 0.0).astype(BF16)
    ch, cm, cl = _split3(cum)
    cum_t = _dot_nt(eye, ch) + _dot_nt(eye, cm) + _dot_nt(eye, cl)
    e = e_ref[...]
    ecx = _exact_dot_r(jnp.exp(cum), e)
    dtx = _exact_dot_r(dt, e)
    xdt = (xs * dtx).astype(BF16)
    edl = ecx[L - 1:L, :]
    dend_t = jnp.exp(cum_t[:, L - 1:L] - cum_t)
    lane_l = lax.broadcasted_iota(I32, (L, LANES), 1)
    lane_n = lax.broadcasted_iota(I32, (SSM_STATE, LANES), 1)
    hpg = H_SSM // SSM_GROUPS
    for g in range(SSM_GROUPS):
        bg = xbc[:, D_SSM + g * SSM_STATE:D_SSM + (g + 1) * SSM_STATE].astype(BF16)
        cg = xbc[:, D_SSM + (SSM_GROUPS + g) * SSM_STATE:D_SSM + (SSM_GROUPS + g + 1) * SSM_STATE].astype(BF16)
        cbt = _dot_nt(cg, bg)
        bg_t = _dot_nt(eye, bg)
        gcols = slice(g * hpg * SSM_HEAD_DIM, (g + 1) * hpg * SSM_HEAD_DIM)
        yoff = _dot(cg, ht_ref[:, gcols].astype(BF16)) * ecx[:, gcols]
        for p in range(hpg // 2):
            h0 = g * hpg + 2 * p
            pcols = slice(h0 * SSM_HEAD_DIM, (h0 + 2) * SSM_HEAD_DIM)
            xpair = xdt[:, pcols]
            res, st = [], []
            for hh in (h0, h0 + 1):
                seg = cum[:, hh:hh + 1] - cum_t[hh:hh + 1, :]
                dec = jnp.where(causal, jnp.exp(seg), 0.0)
                res.append(_dot((cbt * dec).astype(BF16), xpair))
                st.append(_dot((bg_t * dend_t[hh:hh + 1, :]).astype(BF16), xpair))
            yb_ref[:, pcols] = (jnp.where(lane_l < SSM_HEAD_DIM, res[0], res[1])
                                + yoff[:, 2 * p * SSM_HEAD_DIM:(2 * p + 2) * SSM_HEAD_DIM])
            ht_ref[:, pcols] = (ht_ref[:, pcols] * edl[:, pcols]
                                + jnp.where(lane_n < SSM_HEAD_DIM, st[0], st[1]))
    hout_ref[0] = ht_ref[...]
    y = yb_ref[...] + dx_ref[...] * xs
    z = z_ref[...]
    if nv < L:
        z = jnp.concatenate([z, jnp.zeros((L - nv, D_SSM), F32)], axis=0)
    y = _rms(y * _silu(z), ng_ref[...])
    y_ref[...] = y[:nv].astype(y_ref.dtype)


def _ssd(u, prevc, h0t, cw, cb, dtb, alog, dx, ng, emat, *, bsz, t, L, nv):
    nt = t // nv
    row = lambda bi, ti: bi * nt + ti
    c512 = lambda c: (lambda bi, ti: (row(bi, ti), c))
    const2 = lambda bi, ti: (0, 0)
    return pl.pallas_call(
        functools.partial(_ssd_kernel, L=L, nv=nv),
        out_shape=(jax.ShapeDtypeStruct((bsz * t, D_SSM), BF16),
                   jax.ShapeDtypeStruct((bsz, SSM_STATE, D_SSM), F32),
                   jax.ShapeDtypeStruct((bsz, SSD_HALO, D_XBC), F32)),
        grid=(bsz, nt),
        in_specs=[pl.BlockSpec((nv, D_SSM), lambda bi, ti: (row(bi, ti), 0)),
                  pl.BlockSpec((nv, 512), c512(COL_XBC // 512)),
                  pl.BlockSpec((nv, 512), c512(COL_XBC // 512 + 1)),
                  pl.BlockSpec((nv, 512), c512(COL_XBC // 512 + 2)),
                  pl.BlockSpec((nv, LANES), c512(COL_SMALL // LANES)),
                  pl.BlockSpec((1, SSD_HALO, D_XBC), lambda bi, ti: (bi, 0, 0)),
                  pl.BlockSpec((1, SSM_STATE, D_SSM), lambda bi, ti: (bi, 0, 0)),
                  pl.BlockSpec((SSM_CONV, D_XBC), const2),
                  pl.BlockSpec((1, D_XBC), const2),
                  pl.BlockSpec((1, LANES), const2),
                  pl.BlockSpec((1, LANES), const2),
                  pl.BlockSpec((1, D_SSM), const2),
                  pl.BlockSpec((1, D_SSM), const2),
                  pl.BlockSpec((LANES, D_SSM), const2)],
        out_specs=(pl.BlockSpec((nv, D_SSM), lambda bi, ti: (row(bi, ti), 0)),
                   pl.BlockSpec((1, SSM_STATE, D_SSM), lambda bi, ti: (bi, 0, 0)),
                   pl.BlockSpec((1, SSD_HALO, D_XBC), lambda bi, ti: (bi, 0, 0))),
        scratch_shapes=[pltpu.VMEM((SSD_HALO + L, D_XBC), F32),
                        pltpu.VMEM((SSM_STATE, D_SSM), F32),
                        pltpu.VMEM((L, D_SSM), F32)],
        compiler_params=pltpu.CompilerParams(dimension_semantics=("arbitrary", "arbitrary"),
                                             vmem_limit_bytes=VMEM_LIMIT),
        name="ssd_mixer",
    )(u, u, u, u, u, prevc, h0t, cw, cb, dtb, alog, dx, ng, emat)


def _flip_negative(b):
    return b ^ ((b >> 31) & INT_MAX)


def _mono_key(x):
    return _flip_negative(lax.bitcast_convert_type(x, I32))


def _key_value(k):
    return lax.bitcast_convert_type(_flip_negative(k), F32)


F32_BIG = 3e38
SEARCH_MAX_STEPS = 80


def _attn_body(*, tq, kt, nkt, topk, qf, qif, sm, kt_tile, v_tile, ki_tile, adm_fn, bias_fn,
               sc_ref, m_ref, l_ref, acc_ref, y_ref):
    nsl = kt // LANES
    lane = lax.broadcasted_iota(I32, (tq, LANES), 1)
    low = lane < HEAD_DIM
    qb = (qf * (HEAD_DIM ** -0.5)).astype(BF16)
    qib = qif.astype(BF16)
    wi = sm[:, SM_WI:SM_WI + H_IDX] * ((D_IDX ** -0.5) * (H_IDX ** -0.5))
    zero_b = jnp.zeros((tq, LANES), BF16)

    def head_window(x, h):
        win = x[:, (h // 2) * LANES:(h // 2 + 1) * LANES]
        return jnp.where(low if h % 2 == 0 else jnp.logical_not(low), win, zero_b)

    qim = [head_window(qib, h) for h in range(H_IDX)]
    wcol = [wi[:, h:h + 1] for h in range(H_IDX)]

    def p1(j, carry):
        ki = ki_tile(j)
        acc = jnp.zeros((tq, kt), F32)
        for h in range(H_IDX):
            acc = acc + jnp.maximum(_dot(qim[h], ki), 0.0) * wcol[h]
        for s in range(nsl):
            key = jnp.where(adm_fn(j, s), _mono_key(acc[:, s * LANES:(s + 1) * LANES]), INT_MIN)
            sc_ref[j, :, s * LANES:(s + 1) * LANES] = key
        return carry

    lax.fori_loop(0, nkt, p1, 0)

    def count(pred):
        def body(j, acc):
            tile = sc_ref[j]
            for s in range(nsl):
                acc = acc + jnp.where(pred(tile[:, s * LANES:(s + 1) * LANES], j, s), 1.0, 0.0)
            return acc
        acc = lax.fori_loop(0, nkt, body, jnp.zeros((tq, LANES), F32))
        return jnp.sum(acc, axis=1, keepdims=True)

    def count_ge(cand):
        cb = jnp.broadcast_to(cand, (tq, LANES))
        return count(lambda t, j, s: t >= cb)

    kf = float(topk)
    prefix = jnp.where(count_ge(jnp.zeros((tq, 1), I32)) >= kf, 0, INT_MIN).astype(I32)

    def bit_step(it, prefix):
        trial = prefix | lax.shift_left(jnp.int32(1), 30 - it)
        return jnp.where(count_ge(trial) >= kf, trial, prefix)

    thr = lax.fori_loop(0, 31, bit_step, prefix)
    thr = jnp.maximum(thr, INT_MIN + 1)
    thr_b = jnp.broadcast_to(thr, (tq, LANES))
    n_gt = count(lambda t, j, s: t > thr_b)
    n_eq = count(lambda t, j, s: t == thr_b)
    take = kf - n_gt

    def kpos(j, s):
        return j * kt + s * LANES + lane

    def tie_cut(_):
        def step(it, cut):
            trial = cut | lax.shift_left(jnp.int32(1), 30 - it)
            tb = jnp.broadcast_to(trial, (tq, LANES))
            c = count(lambda t, j, s: (t == thr_b) & (kpos(j, s) < tb))
            return jnp.where(c <= take, trial, cut)
        return lax.fori_loop(0, 31, step, jnp.zeros((tq, 1), I32))

    has_tie = jnp.max(jnp.where(n_gt + n_eq > kf, 1.0, 0.0)) > 0.0
    cut = lax.cond(has_tie, tie_cut, lambda _: jnp.full((tq, 1), INT_MAX, I32), 0)
    cut_b = jnp.broadcast_to(cut, (tq, LANES))

    qm = [head_window(qb, h) for h in range(H_ATT)]
    m_ref[...] = jnp.full(m_ref.shape, NEG_BIG, F32)
    l_ref[...] = jnp.zeros(l_ref.shape, F32)
    acc_ref[...] = jnp.zeros(acc_ref.shape, F32)

    def p3(j, carry):
        keyt = sc_ref[j]
        sel = []
        for s in range(nsl):
            ks = keyt[:, s * LANES:(s + 1) * LANES]
            sel.append((ks > thr_b) | ((ks == thr_b) & (kpos(j, s) < cut_b)))
        for p in range(H_ATT // 2):
            kp = kt_tile(j, p)
            vp = v_tile(j, p)
            for hsub in range(2):
                h = 2 * p + hsub
                s_all = _dot(qm[h], kp)
                parts = [jnp.where(sel[s], s_all[:, s * LANES:(s + 1) * LANES] + bias_fn(j, s, h), NEG_BIG)
                         for s in range(nsl)]
                mx = parts[0]
                for s in range(1, nsl):
                    mx = jnp.maximum(mx, parts[s])
                m_old = m_ref[h]
                m_new = jnp.maximum(m_old, jnp.max(mx, axis=1, keepdims=True))
                alpha = jnp.exp(m_old - m_new)
                pr = [jnp.exp(part - m_new) for part in parts]
                psum = pr[0]
                for s in range(1, nsl):
                    psum = psum + pr[s]
                l_ref[h] = alpha * l_ref[h] + psum
                pb = jnp.concatenate([x.astype(BF16) for x in pr], axis=1)
                acc_ref[h] = alpha * acc_ref[h] + _dot(pb, vp)
                m_ref[h] = m_new
        return carry

    lax.fori_loop(0, nkt, p3, 0)
    for p in range(H_ATT // 2):
        outs = []
        for hsub in range(2):
            h = 2 * p + hsub
            lsum = jnp.sum(l_ref[h], axis=1, keepdims=True)
            outs.append(acc_ref[h] / lsum)
        y_ref[:, p * LANES:(p + 1) * LANES] = jnp.where(low, outs[0], outs[1]).astype(y_ref.dtype)


N_BIAS_NEAR = 7


LOG2E = math.log2(math.e)
PV_ROWS = 256


def _attn_prompt_kernel(q_ref, qi_ref, sm_ref, k_ref, vt_ref, ki_ref, bt_ref, cf_ref, y_ref,
                        sc_ref, qt_ref, zb_ref, nm_ref, m_ref, l_ref, acc_ref, *, tq, kt, topk):
    i = pl.program_id(1)
    nql = tq // LANES
    nsl = kt // LANES
    nkt = lax.div((i + 1) * tq + (kt - 1), kt)

    def fold(x, op, chains=4):
        groups = x.shape[0] // SUBLANES
        accs = [x[a * SUBLANES:(a + 1) * SUBLANES] for a in range(chains)]
        for r in range(chains, groups):
            accs[r % chains] = op(accs[r % chains], x[r * SUBLANES:(r + 1) * SUBLANES])
        while len(accs) > 1:
            accs = [op(accs[a], accs[a + len(accs) // 2]) for a in range(len(accs) // 2)]
        return accs[0]

    eye = jnp.where(lax.broadcasted_iota(I32, (LANES, LANES), 0) == lax.broadcasted_iota(I32, (LANES, LANES), 1),
                    1.0, 0.0).astype(BF16)
    lane = lax.broadcasted_iota(I32, (tq, LANES), 1)
    low = lane < HEAD_DIM
    qb = (q_ref[...] * (HEAD_DIM ** -0.5 * LOG2E)).astype(BF16)
    qib = qi_ref[...].astype(BF16)
    zero_b = jnp.zeros((tq, LANES), BF16)
    for src, base in ((qib, 0), (qb, H_IDX)):
        for h in range(H_ATT):
            win = src[:, (h // 2) * LANES:(h // 2 + 1) * LANES]
            win = jnp.where(low if h % 2 == 0 else jnp.logical_not(low), win, zero_b)
            qt_ref[base + h] = _dot_nt(eye, win).astype(BF16)
    sh, smm, sl_ = _split3(sm_ref[...])
    sm_t = _dot_nt(eye, sh) + _dot_nt(eye, smm) + _dot_nt(eye, sl_)
    wrow = [sm_t[SM_WI + h:SM_WI + h + 1, :] * ((D_IDX ** -0.5) * (H_IDX ** -0.5)) for h in range(H_IDX)]

    qpos = i * tq + lax.broadcasted_iota(I32, (1, tq), 1)
    cend = (lax.shift_right_logical(qpos, int(math.log2(CHUNK))) + 1) * CHUNK
    krow = lax.broadcasted_iota(I32, (LANES, tq), 0)

    def p1(j, carry):
        vmax, vmin = carry
        ki2 = ki_ref[0, j]
        for h in range(H_IDX):
            t = jnp.maximum(_dot(ki2, qt_ref[h]), 0.0) * wrow[h]
            if h == 0:
                zb_ref[0] = t
            elif h < H_IDX - 1:
                zb_ref[0] += t
            else:
                for sl in range(nsl):
                    rows = slice(sl * LANES, (sl + 1) * LANES)
                    sc = zb_ref[0, rows, :] + t[rows]
                    adm = (j * nsl + sl) * LANES + krow < cend
                    sc_ref[j * nsl + sl] = jnp.where(adm, _mono_key(sc), INT_MIN)
                    vmax = jnp.maximum(vmax, fold(jnp.where(adm, sc, -F32_BIG), jnp.maximum))
                    vmin = jnp.minimum(vmin, fold(jnp.where(adm, sc, F32_BIG), jnp.minimum))
        return vmax, vmin

    vmax, vmin = lax.fori_loop(0, nkt, p1, (jnp.full((SUBLANES, tq), -F32_BIG, F32),
                                            jnp.full((SUBLANES, tq), F32_BIG, F32)))

    def count(pred):
        def body(g, acc):
            return acc + fold(jnp.where(pred(sc_ref[g], g), 1.0, 0.0), jnp.add)
        acc = lax.fori_loop(0, nkt * nsl, body, jnp.zeros((SUBLANES, tq), F32))
        return jnp.sum(acc, axis=0, keepdims=True)

    kf = float(topk)

    def active_of(lo, hi, clo):
        return (clo > kf) & (hi - 1 > lo)

    def search_cond(st):
        it, lo, hi, clo, chi = st
        return (it < SEARCH_MAX_STEPS) & (jnp.max(jnp.where(active_of(lo, hi, clo), 1.0, 0.0)) > 0.0)

    def search_step(st):
        it, lo, hi, clo, chi = st
        active = active_of(lo, hi, clo)
        v_lo = _key_value(lo)
        v_hi = _key_value(hi)
        frac = (jnp.log(clo) - math.log(kf)) / (jnp.log(clo) - jnp.log(jnp.maximum(chi, 0.5)))
        t_int = _mono_key(v_lo + (v_hi - v_lo) * frac)
        t_mid = (lo >> 1) + (hi >> 1) + (lo & hi & 1)
        trial = jnp.where(lax.rem(it, 2) == 0, t_int, t_mid)
        trial = jnp.where(it == 0, 0, jnp.where((it == 1) & (lo == 0), 1, trial))
        trial = jnp.minimum(jnp.maximum(trial, lo + 1), hi - 1)
        c = count(lambda t, g: t >= trial)
        up = active & (c >= kf)
        dn = active & (c < kf)
        return (it + 1, jnp.where(up, trial, lo), jnp.where(dn, trial, hi),
                jnp.where(up, c, clo), jnp.where(dn, c, chi))

    lo0 = _mono_key(jnp.min(vmin, axis=0, keepdims=True))
    hi0 = _mono_key(jnp.max(vmax, axis=0, keepdims=True)) + 1
    _, thr, _, n_ge, n_gt = lax.while_loop(
        search_cond, search_step,
        (jnp.int32(0), lo0, hi0, cend.astype(F32), jnp.zeros((1, tq), F32)))
    take = kf - n_gt

    def tie_cut(_):
        def step(it, cut):
            trial = cut | lax.shift_left(jnp.int32(1), 30 - it)
            c = count(lambda t, g: (t == thr) & (g * LANES + krow < trial))
            return jnp.where(c <= take, trial, cut)
        return lax.fori_loop(0, 31, step, jnp.zeros((1, tq), I32))

    has_tie = jnp.max(jnp.where(n_ge > kf, 1.0, 0.0)) > 0.0
    cut = lax.cond(has_tie, tie_cut, lambda _: jnp.full((1, tq), INT_MAX, I32), 0)

    m_ref[...] = jnp.full(m_ref.shape, NEG_BIG, F32)
    l_ref[...] = jnp.zeros(l_ref.shape, F32)
    acc_ref[...] = jnp.zeros(acc_ref.shape, F32)

    def tile_step(j, near):
        for sl in range(nsl):
            g = j * nsl + sl
            keyt = sc_ref[g]
            sel = (keyt > thr) | ((keyt == thr) & (g * LANES + krow < cut))
            nm_ref[sl * LANES:(sl + 1) * LANES, :] = jnp.where(sel, 0.0, NEG_BIG)

        def logits(h):
            mx = None
            for sl in range(nsl):
                rows = slice(sl * LANES, (sl + 1) * LANES)
                z = _dot(k_ref[0, j, rows, (h // 2) * LANES:(h // 2 + 1) * LANES], qt_ref[H_IDX + h]) + nm_ref[rows, :]
                if near:
                    z = z + jnp.concatenate(
                        [bt_ref[jnp.clip((i * nql + hf) - (j * nsl + sl), 0, N_BIAS_NEAR - 1), h]
                         for hf in range(nql)], axis=1)
                zb_ref[h % 2, rows, :] = z
                cm = fold(z, jnp.maximum)
                mx = cm if mx is None else jnp.maximum(mx, cm)
            return mx

        def accumulate(h, mx):
            shift = 0.0 if near else cf_ref[h]
            m_old = m_ref[h]
            m_new = jnp.maximum(m_old, jnp.max(mx, axis=0, keepdims=True) + shift)
            alpha = jnp.exp2(m_old - m_new)
            msub = m_new - shift
            lsum, pv = None, None
            for c in range(kt // PV_ROWS):
                rows = slice(c * PV_ROWS, (c + 1) * PV_ROWS)
                p = jnp.exp2(zb_ref[h % 2, rows, :] - msub)
                ls = fold(p, jnp.add)
                pc = _dot(vt_ref[0, j, (h // 2) * LANES:(h // 2 + 1) * LANES, rows], p.astype(BF16))
                lsum = ls if lsum is None else lsum + ls
                pv = pc if pv is None else pv + pc
            l_ref[h] = alpha * l_ref[h] + lsum
            acc_ref[h] = alpha * acc_ref[h] + pv
            m_ref[h] = m_new

        mx_next = logits(0)
        for h in range(H_ATT):
            mx_cur = mx_next
            if h + 1 < H_ATT:
                mx_next = logits(h + 1)
            accumulate(h, mx_cur)

    n_far = jnp.minimum(lax.div(jnp.maximum(nql * i - (N_BIAS_NEAR - 1) - (nsl - 1) + nsl, 0), nsl), nkt)

    def p3_far(j, carry):
        tile_step(j, False)
        return carry

    def p3_near(j, carry):
        tile_step(j, True)
        return carry

    lax.fori_loop(0, n_far, p3_far, 0)
    lax.fori_loop(n_far, nkt, p3_near, 0)

    eye_q = jnp.where(lax.broadcasted_iota(I32, (tq, tq), 0) == lax.broadcasted_iota(I32, (tq, tq), 1),
                      1.0, 0.0).astype(BF16)
    rowd = lax.broadcasted_iota(I32, (LANES, tq), 0)
    for p in range(H_ATT // 2):
        outs = []
        for hsub in range(2):
            h = 2 * p + hsub
            outs.append(acc_ref[h] / jnp.sum(l_ref[h], axis=0, keepdims=True))
        y_t = jnp.where(rowd < HEAD_DIM, outs[0], outs[1]).astype(BF16)
        y_ref[:, p * LANES:(p + 1) * LANES] = _dot_nt(eye_q, y_t).astype(y_ref.dtype)


def _attn_prompt(u, k_tiles, vt_tiles, ki_tiles, btab, cfar, *, bsz, t, tq, kt):
    assert t % kt == 0 and t % tq == 0 and tq % LANES == 0 and kt % LANES == 0
    nq = t // tq
    nk = t // kt
    row = lambda bi, qi: bi * nq + qi
    once = pl.Buffered(1)
    return pl.pallas_call(
        functools.partial(_attn_prompt_kernel, tq=tq, kt=kt, topk=min(TOPK, t // 4)),
        out_shape=jax.ShapeDtypeStruct((bsz * t, D_ATT), BF16),
        grid=(bsz, nq),
        in_specs=[pl.BlockSpec((tq, D_ATT), lambda bi, qi: (row(bi, qi), COL_Q // D_ATT)),
                  pl.BlockSpec((tq, D_ATT), lambda bi, qi: (row(bi, qi), COL_QI // D_ATT)),
                  pl.BlockSpec((tq, LANES), lambda bi, qi: (row(bi, qi), COL_SMALL // LANES)),
                  pl.BlockSpec((1, nk, kt, D_ATT), lambda bi, qi: (bi, 0, 0, 0), pipeline_mode=once),
                  pl.BlockSpec((1, nk, D_ATT, kt), lambda bi, qi: (bi, 0, 0, 0), pipeline_mode=once),
                  pl.BlockSpec((1, nk, kt, LANES), lambda bi, qi: (bi, 0, 0, 0), pipeline_mode=once),
                  pl.BlockSpec((N_BIAS_NEAR, H_ATT, LANES, LANES), lambda bi, qi: (0, 0, 0, 0), pipeline_mode=once),
                  pl.BlockSpec((H_ATT, 1, tq), lambda bi, qi: (0, 0, 0), pipeline_mode=once)],
        out_specs=pl.BlockSpec((tq, D_ATT), lambda bi, qi: (row(bi, qi), 0)),
        scratch_shapes=[pltpu.VMEM((t // LANES, LANES, tq), I32),
                        pltpu.VMEM((H_IDX + H_ATT, LANES, tq), BF16),
                        pltpu.VMEM((2, kt, tq), F32),
                        pltpu.VMEM((kt, tq), F32),
                        pltpu.VMEM((H_ATT, 1, tq), F32),
                        pltpu.VMEM((H_ATT, SUBLANES, tq), F32),
                        pltpu.VMEM((H_ATT, LANES, tq), F32)],
        compiler_params=pltpu.CompilerParams(dimension_semantics=("arbitrary", "arbitrary"),
                                             vmem_limit_bytes=VMEM_LIMIT),
        name="sparse_attn_prompt",
    )(u, u, u, k_tiles, vt_tiles, ki_tiles, btab, cfar)


def _attn_sample_kernel(q_ref, qi_ref, sm_ref, kt_ref, v_ref, ki_ref, bt_ref, y_ref, sc_ref, m_ref, l_ref, acc_ref,
                        *, tq, kt, n_keys, topk):
    lane = lax.broadcasted_iota(I32, (tq, LANES), 1)

    def adm_fn(j, s):
        return (j * kt + s * LANES + lane) < n_keys

    def bias_fn(j, s, h):
        return bt_ref[0, h, :, s * LANES:(s + 1) * LANES]

    _attn_body(tq=tq, kt=kt, nkt=1, topk=topk, qf=q_ref[...], qif=qi_ref[...], sm=sm_ref[...],
               kt_tile=lambda j, p: kt_ref[0, p * LANES:(p + 1) * LANES, :],
               v_tile=lambda j, p: v_ref[0, :, p * LANES:(p + 1) * LANES],
               ki_tile=lambda j: ki_ref[0],
               adm_fn=adm_fn, bias_fn=bias_fn,
               sc_ref=sc_ref, m_ref=m_ref, l_ref=l_ref, acc_ref=acc_ref, y_ref=y_ref)


def _attn_sample(u, kt_all, v_all, ki_all, btab, *, bsz, tq, kt, n_keys):
    return pl.pallas_call(
        functools.partial(_attn_sample_kernel, tq=tq, kt=kt, n_keys=n_keys, topk=min(TOPK, n_keys // 4)),
        out_shape=jax.ShapeDtypeStruct((bsz * tq, D_ATT), BF16),
        grid=(bsz,),
        in_specs=[pl.BlockSpec((tq, D_ATT), lambda bi: (bi, COL_Q // D_ATT)),
                  pl.BlockSpec((tq, D_ATT), lambda bi: (bi, COL_QI // D_ATT)),
                  pl.BlockSpec((tq, LANES), lambda bi: (bi, COL_SMALL // LANES)),
                  pl.BlockSpec((1, D_ATT, kt), lambda bi: (bi, 0, 0)),
                  pl.BlockSpec((1, kt, D_ATT), lambda bi: (bi, 0, 0)),
                  pl.BlockSpec((1, LANES, kt), lambda bi: (bi, 0, 0)),
                  pl.BlockSpec((1, H_ATT, tq, kt), lambda bi: (0, 0, 0, 0))],
        out_specs=pl.BlockSpec((tq, D_ATT), lambda bi: (bi, 0)),
        scratch_shapes=[pltpu.VMEM((1, tq, kt), I32),
                        pltpu.VMEM((H_ATT, tq, LANES), F32),
                        pltpu.VMEM((H_ATT, tq, LANES), F32),
                        pltpu.VMEM((H_ATT, tq, LANES), F32)],
        compiler_params=pltpu.CompilerParams(dimension_semantics=("arbitrary",),
                                             vmem_limit_bytes=VMEM_LIMIT),
        name="sparse_attn_sample",
    )(u, u, u, kt_all, v_all, ki_all, btab)


def _out_proj_kernel(x_ref, ys_ref, yc_ref, ya_ref, w_ref, g_ref, o_ref):
    acc = _dot(ys_ref[...], w_ref[0:D_SSM, :])
    acc = acc + _dot(yc_ref[...], w_ref[D_SSM:D_SSM + D_CONV, :])
    acc = acc + _dot(ya_ref[...], w_ref[D_SSM + D_CONV:D_MODEL, :])
    o_ref[...] = x_ref[...] + _rms(acc, g_ref[...])


def _out_proj(x, ys, yc, ya, w, g, *, tm):
    m = x.shape[0]
    return pl.pallas_call(
        _out_proj_kernel,
        out_shape=jax.ShapeDtypeStruct((m, D_MODEL), F32),
        grid=(m // tm,),
        in_specs=[pl.BlockSpec((tm, D_MODEL), lambda i: (i, 0)),
                  pl.BlockSpec((tm, D_SSM), lambda i: (i, 0)),
                  pl.BlockSpec((tm, D_CONV), lambda i: (i, 0)),
                  pl.BlockSpec((tm, D_ATT), lambda i: (i, 0)),
                  pl.BlockSpec((D_MODEL, D_MODEL), lambda i: (0, 0)),
                  pl.BlockSpec((1, D_MODEL), lambda i: (0, 0))],
        out_specs=pl.BlockSpec((tm, D_MODEL), lambda i: (i, 0)),
        compiler_params=pltpu.CompilerParams(dimension_semantics=("arbitrary",),
                                             vmem_limit_bytes=VMEM_LIMIT),
        name="out_proj",
    )(x, ys, yc, ya, w, g)


FFN_HALO = 8


def _ffn_kernel(x_ref, gpre_ref, wg_ref, wu_ref, wd_ref, cw_ref, cb_ref, gpost_ref, p1_ref, p2_ref,
                o_ref, aux_ref, h_ref, acc_ref, buf_ref, tail_ref, *, tm, tps, seq_len, chained):
    i = pl.program_id(0)
    j = pl.program_id(1)

    @pl.when(j == 0)
    def _():
        h_ref[...] = _rms(x_ref[...], gpre_ref[...]).astype(BF16)
        acc_ref[...] = jnp.zeros(acc_ref.shape, F32)

    h = h_ref[...]
    a_pre = _dot(h, wg_ref[...])
    buf_ref[FFN_HALO:FFN_HALO + tm, :] = a_pre
    if chained:
        seq_start = lax.rem(i, tps) == 0
        buf_ref[0:FFN_HALO, :] = jnp.where(seq_start, p1_ref[0], tail_ref[j])
        prev1 = buf_ref[FFN_HALO - 1:FFN_HALO - 1 + tm, :]
        prev2 = buf_ref[FFN_HALO - 2:FFN_HALO - 2 + tm, :]
        last = a_pre[tm - FFN_HALO:tm, :]
        tail_ref[j] = last
        aux_ref[0] = last
    else:
        buf_ref[0:FFN_HALO, :] = jnp.zeros((FFN_HALO, a_pre.shape[1]), F32)
        tpos = lax.rem(lax.broadcasted_iota(I32, a_pre.shape, 0), seq_len)
        prev1 = jnp.where(tpos >= 1, buf_ref[FFN_HALO - 1:FFN_HALO - 1 + tm, :], p1_ref[...])
        prev2 = jnp.where(tpos >= 2, buf_ref[FFN_HALO - 2:FFN_HALO - 2 + tm, :], p2_ref[...])
        aux_ref[...] = a_pre
    a = cw_ref[0:1, :] * prev2 + cw_ref[1:2, :] * prev1 + cw_ref[2:3, :] * a_pre + cb_ref[...]
    f = (_silu(a) * _dot(h, wu_ref[...])).astype(BF16)
    acc_ref[...] += _dot(f, wd_ref[...])

    @pl.when(j == pl.num_programs(1) - 1)
    def _():
        o_ref[...] = x_ref[...] + _rms(acc_ref[...], gpost_ref[...])


def _ffn(x, gpre, wg, wu, wd, cw, cb, gpost, p1, p2, *, tm, tf, seq_len, chained):
    m = x.shape[0]
    nf = D_FF // tf
    tps = max(seq_len // tm, 1)
    if chained:
        nseq = m // seq_len
        p_specs = [pl.BlockSpec((1, FFN_HALO, tf), lambda i, j: (i // tps, 0, j)),
                   pl.BlockSpec((1, FFN_HALO, tf), lambda i, j: (i // tps, 0, j))]
        aux_shape = jax.ShapeDtypeStruct((m // tm, FFN_HALO, D_FF), F32)
        aux_spec = pl.BlockSpec((1, FFN_HALO, tf), lambda i, j: (i, 0, j))
    else:
        p_specs = [pl.BlockSpec((tm, tf), lambda i, j: (i, j)), pl.BlockSpec((tm, tf), lambda i, j: (i, j))]
        aux_shape = jax.ShapeDtypeStruct((m, D_FF), F32)
        aux_spec = pl.BlockSpec((tm, tf), lambda i, j: (i, j))
    return pl.pallas_call(
        functools.partial(_ffn_kernel, tm=tm, tps=tps, seq_len=seq_len, chained=chained),
        out_shape=(jax.ShapeDtypeStruct((m, D_MODEL), F32), aux_shape),
        grid=(m // tm, nf),
        in_specs=[pl.BlockSpec((tm, D_MODEL), lambda i, j: (i, 0)),
                  pl.BlockSpec((1, D_MODEL), lambda i, j: (0, 0)),
                  pl.BlockSpec((D_MODEL, tf), lambda i, j: (0, j)),
                  pl.BlockSpec((D_MODEL, tf), lambda i, j: (0, j)),
                  pl.BlockSpec((tf, D_MODEL), lambda i, j: (j, 0)),
                  pl.BlockSpec((FFN_CONV, tf), lambda i, j: (0, j)),
                  pl.BlockSpec((1, tf), lambda i, j: (0, j)),
                  pl.BlockSpec((1, D_MODEL), lambda i, j: (0, 0))] + p_specs,
        out_specs=(pl.BlockSpec((tm, D_MODEL), lambda i, j: (i, 0)), aux_spec),
        scratch_shapes=[pltpu.VMEM((tm, D_MODEL), BF16),
                        pltpu.VMEM((tm, D_MODEL), F32),
                        pltpu.VMEM((FFN_HALO + tm, tf), F32),
                        pltpu.VMEM((nf, FFN_HALO, tf), F32)],
        compiler_params=pltpu.CompilerParams(dimension_semantics=("arbitrary", "arbitrary"),
                                             vmem_limit_bytes=VMEM_LIMIT),
        name="conv_ffn",
    )(x, gpre, wg, wu, wd, cw, cb, gpost, p1, p2)


def _prep_layer_weights(w):
    w_in = w["w_in"]
    o_dt = D_SSM + D_XBC
    o_glu = o_dt + H_SSM
    o_ki = o_glu + 2 * D_CONV + 4 * D_ATT
    o_wi = o_ki + D_IDX
    pad = jnp.zeros((D_MODEL, LANES - H_SSM - H_IDX - D_IDX), w_in.dtype)
    w_r = jnp.concatenate([w_in[:, :o_dt], w_in[:, o_glu:o_ki], w_in[:, o_dt:o_glu], w_in[:, o_wi:o_wi + H_IDX],
                           pad, w_in[:, o_ki:o_wi]], axis=1).astype(BF16)
    assert w_r.shape[1] == D_PROJ
    row = lambda v: v.reshape(1, -1).astype(F32)
    padl = lambda v: jnp.pad(v.astype(F32), (0, LANES - v.shape[0])).reshape(1, LANES)
    return dict(
        w_in=w_r, g_mix_pre=row(w["g_mix_pre"]),
        ssm_conv_w=w["ssm_conv_w"].astype(F32), ssm_conv_b=row(w["ssm_conv_b"]),
        dt_bias=padl(w["ssm_dt_bias"]), a_log=padl(w["ssm_a_log"]),
        d_x=row(jnp.repeat(w["ssm_d"], SSM_HEAD_DIM)), ssm_norm_g=row(w["ssm_norm_g"]),
        cconv_w=w["cconv_w"].astype(F32), cconv_b=row(w["cconv_b"]),
        cconv_ln_g=row(w["cconv_ln_g"]), cconv_ln_b=row(w["cconv_ln_b"]),
        w_out=w["w_out"].astype(BF16), g_mix_post=row(w["g_mix_post"]), g_ffn_pre=row(w["g_ffn_pre"]),
        ffn_w_gate=w["ffn_w_gate"].astype(BF16), ffn_w_up=w["ffn_w_up"].astype(BF16),
        ffn_w_down=w["ffn_w_down"].astype(BF16), ffn_conv_w=w["ffn_conv_w"].astype(F32),
        ffn_conv_b=row(w["ffn_conv_b"]), g_ffn_post=row(w["g_ffn_post"]))


def _expand_matrix():
    e = np.zeros((LANES, D_SSM), np.float32)
    for h in range(H_SSM):
        e[h, h * SSM_HEAD_DIM:(h + 1) * SSM_HEAD_DIM] = 1.0
    return jnp.asarray(e, BF16)


def _front_pad(state, halo):
    return jnp.pad(state.astype(F32), ((0, 0), (halo - state.shape[1], 0), (0, 0)))


def _state_t(h):
    b = h.shape[0]
    return jnp.transpose(h.astype(F32), (0, 3, 1, 2)).reshape(b, SSM_STATE, D_SSM)


def _state_from_t(ht):
    b = ht.shape[0]
    return jnp.transpose(ht.reshape(b, SSM_STATE, H_SSM, SSM_HEAD_DIM), (0, 2, 3, 1))


def _mixer_common(x, lw, emat, ssm_conv_prev, ssm_h0, cconv_prev, *, bsz, t, tm_proj, tn_proj, ssd_l, ssd_nv, cc_tm):
    u = _rms_matmul(x, lw["g_mix_pre"], lw["w_in"], tm=tm_proj, tn=tn_proj)
    y_ssm, ht, ctail = _ssd(u, _front_pad(ssm_conv_prev, SSD_HALO), _state_t(ssm_h0), lw["ssm_conv_w"],
                            lw["ssm_conv_b"], lw["dt_bias"], lw["a_log"], lw["d_x"], lw["ssm_norm_g"], emat,
                            bsz=bsz, t=t, L=ssd_l, nv=ssd_nv)
    y_conv, cctail = _cconv(u, _front_pad(cconv_prev, CC_HALO), lw["cconv_w"], lw["cconv_b"], lw["cconv_ln_g"],
                            lw["cconv_ln_b"], bsz=bsz, t=t, tm=cc_tm)
    k = u[:, COL_K:COL_K + D_ATT]
    v = u[:, COL_V:COL_V + D_ATT]
    ki = u[:, COL_SMALL + SM_KI:COL_SMALL + SM_KI + D_IDX]
    states = dict(k=k.reshape(bsz, t, H_ATT, HEAD_DIM), v=v.reshape(bsz, t, H_ATT, HEAD_DIM),
                  ki=ki.reshape(bsz, t, D_IDX), h=_state_from_t(ht),
                  ssm_conv=ctail[:, SSD_HALO - (SSM_CONV - 1):], cconv=cctail[:, CC_HALO - (CONV_WIDTH - 1):])
    return u, y_ssm, y_conv, k, v, ki, states


def _layer_prompt(x, lw, emat, btab, *, bsz, t, cfg):
    zeros = lambda *s: jnp.zeros(s, F32)
    u, y_ssm, y_conv, k, v, ki, st = _mixer_common(
        x, lw, emat, zeros(bsz, SSM_CONV - 1, D_XBC), zeros(bsz, H_SSM, SSM_HEAD_DIM, SSM_STATE),
        zeros(bsz, CONV_WIDTH - 1, D_CONV), bsz=bsz, t=t, tm_proj=cfg["tm_proj"], tn_proj=cfg["tn_proj"],
        ssd_l=cfg["ssd_l"], ssd_nv=cfg["ssd_l"], cc_tm=cfg["cc_tm"])
    kt = cfg["kt"]
    nk = t // kt
    k_tiles = k.astype(BF16).reshape(bsz, nk, kt, D_ATT)
    vt_tiles = jnp.transpose(v.astype(BF16).reshape(bsz, nk, kt, D_ATT), (0, 1, 3, 2))
    kib = ki.astype(BF16).reshape(bsz, nk, kt, D_IDX)
    ki_tiles = jnp.concatenate([kib, kib], axis=3)
    tq = cfg["tq"]
    cfar = jnp.tile(btab[N_BIAS_NEAR - 1, :, 0:1, :], (1, 1, tq // LANES))
    y_att = _attn_prompt(u, k_tiles, vt_tiles, ki_tiles, btab, cfar, bsz=bsz, t=t, tq=tq, kt=kt)
    x1 = _out_proj(x, y_ssm, y_conv, y_att, lw["w_out"], lw["g_mix_post"], tm=cfg["tm_out"])
    prev = zeros(bsz, FFN_HALO, D_FF)
    x2, ftail = _ffn(x1, lw["g_ffn_pre"], lw["ffn_w_gate"], lw["ffn_w_up"], lw["ffn_w_down"], lw["ffn_conv_w"],
                     lw["ffn_conv_b"], lw["g_ffn_post"], prev, prev, tm=cfg["tm_ffn"], tf=cfg["tf"], seq_len=t,
                     chained=True)
    ftail = ftail.reshape(bsz, t // cfg["tm_ffn"], FFN_HALO, D_FF)[:, -1]
    st["ffn_conv"] = ftail[:, FFN_HALO - (FFN_CONV - 1):]
    return x2, st


def _layer_sample(x, lw, emat, btab, past_k, past_v, past_ki, ssm_conv_prev, ssm_h0, cconv_prev, fconv_prev,
                  *, bsz, t, cfg):
    m = bsz * t
    u, y_ssm, y_conv, k, v, ki, st = _mixer_common(
        x, lw, emat, ssm_conv_prev, ssm_h0, cconv_prev, bsz=bsz, t=t, tm_proj=m, tn_proj=cfg["tn_proj"],
        ssd_l=LANES, ssd_nv=t, cc_tm=t)
    past = past_k.shape[1]
    n_keys = past + t
    ktp = cfg["kt_sample"]
    padk = lambda a: jnp.pad(a, ((0, 0), (0, ktp - n_keys), (0, 0)))
    k_all = padk(jnp.concatenate([past_k.reshape(bsz, past, D_ATT), k.reshape(bsz, t, D_ATT)], axis=1).astype(BF16))
    v_all = padk(jnp.concatenate([past_v.reshape(bsz, past, D_ATT), v.reshape(bsz, t, D_ATT)], axis=1).astype(BF16))
    ki_all = padk(jnp.concatenate([past_ki, ki.reshape(bsz, t, D_IDX)], axis=1).astype(BF16))
    kt_all = jnp.transpose(k_all, (0, 2, 1))
    kit = jnp.transpose(ki_all, (0, 2, 1))
    y_att = _attn_sample(u, kt_all, v_all, jnp.concatenate([kit, kit], axis=1), btab, bsz=bsz, tq=t, kt=ktp,
                         n_keys=n_keys)
    x1 = _out_proj(x, y_ssm, y_conv, y_att, lw["w_out"], lw["g_mix_post"], tm=m)
    fprev = fconv_prev.astype(F32)
    zrow = jnp.zeros((bsz, t - 1, D_FF), F32)
    p1 = jnp.concatenate([fprev[:, 1:2], zrow], axis=1).reshape(m, D_FF)
    p2 = jnp.concatenate([fprev, zrow[:, 1:]], axis=1).reshape(m, D_FF)
    x2, a_pre = _ffn(x1, lw["g_ffn_pre"], lw["ffn_w_gate"], lw["ffn_w_up"], lw["ffn_w_down"], lw["ffn_conv_w"],
                     lw["ffn_conv_b"], lw["g_ffn_post"], p1, p2, tm=m, tf=cfg["tf"], seq_len=t, chained=False)
    st["ffn_conv"] = a_pre.reshape(bsz, t, D_FF)[:, t - (FFN_CONV - 1):]
    return x2, st


_STATE_ORDER = ("k", "v", "ki", "h", "ssm_conv", "cconv", "ffn_conv")


def _prompt_cfg(t):
    big = t >= 4096
    return dict(tm_proj=1024 if big else 256, tn_proj=640, ssd_l=256, cc_tm=256, kt=512, tq=256,
                tm_out=512 if big else 256, tm_ffn=512 if big else 256, tf=512)


def _forward(x_prompt, x_sample, cache_k, cache_v, cache_kidx, state_ssm, state_ssm_conv, state_cconv,
             state_ffn_conv, rel_bias, weights):
    bp, tp, _ = x_prompt.shape
    bs, ts, _ = x_sample.shape
    depth = weights["w_in"].shape[0]
    past = cache_k.shape[2]
    emat = _expand_matrix()
    cfg_p = _prompt_cfg(tp)
    kt_sample = -(-(past + ts) // LANES) * LANES
    cfg_s = dict(tn_proj=640, tf=512, kt_sample=kt_sample)
    rb = rel_bias.astype(F32)
    btab_p = _bias_table(rb, nd=N_BIAS_NEAR, rows=LANES, cols=LANES, off0=0, step=LANES, key_axis=0, scale=LOG2E)
    btab_s = _bias_table(rb, nd=1, rows=ts, cols=kt_sample, off0=-past, step=0)
    xp = x_prompt.reshape(bp * tp, D_MODEL)
    xs = x_sample.reshape(bs * ts, D_MODEL)
    p_states = {n: [] for n in _STATE_ORDER}
    s_states = {n: [] for n in _STATE_ORDER}
    for l in range(depth):
        lw = _prep_layer_weights({n: w[l] for n, w in weights.items()})
        xp, st_p = _layer_prompt(xp, lw, emat, btab_p, bsz=bp, t=tp, cfg=cfg_p)
        xs, st_s = _layer_sample(xs, lw, emat, btab_s, cache_k[l], cache_v[l], cache_kidx[l], state_ssm_conv[l],
                                 state_ssm[l], state_cconv[l], state_ffn_conv[l], bsz=bs, t=ts, cfg=cfg_s)
        for n in _STATE_ORDER:
            p_states[n].append(st_p[n])
            s_states[n].append(st_s[n])
    outs = [xp.reshape(bp, tp, D_MODEL), xs.reshape(bs, ts, D_MODEL)]
    outs += [jnp.stack(p_states[n]) for n in _STATE_ORDER]
    outs += [jnp.stack(s_states[n]) for n in _STATE_ORDER]
    return tuple(outs)


def kernel(x_prompt, x_sample, cache_k, cache_v, cache_kidx, state_ssm, state_ssm_conv, state_cconv, state_ffn_conv, rel_bias, g_mix_pre, w_in, ssm_conv_w, ssm_conv_b, ssm_dt_bias, ssm_a_log, ssm_d, ssm_norm_g, cconv_w, cconv_b, cconv_ln_g, cconv_ln_b, w_out, g_mix_post, g_ffn_pre, ffn_w_gate, ffn_w_up, ffn_conv_w, ffn_conv_b, ffn_w_down, g_ffn_post):
    weights = dict(g_mix_pre=g_mix_pre, w_in=w_in, ssm_conv_w=ssm_conv_w, ssm_conv_b=ssm_conv_b,
                   ssm_dt_bias=ssm_dt_bias, ssm_a_log=ssm_a_log, ssm_d=ssm_d, ssm_norm_g=ssm_norm_g,
                   cconv_w=cconv_w, cconv_b=cconv_b, cconv_ln_g=cconv_ln_g, cconv_ln_b=cconv_ln_b, w_out=w_out,
                   g_mix_post=g_mix_post, g_ffn_pre=g_ffn_pre, ffn_w_gate=ffn_w_gate, ffn_w_up=ffn_w_up,
                   ffn_conv_w=ffn_conv_w, ffn_conv_b=ffn_conv_b, ffn_w_down=ffn_w_down, g_ffn_post=g_ffn_post)
    return _forward(x_prompt, x_sample, cache_k, cache_v, cache_kidx, state_ssm, state_ssm_conv, state_cconv,
                    state_ffn_conv, rel_bias, weights)
```

```python
import functools
import math

import numpy as np
import jax
import jax.numpy as jnp
from jax import lax
from jax.experimental import pallas as pl
from jax.experimental.pallas import tpu as pltpu

F32 = jnp.float32
BF16 = jnp.bfloat16
I32 = jnp.int32

D_MODEL = 2048
D_SSM = 1024
SSM_HEAD_DIM = 64
H_SSM = 16
SSM_GROUPS = 2
SSM_STATE = 128
SSM_CONV = 4
D_XBC = D_SSM + 2 * SSM_GROUPS * SSM_STATE
D_CONV = 512
CONV_WIDTH = 31
D_ATT = 512
HEAD_DIM = 64
H_ATT = 8
H_IDX = 8
D_IDX = 64
TOPK = 256
CHUNK = 64
N_BUCKETS = 32
REL_MAX_DIST = 1024
D_FF = 5632
FFN_CONV = 3
EPS = 1e-6

LANES = 128
SUBLANES = 8

COL_Z, COL_XBC, COL_GLU, COL_Q, COL_K, COL_V, COL_QI, COL_SMALL = 0, 1024, 2560, 3584, 4096, 4608, 5120, 5632
D_PROJ = COL_SMALL + LANES
SM_DT, SM_WI, SM_KI = 0, 16, 64

INT_MIN = -(2 ** 31)
INT_MAX = 2 ** 31 - 1
NEG_BIG = -1e30
VMEM_LIMIT = 56 * 1024 * 1024


def _bucket_thresholds():
    nb = N_BUCKETS // 2
    max_exact = nb // 2
    n = np.arange(0, 4 * REL_MAX_DIST, dtype=np.int64)
    nf = np.maximum(n, 1).astype(np.float32)
    large = max_exact + (np.log(nf / np.float32(max_exact)) / np.float32(math.log(REL_MAX_DIST / max_exact))
                         * np.float32(nb - max_exact)).astype(np.int32)
    large = np.minimum(large, nb - 1)
    bucket = np.where(n < max_exact, n, large)
    steps = np.nonzero(np.diff(bucket))[0] + 1
    assert np.all(np.diff(bucket) >= 0) and np.all(np.diff(bucket) <= 1) and bucket[-1] == nb - 1
    return tuple(int(s) for s in steps)


BUCKET_STEPS = _bucket_thresholds()


def _sigmoid(x):
    return 1.0 / (1.0 + jnp.exp(-x))


def _silu(x):
    return x * _sigmoid(x)


def _split3(x):
    hi = x.astype(BF16)
    r1 = x - hi.astype(F32)
    mid = r1.astype(BF16)
    lo = (r1 - mid.astype(F32)).astype(BF16)
    return hi, mid, lo


def _dot(a, b):
    return jnp.dot(a, b, preferred_element_type=F32)


def _dot_nt(a, b):
    return lax.dot_general(a, b, (((1,), (1,)), ((), ())), preferred_element_type=F32)


def _exact_dot(sel_bf16, x_f32):
    hi, mid, lo = _split3(x_f32)
    return _dot(sel_bf16, hi) + _dot(sel_bf16, mid) + _dot(sel_bf16, lo)


def _exact_dot_r(x_f32, sel_bf16):
    hi, mid, lo = _split3(x_f32)
    return _dot(hi, sel_bf16) + _dot(mid, sel_bf16) + _dot(lo, sel_bf16)


def _rms(x, g):
    ms = jnp.mean(x * x, axis=-1, keepdims=True)
    return x * lax.rsqrt(ms + EPS) * g


def _rms_matmul_kernel(x_ref, g_ref, w_ref, o_ref, h_ref):
    @pl.when(pl.program_id(1) == 0)
    def _():
        h_ref[...] = _rms(x_ref[...], g_ref[...]).astype(BF16)

    o_ref[...] = _dot(h_ref[...], w_ref[...])


def _rms_matmul(x, g, w, *, tm, tn):
    m, d = x.shape
    n = w.shape[1]
    assert m % tm == 0 and n % tn == 0
    return pl.pallas_call(
        _rms_matmul_kernel,
        out_shape=jax.ShapeDtypeStruct((m, n), F32),
        grid=(m // tm, n // tn),
        in_specs=[pl.BlockSpec((tm, d), lambda i, j: (i, 0)),
                  pl.BlockSpec((1, d), lambda i, j: (0, 0)),
                  pl.BlockSpec((d, tn), lambda i, j: (0, j))],
        out_specs=pl.BlockSpec((tm, tn), lambda i, j: (i, j)),
        scratch_shapes=[pltpu.VMEM((tm, d), BF16)],
        compiler_params=pltpu.CompilerParams(dimension_semantics=("arbitrary", "arbitrary"),
                                             vmem_limit_bytes=VMEM_LIMIT),
        name="rms_in_proj",
    )(x, g, w)


def _bias_table_kernel(rb_ref, o_ref, *, off0, step, key_axis, scale):
    d = pl.program_id(0)
    h = pl.program_id(1)
    rows, cols = o_ref.shape[2], o_ref.shape[3]
    rel = (lax.broadcasted_iota(I32, (rows, cols), key_axis) - lax.broadcasted_iota(I32, (rows, cols), 1 - key_axis)
           + (off0 - d * step))
    n = jnp.abs(rel)
    bucket = jnp.where(rel > 0, N_BUCKETS // 2, 0)
    for s in BUCKET_STEPS:
        bucket = bucket + jnp.where(n >= s, 1, 0)
    acc = jnp.zeros((rows, cols), F32)
    for b in range(N_BUCKETS):
        acc = jnp.where(bucket == b, rb_ref[b, h], acc)
    o_ref[0, 0] = acc * scale


def _bias_table(rel_bias, *, nd, rows, cols, off0, step, key_axis=1, scale=1.0):
    return pl.pallas_call(
        functools.partial(_bias_table_kernel, off0=off0, step=step, key_axis=key_axis, scale=scale),
        out_shape=jax.ShapeDtypeStruct((nd, H_ATT, rows, cols), F32),
        grid=(nd, H_ATT),
        in_specs=[pl.BlockSpec(memory_space=pltpu.SMEM)],
        out_specs=pl.BlockSpec((1, 1, rows, cols), lambda d, h: (d, h, 0, 0)),
        name="bias_table",
    )(rel_bias)


CC_HALO = 32


def _cconv_kernel(val_ref, gate_ref, prev_ref, w_ref, b_ref, lg_ref, lb_ref, y_ref, tail_ref, buf_ref, *, tm):
    @pl.when(pl.program_id(1) == 0)
    def _():
        buf_ref[0:CC_HALO, :] = prev_ref[0]

    buf_ref[CC_HALO:CC_HALO + tm, :] = val_ref[...] * _sigmoid(gate_ref[...])
    first = CC_HALO - (CONV_WIDTH - 1)
    acc = jnp.zeros((tm, D_CONV), F32) + b_ref[...]
    for k in range(CONV_WIDTH):
        acc = acc + w_ref[k:k + 1, :] * buf_ref[first + k:first + k + tm, :]
    mu = jnp.mean(acc, axis=-1, keepdims=True)
    xc = acc - mu
    var = jnp.mean(xc * xc, axis=-1, keepdims=True)
    y = xc * lax.rsqrt(var + EPS) * lg_ref[...] + lb_ref[...]
    y_ref[...] = _silu(y).astype(y_ref.dtype)
    tail = buf_ref[tm:tm + CC_HALO, :]
    tail_ref[0] = tail
    buf_ref[0:CC_HALO, :] = tail


def _cconv(u, prev, w, b, lg, lb, *, bsz, t, tm):
    nt = t // tm
    row = lambda bi, ti: bi * nt + ti
    return pl.pallas_call(
        functools.partial(_cconv_kernel, tm=tm),
        out_shape=(jax.ShapeDtypeStruct((bsz * t, D_CONV), BF16),
                   jax.ShapeDtypeStruct((bsz, CC_HALO, D_CONV), F32)),
        grid=(bsz, nt),
        in_specs=[pl.BlockSpec((tm, D_CONV), lambda bi, ti: (row(bi, ti), COL_GLU // D_CONV)),
                  pl.BlockSpec((tm, D_CONV), lambda bi, ti: (row(bi, ti), COL_GLU // D_CONV + 1)),
                  pl.BlockSpec((1, CC_HALO, D_CONV), lambda bi, ti: (bi, 0, 0)),
                  pl.BlockSpec((CONV_WIDTH, D_CONV), lambda bi, ti: (0, 0)),
                  pl.BlockSpec((1, D_CONV), lambda bi, ti: (0, 0)),
                  pl.BlockSpec((1, D_CONV), lambda bi, ti: (0, 0)),
                  pl.BlockSpec((1, D_CONV), lambda bi, ti: (0, 0))],
        out_specs=(pl.BlockSpec((tm, D_CONV), lambda bi, ti: (row(bi, ti), 0)),
                   pl.BlockSpec((1, CC_HALO, D_CONV), lambda bi, ti: (bi, 0, 0))),
        scratch_shapes=[pltpu.VMEM((CC_HALO + tm, D_CONV), F32)],
        compiler_params=pltpu.CompilerParams(dimension_semantics=("arbitrary", "arbitrary"),
                                             vmem_limit_bytes=VMEM_LIMIT),
        name="conformer_conv",
    )(u, u, prev, w, b, lg, lb)


SSD_HALO = 8


def _ssd_kernel(z_ref, x0_ref, x1_ref, x2_ref, sm_ref, prevc_ref, h0_ref, cw_ref, cb_ref, dtb_ref, alog_ref,
                dx_ref, ng_ref, e_ref, y_ref, hout_ref, ctail_ref, buf_ref, ht_ref, yb_ref, *, L, nv):
    @pl.when(pl.program_id(1) == 0)
    def _():
        buf_ref[0:SSD_HALO, :] = prevc_ref[0]
        ht_ref[...] = h0_ref[0]

    if nv < L:
        buf_ref[SSD_HALO + nv:SSD_HALO + L, :] = jnp.zeros((L - nv, D_XBC), F32)
    for c, r in enumerate((x0_ref, x1_ref, x2_ref)):
        buf_ref[SSD_HALO:SSD_HALO + nv, c * 512:(c + 1) * 512] = r[...]
    first = SSD_HALO - (SSM_CONV - 1)
    acc = jnp.zeros((L, D_XBC), F32) + cb_ref[...]
    for k in range(SSM_CONV):
        acc = acc + cw_ref[k:k + 1, :] * buf_ref[first + k:first + k + L, :]
    xbc = _silu(acc)
    tail = buf_ref[nv:nv + SSD_HALO, :]
    ctail_ref[0] = tail
    buf_ref[0:SSD_HALO, :] = tail

    xs = xbc[:, :D_SSM]
    lane = lax.broadcasted_iota(I32, (L, LANES), 1)
    rowi = lax.broadcasted_iota(I32, (L, LANES), 0)
    sm = sm_ref[...]
    if nv < L:
        sm = jnp.concatenate([sm, jnp.zeros((L - nv, LANES), F32)], axis=0)
    dtr = sm + dtb_ref[...]
    dt = jnp.maximum(dtr, 0.0) + jnp.log(1.0 + jnp.exp(-jnp.abs(dtr)))
    dt = jnp.where((lane < H_SSM) & (rowi < nv), dt, 0.0)
    a = -jnp.exp(alog_ref[...])
    da = dt * a
    ri = lax.broadcasted_iota(I32, (L, L), 0)
    ci = lax.broadcasted_iota(I32, (L, L), 1)
    causal = ri >= ci
    tril = jnp.where(causal, 1.0, 0.0).astype(BF16)
    cum = _exact_dot(tril, da)
    eye = jnp.where(lax.broadcasted_iota(I32, (LANES, LANES), 0) == lax.broadcasted_iota(I32, (LANES, LANES), 1),
                    1.0, 0.0).astype(BF16)
    ch, cm, cl = _split3(cum)
    cum_t = _dot_nt(eye, ch) + _dot_nt(eye, cm) + _dot_nt(eye, cl)
    e = e_ref[...]
    ecx = _exact_dot_r(jnp.exp(cum), e)
    dtx = _exact_dot_r(dt, e)
    xdt = (xs * dtx).astype(BF16)
    edl = ecx[L - 1:L, :]
    dend_t = jnp.exp(cum_t[:, L - 1:L] - cum_t)
    lane_l = lax.broadcasted_iota(I32, (L, LANES), 1)
    lane_n = lax.broadcasted_iota(I32, (SSM_STATE, LANES), 1)
    hpg = H_SSM // SSM_GROUPS
    for g in range(SSM_GROUPS):
        bg = xbc[:, D_SSM + g * SSM_STATE:D_SSM + (g + 1) * SSM_STATE].astype(BF16)
        cg = xbc[:, D_SSM + (SSM_GROUPS + g) * SSM_STATE:D_SSM + (SSM_GROUPS + g + 1) * SSM_STATE].astype(BF16)
        cbt = _dot_nt(cg, bg)
        bg_t = _dot_nt(eye, bg)
        gcols = slice(g * hpg * SSM_HEAD_DIM, (g + 1) * hpg * SSM_HEAD_DIM)
        yoff = _dot(cg, ht_ref[:, gcols].astype(BF16)) * ecx[:, gcols]
        for p in range(hpg // 2):
            h0 = g * hpg + 2 * p
            pcols = slice(h0 * SSM_HEAD_DIM, (h0 + 2) * SSM_HEAD_DIM)
            xpair = xdt[:, pcols]
            res, st = [], []
            for hh in (h0, h0 + 1):
                seg = cum[:, hh:hh + 1] - cum_t[hh:hh + 1, :]
                dec = jnp.where(causal, jnp.exp(seg), 0.0)
                res.append(_dot((cbt * dec).astype(BF16), xpair))
                st.append(_dot((bg_t * dend_t[hh:hh + 1, :]).astype(BF16), xpair))
            yb_ref[:, pcols] = (jnp.where(lane_l < SSM_HEAD_DIM, res[0], res[1])
                                + yoff[:, 2 * p * SSM_HEAD_DIM:(2 * p + 2) * SSM_HEAD_DIM])
            ht_ref[:, pcols] = (ht_ref[:, pcols] * edl[:, pcols]
                                + jnp.where(lane_n < SSM_HEAD_DIM, st[0], st[1]))
    hout_ref[0] = ht_ref[...]
    y = yb_ref[...] + dx_ref[...] * xs
    z = z_ref[...]
    if nv < L:
        z = jnp.concatenate([z, jnp.zeros((L - nv, D_SSM), F32)], axis=0)
    y = _rms(y * _silu(z), ng_ref[...])
    y_ref[...] = y[:nv].astype(y_ref.dtype)


def _ssd(u, prevc, h0t, cw, cb, dtb, alog, dx, ng, emat, *, bsz, t, L, nv):
    nt = t // nv
    row = lambda bi, ti: bi * nt + ti
    c512 = lambda c: (lambda bi, ti: (row(bi, ti), c))
    const2 = lambda bi, ti: (0, 0)
    return pl.pallas_call(
        functools.partial(_ssd_kernel, L=L, nv=nv),
        out_shape=(jax.ShapeDtypeStruct((bsz * t, D_SSM), BF16),
                   jax.ShapeDtypeStruct((bsz, SSM_STATE, D_SSM), F32),
                   jax.ShapeDtypeStruct((bsz, SSD_HALO, D_XBC), F32)),
        grid=(bsz, nt),
        in_specs=[pl.BlockSpec((nv, D_SSM), lambda bi, ti: (row(bi, ti), 0)),
                  pl.BlockSpec((nv, 512), c512(COL_XBC // 512)),
                  pl.BlockSpec((nv, 512), c512(COL_XBC // 512 + 1)),
                  pl.BlockSpec((nv, 512), c512(COL_XBC // 512 + 2)),
                  pl.BlockSpec((nv, LANES), c512(COL_SMALL // LANES)),
                  pl.BlockSpec((1, SSD_HALO, D_XBC), lambda bi, ti: (bi, 0, 0)),
                  pl.BlockSpec((1, SSM_STATE, D_SSM), lambda bi, ti: (bi, 0, 0)),
                  pl.BlockSpec((SSM_CONV, D_XBC), const2),
                  pl.BlockSpec((1, D_XBC), const2),
                  pl.BlockSpec((1, LANES), const2),
                  pl.BlockSpec((1, LANES), const2),
                  pl.BlockSpec((1, D_SSM), const2),
                  pl.BlockSpec((1, D_SSM), const2),
                  pl.BlockSpec((LANES, D_SSM), const2)],
        out_specs=(pl.BlockSpec((nv, D_SSM), lambda bi, ti: (row(bi, ti), 0)),
                   pl.BlockSpec((1, SSM_STATE, D_SSM), lambda bi, ti: (bi, 0, 0)),
                   pl.BlockSpec((1, SSD_HALO, D_XBC), lambda bi, ti: (bi, 0, 0))),
        scratch_shapes=[pltpu.VMEM((SSD_HALO + L, D_XBC), F32),
                        pltpu.VMEM((SSM_STATE, D_SSM), F32),
                        pltpu.VMEM((L, D_SSM), F32)],
        compiler_params=pltpu.CompilerParams(dimension_semantics=("arbitrary", "arbitrary"),
                                             vmem_limit_bytes=VMEM_LIMIT),
        name="ssd_mixer",
    )(u, u, u, u, u, prevc, h0t, cw, cb, dtb, alog, dx, ng, emat)


def _flip_negative(b):
    return b ^ ((b >> 31) & INT_MAX)


def _mono_key(x):
    return _flip_negative(lax.bitcast_convert_type(x, I32))


def _key_value(k):
    return lax.bitcast_convert_type(_flip_negative(k), F32)


F32_BIG = 3e38
SEARCH_MAX_STEPS = 80


def _attn_body(*, tq, kt, nkt, topk, qf, qif, sm, kt_tile, v_tile, ki_tile, adm_fn, bias_fn,
               sc_ref, m_ref, l_ref, acc_ref, y_ref):
    nsl = kt // LANES
    lane = lax.broadcasted_iota(I32, (tq, LANES), 1)
    low = lane < HEAD_DIM
    qb = (qf * (HEAD_DIM ** -0.5)).astype(BF16)
    qib = qif.astype(BF16)
    wi = sm[:, SM_WI:SM_WI + H_IDX] * ((D_IDX ** -0.5) * (H_IDX ** -0.5))
    zero_b = jnp.zeros((tq, LANES), BF16)

    def head_window(x, h):
        win = x[:, (h // 2) * LANES:(h // 2 + 1) * LANES]
        return jnp.where(low if h % 2 == 0 else jnp.logical_not(low), win, zero_b)

    qim = [head_window(qib, h) for h in range(H_IDX)]
    wcol = [wi[:, h:h + 1] for h in range(H_IDX)]

    def p1(j, carry):
        ki = ki_tile(j)
        acc = jnp.zeros((tq, kt), F32)
        for h in range(H_IDX):
            acc = acc + jnp.maximum(_dot(qim[h], ki), 0.0) * wcol[h]
        for s in range(nsl):
            key = jnp.where(adm_fn(j, s), _mono_key(acc[:, s * LANES:(s + 1) * LANES]), INT_MIN)
            sc_ref[j, :, s * LANES:(s + 1) * LANES] = key
        return carry

    lax.fori_loop(0, nkt, p1, 0)

    def count(pred):
        def body(j, acc):
            tile = sc_ref[j]
            for s in range(nsl):
                acc = acc + jnp.where(pred(tile[:, s * LANES:(s + 1) * LANES], j, s), 1.0, 0.0)
            return acc
        acc = lax.fori_loop(0, nkt, body, jnp.zeros((tq, LANES), F32))
        return jnp.sum(acc, axis=1, keepdims=True)

    def count_ge(cand):
        cb = jnp.broadcast_to(cand, (tq, LANES))
        return count(lambda t, j, s: t >= cb)

    kf = float(topk)
    prefix = jnp.where(count_ge(jnp.zeros((tq, 1), I32)) >= kf, 0, INT_MIN).astype(I32)

    def bit_step(it, prefix):
        trial = prefix | lax.shift_left(jnp.int32(1), 30 - it)
        return jnp.where(count_ge(trial) >= kf, trial, prefix)

    thr = lax.fori_loop(0, 31, bit_step, prefix)
    thr = jnp.maximum(thr, INT_MIN + 1)
    thr_b = jnp.broadcast_to(thr, (tq, LANES))
    n_gt = count(lambda t, j, s: t > thr_b)
    n_eq = count(lambda t, j, s: t == thr_b)
    take = kf - n_gt

    def kpos(j, s):
        return j * kt + s * LANES + lane

    def tie_cut(_):
        def step(it, cut):
            trial = cut | lax.shift_left(jnp.int32(1), 30 - it)
            tb = jnp.broadcast_to(trial, (tq, LANES))
            c = count(lambda t, j, s: (t == thr_b) & (kpos(j, s) < tb))
            return jnp.where(c <= take, trial, cut)
        return lax.fori_loop(0, 31, step, jnp.zeros((tq, 1), I32))

    has_tie = jnp.max(jnp.where(n_gt + n_eq > kf, 1.0, 0.0)) > 0.0
    cut = lax.cond(has_tie, tie_cut, lambda _: jnp.full((tq, 1), INT_MAX, I32), 0)
    cut_b = jnp.broadcast_to(cut, (tq, LANES))

    qm = [head_window(qb, h) for h in range(H_ATT)]
    m_ref[...] = jnp.full(m_ref.shape, NEG_BIG, F32)
    l_ref[...] = jnp.zeros(l_ref.shape, F32)
    acc_ref[...] = jnp.zeros(acc_ref.shape, F32)

    def p3(j, carry):
        keyt = sc_ref[j]
        sel = []
        for s in range(nsl):
            ks = keyt[:, s * LANES:(s + 1) * LANES]
            sel.append((ks > thr_b) | ((ks == thr_b) & (kpos(j, s) < cut_b)))
        for p in range(H_ATT // 2):
            kp = kt_tile(j, p)
            vp = v_tile(j, p)
            for hsub in range(2):
                h = 2 * p + hsub
                s_all = _dot(qm[h], kp)
                parts = [jnp.where(sel[s], s_all[:, s * LANES:(s + 1) * LANES] + bias_fn(j, s, h), NEG_BIG)
                         for s in range(nsl)]
                mx = parts[0]
                for s in range(1, nsl):
                    mx = jnp.maximum(mx, parts[s])
                m_old = m_ref[h]
                m_new = jnp.maximum(m_old, jnp.max(mx, axis=1, keepdims=True))
                alpha = jnp.exp(m_old - m_new)
                pr = [jnp.exp(part - m_new) for part in parts]
                psum = pr[0]
                for s in range(1, nsl):
                    psum = psum + pr[s]
                l_ref[h] = alpha * l_ref[h] + psum
                pb = jnp.concatenate([x.astype(BF16) for x in pr], axis=1)
                acc_ref[h] = alpha * acc_ref[h] + _dot(pb, vp)
                m_ref[h] = m_new
        return carry

    lax.fori_loop(0, nkt, p3, 0)
    for p in range(H_ATT // 2):
        outs = []
        for hsub in range(2):
            h = 2 * p + hsub
            lsum = jnp.sum(l_ref[h], axis=1, keepdims=True)
            outs.append(acc_ref[h] / lsum)
        y_ref[:, p * LANES:(p + 1) * LANES] = jnp.where(low, outs[0], outs[1]).astype(y_ref.dtype)


N_BIAS_NEAR = 7


LOG2E = math.log2(math.e)
PV_ROWS = 256


def _attn_prompt_kernel(q_ref, qi_ref, sm_ref, k_ref, vt_ref, ki_ref, bt_ref, cf_ref, y_ref,
                        sc_ref, qt_ref, zb_ref, nm_ref, m_ref, l_ref, acc_ref, *, tq, kt, topk):
    i = pl.program_id(1)
    nql = tq // LANES
    nsl = kt // LANES
    nkt = lax.div((i + 1) * tq + (kt - 1), kt)

    def fold(x, op, chains=4):
        groups = x.shape[0] // SUBLANES
        accs = [x[a * SUBLANES:(a + 1) * SUBLANES] for a in range(chains)]
        for r in range(chains, groups):
            accs[r % chains] = op(accs[r % chains], x[r * SUBLANES:(r + 1) * SUBLANES])
        while len(accs) > 1:
            accs = [op(accs[a], accs[a + len(accs) // 2]) for a in range(len(accs) // 2)]
        return accs[0]

    eye = jnp.where(lax.broadcasted_iota(I32, (LANES, LANES), 0) == lax.broadcasted_iota(I32, (LANES, LANES), 1),
                    1.0, 0.0).astype(BF16)
    lane = lax.broadcasted_iota(I32, (tq, LANES), 1)
    low = lane < HEAD_DIM
    qb = (q_ref[...] * (HEAD_DIM ** -0.5 * LOG2E)).astype(BF16)
    qib = qi_ref[...].astype(BF16)
    zero_b = jnp.zeros((tq, LANES), BF16)
    for src, base in ((qib, 0), (qb, H_IDX)):
        for h in range(H_ATT):
            win = src[:, (h // 2) * LANES:(h // 2 + 1) * LANES]
            win = jnp.where(low if h % 2 == 0 else jnp.logical_not(low), win, zero_b)
            qt_ref[base + h] = _dot_nt(eye, win).astype(BF16)
    sh, smm, sl_ = _split3(sm_ref[...])
    sm_t = _dot_nt(eye, sh) + _dot_nt(eye, smm) + _dot_nt(eye, sl_)
    wrow = [sm_t[SM_WI + h:SM_WI + h + 1, :] * ((D_IDX ** -0.5) * (H_IDX ** -0.5)) for h in range(H_IDX)]

    qpos = i * tq + lax.broadcasted_iota(I32, (1, tq), 1)
    cend = (lax.shift_right_logical(qpos, int(math.log2(CHUNK))) + 1) * CHUNK
    krow = lax.broadcasted_iota(I32, (LANES, tq), 0)

    def p1(j, carry):
        vmax, vmin = carry
        ki2 = ki_ref[0, j]
        for h in range(H_IDX):
            t = jnp.maximum(_dot(ki2, qt_ref[h]), 0.0) * wrow[h]
            if h == 0:
                zb_ref[0] = t
            elif h < H_IDX - 1:
                zb_ref[0] += t
            else:
                for sl in range(nsl):
                    rows = slice(sl * LANES, (sl + 1) * LANES)
                    sc = zb_ref[0, rows, :] + t[rows]
                    sc = jnp.where(sc == 0.0, 0.0, sc)
                    adm = (j * nsl + sl) * LANES + krow < cend
                    sc_ref[j * nsl + sl] = jnp.where(adm, _mono_key(sc), INT_MIN)
                    vmax = jnp.maximum(vmax, fold(jnp.where(adm, sc, -F32_BIG), jnp.maximum))
                    vmin = jnp.minimum(vmin, fold(jnp.where(adm, sc, F32_BIG), jnp.minimum))
        return vmax, vmin

    vmax, vmin = lax.fori_loop(0, nkt, p1, (jnp.full((SUBLANES, tq), -F32_BIG, F32),
                                            jnp.full((SUBLANES, tq), F32_BIG, F32)))

    def count(pred):
        def body(j, acc):
            for sl in range(nsl):
                g = j * nsl + sl
                acc = acc + fold(jnp.where(pred(sc_ref[g], g), 1.0, 0.0), jnp.add)
            return acc
        acc = lax.fori_loop(0, nkt, body, jnp.zeros((SUBLANES, tq), F32))
        return jnp.sum(acc, axis=0, keepdims=True)

    kf = float(topk)

    def active_of(lo, hi, clo):
        return (clo > kf) & (hi - 1 > lo)

    def search_cond(st):
        it, lo, hi, clo, chi = st
        return (it < SEARCH_MAX_STEPS) & (jnp.max(jnp.where(active_of(lo, hi, clo), 1.0, 0.0)) > 0.0)

    def search_step(st):
        it, lo, hi, clo, chi = st
        active = active_of(lo, hi, clo)
        v_lo = _key_value(lo)
        v_hi = _key_value(hi)
        frac = (jnp.log(clo) - math.log(kf)) / (jnp.log(clo) - jnp.log(jnp.maximum(chi, 0.5)))
        t_int = _mono_key(v_lo + (v_hi - v_lo) * frac)
        t_mid = (lo >> 1) + (hi >> 1) + (lo & hi & 1)
        trial = jnp.where(lax.rem(it, 2) == 0, t_int, t_mid)
        trial = jnp.where(it == 0, 0, jnp.where((it == 1) & (lo == 0), 1, trial))
        trial = jnp.minimum(jnp.maximum(trial, lo + 1), hi - 1)
        c = count(lambda t, g: t >= trial)
        up = active & (c >= kf)
        dn = active & (c < kf)
        return (it + 1, jnp.where(up, trial, lo), jnp.where(dn, trial, hi),
                jnp.where(up, c, clo), jnp.where(dn, c, chi))

    lo0 = _mono_key(jnp.min(vmin, axis=0, keepdims=True))
    hi0 = _mono_key(jnp.max(vmax, axis=0, keepdims=True)) + 1
    _, thr, _, n_ge, n_gt = lax.while_loop(
        search_cond, search_step,
        (jnp.int32(0), lo0, hi0, cend.astype(F32), jnp.zeros((1, tq), F32)))
    take = kf - n_gt

    def tie_cut(_):
        def step(it, cut):
            trial = cut | lax.shift_left(jnp.int32(1), 30 - it)
            c = count(lambda t, g: (t == thr) & (g * LANES + krow < trial))
            return jnp.where(c <= take, trial, cut)
        return lax.fori_loop(0, 31, step, jnp.zeros((1, tq), I32))

    has_tie = jnp.max(jnp.where(n_ge > kf, 1.0, 0.0)) > 0.0
    cut = lax.cond(has_tie, tie_cut, lambda _: jnp.full((1, tq), INT_MAX, I32), 0)

    m_ref[...] = jnp.full(m_ref.shape, NEG_BIG, F32)
    l_ref[...] = jnp.zeros(l_ref.shape, F32)
    acc_ref[...] = jnp.zeros(acc_ref.shape, F32)

    def tile_step(j, near):
        for sl in range(nsl):
            g = j * nsl + sl
            keyt = sc_ref[g]
            sel = (keyt > thr) | ((keyt == thr) & (g * LANES + krow < cut))
            nm_ref[sl * LANES:(sl + 1) * LANES, :] = jnp.where(sel, 0.0, NEG_BIG)

        def logits(h):
            mx = None
            for sl in range(nsl):
                rows = slice(sl * LANES, (sl + 1) * LANES)
                z = _dot(k_ref[0, j, rows, (h // 2) * LANES:(h // 2 + 1) * LANES], qt_ref[H_IDX + h]) + nm_ref[rows, :]
                if near:
                    z = z + jnp.concatenate(
                        [bt_ref[jnp.clip((i * nql + hf) - (j * nsl + sl), 0, N_BIAS_NEAR - 1), h]
                         for hf in range(nql)], axis=1)
                zb_ref[h % 2, rows, :] = z
                cm = fold(z, jnp.maximum)
                mx = cm if mx is None else jnp.maximum(mx, cm)
            return mx

        def accumulate(h, mx):
            shift = 0.0 if near else cf_ref[h]
            m_old = m_ref[h]
            m_new = jnp.maximum(m_old, jnp.max(mx, axis=0, keepdims=True) + shift)
            alpha = jnp.exp2(m_old - m_new)
            msub = m_new - shift
            lsum, pv = None, None
            for c in range(kt // PV_ROWS):
                rows = slice(c * PV_ROWS, (c + 1) * PV_ROWS)
                p = jnp.exp2(zb_ref[h % 2, rows, :] - msub)
                ls = fold(p, jnp.add)
                pc = _dot(vt_ref[0, j, (h // 2) * LANES:(h // 2 + 1) * LANES, rows], p.astype(BF16))
                lsum = ls if lsum is None else lsum + ls
                pv = pc if pv is None else pv + pc
            l_ref[h] = alpha * l_ref[h] + lsum
            acc_ref[h] = alpha * acc_ref[h] + pv
            m_ref[h] = m_new

        mx_next = logits(0)
        for h in range(H_ATT):
            mx_cur = mx_next
            if h + 1 < H_ATT:
                mx_next = logits(h + 1)
            accumulate(h, mx_cur)

    n_far = jnp.minimum(lax.div(jnp.maximum(nql * i - (N_BIAS_NEAR - 1) - (nsl - 1) + nsl, 0), nsl), nkt)

    def p3_far(j, carry):
        tile_step(j, False)
        return carry

    def p3_near(j, carry):
        tile_step(j, True)
        return carry

    lax.fori_loop(0, n_far, p3_far, 0)
    lax.fori_loop(n_far, nkt, p3_near, 0)

    eye_q = jnp.where(lax.broadcasted_iota(I32, (tq, tq), 0) == lax.broadcasted_iota(I32, (tq, tq), 1),
                      1.0, 0.0).astype(BF16)
    rowd = lax.broadcasted_iota(I32, (LANES, tq), 0)
    for p in range(H_ATT // 2):
        outs = []
        for hsub in range(2):
            h = 2 * p + hsub
            outs.append(acc_ref[h] / jnp.sum(l_ref[h], axis=0, keepdims=True))
        y_t = jnp.where(rowd < HEAD_DIM, outs[0], outs[1]).astype(BF16)
        y_ref[:, p * LANES:(p + 1) * LANES] = _dot_nt(eye_q, y_t).astype(y_ref.dtype)


def _attn_prompt(u, k_tiles, vt_tiles, ki_tiles, btab, cfar, *, bsz, t, tq, kt):
    assert t % kt == 0 and t % tq == 0 and tq % LANES == 0 and kt % LANES == 0
    nq = t // tq
    nk = t // kt
    row = lambda bi, qi: bi * nq + qi
    once = pl.Buffered(1)
    return pl.pallas_call(
        functools.partial(_attn_prompt_kernel, tq=tq, kt=kt, topk=min(TOPK, t // 4)),
        out_shape=jax.ShapeDtypeStruct((bsz * t, D_ATT), BF16),
        grid=(bsz, nq),
        in_specs=[pl.BlockSpec((tq, D_ATT), lambda bi, qi: (row(bi, qi), COL_Q // D_ATT)),
                  pl.BlockSpec((tq, D_ATT), lambda bi, qi: (row(bi, qi), COL_QI // D_ATT)),
                  pl.BlockSpec((tq, LANES), lambda bi, qi: (row(bi, qi), COL_SMALL // LANES)),
                  pl.BlockSpec((1, nk, kt, D_ATT), lambda bi, qi: (bi, 0, 0, 0), pipeline_mode=once),
                  pl.BlockSpec((1, nk, D_ATT, kt), lambda bi, qi: (bi, 0, 0, 0), pipeline_mode=once),
                  pl.BlockSpec((1, nk, kt, LANES), lambda bi, qi: (bi, 0, 0, 0), pipeline_mode=once),
                  pl.BlockSpec((N_BIAS_NEAR, H_ATT, LANES, LANES), lambda bi, qi: (0, 0, 0, 0), pipeline_mode=once),
                  pl.BlockSpec((H_ATT, 1, tq), lambda bi, qi: (0, 0, 0), pipeline_mode=once)],
        out_specs=pl.BlockSpec((tq, D_ATT), lambda bi, qi: (row(bi, qi), 0)),
        scratch_shapes=[pltpu.VMEM((t // LANES, LANES, tq), I32),
                        pltpu.VMEM((H_IDX + H_ATT, LANES, tq), BF16),
                        pltpu.VMEM((2, kt, tq), F32),
                        pltpu.VMEM((kt, tq), F32),
                        pltpu.VMEM((H_ATT, 1, tq), F32),
                        pltpu.VMEM((H_ATT, SUBLANES, tq), F32),
                        pltpu.VMEM((H_ATT, LANES, tq), F32)],
        compiler_params=pltpu.CompilerParams(dimension_semantics=("arbitrary", "arbitrary"),
                                             vmem_limit_bytes=VMEM_LIMIT),
        name="sparse_attn_prompt",
    )(u, u, u, k_tiles, vt_tiles, ki_tiles, btab, cfar)


def _attn_sample_kernel(q_ref, qi_ref, sm_ref, kt_ref, v_ref, ki_ref, bt_ref, y_ref, sc_ref, m_ref, l_ref, acc_ref,
                        *, tq, kt, n_keys, topk):
    lane = lax.broadcasted_iota(I32, (tq, LANES), 1)

    def adm_fn(j, s):
        return (j * kt + s * LANES + lane) < n_keys

    def bias_fn(j, s, h):
        return bt_ref[0, h, :, s * LANES:(s + 1) * LANES]

    _attn_body(tq=tq, kt=kt, nkt=1, topk=topk, qf=q_ref[...], qif=qi_ref[...], sm=sm_ref[...],
               kt_tile=lambda j, p: kt_ref[0, p * LANES:(p + 1) * LANES, :],
               v_tile=lambda j, p: v_ref[0, :, p * LANES:(p + 1) * LANES],
               ki_tile=lambda j: ki_ref[0],
               adm_fn=adm_fn, bias_fn=bias_fn,
               sc_ref=sc_ref, m_ref=m_ref, l_ref=l_ref, acc_ref=acc_ref, y_ref=y_ref)


def _attn_sample(u, kt_all, v_all, ki_all, btab, *, bsz, tq, kt, n_keys):
    return pl.pallas_call(
        functools.partial(_attn_sample_kernel, tq=tq, kt=kt, n_keys=n_keys, topk=min(TOPK, n_keys // 4)),
        out_shape=jax.ShapeDtypeStruct((bsz * tq, D_ATT), BF16),
        grid=(bsz,),
        in_specs=[pl.BlockSpec((tq, D_ATT), lambda bi: (bi, COL_Q // D_ATT)),
                  pl.BlockSpec((tq, D_ATT), lambda bi: (bi, COL_QI // D_ATT)),
                  pl.BlockSpec((tq, LANES), lambda bi: (bi, COL_SMALL // LANES)),
                  pl.BlockSpec((1, D_ATT, kt), lambda bi: (bi, 0, 0)),
                  pl.BlockSpec((1, kt, D_ATT), lambda bi: (bi, 0, 0)),
                  pl.BlockSpec((1, LANES, kt), lambda bi: (bi, 0, 0)),
                  pl.BlockSpec((1, H_ATT, tq, kt), lambda bi: (0, 0, 0, 0))],
        out_specs=pl.BlockSpec((tq, D_ATT), lambda bi: (bi, 0)),
        scratch_shapes=[pltpu.VMEM((1, tq, kt), I32),
                        pltpu.VMEM((H_ATT, tq, LANES), F32),
                        pltpu.VMEM((H_ATT, tq, LANES), F32),
                        pltpu.VMEM((H_ATT, tq, LANES), F32)],
        compiler_params=pltpu.CompilerParams(dimension_semantics=("arbitrary",),
                                             vmem_limit_bytes=VMEM_LIMIT),
        name="sparse_attn_sample",
    )(u, u, u, kt_all, v_all, ki_all, btab)


def _out_proj_kernel(x_ref, ys_ref, yc_ref, ya_ref, w_ref, g_ref, o_ref):
    acc = _dot(ys_ref[...], w_ref[0:D_SSM, :])
    acc = acc + _dot(yc_ref[...], w_ref[D_SSM:D_SSM + D_CONV, :])
    acc = acc + _dot(ya_ref[...], w_ref[D_SSM + D_CONV:D_MODEL, :])
    o_ref[...] = x_ref[...] + _rms(acc, g_ref[...])


def _out_proj(x, ys, yc, ya, w, g, *, tm):
    m = x.shape[0]
    return pl.pallas_call(
        _out_proj_kernel,
        out_shape=jax.ShapeDtypeStruct((m, D_MODEL), F32),
        grid=(m // tm,),
        in_specs=[pl.BlockSpec((tm, D_MODEL), lambda i: (i, 0)),
                  pl.BlockSpec((tm, D_SSM), lambda i: (i, 0)),
                  pl.BlockSpec((tm, D_CONV), lambda i: (i, 0)),
                  pl.BlockSpec((tm, D_ATT), lambda i: (i, 0)),
                  pl.BlockSpec((D_MODEL, D_MODEL), lambda i: (0, 0)),
                  pl.BlockSpec((1, D_MODEL), lambda i: (0, 0))],
        out_specs=pl.BlockSpec((tm, D_MODEL), lambda i: (i, 0)),
        compiler_params=pltpu.CompilerParams(dimension_semantics=("arbitrary",),
                                             vmem_limit_bytes=VMEM_LIMIT),
        name="out_proj",
    )(x, ys, yc, ya, w, g)


FFN_HALO = 8


def _ffn_kernel(x_ref, gpre_ref, wg_ref, wu_ref, wd_ref, cw_ref, cb_ref, gpost_ref, p1_ref, p2_ref,
                o_ref, aux_ref, h_ref, acc_ref, buf_ref, tail_ref, *, tm, tps, seq_len, chained):
    i = pl.program_id(0)
    j = pl.program_id(1)

    @pl.when(j == 0)
    def _():
        h_ref[...] = _rms(x_ref[...], gpre_ref[...]).astype(BF16)
        acc_ref[...] = jnp.zeros(acc_ref.shape, F32)

    h = h_ref[...]
    a_pre = _dot(h, wg_ref[...])
    buf_ref[FFN_HALO:FFN_HALO + tm, :] = a_pre
    if chained:
        seq_start = lax.rem(i, tps) == 0
        buf_ref[0:FFN_HALO, :] = jnp.where(seq_start, p1_ref[0], tail_ref[j])
        prev1 = buf_ref[FFN_HALO - 1:FFN_HALO - 1 + tm, :]
        prev2 = buf_ref[FFN_HALO - 2:FFN_HALO - 2 + tm, :]
        last = a_pre[tm - FFN_HALO:tm, :]
        tail_ref[j] = last
        aux_ref[0] = last
    else:
        buf_ref[0:FFN_HALO, :] = jnp.zeros((FFN_HALO, a_pre.shape[1]), F32)
        tpos = lax.rem(lax.broadcasted_iota(I32, a_pre.shape, 0), seq_len)
        prev1 = jnp.where(tpos >= 1, buf_ref[FFN_HALO - 1:FFN_HALO - 1 + tm, :], p1_ref[...])
        prev2 = jnp.where(tpos >= 2, buf_ref[FFN_HALO - 2:FFN_HALO - 2 + tm, :], p2_ref[...])
        aux_ref[...] = a_pre
    a = cw_ref[0:1, :] * prev2 + cw_ref[1:2, :] * prev1 + cw_ref[2:3, :] * a_pre + cb_ref[...]
    f = (_silu(a) * _dot(h, wu_ref[...])).astype(BF16)
    acc_ref[...] += _dot(f, wd_ref[...])

    @pl.when(j == pl.num_programs(1) - 1)
    def _():
        o_ref[...] = x_ref[...] + _rms(acc_ref[...], gpost_ref[...])


def _ffn(x, gpre, wg, wu, wd, cw, cb, gpost, p1, p2, *, tm, tf, seq_len, chained):
    m = x.shape[0]
    nf = D_FF // tf
    tps = max(seq_len // tm, 1)
    if chained:
        nseq = m // seq_len
        p_specs = [pl.BlockSpec((1, FFN_HALO, tf), lambda i, j: (i // tps, 0, j)),
                   pl.BlockSpec((1, FFN_HALO, tf), lambda i, j: (i // tps, 0, j))]
        aux_shape = jax.ShapeDtypeStruct((m // tm, FFN_HALO, D_FF), F32)
        aux_spec = pl.BlockSpec((1, FFN_HALO, tf), lambda i, j: (i, 0, j))
    else:
        p_specs = [pl.BlockSpec((tm, tf), lambda i, j: (i, j)), pl.BlockSpec((tm, tf), lambda i, j: (i, j))]
        aux_shape = jax.ShapeDtypeStruct((m, D_FF), F32)
        aux_spec = pl.BlockSpec((tm, tf), lambda i, j: (i, j))
    return pl.pallas_call(
        functools.partial(_ffn_kernel, tm=tm, tps=tps, seq_len=seq_len, chained=chained),
        out_shape=(jax.ShapeDtypeStruct((m, D_MODEL), F32), aux_shape),
        grid=(m // tm, nf),
        in_specs=[pl.BlockSpec((tm, D_MODEL), lambda i, j: (i, 0)),
                  pl.BlockSpec((1, D_MODEL), lambda i, j: (0, 0)),
                  pl.BlockSpec((D_MODEL, tf), lambda i, j: (0, j)),
                  pl.BlockSpec((D_MODEL, tf), lambda i, j: (0, j)),
                  pl.BlockSpec((tf, D_MODEL), lambda i, j: (j, 0)),
                  pl.BlockSpec((FFN_CONV, tf), lambda i, j: (0, j)),
                  pl.BlockSpec((1, tf), lambda i, j: (0, j)),
                  pl.BlockSpec((1, D_MODEL), lambda i, j: (0, 0))] + p_specs,
        out_specs=(pl.BlockSpec((tm, D_MODEL), lambda i, j: (i, 0)), aux_spec),
        scratch_shapes=[pltpu.VMEM((tm, D_MODEL), BF16),
                        pltpu.VMEM((tm, D_MODEL), F32),
                        pltpu.VMEM((FFN_HALO + tm, tf), F32),
                        pltpu.VMEM((nf, FFN_HALO, tf), F32)],
        compiler_params=pltpu.CompilerParams(dimension_semantics=("arbitrary", "arbitrary"),
                                             vmem_limit_bytes=VMEM_LIMIT),
        name="conv_ffn",
    )(x, gpre, wg, wu, wd, cw, cb, gpost, p1, p2)


def _prep_layer_weights(w):
    w_in = w["w_in"]
    o_dt = D_SSM + D_XBC
    o_glu = o_dt + H_SSM
    o_ki = o_glu + 2 * D_CONV + 4 * D_ATT
    o_wi = o_ki + D_IDX
    pad = jnp.zeros((D_MODEL, LANES - H_SSM - H_IDX - D_IDX), w_in.dtype)
    w_r = jnp.concatenate([w_in[:, :o_dt], w_in[:, o_glu:o_ki], w_in[:, o_dt:o_glu], w_in[:, o_wi:o_wi + H_IDX],
                           pad, w_in[:, o_ki:o_wi]], axis=1).astype(BF16)
    assert w_r.shape[1] == D_PROJ
    row = lambda v: v.reshape(1, -1).astype(F32)
    padl = lambda v: jnp.pad(v.astype(F32), (0, LANES - v.shape[0])).reshape(1, LANES)
    return dict(
        w_in=w_r, g_mix_pre=row(w["g_mix_pre"]),
        ssm_conv_w=w["ssm_conv_w"].astype(F32), ssm_conv_b=row(w["ssm_conv_b"]),
        dt_bias=padl(w["ssm_dt_bias"]), a_log=padl(w["ssm_a_log"]),
        d_x=row(jnp.repeat(w["ssm_d"], SSM_HEAD_DIM)), ssm_norm_g=row(w["ssm_norm_g"]),
        cconv_w=w["cconv_w"].astype(F32), cconv_b=row(w["cconv_b"]),
        cconv_ln_g=row(w["cconv_ln_g"]), cconv_ln_b=row(w["cconv_ln_b"]),
        w_out=w["w_out"].astype(BF16), g_mix_post=row(w["g_mix_post"]), g_ffn_pre=row(w["g_ffn_pre"]),
        ffn_w_gate=w["ffn_w_gate"].astype(BF16), ffn_w_up=w["ffn_w_up"].astype(BF16),
        ffn_w_down=w["ffn_w_down"].astype(BF16), ffn_conv_w=w["ffn_conv_w"].astype(F32),
        ffn_conv_b=row(w["ffn_conv_b"]), g_ffn_post=row(w["g_ffn_post"]))


def _expand_matrix():
    e = np.zeros((LANES, D_SSM), np.float32)
    for h in range(H_SSM):
        e[h, h * SSM_HEAD_DIM:(h + 1) * SSM_HEAD_DIM] = 1.0
    return jnp.asarray(e, BF16)


def _front_pad(state, halo):
    return jnp.pad(state.astype(F32), ((0, 0), (halo - state.shape[1], 0), (0, 0)))


def _state_t(h):
    b = h.shape[0]
    return jnp.transpose(h.astype(F32), (0, 3, 1, 2)).reshape(b, SSM_STATE, D_SSM)


def _state_from_t(ht):
    b = ht.shape[0]
    return jnp.transpose(ht.reshape(b, SSM_STATE, H_SSM, SSM_HEAD_DIM), (0, 2, 3, 1))


def _mixer_common(x, lw, emat, ssm_conv_prev, ssm_h0, cconv_prev, *, bsz, t, tm_proj, tn_proj, ssd_l, ssd_nv, cc_tm):
    u = _rms_matmul(x, lw["g_mix_pre"], lw["w_in"], tm=tm_proj, tn=tn_proj)
    y_ssm, ht, ctail = _ssd(u, _front_pad(ssm_conv_prev, SSD_HALO), _state_t(ssm_h0), lw["ssm_conv_w"],
                            lw["ssm_conv_b"], lw["dt_bias"], lw["a_log"], lw["d_x"], lw["ssm_norm_g"], emat,
                            bsz=bsz, t=t, L=ssd_l, nv=ssd_nv)
    y_conv, cctail = _cconv(u, _front_pad(cconv_prev, CC_HALO), lw["cconv_w"], lw["cconv_b"], lw["cconv_ln_g"],
                            lw["cconv_ln_b"], bsz=bsz, t=t, tm=cc_tm)
    k = u[:, COL_K:COL_K + D_ATT]
    v = u[:, COL_V:COL_V + D_ATT]
    ki = u[:, COL_SMALL + SM_KI:COL_SMALL + SM_KI + D_IDX]
    states = dict(k=k.reshape(bsz, t, H_ATT, HEAD_DIM), v=v.reshape(bsz, t, H_ATT, HEAD_DIM),
                  ki=ki.reshape(bsz, t, D_IDX), h=_state_from_t(ht),
                  ssm_conv=ctail[:, SSD_HALO - (SSM_CONV - 1):], cconv=cctail[:, CC_HALO - (CONV_WIDTH - 1):])
    return u, y_ssm, y_conv, k, v, ki, states


def _layer_prompt(x, lw, emat, btab, *, bsz, t, cfg):
    zeros = lambda *s: jnp.zeros(s, F32)
    u, y_ssm, y_conv, k, v, ki, st = _mixer_common(
        x, lw, emat, zeros(bsz, SSM_CONV - 1, D_XBC), zeros(bsz, H_SSM, SSM_HEAD_DIM, SSM_STATE),
        zeros(bsz, CONV_WIDTH - 1, D_CONV), bsz=bsz, t=t, tm_proj=cfg["tm_proj"], tn_proj=cfg["tn_proj"],
        ssd_l=cfg["ssd_l"], ssd_nv=cfg["ssd_l"], cc_tm=cfg["cc_tm"])
    kt = cfg["kt"]
    nk = t // kt
    k_tiles = k.astype(BF16).reshape(bsz, nk, kt, D_ATT)
    vt_tiles = jnp.transpose(v.astype(BF16).reshape(bsz, nk, kt, D_ATT), (0, 1, 3, 2))
    kib = ki.astype(BF16).reshape(bsz, nk, kt, D_IDX)
    ki_tiles = jnp.concatenate([kib, kib], axis=3)
    tq = cfg["tq"]
    cfar = jnp.tile(btab[N_BIAS_NEAR - 1, :, 0:1, :], (1, 1, tq // LANES))
    y_att = _attn_prompt(u, k_tiles, vt_tiles, ki_tiles, btab, cfar, bsz=bsz, t=t, tq=tq, kt=kt)
    x1 = _out_proj(x, y_ssm, y_conv, y_att, lw["w_out"], lw["g_mix_post"], tm=cfg["tm_out"])
    prev = zeros(bsz, FFN_HALO, D_FF)
    x2, ftail = _ffn(x1, lw["g_ffn_pre"], lw["ffn_w_gate"], lw["ffn_w_up"], lw["ffn_w_down"], lw["ffn_conv_w"],
                     lw["ffn_conv_b"], lw["g_ffn_post"], prev, prev, tm=cfg["tm_ffn"], tf=cfg["tf"], seq_len=t,
                     chained=True)
    ftail = ftail.reshape(bsz, t // cfg["tm_ffn"], FFN_HALO, D_FF)[:, -1]
    st["ffn_conv"] = ftail[:, FFN_HALO - (FFN_CONV - 1):]
    return x2, st


def _layer_sample(x, lw, emat, btab, past_k, past_v, past_ki, ssm_conv_prev, ssm_h0, cconv_prev, fconv_prev,
                  *, bsz, t, cfg):
    m = bsz * t
    u, y_ssm, y_conv, k, v, ki, st = _mixer_common(
        x, lw, emat, ssm_conv_prev, ssm_h0, cconv_prev, bsz=bsz, t=t, tm_proj=m, tn_proj=cfg["tn_proj"],
        ssd_l=LANES, ssd_nv=t, cc_tm=t)
    past = past_k.shape[1]
    n_keys = past + t
    ktp = cfg["kt_sample"]
    padk = lambda a: jnp.pad(a, ((0, 0), (0, ktp - n_keys), (0, 0)))
    k_all = padk(jnp.concatenate([past_k.reshape(bsz, past, D_ATT), k.reshape(bsz, t, D_ATT)], axis=1).astype(BF16))
    v_all = padk(jnp.concatenate([past_v.reshape(bsz, past, D_ATT), v.reshape(bsz, t, D_ATT)], axis=1).astype(BF16))
    ki_all = padk(jnp.concatenate([past_ki, ki.reshape(bsz, t, D_IDX)], axis=1).astype(BF16))
    kt_all = jnp.transpose(k_all, (0, 2, 1))
    kit = jnp.transpose(ki_all, (0, 2, 1))
    y_att = _attn_sample(u, kt_all, v_all, jnp.concatenate([kit, kit], axis=1), btab, bsz=bsz, tq=t, kt=ktp,
                         n_keys=n_keys)
    x1 = _out_proj(x, y_ssm, y_conv, y_att, lw["w_out"], lw["g_mix_post"], tm=m)
    fprev = fconv_prev.astype(F32)
    zrow = jnp.zeros((bsz, t - 1, D_FF), F32)
    p1 = jnp.concatenate([fprev[:, 1:2], zrow], axis=1).reshape(m, D_FF)
    p2 = jnp.concatenate([fprev, zrow[:, 1:]], axis=1).reshape(m, D_FF)
    x2, a_pre = _ffn(x1, lw["g_ffn_pre"], lw["ffn_w_gate"], lw["ffn_w_up"], lw["ffn_w_down"], lw["ffn_conv_w"],
                     lw["ffn_conv_b"], lw["g_ffn_post"], p1, p2, tm=m, tf=cfg["tf"], seq_len=t, chained=False)
    st["ffn_conv"] = a_pre.reshape(bsz, t, D_FF)[:, t - (FFN_CONV - 1):]
    return x2, st


_STATE_ORDER = ("k", "v", "ki", "h", "ssm_conv", "cconv", "ffn_conv")


def _prompt_cfg(t):
    big = t >= 4096
    return dict(tm_proj=1024 if big else 256, tn_proj=640, ssd_l=256, cc_tm=256, kt=512, tq=256,
                tm_out=512 if big else 256, tm_ffn=512 if big else 256, tf=512)


def _forward(x_prompt, x_sample, cache_k, cache_v, cache_kidx, state_ssm, state_ssm_conv, state_cconv,
             state_ffn_conv, rel_bias, weights):
    bp, tp, _ = x_prompt.shape
    bs, ts, _ = x_sample.shape
    depth = weights["w_in"].shape[0]
    past = cache_k.shape[2]
    emat = _expand_matrix()
    cfg_p = _prompt_cfg(tp)
    kt_sample = -(-(past + ts) // LANES) * LANES
    cfg_s = dict(tn_proj=640, tf=512, kt_sample=kt_sample)
    rb = rel_bias.astype(F32)
    btab_p = _bias_table(rb, nd=N_BIAS_NEAR, rows=LANES, cols=LANES, off0=0, step=LANES, key_axis=0, scale=LOG2E)
    btab_s = _bias_table(rb, nd=1, rows=ts, cols=kt_sample, off0=-past, step=0)
    xp = x_prompt.reshape(bp * tp, D_MODEL)
    xs = x_sample.reshape(bs * ts, D_MODEL)
    p_states = {n: [] for n in _STATE_ORDER}
    s_states = {n: [] for n in _STATE_ORDER}
    for l in range(depth):
        lw = _prep_layer_weights({n: w[l] for n, w in weights.items()})
        xp, st_p = _layer_prompt(xp, lw, emat, btab_p, bsz=bp, t=tp, cfg=cfg_p)
        xs, st_s = _layer_sample(xs, lw, emat, btab_s, cache_k[l], cache_v[l], cache_kidx[l], state_ssm_conv[l],
                                 state_ssm[l], state_cconv[l], state_ffn_conv[l], bsz=bs, t=ts, cfg=cfg_s)
        for n in _STATE_ORDER:
            p_states[n].append(st_p[n])
            s_states[n].append(st_s[n])
    outs = [xp.reshape(bp, tp, D_MODEL), xs.reshape(bs, ts, D_MODEL)]
    outs += [jnp.stack(p_states[n]) for n in _STATE_ORDER]
    outs += [jnp.stack(s_states[n]) for n in _STATE_ORDER]
    return tuple(outs)


def kernel(x_prompt, x_sample, cache_k, cache_v, cache_kidx, state_ssm, state_ssm_conv, state_cconv, state_ffn_conv, rel_bias, g_mix_pre, w_in, ssm_conv_w, ssm_conv_b, ssm_dt_bias, ssm_a_log, ssm_d, ssm_norm_g, cconv_w, cconv_b, cconv_ln_g, cconv_ln_b, w_out, g_mix_post, g_ffn_pre, ffn_w_gate, ffn_w_up, ffn_conv_w, ffn_conv_b, ffn_w_down, g_ffn_post):
    weights = dict(g_mix_pre=g_mix_pre, w_in=w_in, ssm_conv_w=ssm_conv_w, ssm_conv_b=ssm_conv_b,
                   ssm_dt_bias=ssm_dt_bias, ssm_a_log=ssm_a_log, ssm_d=ssm_d, ssm_norm_g=ssm_norm_g,
                   cconv_w=cconv_w, cconv_b=cconv_b, cconv_ln_g=cconv_ln_g, cconv_ln_b=cconv_ln_b, w_out=w_out,
                   g_mix_post=g_mix_post, g_ffn_pre=g_ffn_pre, ffn_w_gate=ffn_w_gate, ffn_w_up=ffn_w_up,
                   ffn_conv_w=ffn_conv_w, ffn_conv_b=ffn_conv_b, ffn_w_down=ffn_w_down, g_ffn_post=g_ffn_post)
    return _forward(x_prompt, x_sample, cache_k, cache_v, cache_kidx, state_ssm, state_ssm_conv, state_cconv,
                    state_ffn_conv, rel_bias, weights)
```

```python
import functools
import math

import numpy as np
import jax
import jax.numpy as jnp
from jax import lax
from jax.experimental import pallas as pl
from jax.experimental.pallas import tpu as pltpu

F32 = jnp.float32
BF16 = jnp.bfloat16
I32 = jnp.int32

D_MODEL = 2048
D_SSM = 1024
SSM_HEAD_DIM = 64
H_SSM = 16
SSM_GROUPS = 2
SSM_STATE = 128
SSM_CONV = 4
D_XBC = D_SSM + 2 * SSM_GROUPS * SSM_STATE
D_CONV = 512
CONV_WIDTH = 31
D_ATT = 512
HEAD_DIM = 64
H_ATT = 8
H_IDX = 8
D_IDX = 64
TOPK = 256
CHUNK = 64
N_BUCKETS = 32
REL_MAX_DIST = 1024
D_FF = 5632
FFN_CONV = 3
EPS = 1e-6

LANES = 128
SUBLANES = 8

COL_Z, COL_XBC, COL_GLU, COL_Q, COL_K, COL_V, COL_QI, COL_SMALL = 0, 1024, 2560, 3584, 4096, 4608, 5120, 5632
D_PROJ = COL_SMALL + LANES
SM_DT, SM_WI, SM_KI = 0, 16, 64

INT_MIN = -(2 ** 31)
INT_MAX = 2 ** 31 - 1
NEG_BIG = -1e30
VMEM_LIMIT = 56 * 1024 * 1024


def _bucket_thresholds():
    nb = N_BUCKETS // 2
    max_exact = nb // 2
    n = np.arange(0, 4 * REL_MAX_DIST, dtype=np.int64)
    nf = np.maximum(n, 1).astype(np.float32)
    large = max_exact + (np.log(nf / np.float32(max_exact)) / np.float32(math.log(REL_MAX_DIST / max_exact))
                         * np.float32(nb - max_exact)).astype(np.int32)
    large = np.minimum(large, nb - 1)
    bucket = np.where(n < max_exact, n, large)
    steps = np.nonzero(np.diff(bucket))[0] + 1
    assert np.all(np.diff(bucket) >= 0) and np.all(np.diff(bucket) <= 1) and bucket[-1] == nb - 1
    return tuple(int(s) for s in steps)


BUCKET_STEPS = _bucket_thresholds()


def _sigmoid(x):
    return 1.0 / (1.0 + jnp.exp(-x))


def _silu(x):
    return x * _sigmoid(x)


def _split3(x):
    hi = x.astype(BF16)
    r1 = x - hi.astype(F32)
    mid = r1.astype(BF16)
    lo = (r1 - mid.astype(F32)).astype(BF16)
    return hi, mid, lo


def _dot(a, b):
    return jnp.dot(a, b, preferred_element_type=F32)


def _dot_nt(a, b):
    return lax.dot_general(a, b, (((1,), (1,)), ((), ())), preferred_element_type=F32)


def _exact_dot(sel_bf16, x_f32):
    hi, mid, lo = _split3(x_f32)
    return _dot(sel_bf16, hi) + _dot(sel_bf16, mid) + _dot(sel_bf16, lo)


def _exact_dot_r(x_f32, sel_bf16):
    hi, mid, lo = _split3(x_f32)
    return _dot(hi, sel_bf16) + _dot(mid, sel_bf16) + _dot(lo, sel_bf16)


def _rms(x, g):
    ms = jnp.mean(x * x, axis=-1, keepdims=True)
    return x * lax.rsqrt(ms + EPS) * g


def _rms_matmul_kernel(x_ref, g_ref, w_ref, o_ref, h_ref):
    @pl.when(pl.program_id(1) == 0)
    def _():
        h_ref[...] = _rms(x_ref[...], g_ref[...]).astype(BF16)

    o_ref[...] = _dot(h_ref[...], w_ref[...])


def _rms_matmul(x, g, w, *, tm, tn):
    m, d = x.shape
    n = w.shape[1]
    assert m % tm == 0 and n % tn == 0
    return pl.pallas_call(
        _rms_matmul_kernel,
        out_shape=jax.ShapeDtypeStruct((m, n), F32),
        grid=(m // tm, n // tn),
        in_specs=[pl.BlockSpec((tm, d), lambda i, j: (i, 0)),
                  pl.BlockSpec((1, d), lambda i, j: (0, 0)),
                  pl.BlockSpec((d, tn), lambda i, j: (0, j))],
        out_specs=pl.BlockSpec((tm, tn), lambda i, j: (i, j)),
        scratch_shapes=[pltpu.VMEM((tm, d), BF16)],
        compiler_params=pltpu.CompilerParams(dimension_semantics=("arbitrary", "arbitrary"),
                                             vmem_limit_bytes=VMEM_LIMIT),
        name="rms_in_proj",
    )(x, g, w)


def _bias_table_kernel(rb_ref, o_ref, *, off0, step, key_axis, scale):
    d = pl.program_id(0)
    h = pl.program_id(1)
    rows, cols = o_ref.shape[2], o_ref.shape[3]
    rel = (lax.broadcasted_iota(I32, (rows, cols), key_axis) - lax.broadcasted_iota(I32, (rows, cols), 1 - key_axis)
           + (off0 - d * step))
    n = jnp.abs(rel)
    bucket = jnp.where(rel > 0, N_BUCKETS // 2, 0)
    for s in BUCKET_STEPS:
        bucket = bucket + jnp.where(n >= s, 1, 0)
    acc = jnp.zeros((rows, cols), F32)
    for b in range(N_BUCKETS):
        acc = jnp.where(bucket == b, rb_ref[b, h], acc)
    o_ref[0, 0] = acc * scale


def _bias_table(rel_bias, *, nd, rows, cols, off0, step, key_axis=1, scale=1.0):
    return pl.pallas_call(
        functools.partial(_bias_table_kernel, off0=off0, step=step, key_axis=key_axis, scale=scale),
        out_shape=jax.ShapeDtypeStruct((nd, H_ATT, rows, cols), F32),
        grid=(nd, H_ATT),
        in_specs=[pl.BlockSpec(memory_space=pltpu.SMEM)],
        out_specs=pl.BlockSpec((1, 1, rows, cols), lambda d, h: (d, h, 0, 0)),
        name="bias_table",
    )(rel_bias)


CC_HALO = 32


def _cconv_kernel(val_ref, gate_ref, prev_ref, w_ref, b_ref, lg_ref, lb_ref, y_ref, tail_ref, buf_ref, *, tm):
    @pl.when(pl.program_id(1) == 0)
    def _():
        buf_ref[0:CC_HALO, :] = prev_ref[0]

    buf_ref[CC_HALO:CC_HALO + tm, :] = val_ref[...] * _sigmoid(gate_ref[...])
    first = CC_HALO - (CONV_WIDTH - 1)
    acc = jnp.zeros((tm, D_CONV), F32) + b_ref[...]
    for k in range(CONV_WIDTH):
        acc = acc + w_ref[k:k + 1, :] * buf_ref[first + k:first + k + tm, :]
    mu = jnp.mean(acc, axis=-1, keepdims=True)
    xc = acc - mu
    var = jnp.mean(xc * xc, axis=-1, keepdims=True)
    y = xc * lax.rsqrt(var + EPS) * lg_ref[...] + lb_ref[...]
    y_ref[...] = _silu(y).astype(y_ref.dtype)
    tail = buf_ref[tm:tm + CC_HALO, :]
    tail_ref[0] = tail
    buf_ref[0:CC_HALO, :] = tail


def _cconv(u, prev, w, b, lg, lb, *, bsz, t, tm):
    nt = t // tm
    row = lambda bi, ti: bi * nt + ti
    return pl.pallas_call(
        functools.partial(_cconv_kernel, tm=tm),
        out_shape=(jax.ShapeDtypeStruct((bsz * t, D_CONV), BF16),
                   jax.ShapeDtypeStruct((bsz, CC_HALO, D_CONV), F32)),
        grid=(bsz, nt),
        in_specs=[pl.BlockSpec((tm, D_CONV), lambda bi, ti: (row(bi, ti), COL_GLU // D_CONV)),
                  pl.BlockSpec((tm, D_CONV), lambda bi, ti: (row(bi, ti), COL_GLU // D_CONV + 1)),
                  pl.BlockSpec((1, CC_HALO, D_CONV), lambda bi, ti: (bi, 0, 0)),
                  pl.BlockSpec((CONV_WIDTH, D_CONV), lambda bi, ti: (0, 0)),
                  pl.BlockSpec((1, D_CONV), lambda bi, ti: (0, 0)),
                  pl.BlockSpec((1, D_CONV), lambda bi, ti: (0, 0)),
                  pl.BlockSpec((1, D_CONV), lambda bi, ti: (0, 0))],
        out_specs=(pl.BlockSpec((tm, D_CONV), lambda bi, ti: (row(bi, ti), 0)),
                   pl.BlockSpec((1, CC_HALO, D_CONV), lambda bi, ti: (bi, 0, 0))),
        scratch_shapes=[pltpu.VMEM((CC_HALO + tm, D_CONV), F32)],
        compiler_params=pltpu.CompilerParams(dimension_semantics=("arbitrary", "arbitrary"),
                                             vmem_limit_bytes=VMEM_LIMIT),
        name="conformer_conv",
    )(u, u, prev, w, b, lg, lb)


SSD_HALO = 8


def _ssd_kernel(z_ref, x0_ref, x1_ref, x2_ref, sm_ref, prevc_ref, h0_ref, cw_ref, cb_ref, dtb_ref, alog_ref,
                dx_ref, ng_ref, e_ref, y_ref, hout_ref, ctail_ref, buf_ref, ht_ref, yb_ref, *, L, nv):
    @pl.when(pl.program_id(1) == 0)
    def _():
        buf_ref[0:SSD_HALO, :] = prevc_ref[0]
        ht_ref[...] = h0_ref[0]

    if nv < L:
        buf_ref[SSD_HALO + nv:SSD_HALO + L, :] = jnp.zeros((L - nv, D_XBC), F32)
    for c, r in enumerate((x0_ref, x1_ref, x2_ref)):
        buf_ref[SSD_HALO:SSD_HALO + nv, c * 512:(c + 1) * 512] = r[...]
    first = SSD_HALO - (SSM_CONV - 1)
    acc = jnp.zeros((L, D_XBC), F32) + cb_ref[...]
    for k in range(SSM_CONV):
        acc = acc + cw_ref[k:k + 1, :] * buf_ref[first + k:first + k + L, :]
    xbc = _silu(acc)
    tail = buf_ref[nv:nv + SSD_HALO, :]
    ctail_ref[0] = tail
    buf_ref[0:SSD_HALO, :] = tail

    xs = xbc[:, :D_SSM]
    lane = lax.broadcasted_iota(I32, (L, LANES), 1)
    rowi = lax.broadcasted_iota(I32, (L, LANES), 0)
    sm = sm_ref[...]
    if nv < L:
        sm = jnp.concatenate([sm, jnp.zeros((L - nv, LANES), F32)], axis=0)
    dtr = sm + dtb_ref[...]
    dt = jnp.maximum(dtr, 0.0) + jnp.log(1.0 + jnp.exp(-jnp.abs(dtr)))
    dt = jnp.where((lane < H_SSM) & (rowi < nv), dt, 0.0)
    a = -jnp.exp(alog_ref[...])
    da = dt * a
    ri = lax.broadcasted_iota(I32, (L, L), 0)
    ci = lax.broadcasted_iota(I32, (L, L), 1)
    causal = ri >= ci
    tril = jnp.where(causal, 1.0, 0.0).astype(BF16)
    cum = _exact_dot(tril, da)
    eye = jnp.where(lax.broadcasted_iota(I32, (LANES, LANES), 0) == lax.broadcasted_iota(I32, (LANES, LANES), 1),
                    1.0, 0.0).astype(BF16)
    ch, cm, cl = _split3(cum)
    cum_t = _dot_nt(eye, ch) + _dot_nt(eye, cm) + _dot_nt(eye, cl)
    e = e_ref[...]
    ecx = _exact_dot_r(jnp.exp(cum), e)
    dtx = _exact_dot_r(dt, e)
    xdt = (xs * dtx).astype(BF16)
    edl = ecx[L - 1:L, :]
    dend_t = jnp.exp(cum_t[:, L - 1:L] - cum_t)
    lane_l = lax.broadcasted_iota(I32, (L, LANES), 1)
    lane_n = lax.broadcasted_iota(I32, (SSM_STATE, LANES), 1)
    hpg = H_SSM // SSM_GROUPS
    for g in range(SSM_GROUPS):
        bg = xbc[:, D_SSM + g * SSM_STATE:D_SSM + (g + 1) * SSM_STATE].astype(BF16)
        cg = xbc[:, D_SSM + (SSM_GROUPS + g) * SSM_STATE:D_SSM + (SSM_GROUPS + g + 1) * SSM_STATE].astype(BF16)
        cbt = _dot_nt(cg, bg)
        bg_t = _dot_nt(eye, bg)
        gcols = slice(g * hpg * SSM_HEAD_DIM, (g + 1) * hpg * SSM_HEAD_DIM)
        yoff = _dot(cg, ht_ref[:, gcols].astype(BF16)) * ecx[:, gcols]
        for p in range(hpg // 2):
            h0 = g * hpg + 2 * p
            pcols = slice(h0 * SSM_HEAD_DIM, (h0 + 2) * SSM_HEAD_DIM)
            xpair = xdt[:, pcols]
            res, st = [], []
            for hh in (h0, h0 + 1):
                seg = cum[:, hh:hh + 1] - cum_t[hh:hh + 1, :]
                dec = jnp.where(causal, jnp.exp(seg), 0.0)
                res.append(_dot((cbt * dec).astype(BF16), xpair))
                st.append(_dot((bg_t * dend_t[hh:hh + 1, :]).astype(BF16), xpair))
            yb_ref[:, pcols] = (jnp.where(lane_l < SSM_HEAD_DIM, res[0], res[1])
                                + yoff[:, 2 * p * SSM_HEAD_DIM:(2 * p + 2) * SSM_HEAD_DIM])
            ht_ref[:, pcols] = (ht_ref[:, pcols] * edl[:, pcols]
                                + jnp.where(lane_n < SSM_HEAD_DIM, st[0], st[1]))
    hout_ref[0] = ht_ref[...]
    y = yb_ref[...] + dx_ref[...] * xs
    z = z_ref[...]
    if nv < L:
        z = jnp.concatenate([z, jnp.zeros((L - nv, D_SSM), F32)], axis=0)
    y = _rms(y * _silu(z), ng_ref[...])
    y_ref[...] = y[:nv].astype(y_ref.dtype)


def _ssd(u, prevc, h0t, cw, cb, dtb, alog, dx, ng, emat, *, bsz, t, L, nv):
    nt = t // nv
    row = lambda bi, ti: bi * nt + ti
    c512 = lambda c: (lambda bi, ti: (row(bi, ti), c))
    const2 = lambda bi, ti: (0, 0)
    return pl.pallas_call(
        functools.partial(_ssd_kernel, L=L, nv=nv),
        out_shape=(jax.ShapeDtypeStruct((bsz * t, D_SSM), BF16),
                   jax.ShapeDtypeStruct((bsz, SSM_STATE, D_SSM), F32),
                   jax.ShapeDtypeStruct((bsz, SSD_HALO, D_XBC), F32)),
        grid=(bsz, nt),
        in_specs=[pl.BlockSpec((nv, D_SSM), lambda bi, ti: (row(bi, ti), 0)),
                  pl.BlockSpec((nv, 512), c512(COL_XBC // 512)),
                  pl.BlockSpec((nv, 512), c512(COL_XBC // 512 + 1)),
                  pl.BlockSpec((nv, 512), c512(COL_XBC // 512 + 2)),
                  pl.BlockSpec((nv, LANES), c512(COL_SMALL // LANES)),
                  pl.BlockSpec((1, SSD_HALO, D_XBC), lambda bi, ti: (bi, 0, 0)),
                  pl.BlockSpec((1, SSM_STATE, D_SSM), lambda bi, ti: (bi, 0, 0)),
                  pl.BlockSpec((SSM_CONV, D_XBC), const2),
                  pl.BlockSpec((1, D_XBC), const2),
                  pl.BlockSpec((1, LANES), const2),
                  pl.BlockSpec((1, LANES), const2),
                  pl.BlockSpec((1, D_SSM), const2),
                  pl.BlockSpec((1, D_SSM), const2),
                  pl.BlockSpec((LANES, D_SSM), const2)],
        out_specs=(pl.BlockSpec((nv, D_SSM), lambda bi, ti: (row(bi, ti), 0)),
                   pl.BlockSpec((1, SSM_STATE, D_SSM), lambda bi, ti: (bi, 0, 0)),
                   pl.BlockSpec((1, SSD_HALO, D_XBC), lambda bi, ti: (bi, 0, 0))),
        scratch_shapes=[pltpu.VMEM((SSD_HALO + L, D_XBC), F32),
                        pltpu.VMEM((SSM_STATE, D_SSM), F32),
                        pltpu.VMEM((L, D_SSM), F32)],
        compiler_params=pltpu.CompilerParams(dimension_semantics=("arbitrary", "arbitrary"),
                                             vmem_limit_bytes=VMEM_LIMIT),
        name="ssd_mixer",
    )(u, u, u, u, u, prevc, h0t, cw, cb, dtb, alog, dx, ng, emat)


def _flip_negative(b):
    return b ^ ((b >> 31) & INT_MAX)


def _mono_key(x):
    return _flip_negative(lax.bitcast_convert_type(x, I32))


def _key_value(k):
    return lax.bitcast_convert_type(_flip_negative(k), F32)


F32_BIG = 3e38
BF16_BIG = 3e38
BF16_ROWS = 16
COARSE_UNIT = 1 << 16
COARSE_MASK = -COARSE_UNIT
SUBNORMAL_KEY_HI = 0x00800000
SUBNORMAL_KEY_LO = -0x00810000
SEARCH_MAX_STEPS = 80


def _attn_body(*, tq, kt, nkt, topk, qf, qif, sm, kt_tile, v_tile, ki_tile, adm_fn, bias_fn,
               sc_ref, m_ref, l_ref, acc_ref, y_ref):
    nsl = kt // LANES
    lane = lax.broadcasted_iota(I32, (tq, LANES), 1)
    low = lane < HEAD_DIM
    qb = (qf * (HEAD_DIM ** -0.5)).astype(BF16)
    qib = qif.astype(BF16)
    wi = sm[:, SM_WI:SM_WI + H_IDX] * ((D_IDX ** -0.5) * (H_IDX ** -0.5))
    zero_b = jnp.zeros((tq, LANES), BF16)

    def head_window(x, h):
        win = x[:, (h // 2) * LANES:(h // 2 + 1) * LANES]
        return jnp.where(low if h % 2 == 0 else jnp.logical_not(low), win, zero_b)

    qim = [head_window(qib, h) for h in range(H_IDX)]
    wcol = [wi[:, h:h + 1] for h in range(H_IDX)]

    def p1(j, carry):
        ki = ki_tile(j)
        acc = jnp.zeros((tq, kt), F32)
        for h in range(H_IDX):
            acc = acc + jnp.maximum(_dot(qim[h], ki), 0.0) * wcol[h]
        for s in range(nsl):
            key = jnp.where(adm_fn(j, s), _mono_key(acc[:, s * LANES:(s + 1) * LANES]), INT_MIN)
            sc_ref[j, :, s * LANES:(s + 1) * LANES] = key
        return carry

    lax.fori_loop(0, nkt, p1, 0)

    def count(pred):
        def body(j, acc):
            tile = sc_ref[j]
            for s in range(nsl):
                acc = acc + jnp.where(pred(tile[:, s * LANES:(s + 1) * LANES], j, s), 1.0, 0.0)
            return acc
        acc = lax.fori_loop(0, nkt, body, jnp.zeros((tq, LANES), F32))
        return jnp.sum(acc, axis=1, keepdims=True)

    def count_ge(cand):
        cb = jnp.broadcast_to(cand, (tq, LANES))
        return count(lambda t, j, s: t >= cb)

    kf = float(topk)
    prefix = jnp.where(count_ge(jnp.zeros((tq, 1), I32)) >= kf, 0, INT_MIN).astype(I32)

    def bit_step(it, prefix):
        trial = prefix | lax.shift_left(jnp.int32(1), 30 - it)
        return jnp.where(count_ge(trial) >= kf, trial, prefix)

    thr = lax.fori_loop(0, 31, bit_step, prefix)
    thr = jnp.maximum(thr, INT_MIN + 1)
    thr_b = jnp.broadcast_to(thr, (tq, LANES))
    n_gt = count(lambda t, j, s: t > thr_b)
    n_eq = count(lambda t, j, s: t == thr_b)
    take = kf - n_gt

    def kpos(j, s):
        return j * kt + s * LANES + lane

    def tie_cut(_):
        def step(it, cut):
            trial = cut | lax.shift_left(jnp.int32(1), 30 - it)
            tb = jnp.broadcast_to(trial, (tq, LANES))
            c = count(lambda t, j, s: (t == thr_b) & (kpos(j, s) < tb))
            return jnp.where(c <= take, trial, cut)
        return lax.fori_loop(0, 31, step, jnp.zeros((tq, 1), I32))

    has_tie = jnp.max(jnp.where(n_gt + n_eq > kf, 1.0, 0.0)) > 0.0
    cut = lax.cond(has_tie, tie_cut, lambda _: jnp.full((tq, 1), INT_MAX, I32), 0)
    cut_b = jnp.broadcast_to(cut, (tq, LANES))

    qm = [head_window(qb, h) for h in range(H_ATT)]
    m_ref[...] = jnp.full(m_ref.shape, NEG_BIG, F32)
    l_ref[...] = jnp.zeros(l_ref.shape, F32)
    acc_ref[...] = jnp.zeros(acc_ref.shape, F32)

    def p3(j, carry):
        keyt = sc_ref[j]
        sel = []
        for s in range(nsl):
            ks = keyt[:, s * LANES:(s + 1) * LANES]
            sel.append((ks > thr_b) | ((ks == thr_b) & (kpos(j, s) < cut_b)))
        for p in range(H_ATT // 2):
            kp = kt_tile(j, p)
            vp = v_tile(j, p)
            for hsub in range(2):
                h = 2 * p + hsub
                s_all = _dot(qm[h], kp)
                parts = [jnp.where(sel[s], s_all[:, s * LANES:(s + 1) * LANES] + bias_fn(j, s, h), NEG_BIG)
                         for s in range(nsl)]
                mx = parts[0]
                for s in range(1, nsl):
                    mx = jnp.maximum(mx, parts[s])
                m_old = m_ref[h]
                m_new = jnp.maximum(m_old, jnp.max(mx, axis=1, keepdims=True))
                alpha = jnp.exp(m_old - m_new)
                pr = [jnp.exp(part - m_new) for part in parts]
                psum = pr[0]
                for s in range(1, nsl):
                    psum = psum + pr[s]
                l_ref[h] = alpha * l_ref[h] + psum
                pb = jnp.concatenate([x.astype(BF16) for x in pr], axis=1)
                acc_ref[h] = alpha * acc_ref[h] + _dot(pb, vp)
                m_ref[h] = m_new
        return carry

    lax.fori_loop(0, nkt, p3, 0)
    for p in range(H_ATT // 2):
        outs = []
        for hsub in range(2):
            h = 2 * p + hsub
            lsum = jnp.sum(l_ref[h], axis=1, keepdims=True)
            outs.append(acc_ref[h] / lsum)
        y_ref[:, p * LANES:(p + 1) * LANES] = jnp.where(low, outs[0], outs[1]).astype(y_ref.dtype)


N_BIAS_NEAR = 7


LOG2E = math.log2(math.e)
PV_ROWS = 256


def _attn_prompt_kernel(q_ref, qi_ref, sm_ref, k_ref, vt_ref, ki_ref, bt_ref, cf_ref, y_ref,
                        sc_ref, c16_ref, qt_ref, zb_ref, nm_ref, m_ref, l_ref, acc_ref, *, tq, kt, topk):
    i = pl.program_id(1)
    nql = tq // LANES
    nsl = kt // LANES
    nkt = lax.div((i + 1) * tq + (kt - 1), kt)

    def fold(x, op, chains=4, rows=SUBLANES):
        groups = x.shape[0] // rows
        accs = [x[a * rows:(a + 1) * rows] for a in range(chains)]
        for r in range(chains, groups):
            accs[r % chains] = op(accs[r % chains], x[r * rows:(r + 1) * rows])
        while len(accs) > 1:
            accs = [op(accs[a], accs[a + len(accs) // 2]) for a in range(len(accs) // 2)]
        return accs[0]

    eye = jnp.where(lax.broadcasted_iota(I32, (LANES, LANES), 0) == lax.broadcasted_iota(I32, (LANES, LANES), 1),
                    1.0, 0.0).astype(BF16)
    lane = lax.broadcasted_iota(I32, (tq, LANES), 1)
    low = lane < HEAD_DIM
    qb = (q_ref[...] * (HEAD_DIM ** -0.5 * LOG2E)).astype(BF16)
    qib = qi_ref[...].astype(BF16)
    zero_b = jnp.zeros((tq, LANES), BF16)
    for src, base in ((qib, 0), (qb, H_IDX)):
        for h in range(H_ATT):
            win = src[:, (h // 2) * LANES:(h // 2 + 1) * LANES]
            win = jnp.where(low if h % 2 == 0 else jnp.logical_not(low), win, zero_b)
            qt_ref[base + h] = _dot_nt(eye, win).astype(BF16)
    sh, smm, sl_ = _split3(sm_ref[...])
    sm_t = _dot_nt(eye, sh) + _dot_nt(eye, smm) + _dot_nt(eye, sl_)
    wrow = [sm_t[SM_WI + h:SM_WI + h + 1, :] * ((D_IDX ** -0.5) * (H_IDX ** -0.5)) for h in range(H_IDX)]

    qpos = i * tq + lax.broadcasted_iota(I32, (1, tq), 1)
    cend = (lax.shift_right_logical(qpos, int(math.log2(CHUNK))) + 1) * CHUNK
    krow = lax.broadcasted_iota(I32, (LANES, tq), 0)

    def p1(j, carry):
        vmax, vmin = carry
        ki2 = ki_ref[0, j]
        for h in range(H_IDX):
            t = jnp.maximum(_dot(ki2, qt_ref[h]), 0.0) * wrow[h]
            if h == 0:
                zb_ref[0] = t
            elif h < H_IDX - 1:
                zb_ref[0] += t
            else:
                for sl in range(nsl):
                    rows = slice(sl * LANES, (sl + 1) * LANES)
                    sc = zb_ref[0, rows, :] + t[rows]
                    sc = jnp.where(sc == 0.0, 0.0, sc)
                    adm = (j * nsl + sl) * LANES + krow < cend
                    key = _mono_key(sc)
                    sc_ref[j * nsl + sl] = jnp.where(adm, key, INT_MIN)
                    c16_ref[j * nsl + sl] = jnp.where(adm, _key_value(key & COARSE_MASK), -BF16_BIG).astype(BF16)
                    vmax = jnp.maximum(vmax, fold(jnp.where(adm, sc, -F32_BIG), jnp.maximum))
                    vmin = jnp.minimum(vmin, fold(jnp.where(adm, sc, F32_BIG), jnp.minimum))
        return vmax, vmin

    vmax, vmin = lax.fori_loop(0, nkt, p1, (jnp.full((SUBLANES, tq), -F32_BIG, F32),
                                            jnp.full((SUBLANES, tq), F32_BIG, F32)))

    def count(pred):
        def body(j, acc):
            for sl in range(nsl):
                g = j * nsl + sl
                acc = acc + fold(jnp.where(pred(sc_ref[g], g), 1.0, 0.0), jnp.add)
            return acc
        acc = lax.fori_loop(0, nkt, body, jnp.zeros((SUBLANES, tq), F32))
        return jnp.sum(acc, axis=0, keepdims=True)

    def count_coarse(tv):
        one, zero = jnp.ones((), BF16), jnp.zeros((), BF16)

        def body(j, acc):
            for sl in range(nsl):
                hit = jnp.where(c16_ref[j * nsl + sl] >= tv, one, zero)
                acc = acc + fold(hit, jnp.add, rows=BF16_ROWS).astype(F32)
            return acc
        acc = lax.fori_loop(0, nkt, body, jnp.zeros((BF16_ROWS, tq), F32))
        return jnp.sum(acc, axis=0, keepdims=True)

    kf = float(topk)

    def search(state, coarse):
        unit = COARSE_UNIT if coarse else 1

        def legal(t, lo, hi):
            t = jnp.minimum(jnp.maximum(t, lo + unit), hi - unit)
            if coarse:
                t = jnp.where((t > SUBNORMAL_KEY_LO) & (t < SUBNORMAL_KEY_HI), 0, t)
            return t

        def midpoint(lo, hi):
            t = (lo >> 1) + (hi >> 1) + (lo & hi & 1)
            return legal(t & COARSE_MASK if coarse else t, lo, hi)

        def active_of(lo, hi, clo):
            tm = midpoint(lo, hi)
            return (clo > kf) & (hi - unit > lo) & (tm > lo) & (tm < hi)

        def cond(st):
            it, lo, hi, clo, chi = st
            return (it < SEARCH_MAX_STEPS) & (jnp.max(jnp.where(active_of(lo, hi, clo), 1.0, 0.0)) > 0.0)

        def step(st):
            it, lo, hi, clo, chi = st
            active = active_of(lo, hi, clo)
            v_lo = _key_value(lo)
            v_hi = _key_value(hi)
            frac = (jnp.log(clo) - math.log(kf)) / (jnp.log(clo) - jnp.log(jnp.maximum(chi, 0.5)))
            t_int = _mono_key(v_lo + (v_hi - v_lo) * frac)
            t_mid = midpoint(lo, hi)
            if coarse:
                trial = legal(jnp.where(it == 0, 0, t_int) & COARSE_MASK, lo, hi)
            else:
                trial = legal(jnp.where((it == 0) & (lo == 0), 1, t_int), lo, hi)
            trial = jnp.where((lax.rem(it, 2) == 0) & (trial > lo) & (trial < hi), trial, t_mid)
            if coarse:
                c = count_coarse(_key_value(trial).astype(BF16))
            else:
                c = count(lambda t, g: t >= trial)
            up = active & (c >= kf)
            dn = active & (c < kf)
            return (it + 1, jnp.where(up, trial, lo), jnp.where(dn, trial, hi),
                    jnp.where(up, c, clo), jnp.where(dn, c, chi))

        _, lo, hi, clo, chi = lax.while_loop(cond, step, (jnp.int32(0),) + state)
        return lo, hi, clo, chi

    lo0 = _mono_key(jnp.min(vmin, axis=0, keepdims=True)) & COARSE_MASK
    hi0 = (_mono_key(jnp.max(vmax, axis=0, keepdims=True)) + COARSE_UNIT) & COARSE_MASK
    state = search((lo0, hi0, cend.astype(F32), jnp.zeros((1, tq), F32)), True)
    thr, _, n_ge, n_gt = search(state, False)
    take = kf - n_gt

    def tie_cut(_):
        def step(it, cut):
            trial = cut | lax.shift_left(jnp.int32(1), 30 - it)
            c = count(lambda t, g: (t == thr) & (g * LANES + krow < trial))
            return jnp.where(c <= take, trial, cut)
        return lax.fori_loop(0, 31, step, jnp.zeros((1, tq), I32))

    has_tie = jnp.max(jnp.where(n_ge > kf, 1.0, 0.0)) > 0.0
    cut = lax.cond(has_tie, tie_cut, lambda _: jnp.full((1, tq), INT_MAX, I32), 0)

    m_ref[...] = jnp.full(m_ref.shape, NEG_BIG, F32)
    l_ref[...] = jnp.zeros(l_ref.shape, F32)
    acc_ref[...] = jnp.zeros(acc_ref.shape, F32)

    def tile_step(j, near):
        for sl in range(nsl):
            g = j * nsl + sl
            keyt = sc_ref[g]
            sel = (keyt > thr) | ((keyt == thr) & (g * LANES + krow < cut))
            nm_ref[sl * LANES:(sl + 1) * LANES, :] = jnp.where(sel, 0.0, NEG_BIG)

        def logits(h):
            mx = None
            for sl in range(nsl):
                rows = slice(sl * LANES, (sl + 1) * LANES)
                z = _dot(k_ref[0, j, rows, (h // 2) * LANES:(h // 2 + 1) * LANES], qt_ref[H_IDX + h]) + nm_ref[rows, :]
                if near:
                    z = z + jnp.concatenate(
                        [bt_ref[jnp.clip((i * nql + hf) - (j * nsl + sl), 0, N_BIAS_NEAR - 1), h]
                         for hf in range(nql)], axis=1)
                zb_ref[h % 2, rows, :] = z
                cm = fold(z, jnp.maximum)
                mx = cm if mx is None else jnp.maximum(mx, cm)
            return mx

        def accumulate(h, mx):
            shift = 0.0 if near else cf_ref[h]
            m_old = m_ref[h]
            m_new = jnp.maximum(m_old, jnp.max(mx, axis=0, keepdims=True) + shift)
            alpha = jnp.exp2(m_old - m_new)
            msub = m_new - shift
            lsum, pv = None, None
            for c in range(kt // PV_ROWS):
                rows = slice(c * PV_ROWS, (c + 1) * PV_ROWS)
                p = jnp.exp2(zb_ref[h % 2, rows, :] - msub)
                ls = fold(p, jnp.add)
                pc = _dot(vt_ref[0, j, (h // 2) * LANES:(h // 2 + 1) * LANES, rows], p.astype(BF16))
                lsum = ls if lsum is None else lsum + ls
                pv = pc if pv is None else pv + pc
            l_ref[h] = alpha * l_ref[h] + lsum
            acc_ref[h] = alpha * acc_ref[h] + pv
            m_ref[h] = m_new

        mx_next = logits(0)
        for h in range(H_ATT):
            mx_cur = mx_next
            if h + 1 < H_ATT:
                mx_next = logits(h + 1)
            accumulate(h, mx_cur)

    n_far = jnp.minimum(lax.div(jnp.maximum(nql * i - (N_BIAS_NEAR - 1) - (nsl - 1) + nsl, 0), nsl), nkt)

    def p3_far(j, carry):
        tile_step(j, False)
        return carry

    def p3_near(j, carry):
        tile_step(j, True)
        return carry

    lax.fori_loop(0, n_far, p3_far, 0)
    lax.fori_loop(n_far, nkt, p3_near, 0)

    eye_q = jnp.where(lax.broadcasted_iota(I32, (tq, tq), 0) == lax.broadcasted_iota(I32, (tq, tq), 1),
                      1.0, 0.0).astype(BF16)
    rowd = lax.broadcasted_iota(I32, (LANES, tq), 0)
    for p in range(H_ATT // 2):
        outs = []
        for hsub in range(2):
            h = 2 * p + hsub
            outs.append(acc_ref[h] / jnp.sum(l_ref[h], axis=0, keepdims=True))
        y_t = jnp.where(rowd < HEAD_DIM, outs[0], outs[1]).astype(BF16)
        y_ref[:, p * LANES:(p + 1) * LANES] = _dot_nt(eye_q, y_t).astype(y_ref.dtype)


def _attn_prompt(u, k_tiles, vt_tiles, ki_tiles, btab, cfar, *, bsz, t, tq, kt):
    assert t % kt == 0 and t % tq == 0 and tq % LANES == 0 and kt % LANES == 0
    nq = t // tq
    nk = t // kt
    row = lambda bi, qi: bi * nq + qi
    once = pl.Buffered(1)
    return pl.pallas_call(
        functools.partial(_attn_prompt_kernel, tq=tq, kt=kt, topk=min(TOPK, t // 4)),
        out_shape=jax.ShapeDtypeStruct((bsz * t, D_ATT), BF16),
        grid=(bsz, nq),
        in_specs=[pl.BlockSpec((tq, D_ATT), lambda bi, qi: (row(bi, qi), COL_Q // D_ATT)),
                  pl.BlockSpec((tq, D_ATT), lambda bi, qi: (row(bi, qi), COL_QI // D_ATT)),
                  pl.BlockSpec((tq, LANES), lambda bi, qi: (row(bi, qi), COL_SMALL // LANES)),
                  pl.BlockSpec((1, nk, kt, D_ATT), lambda bi, qi: (bi, 0, 0, 0), pipeline_mode=once),
                  pl.BlockSpec((1, nk, D_ATT, kt), lambda bi, qi: (bi, 0, 0, 0), pipeline_mode=once),
                  pl.BlockSpec((1, nk, kt, LANES), lambda bi, qi: (bi, 0, 0, 0), pipeline_mode=once),
                  pl.BlockSpec((N_BIAS_NEAR, H_ATT, LANES, LANES), lambda bi, qi: (0, 0, 0, 0), pipeline_mode=once),
                  pl.BlockSpec((H_ATT, 1, tq), lambda bi, qi: (0, 0, 0), pipeline_mode=once)],
        out_specs=pl.BlockSpec((tq, D_ATT), lambda bi, qi: (row(bi, qi), 0)),
        scratch_shapes=[pltpu.VMEM((t // LANES, LANES, tq), I32),
                        pltpu.VMEM((t // LANES, LANES, tq), BF16),
                        pltpu.VMEM((H_IDX + H_ATT, LANES, tq), BF16),
                        pltpu.VMEM((2, kt, tq), F32),
                        pltpu.VMEM((kt, tq), F32),
                        pltpu.VMEM((H_ATT, 1, tq), F32),
                        pltpu.VMEM((H_ATT, SUBLANES, tq), F32),
                        pltpu.VMEM((H_ATT, LANES, tq), F32)],
        compiler_params=pltpu.CompilerParams(dimension_semantics=("arbitrary", "arbitrary"),
                                             vmem_limit_bytes=VMEM_LIMIT),
        name="sparse_attn_prompt",
    )(u, u, u, k_tiles, vt_tiles, ki_tiles, btab, cfar)


def _attn_sample_kernel(q_ref, qi_ref, sm_ref, kt_ref, v_ref, ki_ref, bt_ref, y_ref, sc_ref, m_ref, l_ref, acc_ref,
                        *, tq, kt, n_keys, topk):
    lane = lax.broadcasted_iota(I32, (tq, LANES), 1)

    def adm_fn(j, s):
        return (j * kt + s * LANES + lane) < n_keys

    def bias_fn(j, s, h):
        return bt_ref[0, h, :, s * LANES:(s + 1) * LANES]

    _attn_body(tq=tq, kt=kt, nkt=1, topk=topk, qf=q_ref[...], qif=qi_ref[...], sm=sm_ref[...],
               kt_tile=lambda j, p: kt_ref[0, p * LANES:(p + 1) * LANES, :],
               v_tile=lambda j, p: v_ref[0, :, p * LANES:(p + 1) * LANES],
               ki_tile=lambda j: ki_ref[0],
               adm_fn=adm_fn, bias_fn=bias_fn,
               sc_ref=sc_ref, m_ref=m_ref, l_ref=l_ref, acc_ref=acc_ref, y_ref=y_ref)


def _attn_sample(u, kt_all, v_all, ki_all, btab, *, bsz, tq, kt, n_keys):
    return pl.pallas_call(
        functools.partial(_attn_sample_kernel, tq=tq, kt=kt, n_keys=n_keys, topk=min(TOPK, n_keys // 4)),
        out_shape=jax.ShapeDtypeStruct((bsz * tq, D_ATT), BF16),
        grid=(bsz,),
        in_specs=[pl.BlockSpec((tq, D_ATT), lambda bi: (bi, COL_Q // D_ATT)),
                  pl.BlockSpec((tq, D_ATT), lambda bi: (bi, COL_QI // D_ATT)),
                  pl.BlockSpec((tq, LANES), lambda bi: (bi, COL_SMALL // LANES)),
                  pl.BlockSpec((1, D_ATT, kt), lambda bi: (bi, 0, 0)),
                  pl.BlockSpec((1, kt, D_ATT), lambda bi: (bi, 0, 0)),
                  pl.BlockSpec((1, LANES, kt), lambda bi: (bi, 0, 0)),
                  pl.BlockSpec((1, H_ATT, tq, kt), lambda bi: (0, 0, 0, 0))],
        out_specs=pl.BlockSpec((tq, D_ATT), lambda bi: (bi, 0)),
        scratch_shapes=[pltpu.VMEM((1, tq, kt), I32),
                        pltpu.VMEM((H_ATT, tq, LANES), F32),
                        pltpu.VMEM((H_ATT, tq, LANES), F32),
                        pltpu.VMEM((H_ATT, tq, LANES), F32)],
        compiler_params=pltpu.CompilerParams(dimension_semantics=("arbitrary",),
                                             vmem_limit_bytes=VMEM_LIMIT),
        name="sparse_attn_sample",
    )(u, u, u, kt_all, v_all, ki_all, btab)


def _out_proj_kernel(x_ref, ys_ref, yc_ref, ya_ref, w_ref, g_ref, o_ref):
    acc = _dot(ys_ref[...], w_ref[0:D_SSM, :])
    acc = acc + _dot(yc_ref[...], w_ref[D_SSM:D_SSM + D_CONV, :])
    acc = acc + _dot(ya_ref[...], w_ref[D_SSM + D_CONV:D_MODEL, :])
    o_ref[...] = x_ref[...] + _rms(acc, g_ref[...])


def _out_proj(x, ys, yc, ya, w, g, *, tm):
    m = x.shape[0]
    return pl.pallas_call(
        _out_proj_kernel,
        out_shape=jax.ShapeDtypeStruct((m, D_MODEL), F32),
        grid=(m // tm,),
        in_specs=[pl.BlockSpec((tm, D_MODEL), lambda i: (i, 0)),
                  pl.BlockSpec((tm, D_SSM), lambda i: (i, 0)),
                  pl.BlockSpec((tm, D_CONV), lambda i: (i, 0)),
                  pl.BlockSpec((tm, D_ATT), lambda i: (i, 0)),
                  pl.BlockSpec((D_MODEL, D_MODEL), lambda i: (0, 0)),
                  pl.BlockSpec((1, D_MODEL), lambda i: (0, 0))],
        out_specs=pl.BlockSpec((tm, D_MODEL), lambda i: (i, 0)),
        compiler_params=pltpu.CompilerParams(dimension_semantics=("arbitrary",),
                                             vmem_limit_bytes=VMEM_LIMIT),
        name="out_proj",
    )(x, ys, yc, ya, w, g)


FFN_HALO = 8


def _ffn_kernel(x_ref, gpre_ref, wg_ref, wu_ref, wd_ref, cw_ref, cb_ref, gpost_ref, p1_ref, p2_ref,
                o_ref, aux_ref, h_ref, acc_ref, buf_ref, tail_ref, *, tm, tps, seq_len, chained):
    i = pl.program_id(0)
    j = pl.program_id(1)

    @pl.when(j == 0)
    def _():
        h_ref[...] = _rms(x_ref[...], gpre_ref[...]).astype(BF16)
        acc_ref[...] = jnp.zeros(acc_ref.shape, F32)

    h = h_ref[...]
    a_pre = _dot(h, wg_ref[...])
    buf_ref[FFN_HALO:FFN_HALO + tm, :] = a_pre
    if chained:
        seq_start = lax.rem(i, tps) == 0
        buf_ref[0:FFN_HALO, :] = jnp.where(seq_start, p1_ref[0], tail_ref[j])
        prev1 = buf_ref[FFN_HALO - 1:FFN_HALO - 1 + tm, :]
        prev2 = buf_ref[FFN_HALO - 2:FFN_HALO - 2 + tm, :]
        last = a_pre[tm - FFN_HALO:tm, :]
        tail_ref[j] = last
        aux_ref[0] = last
    else:
        buf_ref[0:FFN_HALO, :] = jnp.zeros((FFN_HALO, a_pre.shape[1]), F32)
        tpos = lax.rem(lax.broadcasted_iota(I32, a_pre.shape, 0), seq_len)
        prev1 = jnp.where(tpos >= 1, buf_ref[FFN_HALO - 1:FFN_HALO - 1 + tm, :], p1_ref[...])
        prev2 = jnp.where(tpos >= 2, buf_ref[FFN_HALO - 2:FFN_HALO - 2 + tm, :], p2_ref[...])
        aux_ref[...] = a_pre
    a = cw_ref[0:1, :] * prev2 + cw_ref[1:2, :] * prev1 + cw_ref[2:3, :] * a_pre + cb_ref[...]
    f = (_silu(a) * _dot(h, wu_ref[...])).astype(BF16)
    acc_ref[...] += _dot(f, wd_ref[...])

    @pl.when(j == pl.num_programs(1) - 1)
    def _():
        o_ref[...] = x_ref[...] + _rms(acc_ref[...], gpost_ref[...])


def _ffn(x, gpre, wg, wu, wd, cw, cb, gpost, p1, p2, *, tm, tf, seq_len, chained):
    m = x.shape[0]
    nf = D_FF // tf
    tps = max(seq_len // tm, 1)
    if chained:
        nseq = m // seq_len
        p_specs = [pl.BlockSpec((1, FFN_HALO, tf), lambda i, j: (i // tps, 0, j)),
                   pl.BlockSpec((1, FFN_HALO, tf), lambda i, j: (i // tps, 0, j))]
        aux_shape = jax.ShapeDtypeStruct((m // tm, FFN_HALO, D_FF), F32)
        aux_spec = pl.BlockSpec((1, FFN_HALO, tf), lambda i, j: (i, 0, j))
    else:
        p_specs = [pl.BlockSpec((tm, tf), lambda i, j: (i, j)), pl.BlockSpec((tm, tf), lambda i, j: (i, j))]
        aux_shape = jax.ShapeDtypeStruct((m, D_FF), F32)
        aux_spec = pl.BlockSpec((tm, tf), lambda i, j: (i, j))
    return pl.pallas_call(
        functools.partial(_ffn_kernel, tm=tm, tps=tps, seq_len=seq_len, chained=chained),
        out_shape=(jax.ShapeDtypeStruct((m, D_MODEL), F32), aux_shape),
        grid=(m // tm, nf),
        in_specs=[pl.BlockSpec((tm, D_MODEL), lambda i, j: (i, 0)),
                  pl.BlockSpec((1, D_MODEL), lambda i, j: (0, 0)),
                  pl.BlockSpec((D_MODEL, tf), lambda i, j: (0, j)),
                  pl.BlockSpec((D_MODEL, tf), lambda i, j: (0, j)),
                  pl.BlockSpec((tf, D_MODEL), lambda i, j: (j, 0)),
                  pl.BlockSpec((FFN_CONV, tf), lambda i, j: (0, j)),
                  pl.BlockSpec((1, tf), lambda i, j: (0, j)),
                  pl.BlockSpec((1, D_MODEL), lambda i, j: (0, 0))] + p_specs,
        out_specs=(pl.BlockSpec((tm, D_MODEL), lambda i, j: (i, 0)), aux_spec),
        scratch_shapes=[pltpu.VMEM((tm, D_MODEL), BF16),
                        pltpu.VMEM((tm, D_MODEL), F32),
                        pltpu.VMEM((FFN_HALO + tm, tf), F32),
                        pltpu.VMEM((nf, FFN_HALO, tf), F32)],
        compiler_params=pltpu.CompilerParams(dimension_semantics=("arbitrary", "arbitrary"),
                                             vmem_limit_bytes=VMEM_LIMIT),
        name="conv_ffn",
    )(x, gpre, wg, wu, wd, cw, cb, gpost, p1, p2)


def _prep_layer_weights(w):
    w_in = w["w_in"]
    o_dt = D_SSM + D_XBC
    o_glu = o_dt + H_SSM
    o_ki = o_glu + 2 * D_CONV + 4 * D_ATT
    o_wi = o_ki + D_IDX
    pad = jnp.zeros((D_MODEL, LANES - H_SSM - H_IDX - D_IDX), w_in.dtype)
    w_r = jnp.concatenate([w_in[:, :o_dt], w_in[:, o_glu:o_ki], w_in[:, o_dt:o_glu], w_in[:, o_wi:o_wi + H_IDX],
                           pad, w_in[:, o_ki:o_wi]], axis=1).astype(BF16)
    assert w_r.shape[1] == D_PROJ
    row = lambda v: v.reshape(1, -1).astype(F32)
    padl = lambda v: jnp.pad(v.astype(F32), (0, LANES - v.shape[0])).reshape(1, LANES)
    return dict(
        w_in=w_r, g_mix_pre=row(w["g_mix_pre"]),
        ssm_conv_w=w["ssm_conv_w"].astype(F32), ssm_conv_b=row(w["ssm_conv_b"]),
        dt_bias=padl(w["ssm_dt_bias"]), a_log=padl(w["ssm_a_log"]),
        d_x=row(jnp.repeat(w["ssm_d"], SSM_HEAD_DIM)), ssm_norm_g=row(w["ssm_norm_g"]),
        cconv_w=w["cconv_w"].astype(F32), cconv_b=row(w["cconv_b"]),
        cconv_ln_g=row(w["cconv_ln_g"]), cconv_ln_b=row(w["cconv_ln_b"]),
        w_out=w["w_out"].astype(BF16), g_mix_post=row(w["g_mix_post"]), g_ffn_pre=row(w["g_ffn_pre"]),
        ffn_w_gate=w["ffn_w_gate"].astype(BF16), ffn_w_up=w["ffn_w_up"].astype(BF16),
        ffn_w_down=w["ffn_w_down"].astype(BF16), ffn_conv_w=w["ffn_conv_w"].astype(F32),
        ffn_conv_b=row(w["ffn_conv_b"]), g_ffn_post=row(w["g_ffn_post"]))


def _expand_matrix():
    e = np.zeros((LANES, D_SSM), np.float32)
    for h in range(H_SSM):
        e[h, h * SSM_HEAD_DIM:(h + 1) * SSM_HEAD_DIM] = 1.0
    return jnp.asarray(e, BF16)


def _front_pad(state, halo):
    return jnp.pad(state.astype(F32), ((0, 0), (halo - state.shape[1], 0), (0, 0)))


def _state_t(h):
    b = h.shape[0]
    return jnp.transpose(h.astype(F32), (0, 3, 1, 2)).reshape(b, SSM_STATE, D_SSM)


def _state_from_t(ht):
    b = ht.shape[0]
    return jnp.transpose(ht.reshape(b, SSM_STATE, H_SSM, SSM_HEAD_DIM), (0, 2, 3, 1))


def _mixer_common(x, lw, emat, ssm_conv_prev, ssm_h0, cconv_prev, *, bsz, t, tm_proj, tn_proj, ssd_l, ssd_nv, cc_tm):
    u = _rms_matmul(x, lw["g_mix_pre"], lw["w_in"], tm=tm_proj, tn=tn_proj)
    y_ssm, ht, ctail = _ssd(u, _front_pad(ssm_conv_prev, SSD_HALO), _state_t(ssm_h0), lw["ssm_conv_w"],
                            lw["ssm_conv_b"], lw["dt_bias"], lw["a_log"], lw["d_x"], lw["ssm_norm_g"], emat,
                            bsz=bsz, t=t, L=ssd_l, nv=ssd_nv)
    y_conv, cctail = _cconv(u, _front_pad(cconv_prev, CC_HALO), lw["cconv_w"], lw["cconv_b"], lw["cconv_ln_g"],
                            lw["cconv_ln_b"], bsz=bsz, t=t, tm=cc_tm)
    k = u[:, COL_K:COL_K + D_ATT]
    v = u[:, COL_V:COL_V + D_ATT]
    ki = u[:, COL_SMALL + SM_KI:COL_SMALL + SM_KI + D_IDX]
    states = dict(k=k.reshape(bsz, t, H_ATT, HEAD_DIM), v=v.reshape(bsz, t, H_ATT, HEAD_DIM),
                  ki=ki.reshape(bsz, t, D_IDX), h=_state_from_t(ht),
                  ssm_conv=ctail[:, SSD_HALO - (SSM_CONV - 1):], cconv=cctail[:, CC_HALO - (CONV_WIDTH - 1):])
    return u, y_ssm, y_conv, k, v, ki, states


def _layer_prompt(x, lw, emat, btab, *, bsz, t, cfg):
    zeros = lambda *s: jnp.zeros(s, F32)
    u, y_ssm, y_conv, k, v, ki, st = _mixer_common(
        x, lw, emat, zeros(bsz, SSM_CONV - 1, D_XBC), zeros(bsz, H_SSM, SSM_HEAD_DIM, SSM_STATE),
        zeros(bsz, CONV_WIDTH - 1, D_CONV), bsz=bsz, t=t, tm_proj=cfg["tm_proj"], tn_proj=cfg["tn_proj"],
        ssd_l=cfg["ssd_l"], ssd_nv=cfg["ssd_l"], cc_tm=cfg["cc_tm"])
    kt = cfg["kt"]
    nk = t // kt
    k_tiles = k.astype(BF16).reshape(bsz, nk, kt, D_ATT)
    vt_tiles = jnp.transpose(v.astype(BF16).reshape(bsz, nk, kt, D_ATT), (0, 1, 3, 2))
    kib = ki.astype(BF16).reshape(bsz, nk, kt, D_IDX)
    ki_tiles = jnp.concatenate([kib, kib], axis=3)
    tq = cfg["tq"]
    cfar = jnp.tile(btab[N_BIAS_NEAR - 1, :, 0:1, :], (1, 1, tq // LANES))
    y_att = _attn_prompt(u, k_tiles, vt_tiles, ki_tiles, btab, cfar, bsz=bsz, t=t, tq=tq, kt=kt)
    x1 = _out_proj(x, y_ssm, y_conv, y_att, lw["w_out"], lw["g_mix_post"], tm=cfg["tm_out"])
    prev = zeros(bsz, FFN_HALO, D_FF)
    x2, ftail = _ffn(x1, lw["g_ffn_pre"], lw["ffn_w_gate"], lw["ffn_w_up"], lw["ffn_w_down"], lw["ffn_conv_w"],
                     lw["ffn_conv_b"], lw["g_ffn_post"], prev, prev, tm=cfg["tm_ffn"], tf=cfg["tf"], seq_len=t,
                     chained=True)
    ftail = ftail.reshape(bsz, t // cfg["tm_ffn"], FFN_HALO, D_FF)[:, -1]
    st["ffn_conv"] = ftail[:, FFN_HALO - (FFN_CONV - 1):]
    return x2, st


def _layer_sample(x, lw, emat, btab, past_k, past_v, past_ki, ssm_conv_prev, ssm_h0, cconv_prev, fconv_prev,
                  *, bsz, t, cfg):
    m = bsz * t
    u, y_ssm, y_conv, k, v, ki, st = _mixer_common(
        x, lw, emat, ssm_conv_prev, ssm_h0, cconv_prev, bsz=bsz, t=t, tm_proj=m, tn_proj=cfg["tn_proj"],
        ssd_l=LANES, ssd_nv=t, cc_tm=t)
    past = past_k.shape[1]
    n_keys = past + t
    ktp = cfg["kt_sample"]
    padk = lambda a: jnp.pad(a, ((0, 0), (0, ktp - n_keys), (0, 0)))
    k_all = padk(jnp.concatenate([past_k.reshape(bsz, past, D_ATT), k.reshape(bsz, t, D_ATT)], axis=1).astype(BF16))
    v_all = padk(jnp.concatenate([past_v.reshape(bsz, past, D_ATT), v.reshape(bsz, t, D_ATT)], axis=1).astype(BF16))
    ki_all = padk(jnp.concatenate([past_ki, ki.reshape(bsz, t, D_IDX)], axis=1).astype(BF16))
    kt_all = jnp.transpose(k_all, (0, 2, 1))
    kit = jnp.transpose(ki_all, (0, 2, 1))
    y_att = _attn_sample(u, kt_all, v_all, jnp.concatenate([kit, kit], axis=1), btab, bsz=bsz, tq=t, kt=ktp,
                         n_keys=n_keys)
    x1 = _out_proj(x, y_ssm, y_conv, y_att, lw["w_out"], lw["g_mix_post"], tm=m)
    fprev = fconv_prev.astype(F32)
    zrow = jnp.zeros((bsz, t - 1, D_FF), F32)
    p1 = jnp.concatenate([fprev[:, 1:2], zrow], axis=1).reshape(m, D_FF)
    p2 = jnp.concatenate([fprev, zrow[:, 1:]], axis=1).reshape(m, D_FF)
    x2, a_pre = _ffn(x1, lw["g_ffn_pre"], lw["ffn_w_gate"], lw["ffn_w_up"], lw["ffn_w_down"], lw["ffn_conv_w"],
                     lw["ffn_conv_b"], lw["g_ffn_post"], p1, p2, tm=m, tf=cfg["tf"], seq_len=t, chained=False)
    st["ffn_conv"] = a_pre.reshape(bsz, t, D_FF)[:, t - (FFN_CONV - 1):]
    return x2, st


_STATE_ORDER = ("k", "v", "ki", "h", "ssm_conv", "cconv", "ffn_conv")


def _prompt_cfg(t):
    big = t >= 4096
    return dict(tm_proj=1024 if big else 256, tn_proj=640, ssd_l=256, cc_tm=256, kt=512, tq=256,
                tm_out=512 if big else 256, tm_ffn=512 if big else 256, tf=512)


def _forward(x_prompt, x_sample, cache_k, cache_v, cache_kidx, state_ssm, state_ssm_conv, state_cconv,
             state_ffn_conv, rel_bias, weights):
    bp, tp, _ = x_prompt.shape
    bs, ts, _ = x_sample.shape
    depth = weights["w_in"].shape[0]
    past = cache_k.shape[2]
    emat = _expand_matrix()
    cfg_p = _prompt_cfg(tp)
    kt_sample = -(-(past + ts) // LANES) * LANES
    cfg_s = dict(tn_proj=640, tf=512, kt_sample=kt_sample)
    rb = rel_bias.astype(F32)
    btab_p = _bias_table(rb, nd=N_BIAS_NEAR, rows=LANES, cols=LANES, off0=0, step=LANES, key_axis=0, scale=LOG2E)
    btab_s = _bias_table(rb, nd=1, rows=ts, cols=kt_sample, off0=-past, step=0)
    xp = x_prompt.reshape(bp * tp, D_MODEL)
    xs = x_sample.reshape(bs * ts, D_MODEL)
    p_states = {n: [] for n in _STATE_ORDER}
    s_states = {n: [] for n in _STATE_ORDER}
    for l in range(depth):
        lw = _prep_layer_weights({n: w[l] for n, w in weights.items()})
        xp, st_p = _layer_prompt(xp, lw, emat, btab_p, bsz=bp, t=tp, cfg=cfg_p)
        xs, st_s = _layer_sample(xs, lw, emat, btab_s, cache_k[l], cache_v[l], cache_kidx[l], state_ssm_conv[l],
                                 state_ssm[l], state_cconv[l], state_ffn_conv[l], bsz=bs, t=ts, cfg=cfg_s)
        for n in _STATE_ORDER:
            p_states[n].append(st_p[n])
            s_states[n].append(st_s[n])
    outs = [xp.reshape(bp, tp, D_MODEL), xs.reshape(bs, ts, D_MODEL)]
    outs += [jnp.stack(p_states[n]) for n in _STATE_ORDER]
    outs += [jnp.stack(s_states[n]) for n in _STATE_ORDER]
    return tuple(outs)


def kernel(x_prompt, x_sample, cache_k, cache_v, cache_kidx, state_ssm, state_ssm_conv, state_cconv, state_ffn_conv, rel_bias, g_mix_pre, w_in, ssm_conv_w, ssm_conv_b, ssm_dt_bias, ssm_a_log, ssm_d, ssm_norm_g, cconv_w, cconv_b, cconv_ln_g, cconv_ln_b, w_out, g_mix_post, g_ffn_pre, ffn_w_gate, ffn_w_up, ffn_conv_w, ffn_conv_b, ffn_w_down, g_ffn_post):
    weights = dict(g_mix_pre=g_mix_pre, w_in=w_in, ssm_conv_w=ssm_conv_w, ssm_conv_b=ssm_conv_b,
                   ssm_dt_bias=ssm_dt_bias, ssm_a_log=ssm_a_log, ssm_d=ssm_d, ssm_norm_g=ssm_norm_g,
                   cconv_w=cconv_w, cconv_b=cconv_b, cconv_ln_g=cconv_ln_g, cconv_ln_b=cconv_ln_b, w_out=w_out,
                   g_mix_post=g_mix_post, g_ffn_pre=g_ffn_pre, ffn_w_gate=ffn_w_gate, ffn_w_up=ffn_w_up,
                   ffn_conv_w=ffn_conv_w, ffn_conv_b=ffn_conv_b, ffn_w_down=ffn_w_down, g_ffn_post=g_ffn_post)
    return _forward(x_prompt, x_sample, cache_k, cache_v, cache_kidx, state_ssm, state_ssm_conv, state_cconv,
                    state_ffn_conv, rel_bias, weights)
```

```python
import functools
import math

import numpy as np
import jax
import jax.numpy as jnp
from jax import lax
from jax.experimental import pallas as pl
from jax.experimental.pallas import tpu as pltpu

F32 = jnp.float32
BF16 = jnp.bfloat16
I32 = jnp.int32

D_MODEL = 2048
D_SSM = 1024
SSM_HEAD_DIM = 64
H_SSM = 16
SSM_GROUPS = 2
SSM_STATE = 128
SSM_CONV = 4
D_XBC = D_SSM + 2 * SSM_GROUPS * SSM_STATE
D_CONV = 512
CONV_WIDTH = 31
D_ATT = 512
HEAD_DIM = 64
H_ATT = 8
H_IDX = 8
D_IDX = 64
TOPK = 256
CHUNK = 64
N_BUCKETS = 32
REL_MAX_DIST = 1024
D_FF = 5632
FFN_CONV = 3
EPS = 1e-6

LANES = 128
SUBLANES = 8

COL_Z, COL_XBC, COL_GLU, COL_Q, COL_QI = 0, 1024, 2560, 3584, 4096
D_U = 4608
PROJ_TILE = 1152
TAIL_K, TAIL_V, TAIL_SMALL = 0, 512, 1024
D_PROJ = D_U + PROJ_TILE
SM_DT, SM_WI, SM_KI = 0, 16, 64

INT_MIN = -(2 ** 31)
INT_MAX = 2 ** 31 - 1
NEG_BIG = -1e30
VMEM_LIMIT = 56 * 1024 * 1024


def _bucket_thresholds():
    nb = N_BUCKETS // 2
    max_exact = nb // 2
    n = np.arange(0, 4 * REL_MAX_DIST, dtype=np.int64)
    nf = np.maximum(n, 1).astype(np.float32)
    large = max_exact + (np.log(nf / np.float32(max_exact)) / np.float32(math.log(REL_MAX_DIST / max_exact))
                         * np.float32(nb - max_exact)).astype(np.int32)
    large = np.minimum(large, nb - 1)
    bucket = np.where(n < max_exact, n, large)
    steps = np.nonzero(np.diff(bucket))[0] + 1
    assert np.all(np.diff(bucket) >= 0) and np.all(np.diff(bucket) <= 1) and bucket[-1] == nb - 1
    return tuple(int(s) for s in steps)


BUCKET_STEPS = _bucket_thresholds()


def _sigmoid(x):
    return 1.0 / (1.0 + jnp.exp(-x))


def _silu(x):
    return x * _sigmoid(x)


def _split3(x):
    hi = x.astype(BF16)
    r1 = x - hi.astype(F32)
    mid = r1.astype(BF16)
    lo = (r1 - mid.astype(F32)).astype(BF16)
    return hi, mid, lo


def _dot(a, b):
    return jnp.dot(a, b, preferred_element_type=F32)


def _dot_nt(a, b):
    return lax.dot_general(a, b, (((1,), (1,)), ((), ())), preferred_element_type=F32)


def _exact_dot(sel_bf16, x_f32):
    hi, mid, lo = _split3(x_f32)
    return _dot(sel_bf16, hi) + _dot(sel_bf16, mid) + _dot(sel_bf16, lo)


def _exact_dot_r(x_f32, sel_bf16):
    hi, mid, lo = _split3(x_f32)
    return _dot(hi, sel_bf16) + _dot(mid, sel_bf16) + _dot(lo, sel_bf16)


def _rms(x, g):
    ms = jnp.mean(x * x, axis=-1, keepdims=True)
    return x * lax.rsqrt(ms + EPS) * g


def _in_proj_kernel(x_ref, g_ref, w_ref, u_ref, k_ref, v_ref, sm_ref, *rest, kt):
    h_ref = rest[-1]
    j = pl.program_id(1)
    last = pl.num_programs(1) - 1

    @pl.when(j == 0)
    def _():
        h_ref[...] = _rms(x_ref[...], g_ref[...]).astype(BF16)

    y = _dot(h_ref[...], w_ref[...])

    @pl.when(j < last)
    def _():
        u_ref[...] = y

    @pl.when(j == last)
    def _():
        k = y[:, TAIL_K:TAIL_K + D_ATT]
        v = y[:, TAIL_V:TAIL_V + D_ATT]
        sm = y[:, TAIL_SMALL:TAIL_SMALL + LANES]
        k_ref[...] = k
        v_ref[...] = v
        sm_ref[...] = sm
        if kt is not None:
            kb_ref, vt_ref, ki2_ref = rest[:3]
            kb_ref[...] = k.astype(BF16)
            vb = v.astype(BF16)
            eye = jnp.where(lax.broadcasted_iota(I32, (LANES, LANES), 0)
                            == lax.broadcasted_iota(I32, (LANES, LANES), 1), 1.0, 0.0).astype(BF16)
            for c in range(vb.shape[0] // kt):
                for p in range(D_ATT // LANES):
                    blk = vb[c * kt:(c + 1) * kt, p * LANES:(p + 1) * LANES]
                    vt_ref[c, p * LANES:(p + 1) * LANES, :] = _dot_nt(eye, blk).astype(BF16)
            kib = sm[:, SM_KI:SM_KI + D_IDX].astype(BF16)
            ki2_ref[...] = jnp.concatenate([kib, kib], axis=1)


def _in_proj(x, g, w, *, tm, kt=None):
    m, d = x.shape
    nj = D_PROJ // PROJ_TILE
    assert m % tm == 0 and w.shape[1] == D_PROJ and (kt is None or tm % kt == 0)
    row = lambda i, j: (i, 0)
    out_shape = [jax.ShapeDtypeStruct((m, D_U), F32), jax.ShapeDtypeStruct((m, D_ATT), F32),
                 jax.ShapeDtypeStruct((m, D_ATT), F32), jax.ShapeDtypeStruct((m, LANES), F32)]
    out_specs = [pl.BlockSpec((tm, PROJ_TILE), lambda i, j: (i, jnp.minimum(j, nj - 2))),
                 pl.BlockSpec((tm, D_ATT), row), pl.BlockSpec((tm, D_ATT), row), pl.BlockSpec((tm, LANES), row)]
    if kt is not None:
        out_shape += [jax.ShapeDtypeStruct((m, D_ATT), BF16), jax.ShapeDtypeStruct((m // kt, D_ATT, kt), BF16),
                      jax.ShapeDtypeStruct((m, LANES), BF16)]
        out_specs += [pl.BlockSpec((tm, D_ATT), row), pl.BlockSpec((tm // kt, D_ATT, kt), lambda i, j: (i, 0, 0)),
                      pl.BlockSpec((tm, LANES), row)]
    return pl.pallas_call(
        functools.partial(_in_proj_kernel, kt=kt),
        out_shape=tuple(out_shape),
        grid=(m // tm, nj),
        in_specs=[pl.BlockSpec((tm, d), lambda i, j: (i, 0)),
                  pl.BlockSpec((1, d), lambda i, j: (0, 0)),
                  pl.BlockSpec((d, PROJ_TILE), lambda i, j: (0, j))],
        out_specs=tuple(out_specs),
        scratch_shapes=[pltpu.VMEM((tm, d), BF16)],
        compiler_params=pltpu.CompilerParams(dimension_semantics=("arbitrary", "arbitrary"),
                                             vmem_limit_bytes=VMEM_LIMIT),
        name="rms_in_proj",
    )(x, g, w)


def _bias_table_kernel(rb_ref, o_ref, *, off0, step, key_axis, scale):
    d = pl.program_id(0)
    h = pl.program_id(1)
    rows, cols = o_ref.shape[2], o_ref.shape[3]
    rel = (lax.broadcasted_iota(I32, (rows, cols), key_axis) - lax.broadcasted_iota(I32, (rows, cols), 1 - key_axis)
           + (off0 - d * step))
    n = jnp.abs(rel)
    bucket = jnp.where(rel > 0, N_BUCKETS // 2, 0)
    for s in BUCKET_STEPS:
        bucket = bucket + jnp.where(n >= s, 1, 0)
    acc = jnp.zeros((rows, cols), F32)
    for b in range(N_BUCKETS):
        acc = jnp.where(bucket == b, rb_ref[b, h], acc)
    o_ref[0, 0] = acc * scale


def _bias_table(rel_bias, *, nd, rows, cols, off0, step, key_axis=1, scale=1.0):
    return pl.pallas_call(
        functools.partial(_bias_table_kernel, off0=off0, step=step, key_axis=key_axis, scale=scale),
        out_shape=jax.ShapeDtypeStruct((nd, H_ATT, rows, cols), F32),
        grid=(nd, H_ATT),
        in_specs=[pl.BlockSpec(memory_space=pltpu.SMEM)],
        out_specs=pl.BlockSpec((1, 1, rows, cols), lambda d, h: (d, h, 0, 0)),
        name="bias_table",
    )(rel_bias)


CC_HALO = 32


def _cconv_kernel(val_ref, gate_ref, prev_ref, w_ref, b_ref, lg_ref, lb_ref, y_ref, tail_ref, buf_ref, *, tm):
    @pl.when(pl.program_id(1) == 0)
    def _():
        buf_ref[0:CC_HALO, :] = prev_ref[0]

    buf_ref[CC_HALO:CC_HALO + tm, :] = val_ref[...] * _sigmoid(gate_ref[...])
    first = CC_HALO - (CONV_WIDTH - 1)
    acc = jnp.zeros((tm, D_CONV), F32) + b_ref[...]
    for k in range(CONV_WIDTH):
        acc = acc + w_ref[k:k + 1, :] * buf_ref[first + k:first + k + tm, :]
    mu = jnp.mean(acc, axis=-1, keepdims=True)
    xc = acc - mu
    var = jnp.mean(xc * xc, axis=-1, keepdims=True)
    y = xc * lax.rsqrt(var + EPS) * lg_ref[...] + lb_ref[...]
    y_ref[...] = _silu(y).astype(y_ref.dtype)
    tail = buf_ref[tm:tm + CC_HALO, :]
    tail_ref[0] = tail
    buf_ref[0:CC_HALO, :] = tail


def _cconv(u, prev, w, b, lg, lb, *, bsz, t, tm):
    nt = t // tm
    row = lambda bi, ti: bi * nt + ti
    return pl.pallas_call(
        functools.partial(_cconv_kernel, tm=tm),
        out_shape=(jax.ShapeDtypeStruct((bsz * t, D_CONV), BF16),
                   jax.ShapeDtypeStruct((bsz, CC_HALO, D_CONV), F32)),
        grid=(bsz, nt),
        in_specs=[pl.BlockSpec((tm, D_CONV), lambda bi, ti: (row(bi, ti), COL_GLU // D_CONV)),
                  pl.BlockSpec((tm, D_CONV), lambda bi, ti: (row(bi, ti), COL_GLU // D_CONV + 1)),
                  pl.BlockSpec((1, CC_HALO, D_CONV), lambda bi, ti: (bi, 0, 0)),
                  pl.BlockSpec((CONV_WIDTH, D_CONV), lambda bi, ti: (0, 0)),
                  pl.BlockSpec((1, D_CONV), lambda bi, ti: (0, 0)),
                  pl.BlockSpec((1, D_CONV), lambda bi, ti: (0, 0)),
                  pl.BlockSpec((1, D_CONV), lambda bi, ti: (0, 0))],
        out_specs=(pl.BlockSpec((tm, D_CONV), lambda bi, ti: (row(bi, ti), 0)),
                   pl.BlockSpec((1, CC_HALO, D_CONV), lambda bi, ti: (bi, 0, 0))),
        scratch_shapes=[pltpu.VMEM((CC_HALO + tm, D_CONV), F32)],
        compiler_params=pltpu.CompilerParams(dimension_semantics=("arbitrary", "arbitrary"),
                                             vmem_limit_bytes=VMEM_LIMIT),
        name="conformer_conv",
    )(u, u, prev, w, b, lg, lb)


SSD_HALO = 8


def _ssd_kernel(z_ref, x0_ref, x1_ref, x2_ref, sm_ref, prevc_ref, h0_ref, cw_ref, cb_ref, dtb_ref, alog_ref,
                dx_ref, ng_ref, e_ref, y_ref, hout_ref, ctail_ref, buf_ref, ht_ref, yb_ref, *, L, nv):
    @pl.when(pl.program_id(1) == 0)
    def _():
        buf_ref[0:SSD_HALO, :] = prevc_ref[0]
        ht_ref[...] = h0_ref[0]

    if nv < L:
        buf_ref[SSD_HALO + nv:SSD_HALO + L, :] = jnp.zeros((L - nv, D_XBC), F32)
    for c, r in enumerate((x0_ref, x1_ref, x2_ref)):
        buf_ref[SSD_HALO:SSD_HALO + nv, c * 512:(c + 1) * 512] = r[...]
    first = SSD_HALO - (SSM_CONV - 1)
    acc = jnp.zeros((L, D_XBC), F32) + cb_ref[...]
    for k in range(SSM_CONV):
        acc = acc + cw_ref[k:k + 1, :] * buf_ref[first + k:first + k + L, :]
    xbc = _silu(acc)
    tail = buf_ref[nv:nv + SSD_HALO, :]
    ctail_ref[0] = tail
    buf_ref[0:SSD_HALO, :] = tail

    xs = xbc[:, :D_SSM]
    lane = lax.broadcasted_iota(I32, (L, LANES), 1)
    rowi = lax.broadcasted_iota(I32, (L, LANES), 0)
    sm = sm_ref[...]
    if nv < L:
        sm = jnp.concatenate([sm, jnp.zeros((L - nv, LANES), F32)], axis=0)
    dtr = sm + dtb_ref[...]
    dt = jnp.maximum(dtr, 0.0) + jnp.log(1.0 + jnp.exp(-jnp.abs(dtr)))
    dt = jnp.where((lane < H_SSM) & (rowi < nv), dt, 0.0)
    a = -jnp.exp(alog_ref[...])
    da = dt * a
    ri = lax.broadcasted_iota(I32, (L, L), 0)
    ci = lax.broadcasted_iota(I32, (L, L), 1)
    causal = ri >= ci
    tril = jnp.where(causal, 1.0, 0.0).astype(BF16)
    cum = _exact_dot(tril, da)
    eye = jnp.where(lax.broadcasted_iota(I32, (LANES, LANES), 0) == lax.broadcasted_iota(I32, (LANES, LANES), 1),
                    1.0, 0.0).astype(BF16)
    ch, cm, cl = _split3(cum)
    cum_t = _dot_nt(eye, ch) + _dot_nt(eye, cm) + _dot_nt(eye, cl)
    e = e_ref[...]
    ecx = _exact_dot_r(jnp.exp(cum), e)
    dtx = _exact_dot_r(dt, e)
    xdt = (xs * dtx).astype(BF16)
    edl = ecx[L - 1:L, :]
    dend_t = jnp.exp(cum_t[:, L - 1:L] - cum_t)
    lane_l = lax.broadcasted_iota(I32, (L, LANES), 1)
    lane_n = lax.broadcasted_iota(I32, (SSM_STATE, LANES), 1)
    hpg = H_SSM // SSM_GROUPS
    for g in range(SSM_GROUPS):
        bg = xbc[:, D_SSM + g * SSM_STATE:D_SSM + (g + 1) * SSM_STATE].astype(BF16)
        cg = xbc[:, D_SSM + (SSM_GROUPS + g) * SSM_STATE:D_SSM + (SSM_GROUPS + g + 1) * SSM_STATE].astype(BF16)
        cbt = _dot_nt(cg, bg)
        bg_t = _dot_nt(eye, bg)
        gcols = slice(g * hpg * SSM_HEAD_DIM, (g + 1) * hpg * SSM_HEAD_DIM)
        yoff = _dot(cg, ht_ref[:, gcols].astype(BF16)) * ecx[:, gcols]
        for p in range(hpg // 2):
            h0 = g * hpg + 2 * p
            pcols = slice(h0 * SSM_HEAD_DIM, (h0 + 2) * SSM_HEAD_DIM)
            xpair = xdt[:, pcols]
            res, st = [], []
            for hh in (h0, h0 + 1):
                seg = cum[:, hh:hh + 1] - cum_t[hh:hh + 1, :]
                dec = jnp.where(causal, jnp.exp(seg), 0.0)
                res.append(_dot((cbt * dec).astype(BF16), xpair))
                st.append(_dot((bg_t * dend_t[hh:hh + 1, :]).astype(BF16), xpair))
            yb_ref[:, pcols] = (jnp.where(lane_l < SSM_HEAD_DIM, res[0], res[1])
                                + yoff[:, 2 * p * SSM_HEAD_DIM:(2 * p + 2) * SSM_HEAD_DIM])
            ht_ref[:, pcols] = (ht_ref[:, pcols] * edl[:, pcols]
                                + jnp.where(lane_n < SSM_HEAD_DIM, st[0], st[1]))
    hout_ref[0] = ht_ref[...]
    y = yb_ref[...] + dx_ref[...] * xs
    z = z_ref[...]
    if nv < L:
        z = jnp.concatenate([z, jnp.zeros((L - nv, D_SSM), F32)], axis=0)
    y = _rms(y * _silu(z), ng_ref[...])
    y_ref[...] = y[:nv].astype(y_ref.dtype)


def _ssd(u, small, prevc, h0t, cw, cb, dtb, alog, dx, ng, emat, *, bsz, t, L, nv):
    nt = t // nv
    row = lambda bi, ti: bi * nt + ti
    c512 = lambda c: (lambda bi, ti: (row(bi, ti), c))
    const2 = lambda bi, ti: (0, 0)
    return pl.pallas_call(
        functools.partial(_ssd_kernel, L=L, nv=nv),
        out_shape=(jax.ShapeDtypeStruct((bsz * t, D_SSM), BF16),
                   jax.ShapeDtypeStruct((bsz, SSM_STATE, D_SSM), F32),
                   jax.ShapeDtypeStruct((bsz, SSD_HALO, D_XBC), F32)),
        grid=(bsz, nt),
        in_specs=[pl.BlockSpec((nv, D_SSM), lambda bi, ti: (row(bi, ti), 0)),
                  pl.BlockSpec((nv, 512), c512(COL_XBC // 512)),
                  pl.BlockSpec((nv, 512), c512(COL_XBC // 512 + 1)),
                  pl.BlockSpec((nv, 512), c512(COL_XBC // 512 + 2)),
                  pl.BlockSpec((nv, LANES), c512(0)),
                  pl.BlockSpec((1, SSD_HALO, D_XBC), lambda bi, ti: (bi, 0, 0)),
                  pl.BlockSpec((1, SSM_STATE, D_SSM), lambda bi, ti: (bi, 0, 0)),
                  pl.BlockSpec((SSM_CONV, D_XBC), const2),
                  pl.BlockSpec((1, D_XBC), const2),
                  pl.BlockSpec((1, LANES), const2),
                  pl.BlockSpec((1, LANES), const2),
                  pl.BlockSpec((1, D_SSM), const2),
                  pl.BlockSpec((1, D_SSM), const2),
                  pl.BlockSpec((LANES, D_SSM), const2)],
        out_specs=(pl.BlockSpec((nv, D_SSM), lambda bi, ti: (row(bi, ti), 0)),
                   pl.BlockSpec((1, SSM_STATE, D_SSM), lambda bi, ti: (bi, 0, 0)),
                   pl.BlockSpec((1, SSD_HALO, D_XBC), lambda bi, ti: (bi, 0, 0))),
        scratch_shapes=[pltpu.VMEM((SSD_HALO + L, D_XBC), F32),
                        pltpu.VMEM((SSM_STATE, D_SSM), F32),
                        pltpu.VMEM((L, D_SSM), F32)],
        compiler_params=pltpu.CompilerParams(dimension_semantics=("arbitrary", "arbitrary"),
                                             vmem_limit_bytes=VMEM_LIMIT),
        name="ssd_mixer",
    )(u, u, u, u, small, prevc, h0t, cw, cb, dtb, alog, dx, ng, emat)


def _flip_negative(b):
    return b ^ ((b >> 31) & INT_MAX)


def _mono_key(x):
    return _flip_negative(lax.bitcast_convert_type(x, I32))


def _key_value(k):
    return lax.bitcast_convert_type(_flip_negative(k), F32)


F32_BIG = 3e38
SEARCH_MAX_STEPS = 80


def _attn_body(*, tq, kt, nkt, topk, qf, qif, sm, kt_tile, v_tile, ki_tile, adm_fn, bias_fn,
               sc_ref, m_ref, l_ref, acc_ref, y_ref):
    nsl = kt // LANES
    lane = lax.broadcasted_iota(I32, (tq, LANES), 1)
    low = lane < HEAD_DIM
    qb = (qf * (HEAD_DIM ** -0.5)).astype(BF16)
    qib = qif.astype(BF16)
    wi = sm[:, SM_WI:SM_WI + H_IDX] * ((D_IDX ** -0.5) * (H_IDX ** -0.5))
    zero_b = jnp.zeros((tq, LANES), BF16)

    def head_window(x, h):
        win = x[:, (h // 2) * LANES:(h // 2 + 1) * LANES]
        return jnp.where(low if h % 2 == 0 else jnp.logical_not(low), win, zero_b)

    qim = [head_window(qib, h) for h in range(H_IDX)]
    wcol = [wi[:, h:h + 1] for h in range(H_IDX)]

    def p1(j, carry):
        ki = ki_tile(j)
        acc = jnp.zeros((tq, kt), F32)
        for h in range(H_IDX):
            acc = acc + jnp.maximum(_dot(qim[h], ki), 0.0) * wcol[h]
        for s in range(nsl):
            key = jnp.where(adm_fn(j, s), _mono_key(acc[:, s * LANES:(s + 1) * LANES]), INT_MIN)
            sc_ref[j, :, s * LANES:(s + 1) * LANES] = key
        return carry

    lax.fori_loop(0, nkt, p1, 0)

    def count(pred):
        def body(j, acc):
            tile = sc_ref[j]
            for s in range(nsl):
                acc = acc + jnp.where(pred(tile[:, s * LANES:(s + 1) * LANES], j, s), 1.0, 0.0)
            return acc
        acc = lax.fori_loop(0, nkt, body, jnp.zeros((tq, LANES), F32))
        return jnp.sum(acc, axis=1, keepdims=True)

    def count_ge(cand):
        cb = jnp.broadcast_to(cand, (tq, LANES))
        return count(lambda t, j, s: t >= cb)

    kf = float(topk)
    prefix = jnp.where(count_ge(jnp.zeros((tq, 1), I32)) >= kf, 0, INT_MIN).astype(I32)

    def bit_step(it, prefix):
        trial = prefix | lax.shift_left(jnp.int32(1), 30 - it)
        return jnp.where(count_ge(trial) >= kf, trial, prefix)

    thr = lax.fori_loop(0, 31, bit_step, prefix)
    thr = jnp.maximum(thr, INT_MIN + 1)
    thr_b = jnp.broadcast_to(thr, (tq, LANES))
    n_gt = count(lambda t, j, s: t > thr_b)
    n_eq = count(lambda t, j, s: t == thr_b)
    take = kf - n_gt

    def kpos(j, s):
        return j * kt + s * LANES + lane

    def tie_cut(_):
        def step(it, cut):
            trial = cut | lax.shift_left(jnp.int32(1), 30 - it)
            tb = jnp.broadcast_to(trial, (tq, LANES))
            c = count(lambda t, j, s: (t == thr_b) & (kpos(j, s) < tb))
            return jnp.where(c <= take, trial, cut)
        return lax.fori_loop(0, 31, step, jnp.zeros((tq, 1), I32))

    has_tie = jnp.max(jnp.where(n_gt + n_eq > kf, 1.0, 0.0)) > 0.0
    cut = lax.cond(has_tie, tie_cut, lambda _: jnp.full((tq, 1), INT_MAX, I32), 0)
    cut_b = jnp.broadcast_to(cut, (tq, LANES))

    qm = [head_window(qb, h) for h in range(H_ATT)]
    m_ref[...] = jnp.full(m_ref.shape, NEG_BIG, F32)
    l_ref[...] = jnp.zeros(l_ref.shape, F32)
    acc_ref[...] = jnp.zeros(acc_ref.shape, F32)

    def p3(j, carry):
        keyt = sc_ref[j]
        sel = []
        for s in range(nsl):
            ks = keyt[:, s * LANES:(s + 1) * LANES]
            sel.append((ks > thr_b) | ((ks == thr_b) & (kpos(j, s) < cut_b)))
        for p in range(H_ATT // 2):
            kp = kt_tile(j, p)
            vp = v_tile(j, p)
            for hsub in range(2):
                h = 2 * p + hsub
                s_all = _dot(qm[h], kp)
                parts = [jnp.where(sel[s], s_all[:, s * LANES:(s + 1) * LANES] + bias_fn(j, s, h), NEG_BIG)
                         for s in range(nsl)]
                mx = parts[0]
                for s in range(1, nsl):
                    mx = jnp.maximum(mx, parts[s])
                m_old = m_ref[h]
                m_new = jnp.maximum(m_old, jnp.max(mx, axis=1, keepdims=True))
                alpha = jnp.exp(m_old - m_new)
                pr = [jnp.exp(part - m_new) for part in parts]
                psum = pr[0]
                for s in range(1, nsl):
                    psum = psum + pr[s]
                l_ref[h] = alpha * l_ref[h] + psum
                pb = jnp.concatenate([x.astype(BF16) for x in pr], axis=1)
                acc_ref[h] = alpha * acc_ref[h] + _dot(pb, vp)
                m_ref[h] = m_new
        return carry

    lax.fori_loop(0, nkt, p3, 0)
    for p in range(H_ATT // 2):
        outs = []
        for hsub in range(2):
            h = 2 * p + hsub
            lsum = jnp.sum(l_ref[h], axis=1, keepdims=True)
            outs.append(acc_ref[h] / lsum)
        y_ref[:, p * LANES:(p + 1) * LANES] = jnp.where(low, outs[0], outs[1]).astype(y_ref.dtype)


N_BIAS_NEAR = 7


LOG2E = math.log2(math.e)
PV_ROWS = 256


def _attn_prompt_kernel(q_ref, qi_ref, sm_ref, k_ref, vt_ref, ki_ref, bt_ref, cf_ref, y_ref,
                        sc_ref, qt_ref, zb_ref, nm_ref, m_ref, l_ref, acc_ref, *, tq, kt, topk):
    i = pl.program_id(1)
    nql = tq // LANES
    nsl = kt // LANES
    nkt = lax.div((i + 1) * tq + (kt - 1), kt)

    def fold(x, op, chains=4):
        groups = x.shape[0] // SUBLANES
        accs = [x[a * SUBLANES:(a + 1) * SUBLANES] for a in range(chains)]
        for r in range(chains, groups):
            accs[r % chains] = op(accs[r % chains], x[r * SUBLANES:(r + 1) * SUBLANES])
        while len(accs) > 1:
            accs = [op(accs[a], accs[a + len(accs) // 2]) for a in range(len(accs) // 2)]
        return accs[0]

    eye = jnp.where(lax.broadcasted_iota(I32, (LANES, LANES), 0) == lax.broadcasted_iota(I32, (LANES, LANES), 1),
                    1.0, 0.0).astype(BF16)
    lane = lax.broadcasted_iota(I32, (tq, LANES), 1)
    low = lane < HEAD_DIM
    qb = (q_ref[...] * (HEAD_DIM ** -0.5 * LOG2E)).astype(BF16)
    qib = qi_ref[...].astype(BF16)
    zero_b = jnp.zeros((tq, LANES), BF16)
    for src, base in ((qib, 0), (qb, H_IDX)):
        for h in range(H_ATT):
            win = src[:, (h // 2) * LANES:(h // 2 + 1) * LANES]
            win = jnp.where(low if h % 2 == 0 else jnp.logical_not(low), win, zero_b)
            qt_ref[base + h] = _dot_nt(eye, win).astype(BF16)
    sh, smm, sl_ = _split3(sm_ref[...])
    sm_t = _dot_nt(eye, sh) + _dot_nt(eye, smm) + _dot_nt(eye, sl_)
    wrow = [sm_t[SM_WI + h:SM_WI + h + 1, :] * ((D_IDX ** -0.5) * (H_IDX ** -0.5)) for h in range(H_IDX)]

    qpos = i * tq + lax.broadcasted_iota(I32, (1, tq), 1)
    cend = (lax.shift_right_logical(qpos, int(math.log2(CHUNK))) + 1) * CHUNK
    krow = lax.broadcasted_iota(I32, (LANES, tq), 0)

    def p1(j, carry):
        vmax, vmin = carry
        ki2 = ki_ref[0, j]
        for h in range(H_IDX):
            t = jnp.maximum(_dot(ki2, qt_ref[h]), 0.0) * wrow[h]
            if h == 0:
                zb_ref[0] = t
            elif h < H_IDX - 1:
                zb_ref[0] += t
            else:
                for sl in range(nsl):
                    rows = slice(sl * LANES, (sl + 1) * LANES)
                    sc = zb_ref[0, rows, :] + t[rows]
                    sc = jnp.where(sc == 0.0, 0.0, sc)
                    adm = (j * nsl + sl) * LANES + krow < cend
                    sc_ref[j * nsl + sl] = jnp.where(adm, _mono_key(sc), INT_MIN)
                    vmax = jnp.maximum(vmax, fold(jnp.where(adm, sc, -F32_BIG), jnp.maximum))
                    vmin = jnp.minimum(vmin, fold(jnp.where(adm, sc, F32_BIG), jnp.minimum))
        return vmax, vmin

    vmax, vmin = lax.fori_loop(0, nkt, p1, (jnp.full((SUBLANES, tq), -F32_BIG, F32),
                                            jnp.full((SUBLANES, tq), F32_BIG, F32)))

    def count(pred):
        def body(j, acc):
            for sl in range(nsl):
                g = j * nsl + sl
                acc = acc + fold(jnp.where(pred(sc_ref[g], g), 1.0, 0.0), jnp.add)
            return acc
        acc = lax.fori_loop(0, nkt, body, jnp.zeros((SUBLANES, tq), F32))
        return jnp.sum(acc, axis=0, keepdims=True)

    kf = float(topk)

    def active_of(lo, hi, clo):
        return (clo > kf) & (hi - 1 > lo)

    def search_cond(st):
        it, lo, hi, clo, chi = st
        return (it < SEARCH_MAX_STEPS) & (jnp.max(jnp.where(active_of(lo, hi, clo), 1.0, 0.0)) > 0.0)

    def search_step(st):
        it, lo, hi, clo, chi = st
        active = active_of(lo, hi, clo)
        v_lo = _key_value(lo)
        v_hi = _key_value(hi)
        frac = (jnp.log(clo) - math.log(kf)) / (jnp.log(clo) - jnp.log(jnp.maximum(chi, 0.5)))
        t_int = _mono_key(v_lo + (v_hi - v_lo) * frac)
        t_mid = (lo >> 1) + (hi >> 1) + (lo & hi & 1)
        trial = jnp.where(lax.rem(it, 2) == 0, t_int, t_mid)
        trial = jnp.where(it == 0, 0, jnp.where((it == 1) & (lo == 0), 1, trial))
        trial = jnp.minimum(jnp.maximum(trial, lo + 1), hi - 1)
        c = count(lambda t, g: t >= trial)
        up = active & (c >= kf)
        dn = active & (c < kf)
        return (it + 1, jnp.where(up, trial, lo), jnp.where(dn, trial, hi),
                jnp.where(up, c, clo), jnp.where(dn, c, chi))

    lo0 = _mono_key(jnp.min(vmin, axis=0, keepdims=True))
    hi0 = _mono_key(jnp.max(vmax, axis=0, keepdims=True)) + 1
    _, thr, _, n_ge, n_gt = lax.while_loop(
        search_cond, search_step,
        (jnp.int32(0), lo0, hi0, cend.astype(F32), jnp.zeros((1, tq), F32)))
    take = kf - n_gt

    def tie_cut(_):
        def step(it, cut):
            trial = cut | lax.shift_left(jnp.int32(1), 30 - it)
            c = count(lambda t, g: (t == thr) & (g * LANES + krow < trial))
            return jnp.where(c <= take, trial, cut)
        return lax.fori_loop(0, 31, step, jnp.zeros((1, tq), I32))

    has_tie = jnp.max(jnp.where(n_ge > kf, 1.0, 0.0)) > 0.0
    cut = lax.cond(has_tie, tie_cut, lambda _: jnp.full((1, tq), INT_MAX, I32), 0)

    m_ref[...] = jnp.full(m_ref.shape, NEG_BIG, F32)
    l_ref[...] = jnp.zeros(l_ref.shape, F32)
    acc_ref[...] = jnp.zeros(acc_ref.shape, F32)

    def tile_step(j, near):
        for sl in range(nsl):
            g = j * nsl + sl
            keyt = sc_ref[g]
            sel = (keyt > thr) | ((keyt == thr) & (g * LANES + krow < cut))
            nm_ref[sl * LANES:(sl + 1) * LANES, :] = jnp.where(sel, 0.0, NEG_BIG)

        def logits(h):
            mx = None
            for sl in range(nsl):
                rows = slice(sl * LANES, (sl + 1) * LANES)
                z = _dot(k_ref[0, j, rows, (h // 2) * LANES:(h // 2 + 1) * LANES], qt_ref[H_IDX + h]) + nm_ref[rows, :]
                if near:
                    z = z + jnp.concatenate(
                        [bt_ref[jnp.clip((i * nql + hf) - (j * nsl + sl), 0, N_BIAS_NEAR - 1), h]
                         for hf in range(nql)], axis=1)
                zb_ref[h % 2, rows, :] = z
                cm = fold(z, jnp.maximum)
                mx = cm if mx is None else jnp.maximum(mx, cm)
            return mx

        def accumulate(h, mx):
            shift = 0.0 if near else cf_ref[h]
            m_old = m_ref[h]
            m_new = jnp.maximum(m_old, jnp.max(mx, axis=0, keepdims=True) + shift)
            alpha = jnp.exp2(m_old - m_new)
            msub = m_new - shift
            lsum, pv = None, None
            for c in range(kt // PV_ROWS):
                rows = slice(c * PV_ROWS, (c + 1) * PV_ROWS)
                p = jnp.exp2(zb_ref[h % 2, rows, :] - msub)
                ls = fold(p, jnp.add)
                pc = _dot(vt_ref[0, j, (h // 2) * LANES:(h // 2 + 1) * LANES, rows], p.astype(BF16))
                lsum = ls if lsum is None else lsum + ls
                pv = pc if pv is None else pv + pc
            l_ref[h] = alpha * l_ref[h] + lsum
            acc_ref[h] = alpha * acc_ref[h] + pv
            m_ref[h] = m_new

        mx_next = logits(0)
        for h in range(H_ATT):
            mx_cur = mx_next
            if h + 1 < H_ATT:
                mx_next = logits(h + 1)
            accumulate(h, mx_cur)

    n_far = jnp.minimum(lax.div(jnp.maximum(nql * i - (N_BIAS_NEAR - 1) - (nsl - 1) + nsl, 0), nsl), nkt)

    def p3_far(j, carry):
        tile_step(j, False)
        return carry

    def p3_near(j, carry):
        tile_step(j, True)
        return carry

    lax.fori_loop(0, n_far, p3_far, 0)
    lax.fori_loop(n_far, nkt, p3_near, 0)

    eye_q = jnp.where(lax.broadcasted_iota(I32, (tq, tq), 0) == lax.broadcasted_iota(I32, (tq, tq), 1),
                      1.0, 0.0).astype(BF16)
    rowd = lax.broadcasted_iota(I32, (LANES, tq), 0)
    for p in range(H_ATT // 2):
        outs = []
        for hsub in range(2):
            h = 2 * p + hsub
            outs.append(acc_ref[h] / jnp.sum(l_ref[h], axis=0, keepdims=True))
        y_t = jnp.where(rowd < HEAD_DIM, outs[0], outs[1]).astype(BF16)
        y_ref[:, p * LANES:(p + 1) * LANES] = _dot_nt(eye_q, y_t).astype(y_ref.dtype)


def _attn_prompt(u, small, k_tiles, vt_tiles, ki_tiles, btab, cfar, *, bsz, t, tq, kt):
    assert t % kt == 0 and t % tq == 0 and tq % LANES == 0 and kt % LANES == 0
    nq = t // tq
    nk = t // kt
    row = lambda bi, qi: bi * nq + qi
    once = pl.Buffered(1)
    return pl.pallas_call(
        functools.partial(_attn_prompt_kernel, tq=tq, kt=kt, topk=min(TOPK, t // 4)),
        out_shape=jax.ShapeDtypeStruct((bsz * t, D_ATT), BF16),
        grid=(bsz, nq),
        in_specs=[pl.BlockSpec((tq, D_ATT), lambda bi, qi: (row(bi, qi), COL_Q // D_ATT)),
                  pl.BlockSpec((tq, D_ATT), lambda bi, qi: (row(bi, qi), COL_QI // D_ATT)),
                  pl.BlockSpec((tq, LANES), lambda bi, qi: (row(bi, qi), 0)),
                  pl.BlockSpec((1, nk, kt, D_ATT), lambda bi, qi: (bi, 0, 0, 0), pipeline_mode=once),
                  pl.BlockSpec((1, nk, D_ATT, kt), lambda bi, qi: (bi, 0, 0, 0), pipeline_mode=once),
                  pl.BlockSpec((1, nk, kt, LANES), lambda bi, qi: (bi, 0, 0, 0), pipeline_mode=once),
                  pl.BlockSpec((N_BIAS_NEAR, H_ATT, LANES, LANES), lambda bi, qi: (0, 0, 0, 0), pipeline_mode=once),
                  pl.BlockSpec((H_ATT, 1, tq), lambda bi, qi: (0, 0, 0), pipeline_mode=once)],
        out_specs=pl.BlockSpec((tq, D_ATT), lambda bi, qi: (row(bi, qi), 0)),
        scratch_shapes=[pltpu.VMEM((t // LANES, LANES, tq), I32),
                        pltpu.VMEM((H_IDX + H_ATT, LANES, tq), BF16),
                        pltpu.VMEM((2, kt, tq), F32),
                        pltpu.VMEM((kt, tq), F32),
                        pltpu.VMEM((H_ATT, 1, tq), F32),
                        pltpu.VMEM((H_ATT, SUBLANES, tq), F32),
                        pltpu.VMEM((H_ATT, LANES, tq), F32)],
        compiler_params=pltpu.CompilerParams(dimension_semantics=("arbitrary", "arbitrary"),
                                             vmem_limit_bytes=VMEM_LIMIT),
        name="sparse_attn_prompt",
    )(u, u, small, k_tiles, vt_tiles, ki_tiles, btab, cfar)


def _attn_sample_kernel(q_ref, qi_ref, sm_ref, kt_ref, v_ref, ki_ref, bt_ref, y_ref, sc_ref, m_ref, l_ref, acc_ref,
                        *, tq, kt, n_keys, topk):
    lane = lax.broadcasted_iota(I32, (tq, LANES), 1)

    def adm_fn(j, s):
        return (j * kt + s * LANES + lane) < n_keys

    def bias_fn(j, s, h):
        return bt_ref[0, h, :, s * LANES:(s + 1) * LANES]

    _attn_body(tq=tq, kt=kt, nkt=1, topk=topk, qf=q_ref[...], qif=qi_ref[...], sm=sm_ref[...],
               kt_tile=lambda j, p: kt_ref[0, p * LANES:(p + 1) * LANES, :],
               v_tile=lambda j, p: v_ref[0, :, p * LANES:(p + 1) * LANES],
               ki_tile=lambda j: ki_ref[0],
               adm_fn=adm_fn, bias_fn=bias_fn,
               sc_ref=sc_ref, m_ref=m_ref, l_ref=l_ref, acc_ref=acc_ref, y_ref=y_ref)


def _attn_sample(u, small, kt_all, v_all, ki_all, btab, *, bsz, tq, kt, n_keys):
    return pl.pallas_call(
        functools.partial(_attn_sample_kernel, tq=tq, kt=kt, n_keys=n_keys, topk=min(TOPK, n_keys // 4)),
        out_shape=jax.ShapeDtypeStruct((bsz * tq, D_ATT), BF16),
        grid=(bsz,),
        in_specs=[pl.BlockSpec((tq, D_ATT), lambda bi: (bi, COL_Q // D_ATT)),
                  pl.BlockSpec((tq, D_ATT), lambda bi: (bi, COL_QI // D_ATT)),
                  pl.BlockSpec((tq, LANES), lambda bi: (bi, 0)),
                  pl.BlockSpec((1, D_ATT, kt), lambda bi: (bi, 0, 0)),
                  pl.BlockSpec((1, kt, D_ATT), lambda bi: (bi, 0, 0)),
                  pl.BlockSpec((1, LANES, kt), lambda bi: (bi, 0, 0)),
                  pl.BlockSpec((1, H_ATT, tq, kt), lambda bi: (0, 0, 0, 0))],
        out_specs=pl.BlockSpec((tq, D_ATT), lambda bi: (bi, 0)),
        scratch_shapes=[pltpu.VMEM((1, tq, kt), I32),
                        pltpu.VMEM((H_ATT, tq, LANES), F32),
                        pltpu.VMEM((H_ATT, tq, LANES), F32),
                        pltpu.VMEM((H_ATT, tq, LANES), F32)],
        compiler_params=pltpu.CompilerParams(dimension_semantics=("arbitrary",),
                                             vmem_limit_bytes=VMEM_LIMIT),
        name="sparse_attn_sample",
    )(u, u, small, kt_all, v_all, ki_all, btab)


def _out_proj_kernel(x_ref, ys_ref, yc_ref, ya_ref, w_ref, g_ref, o_ref):
    acc = _dot(ys_ref[...], w_ref[0:D_SSM, :])
    acc = acc + _dot(yc_ref[...], w_ref[D_SSM:D_SSM + D_CONV, :])
    acc = acc + _dot(ya_ref[...], w_ref[D_SSM + D_CONV:D_MODEL, :])
    o_ref[...] = x_ref[...] + _rms(acc, g_ref[...])


def _out_proj(x, ys, yc, ya, w, g, *, tm):
    m = x.shape[0]
    return pl.pallas_call(
        _out_proj_kernel,
        out_shape=jax.ShapeDtypeStruct((m, D_MODEL), F32),
        grid=(m // tm,),
        in_specs=[pl.BlockSpec((tm, D_MODEL), lambda i: (i, 0)),
                  pl.BlockSpec((tm, D_SSM), lambda i: (i, 0)),
                  pl.BlockSpec((tm, D_CONV), lambda i: (i, 0)),
                  pl.BlockSpec((tm, D_ATT), lambda i: (i, 0)),
                  pl.BlockSpec((D_MODEL, D_MODEL), lambda i: (0, 0)),
                  pl.BlockSpec((1, D_MODEL), lambda i: (0, 0))],
        out_specs=pl.BlockSpec((tm, D_MODEL), lambda i: (i, 0)),
        compiler_params=pltpu.CompilerParams(dimension_semantics=("arbitrary",),
                                             vmem_limit_bytes=VMEM_LIMIT),
        name="out_proj",
    )(x, ys, yc, ya, w, g)


FFN_HALO = 8


def _ffn_kernel(x_ref, gpre_ref, wg_ref, wu_ref, wd_ref, cw_ref, cb_ref, gpost_ref, p1_ref, p2_ref,
                o_ref, aux_ref, h_ref, buf_ref, tail_ref, *, tm, tps, seq_len, chained):
    i = pl.program_id(0)
    j = pl.program_id(1)

    @pl.when(j == 0)
    def _():
        h_ref[...] = _rms(x_ref[...], gpre_ref[...]).astype(BF16)
        o_ref[...] = jnp.zeros(o_ref.shape, F32)

    h = h_ref[...]
    a_pre = _dot(h, wg_ref[...])
    buf_ref[FFN_HALO:FFN_HALO + tm, :] = a_pre
    if chained:
        seq_start = lax.rem(i, tps) == 0
        buf_ref[0:FFN_HALO, :] = jnp.where(seq_start, p1_ref[0], tail_ref[j])
        prev1 = buf_ref[FFN_HALO - 1:FFN_HALO - 1 + tm, :]
        prev2 = buf_ref[FFN_HALO - 2:FFN_HALO - 2 + tm, :]
        last = a_pre[tm - FFN_HALO:tm, :]
        tail_ref[j] = last
        aux_ref[0] = last
    else:
        buf_ref[0:FFN_HALO, :] = jnp.zeros((FFN_HALO, a_pre.shape[1]), F32)
        tpos = lax.rem(lax.broadcasted_iota(I32, a_pre.shape, 0), seq_len)
        prev1 = jnp.where(tpos >= 1, buf_ref[FFN_HALO - 1:FFN_HALO - 1 + tm, :], p1_ref[...])
        prev2 = jnp.where(tpos >= 2, buf_ref[FFN_HALO - 2:FFN_HALO - 2 + tm, :], p2_ref[...])
        aux_ref[...] = a_pre
    a = cw_ref[0:1, :] * prev2 + cw_ref[1:2, :] * prev1 + cw_ref[2:3, :] * a_pre + cb_ref[...]
    f = (_silu(a) * _dot(h, wu_ref[...])).astype(BF16)
    o_ref[...] += _dot(f, wd_ref[...])

    @pl.when(j == pl.num_programs(1) - 1)
    def _():
        o_ref[...] = x_ref[...] + _rms(o_ref[...], gpost_ref[...])


def _ffn(x, gpre, wg, wu, wd, cw, cb, gpost, p1, p2, *, tm, tf, seq_len, chained):
    m = x.shape[0]
    nf = D_FF // tf
    tps = max(seq_len // tm, 1)
    if chained:
        nseq = m // seq_len
        p_specs = [pl.BlockSpec((1, FFN_HALO, tf), lambda i, j: (i // tps, 0, j)),
                   pl.BlockSpec((1, FFN_HALO, tf), lambda i, j: (i // tps, 0, j))]
        aux_shape = jax.ShapeDtypeStruct((m // tm, FFN_HALO, D_FF), F32)
        aux_spec = pl.BlockSpec((1, FFN_HALO, tf), lambda i, j: (i, 0, j))
    else:
        p_specs = [pl.BlockSpec((tm, tf), lambda i, j: (i, j)), pl.BlockSpec((tm, tf), lambda i, j: (i, j))]
        aux_shape = jax.ShapeDtypeStruct((m, D_FF), F32)
        aux_spec = pl.BlockSpec((tm, tf), lambda i, j: (i, j))
    return pl.pallas_call(
        functools.partial(_ffn_kernel, tm=tm, tps=tps, seq_len=seq_len, chained=chained),
        out_shape=(jax.ShapeDtypeStruct((m, D_MODEL), F32), aux_shape),
        grid=(m // tm, nf),
        in_specs=[pl.BlockSpec((tm, D_MODEL), lambda i, j: (i, 0), pipeline_mode=pl.Buffered(1)),
                  pl.BlockSpec((1, D_MODEL), lambda i, j: (0, 0)),
                  pl.BlockSpec((D_MODEL, tf), lambda i, j: (0, j)),
                  pl.BlockSpec((D_MODEL, tf), lambda i, j: (0, j)),
                  pl.BlockSpec((tf, D_MODEL), lambda i, j: (j, 0)),
                  pl.BlockSpec((FFN_CONV, tf), lambda i, j: (0, j)),
                  pl.BlockSpec((1, tf), lambda i, j: (0, j)),
                  pl.BlockSpec((1, D_MODEL), lambda i, j: (0, 0))] + p_specs,
        out_specs=(pl.BlockSpec((tm, D_MODEL), lambda i, j: (i, 0)), aux_spec),
        scratch_shapes=[pltpu.VMEM((tm, D_MODEL), BF16),
                        pltpu.VMEM((FFN_HALO + tm, tf), F32),
                        pltpu.VMEM((nf, FFN_HALO, tf), F32)],
        compiler_params=pltpu.CompilerParams(dimension_semantics=("arbitrary", "arbitrary"),
                                             vmem_limit_bytes=VMEM_LIMIT),
        name="conv_ffn",
    )(x, gpre, wg, wu, wd, cw, cb, gpost, p1, p2)


def _prep_layer_weights(w):
    w_in = w["w_in"]
    o_dt = D_SSM + D_XBC
    o_glu = o_dt + H_SSM
    o_ki = o_glu + 2 * D_CONV + 4 * D_ATT
    o_wi = o_ki + D_IDX
    o_q = o_glu + 2 * D_CONV
    o_k = o_q + D_ATT
    o_qi = o_k + 2 * D_ATT
    pad = jnp.zeros((D_MODEL, LANES - H_SSM - H_IDX - D_IDX), w_in.dtype)
    w_r = jnp.concatenate([w_in[:, :o_dt], w_in[:, o_glu:o_k], w_in[:, o_qi:o_ki],
                           w_in[:, o_k:o_qi],
                           w_in[:, o_dt:o_glu], w_in[:, o_wi:o_wi + H_IDX], pad, w_in[:, o_ki:o_wi]],
                          axis=1).astype(BF16)
    assert w_r.shape[1] == D_PROJ and o_qi + H_IDX * D_IDX == o_ki
    row = lambda v: v.reshape(1, -1).astype(F32)
    padl = lambda v: jnp.pad(v.astype(F32), (0, LANES - v.shape[0])).reshape(1, LANES)
    return dict(
        w_in=w_r, g_mix_pre=row(w["g_mix_pre"]),
        ssm_conv_w=w["ssm_conv_w"].astype(F32), ssm_conv_b=row(w["ssm_conv_b"]),
        dt_bias=padl(w["ssm_dt_bias"]), a_log=padl(w["ssm_a_log"]),
        d_x=row(jnp.repeat(w["ssm_d"], SSM_HEAD_DIM)), ssm_norm_g=row(w["ssm_norm_g"]),
        cconv_w=w["cconv_w"].astype(F32), cconv_b=row(w["cconv_b"]),
        cconv_ln_g=row(w["cconv_ln_g"]), cconv_ln_b=row(w["cconv_ln_b"]),
        w_out=w["w_out"].astype(BF16), g_mix_post=row(w["g_mix_post"]), g_ffn_pre=row(w["g_ffn_pre"]),
        ffn_w_gate=w["ffn_w_gate"].astype(BF16), ffn_w_up=w["ffn_w_up"].astype(BF16),
        ffn_w_down=w["ffn_w_down"].astype(BF16), ffn_conv_w=w["ffn_conv_w"].astype(F32),
        ffn_conv_b=row(w["ffn_conv_b"]), g_ffn_post=row(w["g_ffn_post"]))


def _expand_matrix():
    e = np.zeros((LANES, D_SSM), np.float32)
    for h in range(H_SSM):
        e[h, h * SSM_HEAD_DIM:(h + 1) * SSM_HEAD_DIM] = 1.0
    return jnp.asarray(e, BF16)


def _front_pad(state, halo):
    return jnp.pad(state.astype(F32), ((0, 0), (halo - state.shape[1], 0), (0, 0)))


def _state_t(h):
    b = h.shape[0]
    return jnp.transpose(h.astype(F32), (0, 3, 1, 2)).reshape(b, SSM_STATE, D_SSM)


def _state_from_t(ht):
    b = ht.shape[0]
    return jnp.transpose(ht.reshape(b, SSM_STATE, H_SSM, SSM_HEAD_DIM), (0, 2, 3, 1))


def _mixer_common(x, lw, emat, ssm_conv_prev, ssm_h0, cconv_prev, *, bsz, t, tm_proj, kt, ssd_l, ssd_nv, cc_tm):
    u, k, v, small, *attn_ops = _in_proj(x, lw["g_mix_pre"], lw["w_in"], tm=tm_proj, kt=kt)
    y_ssm, ht, ctail = _ssd(u, small, _front_pad(ssm_conv_prev, SSD_HALO), _state_t(ssm_h0), lw["ssm_conv_w"],
                            lw["ssm_conv_b"], lw["dt_bias"], lw["a_log"], lw["d_x"], lw["ssm_norm_g"], emat,
                            bsz=bsz, t=t, L=ssd_l, nv=ssd_nv)
    y_conv, cctail = _cconv(u, _front_pad(cconv_prev, CC_HALO), lw["cconv_w"], lw["cconv_b"], lw["cconv_ln_g"],
                            lw["cconv_ln_b"], bsz=bsz, t=t, tm=cc_tm)
    ki = small[:, SM_KI:SM_KI + D_IDX]
    states = dict(k=k.reshape(bsz, t, H_ATT, HEAD_DIM), v=v.reshape(bsz, t, H_ATT, HEAD_DIM),
                  ki=ki.reshape(bsz, t, D_IDX), h=_state_from_t(ht),
                  ssm_conv=ctail[:, SSD_HALO - (SSM_CONV - 1):], cconv=cctail[:, CC_HALO - (CONV_WIDTH - 1):])
    return u, small, y_ssm, y_conv, k, v, ki, attn_ops, states


def _layer_prompt(x, lw, emat, btab, *, bsz, t, cfg):
    zeros = lambda *s: jnp.zeros(s, F32)
    kt = cfg["kt"]
    nk = t // kt
    u, small, y_ssm, y_conv, k, v, ki, (kb, vt, ki2), st = _mixer_common(
        x, lw, emat, zeros(bsz, SSM_CONV - 1, D_XBC), zeros(bsz, H_SSM, SSM_HEAD_DIM, SSM_STATE),
        zeros(bsz, CONV_WIDTH - 1, D_CONV), bsz=bsz, t=t, tm_proj=cfg["tm_proj"], kt=kt,
        ssd_l=cfg["ssd_l"], ssd_nv=cfg["ssd_l"], cc_tm=cfg["cc_tm"])
    tq = cfg["tq"]
    cfar = jnp.tile(btab[N_BIAS_NEAR - 1, :, 0:1, :], (1, 1, tq // LANES))
    y_att = _attn_prompt(u, small, kb.reshape(bsz, nk, kt, D_ATT), vt.reshape(bsz, nk, D_ATT, kt),
                         ki2.reshape(bsz, nk, kt, LANES), btab, cfar, bsz=bsz, t=t, tq=tq, kt=kt)
    x1 = _out_proj(x, y_ssm, y_conv, y_att, lw["w_out"], lw["g_mix_post"], tm=cfg["tm_out"])
    prev = zeros(bsz, FFN_HALO, D_FF)
    x2, ftail = _ffn(x1, lw["g_ffn_pre"], lw["ffn_w_gate"], lw["ffn_w_up"], lw["ffn_w_down"], lw["ffn_conv_w"],
                     lw["ffn_conv_b"], lw["g_ffn_post"], prev, prev, tm=cfg["tm_ffn"], tf=cfg["tf"], seq_len=t,
                     chained=True)
    ftail = ftail.reshape(bsz, t // cfg["tm_ffn"], FFN_HALO, D_FF)[:, -1]
    st["ffn_conv"] = ftail[:, FFN_HALO - (FFN_CONV - 1):]
    return x2, st


def _layer_sample(x, lw, emat, btab, past_k, past_v, past_ki, ssm_conv_prev, ssm_h0, cconv_prev, fconv_prev,
                  *, bsz, t, cfg):
    m = bsz * t
    u, small, y_ssm, y_conv, k, v, ki, _, st = _mixer_common(
        x, lw, emat, ssm_conv_prev, ssm_h0, cconv_prev, bsz=bsz, t=t, tm_proj=m, kt=None,
        ssd_l=LANES, ssd_nv=t, cc_tm=t)
    past = past_k.shape[1]
    n_keys = past + t
    ktp = cfg["kt_sample"]
    padk = lambda a: jnp.pad(a, ((0, 0), (0, ktp - n_keys), (0, 0)))
    k_all = padk(jnp.concatenate([past_k.reshape(bsz, past, D_ATT), k.reshape(bsz, t, D_ATT)], axis=1).astype(BF16))
    v_all = padk(jnp.concatenate([past_v.reshape(bsz, past, D_ATT), v.reshape(bsz, t, D_ATT)], axis=1).astype(BF16))
    ki_all = padk(jnp.concatenate([past_ki, ki.reshape(bsz, t, D_IDX)], axis=1).astype(BF16))
    kt_all = jnp.transpose(k_all, (0, 2, 1))
    kit = jnp.transpose(ki_all, (0, 2, 1))
    y_att = _attn_sample(u, small, kt_all, v_all, jnp.concatenate([kit, kit], axis=1), btab, bsz=bsz, tq=t, kt=ktp,
                         n_keys=n_keys)
    x1 = _out_proj(x, y_ssm, y_conv, y_att, lw["w_out"], lw["g_mix_post"], tm=m)
    fprev = fconv_prev.astype(F32)
    zrow = jnp.zeros((bsz, t - 1, D_FF), F32)
    p1 = jnp.concatenate([fprev[:, 1:2], zrow], axis=1).reshape(m, D_FF)
    p2 = jnp.concatenate([fprev, zrow[:, 1:]], axis=1).reshape(m, D_FF)
    x2, a_pre = _ffn(x1, lw["g_ffn_pre"], lw["ffn_w_gate"], lw["ffn_w_up"], lw["ffn_w_down"], lw["ffn_conv_w"],
                     lw["ffn_conv_b"], lw["g_ffn_post"], p1, p2, tm=m, tf=cfg["tf"], seq_len=t, chained=False)
    st["ffn_conv"] = a_pre.reshape(bsz, t, D_FF)[:, t - (FFN_CONV - 1):]
    return x2, st


_STATE_ORDER = ("k", "v", "ki", "h", "ssm_conv", "cconv", "ffn_conv")


def _prompt_cfg(t):
    big = t >= 4096
    return dict(tm_proj=512, ssd_l=256, cc_tm=256, kt=512, tq=256,
                tm_out=512 if big else 256, tm_ffn=1024 if big else 256, tf=512)


def _forward(x_prompt, x_sample, cache_k, cache_v, cache_kidx, state_ssm, state_ssm_conv, state_cconv,
             state_ffn_conv, rel_bias, weights):
    bp, tp, _ = x_prompt.shape
    bs, ts, _ = x_sample.shape
    depth = weights["w_in"].shape[0]
    past = cache_k.shape[2]
    emat = _expand_matrix()
    cfg_p = _prompt_cfg(tp)
    kt_sample = -(-(past + ts) // LANES) * LANES
    cfg_s = dict(tf=512, kt_sample=kt_sample)
    rb = rel_bias.astype(F32)
    btab_p = _bias_table(rb, nd=N_BIAS_NEAR, rows=LANES, cols=LANES, off0=0, step=LANES, key_axis=0, scale=LOG2E)
    btab_s = _bias_table(rb, nd=1, rows=ts, cols=kt_sample, off0=-past, step=0)
    xp = x_prompt.reshape(bp * tp, D_MODEL)
    xs = x_sample.reshape(bs * ts, D_MODEL)
    p_states = {n: [] for n in _STATE_ORDER}
    s_states = {n: [] for n in _STATE_ORDER}
    for l in range(depth):
        lw = _prep_layer_weights({n: w[l] for n, w in weights.items()})
        xp, st_p = _layer_prompt(xp, lw, emat, btab_p, bsz=bp, t=tp, cfg=cfg_p)
        xs, st_s = _layer_sample(xs, lw, emat, btab_s, cache_k[l], cache_v[l], cache_kidx[l], state_ssm_conv[l],
                                 state_ssm[l], state_cconv[l], state_ffn_conv[l], bsz=bs, t=ts, cfg=cfg_s)
        for n in _STATE_ORDER:
            p_states[n].append(st_p[n])
            s_states[n].append(st_s[n])
    outs = [xp.reshape(bp, tp, D_MODEL), xs.reshape(bs, ts, D_MODEL)]
    outs += [jnp.stack(p_states[n]) for n in _STATE_ORDER]
    outs += [jnp.stack(s_states[n]) for n in _STATE_ORDER]
    return tuple(outs)


def kernel(x_prompt, x_sample, cache_k, cache_v, cache_kidx, state_ssm, state_ssm_conv, state_cconv, state_ffn_conv, rel_bias, g_mix_pre, w_in, ssm_conv_w, ssm_conv_b, ssm_dt_bias, ssm_a_log, ssm_d, ssm_norm_g, cconv_w, cconv_b, cconv_ln_g, cconv_ln_b, w_out, g_mix_post, g_ffn_pre, ffn_w_gate, ffn_w_up, ffn_conv_w, ffn_conv_b, ffn_w_down, g_ffn_post):
    weights = dict(g_mix_pre=g_mix_pre, w_in=w_in, ssm_conv_w=ssm_conv_w, ssm_conv_b=ssm_conv_b,
                   ssm_dt_bias=ssm_dt_bias, ssm_a_log=ssm_a_log, ssm_d=ssm_d, ssm_norm_g=ssm_norm_g,
                   cconv_w=cconv_w, cconv_b=cconv_b, cconv_ln_g=cconv_ln_g, cconv_ln_b=cconv_ln_b, w_out=w_out,
                   g_mix_post=g_mix_post, g_ffn_pre=g_ffn_pre, ffn_w_gate=ffn_w_gate, ffn_w_up=ffn_w_up,
                   ffn_conv_w=ffn_conv_w, ffn_conv_b=ffn_conv_b, ffn_w_down=ffn_w_down, g_ffn_post=g_ffn_post)
    return _forward(x_prompt, x_sample, cache_k, cache_v, cache_kidx, state_ssm, state_ssm_conv, state_cconv,
                    state_ffn_conv, rel_bias, weights)
```

```python
import functools
import math

import numpy as np
import jax
import jax.numpy as jnp
from jax import lax
from jax.experimental import pallas as pl
from jax.experimental.pallas import tpu as pltpu

F32 = jnp.float32
BF16 = jnp.bfloat16
I32 = jnp.int32

D_MODEL = 2048
D_SSM = 1024
SSM_HEAD_DIM = 64
H_SSM = 16
SSM_GROUPS = 2
SSM_STATE = 128
SSM_CONV = 4
D_XBC = D_SSM + 2 * SSM_GROUPS * SSM_STATE
D_CONV = 512
CONV_WIDTH = 31
D_ATT = 512
HEAD_DIM = 64
H_ATT = 8
H_IDX = 8
D_IDX = 64
TOPK = 256
CHUNK = 64
N_BUCKETS = 32
REL_MAX_DIST = 1024
D_FF = 5632
FFN_CONV = 3
EPS = 1e-6

LANES = 128
SUBLANES = 8

COL_Z, COL_XBC, COL_GLU, COL_Q, COL_QI = 0, 1024, 2560, 3584, 4096
D_U = 4608
PROJ_TILE = 1152
TAIL_K, TAIL_V, TAIL_SMALL = 0, 512, 1024
D_PROJ = D_U + PROJ_TILE
SM_DT, SM_WI, SM_KI = 0, 16, 64

INT_MIN = -(2 ** 31)
INT_MAX = 2 ** 31 - 1
NEG_BIG = -1e30
VMEM_LIMIT = 56 * 1024 * 1024


def _bucket_thresholds():
    nb = N_BUCKETS // 2
    max_exact = nb // 2
    n = np.arange(0, 4 * REL_MAX_DIST, dtype=np.int64)
    nf = np.maximum(n, 1).astype(np.float32)
    large = max_exact + (np.log(nf / np.float32(max_exact)) / np.float32(math.log(REL_MAX_DIST / max_exact))
                         * np.float32(nb - max_exact)).astype(np.int32)
    large = np.minimum(large, nb - 1)
    bucket = np.where(n < max_exact, n, large)
    steps = np.nonzero(np.diff(bucket))[0] + 1
    assert np.all(np.diff(bucket) >= 0) and np.all(np.diff(bucket) <= 1) and bucket[-1] == nb - 1
    return tuple(int(s) for s in steps)


BUCKET_STEPS = _bucket_thresholds()


def _sigmoid(x):
    return 1.0 / (1.0 + jnp.exp(-x))


def _silu(x):
    return x * _sigmoid(x)


def _split3(x):
    hi = x.astype(BF16)
    r1 = x - hi.astype(F32)
    mid = r1.astype(BF16)
    lo = (r1 - mid.astype(F32)).astype(BF16)
    return hi, mid, lo


def _dot(a, b):
    return jnp.dot(a, b, preferred_element_type=F32)


def _dot_nt(a, b):
    return lax.dot_general(a, b, (((1,), (1,)), ((), ())), preferred_element_type=F32)


def _exact_dot(sel_bf16, x_f32):
    hi, mid, lo = _split3(x_f32)
    return _dot(sel_bf16, hi) + _dot(sel_bf16, mid) + _dot(sel_bf16, lo)


def _exact_dot_r(x_f32, sel_bf16):
    hi, mid, lo = _split3(x_f32)
    return _dot(hi, sel_bf16) + _dot(mid, sel_bf16) + _dot(lo, sel_bf16)


def _rms(x, g):
    ms = jnp.mean(x * x, axis=-1, keepdims=True)
    return x * lax.rsqrt(ms + EPS) * g


def _in_proj_kernel(x_ref, g_ref, w_ref, u_ref, k_ref, v_ref, sm_ref, *rest, kt):
    h_ref = rest[-1]
    j = pl.program_id(1)
    last = pl.num_programs(1) - 1

    @pl.when(j == 0)
    def _():
        h_ref[...] = _rms(x_ref[...], g_ref[...]).astype(BF16)

    y = _dot(h_ref[...], w_ref[...])

    @pl.when(j < last)
    def _():
        u_ref[...] = y

    @pl.when(j == last)
    def _():
        k = y[:, TAIL_K:TAIL_K + D_ATT]
        v = y[:, TAIL_V:TAIL_V + D_ATT]
        sm = y[:, TAIL_SMALL:TAIL_SMALL + LANES]
        k_ref[...] = k
        v_ref[...] = v
        sm_ref[...] = sm
        if kt is not None:
            kb_ref, vt_ref, ki2_ref = rest[:3]
            kb_ref[...] = k.astype(BF16)
            vb = v.astype(BF16)
            eye = jnp.where(lax.broadcasted_iota(I32, (LANES, LANES), 0)
                            == lax.broadcasted_iota(I32, (LANES, LANES), 1), 1.0, 0.0).astype(BF16)
            for c in range(vb.shape[0] // kt):
                for p in range(D_ATT // LANES):
                    blk = vb[c * kt:(c + 1) * kt, p * LANES:(p + 1) * LANES]
                    vt_ref[c, p * LANES:(p + 1) * LANES, :] = _dot_nt(eye, blk).astype(BF16)
            kib = sm[:, SM_KI:SM_KI + D_IDX].astype(BF16)
            ki2_ref[...] = jnp.concatenate([kib, kib], axis=1)


def _in_proj(x, g, w, *, tm, kt=None):
    m, d = x.shape
    nj = D_PROJ // PROJ_TILE
    assert m % tm == 0 and w.shape[1] == D_PROJ and (kt is None or tm % kt == 0)
    row = lambda i, j: (i, 0)
    out_shape = [jax.ShapeDtypeStruct((m, D_U), F32), jax.ShapeDtypeStruct((m, D_ATT), F32),
                 jax.ShapeDtypeStruct((m, D_ATT), F32), jax.ShapeDtypeStruct((m, LANES), F32)]
    out_specs = [pl.BlockSpec((tm, PROJ_TILE), lambda i, j: (i, jnp.minimum(j, nj - 2))),
                 pl.BlockSpec((tm, D_ATT), row), pl.BlockSpec((tm, D_ATT), row), pl.BlockSpec((tm, LANES), row)]
    if kt is not None:
        out_shape += [jax.ShapeDtypeStruct((m, D_ATT), BF16), jax.ShapeDtypeStruct((m // kt, D_ATT, kt), BF16),
                      jax.ShapeDtypeStruct((m, LANES), BF16)]
        out_specs += [pl.BlockSpec((tm, D_ATT), row), pl.BlockSpec((tm // kt, D_ATT, kt), lambda i, j: (i, 0, 0)),
                      pl.BlockSpec((tm, LANES), row)]
    return pl.pallas_call(
        functools.partial(_in_proj_kernel, kt=kt),
        out_shape=tuple(out_shape),
        grid=(m // tm, nj),
        in_specs=[pl.BlockSpec((tm, d), lambda i, j: (i, 0)),
                  pl.BlockSpec((1, d), lambda i, j: (0, 0)),
                  pl.BlockSpec((d, PROJ_TILE), lambda i, j: (0, j))],
        out_specs=tuple(out_specs),
        scratch_shapes=[pltpu.VMEM((tm, d), BF16)],
        compiler_params=pltpu.CompilerParams(dimension_semantics=("arbitrary", "arbitrary"),
                                             vmem_limit_bytes=VMEM_LIMIT),
        name="rms_in_proj",
    )(x, g, w)


def _bias_table_kernel(rb_ref, o_ref, *, off0, step, key_axis, scale):
    d = pl.program_id(0)
    h = pl.program_id(1)
    rows, cols = o_ref.shape[2], o_ref.shape[3]
    rel = (lax.broadcasted_iota(I32, (rows, cols), key_axis) - lax.broadcasted_iota(I32, (rows, cols), 1 - key_axis)
           + (off0 - d * step))
    n = jnp.abs(rel)
    bucket = jnp.where(rel > 0, N_BUCKETS // 2, 0)
    for s in BUCKET_STEPS:
        bucket = bucket + jnp.where(n >= s, 1, 0)
    acc = jnp.zeros((rows, cols), F32)
    for b in range(N_BUCKETS):
        acc = jnp.where(bucket == b, rb_ref[b, h], acc)
    o_ref[0, 0] = acc * scale


def _bias_table(rel_bias, *, nd, rows, cols, off0, step, key_axis=1, scale=1.0):
    return pl.pallas_call(
        functools.partial(_bias_table_kernel, off0=off0, step=step, key_axis=key_axis, scale=scale),
        out_shape=jax.ShapeDtypeStruct((nd, H_ATT, rows, cols), F32),
        grid=(nd, H_ATT),
        in_specs=[pl.BlockSpec(memory_space=pltpu.SMEM)],
        out_specs=pl.BlockSpec((1, 1, rows, cols), lambda d, h: (d, h, 0, 0)),
        name="bias_table",
    )(rel_bias)


CC_HALO = 32


def _cconv_kernel(val_ref, gate_ref, prev_ref, w_ref, b_ref, lg_ref, lb_ref, y_ref, tail_ref, buf_ref, *, tm):
    @pl.when(pl.program_id(1) == 0)
    def _():
        buf_ref[0:CC_HALO, :] = prev_ref[0]

    buf_ref[CC_HALO:CC_HALO + tm, :] = val_ref[...] * _sigmoid(gate_ref[...])
    first = CC_HALO - (CONV_WIDTH - 1)
    acc = jnp.zeros((tm, D_CONV), F32) + b_ref[...]
    for k in range(CONV_WIDTH):
        acc = acc + w_ref[k:k + 1, :] * buf_ref[first + k:first + k + tm, :]
    mu = jnp.mean(acc, axis=-1, keepdims=True)
    xc = acc - mu
    var = jnp.mean(xc * xc, axis=-1, keepdims=True)
    y = xc * lax.rsqrt(var + EPS) * lg_ref[...] + lb_ref[...]
    y_ref[...] = _silu(y).astype(y_ref.dtype)
    tail = buf_ref[tm:tm + CC_HALO, :]
    tail_ref[0] = tail
    buf_ref[0:CC_HALO, :] = tail


def _cconv(u, prev, w, b, lg, lb, *, bsz, t, tm):
    nt = t // tm
    row = lambda bi, ti: bi * nt + ti
    return pl.pallas_call(
        functools.partial(_cconv_kernel, tm=tm),
        out_shape=(jax.ShapeDtypeStruct((bsz * t, D_CONV), BF16),
                   jax.ShapeDtypeStruct((bsz, CC_HALO, D_CONV), F32)),
        grid=(bsz, nt),
        in_specs=[pl.BlockSpec((tm, D_CONV), lambda bi, ti: (row(bi, ti), COL_GLU // D_CONV)),
                  pl.BlockSpec((tm, D_CONV), lambda bi, ti: (row(bi, ti), COL_GLU // D_CONV + 1)),
                  pl.BlockSpec((1, CC_HALO, D_CONV), lambda bi, ti: (bi, 0, 0)),
                  pl.BlockSpec((CONV_WIDTH, D_CONV), lambda bi, ti: (0, 0)),
                  pl.BlockSpec((1, D_CONV), lambda bi, ti: (0, 0)),
                  pl.BlockSpec((1, D_CONV), lambda bi, ti: (0, 0)),
                  pl.BlockSpec((1, D_CONV), lambda bi, ti: (0, 0))],
        out_specs=(pl.BlockSpec((tm, D_CONV), lambda bi, ti: (row(bi, ti), 0)),
                   pl.BlockSpec((1, CC_HALO, D_CONV), lambda bi, ti: (bi, 0, 0))),
        scratch_shapes=[pltpu.VMEM((CC_HALO + tm, D_CONV), F32)],
        compiler_params=pltpu.CompilerParams(dimension_semantics=("arbitrary", "arbitrary"),
                                             vmem_limit_bytes=VMEM_LIMIT),
        name="conformer_conv",
    )(u, u, prev, w, b, lg, lb)


SSD_HALO = 8


def _ssd_kernel(z_ref, x0_ref, x1_ref, x2_ref, sm_ref, prevc_ref, h0_ref, cw_ref, cb_ref, dtb_ref, alog_ref,
                dx_ref, ng_ref, e_ref, y_ref, hout_ref, ctail_ref, buf_ref, ht_ref, yb_ref, *, L, nv):
    @pl.when(pl.program_id(1) == 0)
    def _():
        buf_ref[0:SSD_HALO, :] = prevc_ref[0]
        ht_ref[...] = h0_ref[0]

    if nv < L:
        buf_ref[SSD_HALO + nv:SSD_HALO + L, :] = jnp.zeros((L - nv, D_XBC), F32)
    for c, r in enumerate((x0_ref, x1_ref, x2_ref)):
        buf_ref[SSD_HALO:SSD_HALO + nv, c * 512:(c + 1) * 512] = r[...]
    first = SSD_HALO - (SSM_CONV - 1)
    acc = jnp.zeros((L, D_XBC), F32) + cb_ref[...]
    for k in range(SSM_CONV):
        acc = acc + cw_ref[k:k + 1, :] * buf_ref[first + k:first + k + L, :]
    xbc = _silu(acc)
    tail = buf_ref[nv:nv + SSD_HALO, :]
    ctail_ref[0] = tail
    buf_ref[0:SSD_HALO, :] = tail

    xs = xbc[:, :D_SSM]
    lane = lax.broadcasted_iota(I32, (L, LANES), 1)
    rowi = lax.broadcasted_iota(I32, (L, LANES), 0)
    sm = sm_ref[...]
    if nv < L:
        sm = jnp.concatenate([sm, jnp.zeros((L - nv, LANES), F32)], axis=0)
    dtr = sm + dtb_ref[...]
    dt = jnp.maximum(dtr, 0.0) + jnp.log(1.0 + jnp.exp(-jnp.abs(dtr)))
    dt = jnp.where((lane < H_SSM) & (rowi < nv), dt, 0.0)
    a = -jnp.exp(alog_ref[...])
    da = dt * a
    ri = lax.broadcasted_iota(I32, (L, L), 0)
    ci = lax.broadcasted_iota(I32, (L, L), 1)
    causal = ri >= ci
    tril = jnp.where(causal, 1.0, 0.0).astype(BF16)
    cum = _exact_dot(tril, da)
    eye = jnp.where(lax.broadcasted_iota(I32, (LANES, LANES), 0) == lax.broadcasted_iota(I32, (LANES, LANES), 1),
                    1.0, 0.0).astype(BF16)
    ch, cm, cl = _split3(cum)
    cum_t = _dot_nt(eye, ch) + _dot_nt(eye, cm) + _dot_nt(eye, cl)
    e = e_ref[...]
    ecx = _exact_dot_r(jnp.exp(cum), e)
    dtx = _exact_dot_r(dt, e)
    xdt = (xs * dtx).astype(BF16)
    edl = ecx[L - 1:L, :]
    dend_t = jnp.exp(cum_t[:, L - 1:L] - cum_t)
    lane_l = lax.broadcasted_iota(I32, (L, LANES), 1)
    lane_n = lax.broadcasted_iota(I32, (SSM_STATE, LANES), 1)
    hpg = H_SSM // SSM_GROUPS
    for g in range(SSM_GROUPS):
        bg = xbc[:, D_SSM + g * SSM_STATE:D_SSM + (g + 1) * SSM_STATE].astype(BF16)
        cg = xbc[:, D_SSM + (SSM_GROUPS + g) * SSM_STATE:D_SSM + (SSM_GROUPS + g + 1) * SSM_STATE].astype(BF16)
        cbt = _dot_nt(cg, bg)
        bg_t = _dot_nt(eye, bg)
        gcols = slice(g * hpg * SSM_HEAD_DIM, (g + 1) * hpg * SSM_HEAD_DIM)
        yoff = _dot(cg, ht_ref[:, gcols].astype(BF16)) * ecx[:, gcols]
        for p in range(hpg // 2):
            h0 = g * hpg + 2 * p
            pcols = slice(h0 * SSM_HEAD_DIM, (h0 + 2) * SSM_HEAD_DIM)
            xpair = xdt[:, pcols]
            res, st = [], []
            for hh in (h0, h0 + 1):
                seg = cum[:, hh:hh + 1] - cum_t[hh:hh + 1, :]
                dec = jnp.where(causal, jnp.exp(seg), 0.0)
                res.append(_dot((cbt * dec).astype(BF16), xpair))
                st.append(_dot((bg_t * dend_t[hh:hh + 1, :]).astype(BF16), xpair))
            yb_ref[:, pcols] = (jnp.where(lane_l < SSM_HEAD_DIM, res[0], res[1])
                                + yoff[:, 2 * p * SSM_HEAD_DIM:(2 * p + 2) * SSM_HEAD_DIM])
            ht_ref[:, pcols] = (ht_ref[:, pcols] * edl[:, pcols]
                                + jnp.where(lane_n < SSM_HEAD_DIM, st[0], st[1]))
    hout_ref[0] = ht_ref[...]
    y = yb_ref[...] + dx_ref[...] * xs
    z = z_ref[...]
    if nv < L:
        z = jnp.concatenate([z, jnp.zeros((L - nv, D_SSM), F32)], axis=0)
    y = _rms(y * _silu(z), ng_ref[...])
    y_ref[...] = y[:nv].astype(y_ref.dtype)


def _ssd(u, small, prevc, h0t, cw, cb, dtb, alog, dx, ng, emat, *, bsz, t, L, nv):
    nt = t // nv
    row = lambda bi, ti: bi * nt + ti
    c512 = lambda c: (lambda bi, ti: (row(bi, ti), c))
    const2 = lambda bi, ti: (0, 0)
    return pl.pallas_call(
        functools.partial(_ssd_kernel, L=L, nv=nv),
        out_shape=(jax.ShapeDtypeStruct((bsz * t, D_SSM), BF16),
                   jax.ShapeDtypeStruct((bsz, SSM_STATE, D_SSM), F32),
                   jax.ShapeDtypeStruct((bsz, SSD_HALO, D_XBC), F32)),
        grid=(bsz, nt),
        in_specs=[pl.BlockSpec((nv, D_SSM), lambda bi, ti: (row(bi, ti), 0)),
                  pl.BlockSpec((nv, 512), c512(COL_XBC // 512)),
                  pl.BlockSpec((nv, 512), c512(COL_XBC // 512 + 1)),
                  pl.BlockSpec((nv, 512), c512(COL_XBC // 512 + 2)),
                  pl.BlockSpec((nv, LANES), c512(0)),
                  pl.BlockSpec((1, SSD_HALO, D_XBC), lambda bi, ti: (bi, 0, 0)),
                  pl.BlockSpec((1, SSM_STATE, D_SSM), lambda bi, ti: (bi, 0, 0)),
                  pl.BlockSpec((SSM_CONV, D_XBC), const2),
                  pl.BlockSpec((1, D_XBC), const2),
                  pl.BlockSpec((1, LANES), const2),
                  pl.BlockSpec((1, LANES), const2),
                  pl.BlockSpec((1, D_SSM), const2),
                  pl.BlockSpec((1, D_SSM), const2),
                  pl.BlockSpec((LANES, D_SSM), const2)],
        out_specs=(pl.BlockSpec((nv, D_SSM), lambda bi, ti: (row(bi, ti), 0)),
                   pl.BlockSpec((1, SSM_STATE, D_SSM), lambda bi, ti: (bi, 0, 0)),
                   pl.BlockSpec((1, SSD_HALO, D_XBC), lambda bi, ti: (bi, 0, 0))),
        scratch_shapes=[pltpu.VMEM((SSD_HALO + L, D_XBC), F32),
                        pltpu.VMEM((SSM_STATE, D_SSM), F32),
                        pltpu.VMEM((L, D_SSM), F32)],
        compiler_params=pltpu.CompilerParams(dimension_semantics=("arbitrary", "arbitrary"),
                                             vmem_limit_bytes=VMEM_LIMIT),
        name="ssd_mixer",
    )(u, u, u, u, small, prevc, h0t, cw, cb, dtb, alog, dx, ng, emat)


def _flip_negative(b):
    return b ^ ((b >> 31) & INT_MAX)


def _mono_key(x):
    return _flip_negative(lax.bitcast_convert_type(x, I32))


def _key_value(k):
    return lax.bitcast_convert_type(_flip_negative(k), F32)


F32_BIG = 3e38
MID_PERIOD = 12
BOOST_MAX = 2.0 ** 30
SEARCH_MAX_STEPS = 400


def _attn_body(*, tq, kt, nkt, topk, qf, qif, sm, kt_tile, v_tile, ki_tile, adm_fn, bias_fn,
               sc_ref, m_ref, l_ref, acc_ref, y_ref):
    nsl = kt // LANES
    lane = lax.broadcasted_iota(I32, (tq, LANES), 1)
    low = lane < HEAD_DIM
    qb = (qf * (HEAD_DIM ** -0.5)).astype(BF16)
    qib = qif.astype(BF16)
    wi = sm[:, SM_WI:SM_WI + H_IDX] * ((D_IDX ** -0.5) * (H_IDX ** -0.5))
    zero_b = jnp.zeros((tq, LANES), BF16)

    def head_window(x, h):
        win = x[:, (h // 2) * LANES:(h // 2 + 1) * LANES]
        return jnp.where(low if h % 2 == 0 else jnp.logical_not(low), win, zero_b)

    qim = [head_window(qib, h) for h in range(H_IDX)]
    wcol = [wi[:, h:h + 1] for h in range(H_IDX)]

    def p1(j, carry):
        ki = ki_tile(j)
        acc = jnp.zeros((tq, kt), F32)
        for h in range(H_IDX):
            acc = acc + jnp.maximum(_dot(qim[h], ki), 0.0) * wcol[h]
        for s in range(nsl):
            key = jnp.where(adm_fn(j, s), _mono_key(acc[:, s * LANES:(s + 1) * LANES]), INT_MIN)
            sc_ref[j, :, s * LANES:(s + 1) * LANES] = key
        return carry

    lax.fori_loop(0, nkt, p1, 0)

    def count(pred):
        def body(j, acc):
            tile = sc_ref[j]
            for s in range(nsl):
                acc = acc + jnp.where(pred(tile[:, s * LANES:(s + 1) * LANES], j, s), 1.0, 0.0)
            return acc
        acc = lax.fori_loop(0, nkt, body, jnp.zeros((tq, LANES), F32))
        return jnp.sum(acc, axis=1, keepdims=True)

    def count_ge(cand):
        cb = jnp.broadcast_to(cand, (tq, LANES))
        return count(lambda t, j, s: t >= cb)

    kf = float(topk)
    prefix = jnp.where(count_ge(jnp.zeros((tq, 1), I32)) >= kf, 0, INT_MIN).astype(I32)

    def bit_step(it, prefix):
        trial = prefix | lax.shift_left(jnp.int32(1), 30 - it)
        return jnp.where(count_ge(trial) >= kf, trial, prefix)

    thr = lax.fori_loop(0, 31, bit_step, prefix)
    thr = jnp.maximum(thr, INT_MIN + 1)
    thr_b = jnp.broadcast_to(thr, (tq, LANES))
    n_gt = count(lambda t, j, s: t > thr_b)
    n_eq = count(lambda t, j, s: t == thr_b)
    take = kf - n_gt

    def kpos(j, s):
        return j * kt + s * LANES + lane

    def tie_cut(_):
        def step(it, cut):
            trial = cut | lax.shift_left(jnp.int32(1), 30 - it)
            tb = jnp.broadcast_to(trial, (tq, LANES))
            c = count(lambda t, j, s: (t == thr_b) & (kpos(j, s) < tb))
            return jnp.where(c <= take, trial, cut)
        return lax.fori_loop(0, 31, step, jnp.zeros((tq, 1), I32))

    has_tie = jnp.max(jnp.where(n_gt + n_eq > kf, 1.0, 0.0)) > 0.0
    cut = lax.cond(has_tie, tie_cut, lambda _: jnp.full((tq, 1), INT_MAX, I32), 0)
    cut_b = jnp.broadcast_to(cut, (tq, LANES))

    qm = [head_window(qb, h) for h in range(H_ATT)]
    m_ref[...] = jnp.full(m_ref.shape, NEG_BIG, F32)
    l_ref[...] = jnp.zeros(l_ref.shape, F32)
    acc_ref[...] = jnp.zeros(acc_ref.shape, F32)

    def p3(j, carry):
        keyt = sc_ref[j]
        sel = []
        for s in range(nsl):
            ks = keyt[:, s * LANES:(s + 1) * LANES]
            sel.append((ks > thr_b) | ((ks == thr_b) & (kpos(j, s) < cut_b)))
        for p in range(H_ATT // 2):
            kp = kt_tile(j, p)
            vp = v_tile(j, p)
            for hsub in range(2):
                h = 2 * p + hsub
                s_all = _dot(qm[h], kp)
                parts = [jnp.where(sel[s], s_all[:, s * LANES:(s + 1) * LANES] + bias_fn(j, s, h), NEG_BIG)
                         for s in range(nsl)]
                mx = parts[0]
                for s in range(1, nsl):
                    mx = jnp.maximum(mx, parts[s])
                m_old = m_ref[h]
                m_new = jnp.maximum(m_old, jnp.max(mx, axis=1, keepdims=True))
                alpha = jnp.exp(m_old - m_new)
                pr = [jnp.exp(part - m_new) for part in parts]
                psum = pr[0]
                for s in range(1, nsl):
                    psum = psum + pr[s]
                l_ref[h] = alpha * l_ref[h] + psum
                pb = jnp.concatenate([x.astype(BF16) for x in pr], axis=1)
                acc_ref[h] = alpha * acc_ref[h] + _dot(pb, vp)
                m_ref[h] = m_new
        return carry

    lax.fori_loop(0, nkt, p3, 0)
    for p in range(H_ATT // 2):
        outs = []
        for hsub in range(2):
            h = 2 * p + hsub
            lsum = jnp.sum(l_ref[h], axis=1, keepdims=True)
            outs.append(acc_ref[h] / lsum)
        y_ref[:, p * LANES:(p + 1) * LANES] = jnp.where(low, outs[0], outs[1]).astype(y_ref.dtype)


N_BIAS_NEAR = 7


LOG2E = math.log2(math.e)
PV_ROWS = 256


def _attn_prompt_kernel(q_ref, qi_ref, sm_ref, k_ref, vt_ref, ki_ref, bt_ref, cf_ref, y_ref,
                        sc_ref, qt_ref, zb_ref, nm_ref, m_ref, l_ref, acc_ref, *, tq, kt, topk):
    i = pl.program_id(1)
    nql = tq // LANES
    nsl = kt // LANES
    nkt = lax.div((i + 1) * tq + (kt - 1), kt)

    def fold(x, op, chains=4):
        groups = x.shape[0] // SUBLANES
        accs = [x[a * SUBLANES:(a + 1) * SUBLANES] for a in range(chains)]
        for r in range(chains, groups):
            accs[r % chains] = op(accs[r % chains], x[r * SUBLANES:(r + 1) * SUBLANES])
        while len(accs) > 1:
            accs = [op(accs[a], accs[a + len(accs) // 2]) for a in range(len(accs) // 2)]
        return accs[0]

    eye = jnp.where(lax.broadcasted_iota(I32, (LANES, LANES), 0) == lax.broadcasted_iota(I32, (LANES, LANES), 1),
                    1.0, 0.0).astype(BF16)
    lane = lax.broadcasted_iota(I32, (tq, LANES), 1)
    low = lane < HEAD_DIM
    qb = (q_ref[...] * (HEAD_DIM ** -0.5 * LOG2E)).astype(BF16)
    qib = qi_ref[...].astype(BF16)
    zero_b = jnp.zeros((tq, LANES), BF16)
    for src, base in ((qib, 0), (qb, H_IDX)):
        for h in range(H_ATT):
            win = src[:, (h // 2) * LANES:(h // 2 + 1) * LANES]
            win = jnp.where(low if h % 2 == 0 else jnp.logical_not(low), win, zero_b)
            qt_ref[base + h] = _dot_nt(eye, win).astype(BF16)
    sh, smm, sl_ = _split3(sm_ref[...])
    sm_t = _dot_nt(eye, sh) + _dot_nt(eye, smm) + _dot_nt(eye, sl_)
    wrow = [sm_t[SM_WI + h:SM_WI + h + 1, :] * ((D_IDX ** -0.5) * (H_IDX ** -0.5)) for h in range(H_IDX)]

    qpos = i * tq + lax.broadcasted_iota(I32, (1, tq), 1)
    cend = (lax.shift_right_logical(qpos, int(math.log2(CHUNK))) + 1) * CHUNK
    krow = lax.broadcasted_iota(I32, (LANES, tq), 0)

    def p1(j, carry):
        vmax, vmin = carry
        ki2 = ki_ref[0, j]
        for h in range(H_IDX):
            t = jnp.maximum(_dot(ki2, qt_ref[h]), 0.0) * wrow[h]
            if h == 0:
                zb_ref[0] = t
            elif h < H_IDX - 1:
                zb_ref[0] += t
            else:
                for sl in range(nsl):
                    rows = slice(sl * LANES, (sl + 1) * LANES)
                    sc = zb_ref[0, rows, :] + t[rows]
                    sc = jnp.where(sc == 0.0, 0.0, sc)
                    adm = (j * nsl + sl) * LANES + krow < cend
                    sc_ref[j * nsl + sl] = jnp.where(adm, _mono_key(sc), INT_MIN)
                    vmax = jnp.maximum(vmax, fold(jnp.where(adm, sc, -F32_BIG), jnp.maximum))
                    vmin = jnp.minimum(vmin, fold(jnp.where(adm, sc, F32_BIG), jnp.minimum))
        return vmax, vmin

    vmax, vmin = lax.fori_loop(0, nkt, p1, (jnp.full((SUBLANES, tq), -F32_BIG, F32),
                                            jnp.full((SUBLANES, tq), F32_BIG, F32)))

    def count(pred):
        def body(j, acc):
            for sl in range(nsl):
                g = j * nsl + sl
                acc = acc + fold(jnp.where(pred(sc_ref[g], g), 1.0, 0.0), jnp.add)
            return acc
        acc = lax.fori_loop(0, nkt, body, jnp.zeros((SUBLANES, tq), F32))
        return jnp.sum(acc, axis=0, keepdims=True)

    kf = float(topk)

    def active_of(lo, hi, clo):
        return (clo > kf) & (hi - 1 > lo)

    def search_cond(st):
        it, lo, hi, clo, chi, side, boost = st
        return (it < SEARCH_MAX_STEPS) & (jnp.max(jnp.where(active_of(lo, hi, clo), 1.0, 0.0)) > 0.0)

    def search_step(st):
        it, lo, hi, clo, chi, side, boost = st
        active = active_of(lo, hi, clo)
        v_lo = _key_value(lo)
        v_hi = _key_value(hi)
        frac = (jnp.log(clo) - math.log(kf)) / (jnp.log(clo) - jnp.log(jnp.maximum(chi, 0.5)))
        frac = jnp.where(side > 0, jnp.minimum(frac * boost, 0.5),
                         jnp.where(side < 0, 1.0 - jnp.minimum((1.0 - frac) * boost, 0.5), frac))
        t_int = _mono_key(v_lo + (v_hi - v_lo) * frac)
        t_mid = (lo >> 1) + (hi >> 1) + (lo & hi & 1)
        trial = jnp.where(lax.rem(it, MID_PERIOD) == MID_PERIOD - 1, t_mid, t_int)
        trial = jnp.where(it == 0, 0, jnp.where((it == 1) & (lo == 0), 1, trial))
        trial = jnp.minimum(jnp.maximum(trial, lo + 1), hi - 1)
        c = count(lambda t, g: t >= trial)
        up = active & (c >= kf)
        dn = active & (c < kf)
        now = jnp.where(c >= kf, 1, -1)
        boost = jnp.where(active, jnp.where(now == side, jnp.minimum(boost * 2.0, BOOST_MAX), 1.0), boost)
        side = jnp.where(active, now, side)
        return (it + 1, jnp.where(up, trial, lo), jnp.where(dn, trial, hi),
                jnp.where(up, c, clo), jnp.where(dn, c, chi), side, boost)

    lo0 = _mono_key(jnp.min(vmin, axis=0, keepdims=True))
    hi0 = _mono_key(jnp.max(vmax, axis=0, keepdims=True)) + 1
    _, thr, _, n_ge, n_gt, _, _ = lax.while_loop(
        search_cond, search_step,
        (jnp.int32(0), lo0, hi0, cend.astype(F32), jnp.zeros((1, tq), F32),
         jnp.zeros((1, tq), I32), jnp.ones((1, tq), F32)))
    take = kf - n_gt

    def tie_cut(_):
        def step(it, cut):
            trial = cut | lax.shift_left(jnp.int32(1), 30 - it)
            c = count(lambda t, g: (t == thr) & (g * LANES + krow < trial))
            return jnp.where(c <= take, trial, cut)
        return lax.fori_loop(0, 31, step, jnp.zeros((1, tq), I32))

    has_tie = jnp.max(jnp.where(n_ge > kf, 1.0, 0.0)) > 0.0
    cut = lax.cond(has_tie, tie_cut, lambda _: jnp.full((1, tq), INT_MAX, I32), 0)

    m_ref[...] = jnp.full(m_ref.shape, NEG_BIG, F32)
    l_ref[...] = jnp.zeros(l_ref.shape, F32)
    acc_ref[...] = jnp.zeros(acc_ref.shape, F32)

    def tile_step(j, near):
        for sl in range(nsl):
            g = j * nsl + sl
            keyt = sc_ref[g]
            sel = (keyt > thr) | ((keyt == thr) & (g * LANES + krow < cut))
            nm_ref[sl * LANES:(sl + 1) * LANES, :] = jnp.where(sel, 0.0, NEG_BIG)

        def logits(h):
            mx = None
            for sl in range(nsl):
                rows = slice(sl * LANES, (sl + 1) * LANES)
                z = _dot(k_ref[0, j, rows, (h // 2) * LANES:(h // 2 + 1) * LANES], qt_ref[H_IDX + h]) + nm_ref[rows, :]
                if near:
                    z = z + jnp.concatenate(
                        [bt_ref[jnp.clip((i * nql + hf) - (j * nsl + sl), 0, N_BIAS_NEAR - 1), h]
                         for hf in range(nql)], axis=1)
                zb_ref[h % 2, rows, :] = z
                cm = fold(z, jnp.maximum)
                mx = cm if mx is None else jnp.maximum(mx, cm)
            return mx

        def accumulate(h, mx):
            shift = 0.0 if near else cf_ref[h]
            m_old = m_ref[h]
            m_new = jnp.maximum(m_old, jnp.max(mx, axis=0, keepdims=True) + shift)
            alpha = jnp.exp2(m_old - m_new)
            msub = m_new - shift
            lsum, pv = None, None
            for c in range(kt // PV_ROWS):
                rows = slice(c * PV_ROWS, (c + 1) * PV_ROWS)
                p = jnp.exp2(zb_ref[h % 2, rows, :] - msub)
                ls = fold(p, jnp.add)
                pc = _dot(vt_ref[0, j, (h // 2) * LANES:(h // 2 + 1) * LANES, rows], p.astype(BF16))
                lsum = ls if lsum is None else lsum + ls
                pv = pc if pv is None else pv + pc
            l_ref[h] = alpha * l_ref[h] + lsum
            acc_ref[h] = alpha * acc_ref[h] + pv
            m_ref[h] = m_new

        mx_next = logits(0)
        for h in range(H_ATT):
            mx_cur = mx_next
            if h + 1 < H_ATT:
                mx_next = logits(h + 1)
            accumulate(h, mx_cur)

    n_far = jnp.minimum(lax.div(jnp.maximum(nql * i - (N_BIAS_NEAR - 1) - (nsl - 1) + nsl, 0), nsl), nkt)

    def p3_far(j, carry):
        tile_step(j, False)
        return carry

    def p3_near(j, carry):
        tile_step(j, True)
        return carry

    lax.fori_loop(0, n_far, p3_far, 0)
    lax.fori_loop(n_far, nkt, p3_near, 0)

    eye_q = jnp.where(lax.broadcasted_iota(I32, (tq, tq), 0) == lax.broadcasted_iota(I32, (tq, tq), 1),
                      1.0, 0.0).astype(BF16)
    rowd = lax.broadcasted_iota(I32, (LANES, tq), 0)
    for p in range(H_ATT // 2):
        outs = []
        for hsub in range(2):
            h = 2 * p + hsub
            outs.append(acc_ref[h] / jnp.sum(l_ref[h], axis=0, keepdims=True))
        y_t = jnp.where(rowd < HEAD_DIM, outs[0], outs[1]).astype(BF16)
        y_ref[:, p * LANES:(p + 1) * LANES] = _dot_nt(eye_q, y_t).astype(y_ref.dtype)


def _attn_prompt(u, small, k_tiles, vt_tiles, ki_tiles, btab, cfar, *, bsz, t, tq, kt):
    assert t % kt == 0 and t % tq == 0 and tq % LANES == 0 and kt % LANES == 0
    nq = t // tq
    nk = t // kt
    row = lambda bi, qi: bi * nq + qi
    once = pl.Buffered(1)
    return pl.pallas_call(
        functools.partial(_attn_prompt_kernel, tq=tq, kt=kt, topk=min(TOPK, t // 4)),
        out_shape=jax.ShapeDtypeStruct((bsz * t, D_ATT), BF16),
        grid=(bsz, nq),
        in_specs=[pl.BlockSpec((tq, D_ATT), lambda bi, qi: (row(bi, qi), COL_Q // D_ATT)),
                  pl.BlockSpec((tq, D_ATT), lambda bi, qi: (row(bi, qi), COL_QI // D_ATT)),
                  pl.BlockSpec((tq, LANES), lambda bi, qi: (row(bi, qi), 0)),
                  pl.BlockSpec((1, nk, kt, D_ATT), lambda bi, qi: (bi, 0, 0, 0), pipeline_mode=once),
                  pl.BlockSpec((1, nk, D_ATT, kt), lambda bi, qi: (bi, 0, 0, 0), pipeline_mode=once),
                  pl.BlockSpec((1, nk, kt, LANES), lambda bi, qi: (bi, 0, 0, 0), pipeline_mode=once),
                  pl.BlockSpec((N_BIAS_NEAR, H_ATT, LANES, LANES), lambda bi, qi: (0, 0, 0, 0), pipeline_mode=once),
                  pl.BlockSpec((H_ATT, 1, tq), lambda bi, qi: (0, 0, 0), pipeline_mode=once)],
        out_specs=pl.BlockSpec((tq, D_ATT), lambda bi, qi: (row(bi, qi), 0)),
        scratch_shapes=[pltpu.VMEM((t // LANES, LANES, tq), I32),
                        pltpu.VMEM((H_IDX + H_ATT, LANES, tq), BF16),
                        pltpu.VMEM((2, kt, tq), F32),
                        pltpu.VMEM((kt, tq), F32),
                        pltpu.VMEM((H_ATT, 1, tq), F32),
                        pltpu.VMEM((H_ATT, SUBLANES, tq), F32),
                        pltpu.VMEM((H_ATT, LANES, tq), F32)],
        compiler_params=pltpu.CompilerParams(dimension_semantics=("arbitrary", "arbitrary"),
                                             vmem_limit_bytes=VMEM_LIMIT),
        name="sparse_attn_prompt",
    )(u, u, small, k_tiles, vt_tiles, ki_tiles, btab, cfar)


def _attn_sample_kernel(q_ref, qi_ref, sm_ref, kt_ref, v_ref, ki_ref, bt_ref, y_ref, sc_ref, m_ref, l_ref, acc_ref,
                        *, tq, kt, n_keys, topk):
    lane = lax.broadcasted_iota(I32, (tq, LANES), 1)

    def adm_fn(j, s):
        return (j * kt + s * LANES + lane) < n_keys

    def bias_fn(j, s, h):
        return bt_ref[0, h, :, s * LANES:(s + 1) * LANES]

    _attn_body(tq=tq, kt=kt, nkt=1, topk=topk, qf=q_ref[...], qif=qi_ref[...], sm=sm_ref[...],
               kt_tile=lambda j, p: kt_ref[0, p * LANES:(p + 1) * LANES, :],
               v_tile=lambda j, p: v_ref[0, :, p * LANES:(p + 1) * LANES],
               ki_tile=lambda j: ki_ref[0],
               adm_fn=adm_fn, bias_fn=bias_fn,
               sc_ref=sc_ref, m_ref=m_ref, l_ref=l_ref, acc_ref=acc_ref, y_ref=y_ref)


def _attn_sample(u, small, kt_all, v_all, ki_all, btab, *, bsz, tq, kt, n_keys):
    return pl.pallas_call(
        functools.partial(_attn_sample_kernel, tq=tq, kt=kt, n_keys=n_keys, topk=min(TOPK, n_keys // 4)),
        out_shape=jax.ShapeDtypeStruct((bsz * tq, D_ATT), BF16),
        grid=(bsz,),
        in_specs=[pl.BlockSpec((tq, D_ATT), lambda bi: (bi, COL_Q // D_ATT)),
                  pl.BlockSpec((tq, D_ATT), lambda bi: (bi, COL_QI // D_ATT)),
                  pl.BlockSpec((tq, LANES), lambda bi: (bi, 0)),
                  pl.BlockSpec((1, D_ATT, kt), lambda bi: (bi, 0, 0)),
                  pl.BlockSpec((1, kt, D_ATT), lambda bi: (bi, 0, 0)),
                  pl.BlockSpec((1, LANES, kt), lambda bi: (bi, 0, 0)),
                  pl.BlockSpec((1, H_ATT, tq, kt), lambda bi: (0, 0, 0, 0))],
        out_specs=pl.BlockSpec((tq, D_ATT), lambda bi: (bi, 0)),
        scratch_shapes=[pltpu.VMEM((1, tq, kt), I32),
                        pltpu.VMEM((H_ATT, tq, LANES), F32),
                        pltpu.VMEM((H_ATT, tq, LANES), F32),
                        pltpu.VMEM((H_ATT, tq, LANES), F32)],
        compiler_params=pltpu.CompilerParams(dimension_semantics=("arbitrary",),
                                             vmem_limit_bytes=VMEM_LIMIT),
        name="sparse_attn_sample",
    )(u, u, small, kt_all, v_all, ki_all, btab)


def _out_proj_kernel(x_ref, ys_ref, yc_ref, ya_ref, w_ref, g_ref, o_ref):
    acc = _dot(ys_ref[...], w_ref[0:D_SSM, :])
    acc = acc + _dot(yc_ref[...], w_ref[D_SSM:D_SSM + D_CONV, :])
    acc = acc + _dot(ya_ref[...], w_ref[D_SSM + D_CONV:D_MODEL, :])
    o_ref[...] = x_ref[...] + _rms(acc, g_ref[...])


def _out_proj(x, ys, yc, ya, w, g, *, tm):
    m = x.shape[0]
    return pl.pallas_call(
        _out_proj_kernel,
        out_shape=jax.ShapeDtypeStruct((m, D_MODEL), F32),
        grid=(m // tm,),
        in_specs=[pl.BlockSpec((tm, D_MODEL), lambda i: (i, 0)),
                  pl.BlockSpec((tm, D_SSM), lambda i: (i, 0)),
                  pl.BlockSpec((tm, D_CONV), lambda i: (i, 0)),
                  pl.BlockSpec((tm, D_ATT), lambda i: (i, 0)),
                  pl.BlockSpec((D_MODEL, D_MODEL), lambda i: (0, 0)),
                  pl.BlockSpec((1, D_MODEL), lambda i: (0, 0))],
        out_specs=pl.BlockSpec((tm, D_MODEL), lambda i: (i, 0)),
        compiler_params=pltpu.CompilerParams(dimension_semantics=("arbitrary",),
                                             vmem_limit_bytes=VMEM_LIMIT),
        name="out_proj",
    )(x, ys, yc, ya, w, g)


FFN_HALO = 8


def _ffn_kernel(x_ref, gpre_ref, wg_ref, wu_ref, wd_ref, cw_ref, cb_ref, gpost_ref, p1_ref, p2_ref,
                o_ref, aux_ref, h_ref, buf_ref, tail_ref, *, tm, tps, seq_len, chained):
    i = pl.program_id(0)
    j = pl.program_id(1)

    @pl.when(j == 0)
    def _():
        h_ref[...] = _rms(x_ref[...], gpre_ref[...]).astype(BF16)
        o_ref[...] = jnp.zeros(o_ref.shape, F32)

    h = h_ref[...]
    a_pre = _dot(h, wg_ref[...])
    buf_ref[FFN_HALO:FFN_HALO + tm, :] = a_pre
    if chained:
        seq_start = lax.rem(i, tps) == 0
        buf_ref[0:FFN_HALO, :] = jnp.where(seq_start, p1_ref[0], tail_ref[j])
        prev1 = buf_ref[FFN_HALO - 1:FFN_HALO - 1 + tm, :]
        prev2 = buf_ref[FFN_HALO - 2:FFN_HALO - 2 + tm, :]
        last = a_pre[tm - FFN_HALO:tm, :]
        tail_ref[j] = last
        aux_ref[0] = last
    else:
        buf_ref[0:FFN_HALO, :] = jnp.zeros((FFN_HALO, a_pre.shape[1]), F32)
        tpos = lax.rem(lax.broadcasted_iota(I32, a_pre.shape, 0), seq_len)
        prev1 = jnp.where(tpos >= 1, buf_ref[FFN_HALO - 1:FFN_HALO - 1 + tm, :], p1_ref[...])
        prev2 = jnp.where(tpos >= 2, buf_ref[FFN_HALO - 2:FFN_HALO - 2 + tm, :], p2_ref[...])
        aux_ref[...] = a_pre
    a = cw_ref[0:1, :] * prev2 + cw_ref[1:2, :] * prev1 + cw_ref[2:3, :] * a_pre + cb_ref[...]
    f = (_silu(a) * _dot(h, wu_ref[...])).astype(BF16)
    o_ref[...] += _dot(f, wd_ref[...])

    @pl.when(j == pl.num_programs(1) - 1)
    def _():
        o_ref[...] = x_ref[...] + _rms(o_ref[...], gpost_ref[...])


def _ffn(x, gpre, wg, wu, wd, cw, cb, gpost, p1, p2, *, tm, tf, seq_len, chained):
    m = x.shape[0]
    nf = D_FF // tf
    tps = max(seq_len // tm, 1)
    if chained:
        nseq = m // seq_len
        p_specs = [pl.BlockSpec((1, FFN_HALO, tf), lambda i, j: (i // tps, 0, j)),
                   pl.BlockSpec((1, FFN_HALO, tf), lambda i, j: (i // tps, 0, j))]
        aux_shape = jax.ShapeDtypeStruct((m // tm, FFN_HALO, D_FF), F32)
        aux_spec = pl.BlockSpec((1, FFN_HALO, tf), lambda i, j: (i, 0, j))
    else:
        p_specs = [pl.BlockSpec((tm, tf), lambda i, j: (i, j)), pl.BlockSpec((tm, tf), lambda i, j: (i, j))]
        aux_shape = jax.ShapeDtypeStruct((m, D_FF), F32)
        aux_spec = pl.BlockSpec((tm, tf), lambda i, j: (i, j))
    return pl.pallas_call(
        functools.partial(_ffn_kernel, tm=tm, tps=tps, seq_len=seq_len, chained=chained),
        out_shape=(jax.ShapeDtypeStruct((m, D_MODEL), F32), aux_shape),
        grid=(m // tm, nf),
        in_specs=[pl.BlockSpec((tm, D_MODEL), lambda i, j: (i, 0), pipeline_mode=pl.Buffered(1)),
                  pl.BlockSpec((1, D_MODEL), lambda i, j: (0, 0)),
                  pl.BlockSpec((D_MODEL, tf), lambda i, j: (0, j)),
                  pl.BlockSpec((D_MODEL, tf), lambda i, j: (0, j)),
                  pl.BlockSpec((tf, D_MODEL), lambda i, j: (j, 0)),
                  pl.BlockSpec((FFN_CONV, tf), lambda i, j: (0, j)),
                  pl.BlockSpec((1, tf), lambda i, j: (0, j)),
                  pl.BlockSpec((1, D_MODEL), lambda i, j: (0, 0))] + p_specs,
        out_specs=(pl.BlockSpec((tm, D_MODEL), lambda i, j: (i, 0)), aux_spec),
        scratch_shapes=[pltpu.VMEM((tm, D_MODEL), BF16),
                        pltpu.VMEM((FFN_HALO + tm, tf), F32),
                        pltpu.VMEM((nf, FFN_HALO, tf), F32)],
        compiler_params=pltpu.CompilerParams(dimension_semantics=("arbitrary", "arbitrary"),
                                             vmem_limit_bytes=VMEM_LIMIT),
        name="conv_ffn",
    )(x, gpre, wg, wu, wd, cw, cb, gpost, p1, p2)


def _prep_layer_weights(w):
    w_in = w["w_in"]
    o_dt = D_SSM + D_XBC
    o_glu = o_dt + H_SSM
    o_ki = o_glu + 2 * D_CONV + 4 * D_ATT
    o_wi = o_ki + D_IDX
    o_q = o_glu + 2 * D_CONV
    o_k = o_q + D_ATT
    o_qi = o_k + 2 * D_ATT
    pad = jnp.zeros((D_MODEL, LANES - H_SSM - H_IDX - D_IDX), w_in.dtype)
    w_r = jnp.concatenate([w_in[:, :o_dt], w_in[:, o_glu:o_k], w_in[:, o_qi:o_ki],
                           w_in[:, o_k:o_qi],
                           w_in[:, o_dt:o_glu], w_in[:, o_wi:o_wi + H_IDX], pad, w_in[:, o_ki:o_wi]],
                          axis=1).astype(BF16)
    assert w_r.shape[1] == D_PROJ and o_qi + H_IDX * D_IDX == o_ki
    row = lambda v: v.reshape(1, -1).astype(F32)
    padl = lambda v: jnp.pad(v.astype(F32), (0, LANES - v.shape[0])).reshape(1, LANES)
    return dict(
        w_in=w_r, g_mix_pre=row(w["g_mix_pre"]),
        ssm_conv_w=w["ssm_conv_w"].astype(F32), ssm_conv_b=row(w["ssm_conv_b"]),
        dt_bias=padl(w["ssm_dt_bias"]), a_log=padl(w["ssm_a_log"]),
        d_x=row(jnp.repeat(w["ssm_d"], SSM_HEAD_DIM)), ssm_norm_g=row(w["ssm_norm_g"]),
        cconv_w=w["cconv_w"].astype(F32), cconv_b=row(w["cconv_b"]),
        cconv_ln_g=row(w["cconv_ln_g"]), cconv_ln_b=row(w["cconv_ln_b"]),
        w_out=w["w_out"].astype(BF16), g_mix_post=row(w["g_mix_post"]), g_ffn_pre=row(w["g_ffn_pre"]),
        ffn_w_gate=w["ffn_w_gate"].astype(BF16), ffn_w_up=w["ffn_w_up"].astype(BF16),
        ffn_w_down=w["ffn_w_down"].astype(BF16), ffn_conv_w=w["ffn_conv_w"].astype(F32),
        ffn_conv_b=row(w["ffn_conv_b"]), g_ffn_post=row(w["g_ffn_post"]))


def _expand_matrix():
    e = np.zeros((LANES, D_SSM), np.float32)
    for h in range(H_SSM):
        e[h, h * SSM_HEAD_DIM:(h + 1) * SSM_HEAD_DIM] = 1.0
    return jnp.asarray(e, BF16)


def _front_pad(state, halo):
    return jnp.pad(state.astype(F32), ((0, 0), (halo - state.shape[1], 0), (0, 0)))


def _state_t(h):
    b = h.shape[0]
    return jnp.transpose(h.astype(F32), (0, 3, 1, 2)).reshape(b, SSM_STATE, D_SSM)


def _state_from_t(ht):
    b = ht.shape[0]
    return jnp.transpose(ht.reshape(b, SSM_STATE, H_SSM, SSM_HEAD_DIM), (0, 2, 3, 1))


def _mixer_common(x, lw, emat, ssm_conv_prev, ssm_h0, cconv_prev, *, bsz, t, tm_proj, kt, ssd_l, ssd_nv, cc_tm):
    u, k, v, small, *attn_ops = _in_proj(x, lw["g_mix_pre"], lw["w_in"], tm=tm_proj, kt=kt)
    y_ssm, ht, ctail = _ssd(u, small, _front_pad(ssm_conv_prev, SSD_HALO), _state_t(ssm_h0), lw["ssm_conv_w"],
                            lw["ssm_conv_b"], lw["dt_bias"], lw["a_log"], lw["d_x"], lw["ssm_norm_g"], emat,
                            bsz=bsz, t=t, L=ssd_l, nv=ssd_nv)
    y_conv, cctail = _cconv(u, _front_pad(cconv_prev, CC_HALO), lw["cconv_w"], lw["cconv_b"], lw["cconv_ln_g"],
                            lw["cconv_ln_b"], bsz=bsz, t=t, tm=cc_tm)
    ki = small[:, SM_KI:SM_KI + D_IDX]
    states = dict(k=k.reshape(bsz, t, H_ATT, HEAD_DIM), v=v.reshape(bsz, t, H_ATT, HEAD_DIM),
                  ki=ki.reshape(bsz, t, D_IDX), h=_state_from_t(ht),
                  ssm_conv=ctail[:, SSD_HALO - (SSM_CONV - 1):], cconv=cctail[:, CC_HALO - (CONV_WIDTH - 1):])
    return u, small, y_ssm, y_conv, k, v, ki, attn_ops, states


def _layer_prompt(x, lw, emat, btab, *, bsz, t, cfg):
    zeros = lambda *s: jnp.zeros(s, F32)
    kt = cfg["kt"]
    nk = t // kt
    u, small, y_ssm, y_conv, k, v, ki, (kb, vt, ki2), st = _mixer_common(
        x, lw, emat, zeros(bsz, SSM_CONV - 1, D_XBC), zeros(bsz, H_SSM, SSM_HEAD_DIM, SSM_STATE),
        zeros(bsz, CONV_WIDTH - 1, D_CONV), bsz=bsz, t=t, tm_proj=cfg["tm_proj"], kt=kt,
        ssd_l=cfg["ssd_l"], ssd_nv=cfg["ssd_l"], cc_tm=cfg["cc_tm"])
    tq = cfg["tq"]
    cfar = jnp.tile(btab[N_BIAS_NEAR - 1, :, 0:1, :], (1, 1, tq // LANES))
    y_att = _attn_prompt(u, small, kb.reshape(bsz, nk, kt, D_ATT), vt.reshape(bsz, nk, D_ATT, kt),
                         ki2.reshape(bsz, nk, kt, LANES), btab, cfar, bsz=bsz, t=t, tq=tq, kt=kt)
    x1 = _out_proj(x, y_ssm, y_conv, y_att, lw["w_out"], lw["g_mix_post"], tm=cfg["tm_out"])
    prev = zeros(bsz, FFN_HALO, D_FF)
    x2, ftail = _ffn(x1, lw["g_ffn_pre"], lw["ffn_w_gate"], lw["ffn_w_up"], lw["ffn_w_down"], lw["ffn_conv_w"],
                     lw["ffn_conv_b"], lw["g_ffn_post"], prev, prev, tm=cfg["tm_ffn"], tf=cfg["tf"], seq_len=t,
                     chained=True)
    ftail = ftail.reshape(bsz, t // cfg["tm_ffn"], FFN_HALO, D_FF)[:, -1]
    st["ffn_conv"] = ftail[:, FFN_HALO - (FFN_CONV - 1):]
    return x2, st


def _layer_sample(x, lw, emat, btab, past_k, past_v, past_ki, ssm_conv_prev, ssm_h0, cconv_prev, fconv_prev,
                  *, bsz, t, cfg):
    m = bsz * t
    u, small, y_ssm, y_conv, k, v, ki, _, st = _mixer_common(
        x, lw, emat, ssm_conv_prev, ssm_h0, cconv_prev, bsz=bsz, t=t, tm_proj=m, kt=None,
        ssd_l=LANES, ssd_nv=t, cc_tm=t)
    past = past_k.shape[1]
    n_keys = past + t
    ktp = cfg["kt_sample"]
    padk = lambda a: jnp.pad(a, ((0, 0), (0, ktp - n_keys), (0, 0)))
    k_all = padk(jnp.concatenate([past_k.reshape(bsz, past, D_ATT), k.reshape(bsz, t, D_ATT)], axis=1).astype(BF16))
    v_all = padk(jnp.concatenate([past_v.reshape(bsz, past, D_ATT), v.reshape(bsz, t, D_ATT)], axis=1).astype(BF16))
    ki_all = padk(jnp.concatenate([past_ki, ki.reshape(bsz, t, D_IDX)], axis=1).astype(BF16))
    kt_all = jnp.transpose(k_all, (0, 2, 1))
    kit = jnp.transpose(ki_all, (0, 2, 1))
    y_att = _attn_sample(u, small, kt_all, v_all, jnp.concatenate([kit, kit], axis=1), btab, bsz=bsz, tq=t, kt=ktp,
                         n_keys=n_keys)
    x1 = _out_proj(x, y_ssm, y_conv, y_att, lw["w_out"], lw["g_mix_post"], tm=m)
    fprev = fconv_prev.astype(F32)
    zrow = jnp.zeros((bsz, t - 1, D_FF), F32)
    p1 = jnp.concatenate([fprev[:, 1:2], zrow], axis=1).reshape(m, D_FF)
    p2 = jnp.concatenate([fprev, zrow[:, 1:]], axis=1).reshape(m, D_FF)
    x2, a_pre = _ffn(x1, lw["g_ffn_pre"], lw["ffn_w_gate"], lw["ffn_w_up"], lw["ffn_w_down"], lw["ffn_conv_w"],
                     lw["ffn_conv_b"], lw["g_ffn_post"], p1, p2, tm=m, tf=cfg["tf"], seq_len=t, chained=False)
    st["ffn_conv"] = a_pre.reshape(bsz, t, D_FF)[:, t - (FFN_CONV - 1):]
    return x2, st


_STATE_ORDER = ("k", "v", "ki", "h", "ssm_conv", "cconv", "ffn_conv")


def _prompt_cfg(t):
    big = t >= 4096
    return dict(tm_proj=512, ssd_l=256, cc_tm=256, kt=512, tq=256,
                tm_out=512 if big else 256, tm_ffn=1024 if big else 256, tf=512)


def _forward(x_prompt, x_sample, cache_k, cache_v, cache_kidx, state_ssm, state_ssm_conv, state_cconv,
             state_ffn_conv, rel_bias, weights):
    bp, tp, _ = x_prompt.shape
    bs, ts, _ = x_sample.shape
    depth = weights["w_in"].shape[0]
    past = cache_k.shape[2]
    emat = _expand_matrix()
    cfg_p = _prompt_cfg(tp)
    kt_sample = -(-(past + ts) // LANES) * LANES
    cfg_s = dict(tf=512, kt_sample=kt_sample)
    rb = rel_bias.astype(F32)
    btab_p = _bias_table(rb, nd=N_BIAS_NEAR, rows=LANES, cols=LANES, off0=0, step=LANES, key_axis=0, scale=LOG2E)
    btab_s = _bias_table(rb, nd=1, rows=ts, cols=kt_sample, off0=-past, step=0)
    xp = x_prompt.reshape(bp * tp, D_MODEL)
    xs = x_sample.reshape(bs * ts, D_MODEL)
    p_states = {n: [] for n in _STATE_ORDER}
    s_states = {n: [] for n in _STATE_ORDER}
    for l in range(depth):
        lw = _prep_layer_weights({n: w[l] for n, w in weights.items()})
        xp, st_p = _layer_prompt(xp, lw, emat, btab_p, bsz=bp, t=tp, cfg=cfg_p)
        xs, st_s = _layer_sample(xs, lw, emat, btab_s, cache_k[l], cache_v[l], cache_kidx[l], state_ssm_conv[l],
                                 state_ssm[l], state_cconv[l], state_ffn_conv[l], bsz=bs, t=ts, cfg=cfg_s)
        for n in _STATE_ORDER:
            p_states[n].append(st_p[n])
            s_states[n].append(st_s[n])
    outs = [xp.reshape(bp, tp, D_MODEL), xs.reshape(bs, ts, D_MODEL)]
    outs += [jnp.stack(p_states[n]) for n in _STATE_ORDER]
    outs += [jnp.stack(s_states[n]) for n in _STATE_ORDER]
    return tuple(outs)


def kernel(x_prompt, x_sample, cache_k, cache_v, cache_kidx, state_ssm, state_ssm_conv, state_cconv, state_ffn_conv, rel_bias, g_mix_pre, w_in, ssm_conv_w, ssm_conv_b, ssm_dt_bias, ssm_a_log, ssm_d, ssm_norm_g, cconv_w, cconv_b, cconv_ln_g, cconv_ln_b, w_out, g_mix_post, g_ffn_pre, ffn_w_gate, ffn_w_up, ffn_conv_w, ffn_conv_b, ffn_w_down, g_ffn_post):
    weights = dict(g_mix_pre=g_mix_pre, w_in=w_in, ssm_conv_w=ssm_conv_w, ssm_conv_b=ssm_conv_b,
                   ssm_dt_bias=ssm_dt_bias, ssm_a_log=ssm_a_log, ssm_d=ssm_d, ssm_norm_g=ssm_norm_g,
                   cconv_w=cconv_w, cconv_b=cconv_b, cconv_ln_g=cconv_ln_g, cconv_ln_b=cconv_ln_b, w_out=w_out,
                   g_mix_post=g_mix_post, g_ffn_pre=g_ffn_pre, ffn_w_gate=ffn_w_gate, ffn_w_up=ffn_w_up,
                   ffn_conv_w=ffn_conv_w, ffn_conv_b=ffn_conv_b, ffn_w_down=ffn_w_down, g_ffn_post=g_ffn_post)
    return _forward(x_prompt, x_sample, cache_k, cache_v, cache_kidx, state_ssm, state_ssm_conv, state_cconv,
                    state_ffn_conv, rel_bias, weights)
```

```python
import functools
import math

import numpy as np
import jax
import jax.numpy as jnp
from jax import lax
from jax.experimental import pallas as pl
from jax.experimental.pallas import tpu as pltpu

F32 = jnp.float32
BF16 = jnp.bfloat16
I32 = jnp.int32

D_MODEL = 2048
D_SSM = 1024
SSM_HEAD_DIM = 64
H_SSM = 16
SSM_GROUPS = 2
SSM_STATE = 128
SSM_CONV = 4
D_XBC = D_SSM + 2 * SSM_GROUPS * SSM_STATE
D_CONV = 512
CONV_WIDTH = 31
D_ATT = 512
HEAD_DIM = 64
H_ATT = 8
H_IDX = 8
D_IDX = 64
TOPK = 256
CHUNK = 64
N_BUCKETS = 32
REL_MAX_DIST = 1024
D_FF = 5632
FFN_CONV = 3
EPS = 1e-6

LANES = 128
SUBLANES = 8

COL_Z, COL_XBC, COL_GLU, COL_Q, COL_QI = 0, 1024, 2560, 3584, 4096
D_U = 4608
PROJ_TILE = 1152
TAIL_K, TAIL_V, TAIL_SMALL = 0, 512, 1024
D_PROJ = D_U + PROJ_TILE
SM_DT, SM_WI, SM_KI = 0, 16, 64

INT_MIN = -(2 ** 31)
INT_MAX = 2 ** 31 - 1
NEG_BIG = -1e30
VMEM_LIMIT = 56 * 1024 * 1024


def _bucket_thresholds():
    nb = N_BUCKETS // 2
    max_exact = nb // 2
    n = np.arange(0, 4 * REL_MAX_DIST, dtype=np.int64)
    nf = np.maximum(n, 1).astype(np.float32)
    large = max_exact + (np.log(nf / np.float32(max_exact)) / np.float32(math.log(REL_MAX_DIST / max_exact))
                         * np.float32(nb - max_exact)).astype(np.int32)
    large = np.minimum(large, nb - 1)
    bucket = np.where(n < max_exact, n, large)
    steps = np.nonzero(np.diff(bucket))[0] + 1
    assert np.all(np.diff(bucket) >= 0) and np.all(np.diff(bucket) <= 1) and bucket[-1] == nb - 1
    return tuple(int(s) for s in steps)


BUCKET_STEPS = _bucket_thresholds()


def _sigmoid(x):
    return 1.0 / (1.0 + jnp.exp(-x))


def _silu(x):
    return x * _sigmoid(x)


def _split3(x):
    hi = x.astype(BF16)
    r1 = x - hi.astype(F32)
    mid = r1.astype(BF16)
    lo = (r1 - mid.astype(F32)).astype(BF16)
    return hi, mid, lo


def _dot(a, b):
    return jnp.dot(a, b, preferred_element_type=F32)


def _dot_nt(a, b):
    return lax.dot_general(a, b, (((1,), (1,)), ((), ())), preferred_element_type=F32)


def _exact_dot(sel_bf16, x_f32):
    hi, mid, lo = _split3(x_f32)
    return _dot(sel_bf16, hi) + _dot(sel_bf16, mid) + _dot(sel_bf16, lo)


def _exact_dot_r(x_f32, sel_bf16):
    hi, mid, lo = _split3(x_f32)
    return _dot(hi, sel_bf16) + _dot(mid, sel_bf16) + _dot(lo, sel_bf16)


def _rms(x, g):
    ms = jnp.mean(x * x, axis=-1, keepdims=True)
    return x * lax.rsqrt(ms + EPS) * g


def _in_proj_kernel(x_ref, g_ref, w_ref, u_ref, k_ref, v_ref, sm_ref, *rest, kt):
    h_ref = rest[-1]
    j = pl.program_id(1)
    last = pl.num_programs(1) - 1

    @pl.when(j == 0)
    def _():
        h_ref[...] = _rms(x_ref[...], g_ref[...]).astype(BF16)

    y = _dot(h_ref[...], w_ref[...])

    @pl.when(j < last)
    def _():
        u_ref[...] = y

    @pl.when(j == last)
    def _():
        k = y[:, TAIL_K:TAIL_K + D_ATT]
        v = y[:, TAIL_V:TAIL_V + D_ATT]
        sm = y[:, TAIL_SMALL:TAIL_SMALL + LANES]
        k_ref[...] = k
        v_ref[...] = v
        sm_ref[...] = sm
        if kt is not None:
            kb_ref, vt_ref, ki2_ref = rest[:3]
            kb_ref[...] = k.astype(BF16)
            vb = v.astype(BF16)
            eye = jnp.where(lax.broadcasted_iota(I32, (LANES, LANES), 0)
                            == lax.broadcasted_iota(I32, (LANES, LANES), 1), 1.0, 0.0).astype(BF16)
            for c in range(vb.shape[0] // kt):
                for p in range(D_ATT // LANES):
                    blk = vb[c * kt:(c + 1) * kt, p * LANES:(p + 1) * LANES]
                    vt_ref[c, p * LANES:(p + 1) * LANES, :] = _dot_nt(eye, blk).astype(BF16)
            kib = sm[:, SM_KI:SM_KI + D_IDX].astype(BF16)
            ki2_ref[...] = jnp.concatenate([kib, kib], axis=1)


def _in_proj(x, g, w, *, tm, kt=None):
    m, d = x.shape
    nj = D_PROJ // PROJ_TILE
    assert m % tm == 0 and w.shape[1] == D_PROJ and (kt is None or tm % kt == 0)
    row = lambda i, j: (i, 0)
    out_shape = [jax.ShapeDtypeStruct((m, D_U), F32), jax.ShapeDtypeStruct((m, D_ATT), F32),
                 jax.ShapeDtypeStruct((m, D_ATT), F32), jax.ShapeDtypeStruct((m, LANES), F32)]
    out_specs = [pl.BlockSpec((tm, PROJ_TILE), lambda i, j: (i, jnp.minimum(j, nj - 2))),
                 pl.BlockSpec((tm, D_ATT), row), pl.BlockSpec((tm, D_ATT), row), pl.BlockSpec((tm, LANES), row)]
    if kt is not None:
        out_shape += [jax.ShapeDtypeStruct((m, D_ATT), BF16), jax.ShapeDtypeStruct((m // kt, D_ATT, kt), BF16),
                      jax.ShapeDtypeStruct((m, LANES), BF16)]
        out_specs += [pl.BlockSpec((tm, D_ATT), row), pl.BlockSpec((tm // kt, D_ATT, kt), lambda i, j: (i, 0, 0)),
                      pl.BlockSpec((tm, LANES), row)]
    return pl.pallas_call(
        functools.partial(_in_proj_kernel, kt=kt),
        out_shape=tuple(out_shape),
        grid=(m // tm, nj),
        in_specs=[pl.BlockSpec((tm, d), lambda i, j: (i, 0), pipeline_mode=pl.Buffered(1)),
                  pl.BlockSpec((1, d), lambda i, j: (0, 0)),
                  pl.BlockSpec((d, PROJ_TILE), lambda i, j: (0, j))],
        out_specs=tuple(out_specs),
        scratch_shapes=[pltpu.VMEM((tm, d), BF16)],
        compiler_params=pltpu.CompilerParams(dimension_semantics=("arbitrary", "arbitrary"),
                                             vmem_limit_bytes=VMEM_LIMIT),
        name="rms_in_proj",
    )(x, g, w)


def _bias_table_kernel(rb_ref, o_ref, *, off0, step, key_axis, scale):
    d = pl.program_id(0)
    h = pl.program_id(1)
    rows, cols = o_ref.shape[2], o_ref.shape[3]
    rel = (lax.broadcasted_iota(I32, (rows, cols), key_axis) - lax.broadcasted_iota(I32, (rows, cols), 1 - key_axis)
           + (off0 - d * step))
    n = jnp.abs(rel)
    bucket = jnp.where(rel > 0, N_BUCKETS // 2, 0)
    for s in BUCKET_STEPS:
        bucket = bucket + jnp.where(n >= s, 1, 0)
    acc = jnp.zeros((rows, cols), F32)
    for b in range(N_BUCKETS):
        acc = jnp.where(bucket == b, rb_ref[b, h], acc)
    o_ref[0, 0] = acc * scale


def _bias_table(rel_bias, *, nd, rows, cols, off0, step, key_axis=1, scale=1.0):
    return pl.pallas_call(
        functools.partial(_bias_table_kernel, off0=off0, step=step, key_axis=key_axis, scale=scale),
        out_shape=jax.ShapeDtypeStruct((nd, H_ATT, rows, cols), F32),
        grid=(nd, H_ATT),
        in_specs=[pl.BlockSpec(memory_space=pltpu.SMEM)],
        out_specs=pl.BlockSpec((1, 1, rows, cols), lambda d, h: (d, h, 0, 0)),
        name="bias_table",
    )(rel_bias)


CC_HALO = 32


def _cconv_kernel(val_ref, gate_ref, prev_ref, w_ref, b_ref, lg_ref, lb_ref, y_ref, tail_ref, buf_ref, sh_ref, *, tm):
    @pl.when(pl.program_id(1) == 0)
    def _():
        buf_ref[0:CC_HALO, :] = prev_ref[0]

    buf_ref[CC_HALO:CC_HALO + tm, :] = val_ref[...] * _sigmoid(gate_ref[...])
    span = CC_HALO + tm - SUBLANES
    for b in range(1, SUBLANES):
        sh_ref[b - 1, 0:span, :] = buf_ref[b:b + span, :]
    first = CC_HALO - (CONV_WIDTH - 1)
    acc = jnp.zeros((tm, D_CONV), F32) + b_ref[...]
    for k in range(CONV_WIDTH):
        base, b = (first + k) // SUBLANES * SUBLANES, (first + k) % SUBLANES
        rows = buf_ref[base:base + tm, :] if b == 0 else sh_ref[b - 1, base:base + tm, :]
        acc = acc + w_ref[k:k + 1, :] * rows
    mu = jnp.mean(acc, axis=-1, keepdims=True)
    xc = acc - mu
    var = jnp.mean(xc * xc, axis=-1, keepdims=True)
    y = xc * lax.rsqrt(var + EPS) * lg_ref[...] + lb_ref[...]
    y_ref[...] = _silu(y).astype(y_ref.dtype)
    tail = buf_ref[tm:tm + CC_HALO, :]
    tail_ref[0] = tail
    buf_ref[0:CC_HALO, :] = tail


def _cconv(u, prev, w, b, lg, lb, *, bsz, t, tm):
    nt = t // tm
    row = lambda bi, ti: bi * nt + ti
    return pl.pallas_call(
        functools.partial(_cconv_kernel, tm=tm),
        out_shape=(jax.ShapeDtypeStruct((bsz * t, D_CONV), BF16),
                   jax.ShapeDtypeStruct((bsz, CC_HALO, D_CONV), F32)),
        grid=(bsz, nt),
        in_specs=[pl.BlockSpec((tm, D_CONV), lambda bi, ti: (row(bi, ti), COL_GLU // D_CONV)),
                  pl.BlockSpec((tm, D_CONV), lambda bi, ti: (row(bi, ti), COL_GLU // D_CONV + 1)),
                  pl.BlockSpec((1, CC_HALO, D_CONV), lambda bi, ti: (bi, 0, 0)),
                  pl.BlockSpec((CONV_WIDTH, D_CONV), lambda bi, ti: (0, 0)),
                  pl.BlockSpec((1, D_CONV), lambda bi, ti: (0, 0)),
                  pl.BlockSpec((1, D_CONV), lambda bi, ti: (0, 0)),
                  pl.BlockSpec((1, D_CONV), lambda bi, ti: (0, 0))],
        out_specs=(pl.BlockSpec((tm, D_CONV), lambda bi, ti: (row(bi, ti), 0)),
                   pl.BlockSpec((1, CC_HALO, D_CONV), lambda bi, ti: (bi, 0, 0))),
        scratch_shapes=[pltpu.VMEM((CC_HALO + tm, D_CONV), F32),
                        pltpu.VMEM((SUBLANES - 1, CC_HALO + tm, D_CONV), F32)],
        compiler_params=pltpu.CompilerParams(dimension_semantics=("arbitrary", "arbitrary"),
                                             vmem_limit_bytes=VMEM_LIMIT),
        name="conformer_conv",
    )(u, u, prev, w, b, lg, lb)


SSD_HALO = 8


def _ssd_kernel(z_ref, x0_ref, x1_ref, x2_ref, sm_ref, prevc_ref, h0_ref, cw_ref, cb_ref, dtb_ref, alog_ref,
                dx_ref, ng_ref, e_ref, y_ref, hout_ref, ctail_ref, buf_ref, ht_ref, yb_ref, *, L, nv):
    @pl.when(pl.program_id(1) == 0)
    def _():
        buf_ref[0:SSD_HALO, :] = prevc_ref[0]
        ht_ref[...] = h0_ref[0]

    if nv < L:
        buf_ref[SSD_HALO + nv:SSD_HALO + L, :] = jnp.zeros((L - nv, D_XBC), F32)
    for c, r in enumerate((x0_ref, x1_ref, x2_ref)):
        buf_ref[SSD_HALO:SSD_HALO + nv, c * 512:(c + 1) * 512] = r[...]
    first = SSD_HALO - (SSM_CONV - 1)
    acc = jnp.zeros((L, D_XBC), F32) + cb_ref[...]
    for k in range(SSM_CONV):
        acc = acc + cw_ref[k:k + 1, :] * buf_ref[first + k:first + k + L, :]
    xbc = _silu(acc)
    tail = buf_ref[nv:nv + SSD_HALO, :]
    ctail_ref[0] = tail
    buf_ref[0:SSD_HALO, :] = tail

    xs = xbc[:, :D_SSM]
    lane = lax.broadcasted_iota(I32, (L, LANES), 1)
    rowi = lax.broadcasted_iota(I32, (L, LANES), 0)
    sm = sm_ref[...]
    if nv < L:
        sm = jnp.concatenate([sm, jnp.zeros((L - nv, LANES), F32)], axis=0)
    dtr = sm + dtb_ref[...]
    dt = jnp.maximum(dtr, 0.0) + jnp.log(1.0 + jnp.exp(-jnp.abs(dtr)))
    dt = jnp.where((lane < H_SSM) & (rowi < nv), dt, 0.0)
    a = -jnp.exp(alog_ref[...])
    da = dt * a
    ri = lax.broadcasted_iota(I32, (L, L), 0)
    ci = lax.broadcasted_iota(I32, (L, L), 1)
    causal = ri >= ci
    tril = jnp.where(causal, 1.0, 0.0).astype(BF16)
    cum = _exact_dot(tril, da)
    eye = jnp.where(lax.broadcasted_iota(I32, (LANES, LANES), 0) == lax.broadcasted_iota(I32, (LANES, LANES), 1),
                    1.0, 0.0).astype(BF16)
    ch, cm, cl = _split3(cum)
    cum_t = _dot_nt(eye, ch) + _dot_nt(eye, cm) + _dot_nt(eye, cl)
    e = e_ref[...]
    ecx = _exact_dot_r(jnp.exp(cum), e)
    dtx = _exact_dot_r(dt, e)
    xdt = (xs * dtx).astype(BF16)
    edl = ecx[L - 1:L, :]
    dend_t = jnp.exp(cum_t[:, L - 1:L] - cum_t)
    lane_l = lax.broadcasted_iota(I32, (L, LANES), 1)
    lane_n = lax.broadcasted_iota(I32, (SSM_STATE, LANES), 1)
    hpg = H_SSM // SSM_GROUPS
    for g in range(SSM_GROUPS):
        bg = xbc[:, D_SSM + g * SSM_STATE:D_SSM + (g + 1) * SSM_STATE].astype(BF16)
        cg = xbc[:, D_SSM + (SSM_GROUPS + g) * SSM_STATE:D_SSM + (SSM_GROUPS + g + 1) * SSM_STATE].astype(BF16)
        cbt = _dot_nt(cg, bg)
        bg_t = _dot_nt(eye, bg)
        gcols = slice(g * hpg * SSM_HEAD_DIM, (g + 1) * hpg * SSM_HEAD_DIM)
        yoff = _dot(cg, ht_ref[:, gcols].astype(BF16)) * ecx[:, gcols]
        for p in range(hpg // 2):
            h0 = g * hpg + 2 * p
            pcols = slice(h0 * SSM_HEAD_DIM, (h0 + 2) * SSM_HEAD_DIM)
            xpair = xdt[:, pcols]
            res, st = [], []
            for hh in (h0, h0 + 1):
                seg = cum[:, hh:hh + 1] - cum_t[hh:hh + 1, :]
                dec = jnp.where(causal, jnp.exp(seg), 0.0)
                res.append(_dot((cbt * dec).astype(BF16), xpair))
                st.append(_dot((bg_t * dend_t[hh:hh + 1, :]).astype(BF16), xpair))
            yb_ref[:, pcols] = (jnp.where(lane_l < SSM_HEAD_DIM, res[0], res[1])
                                + yoff[:, 2 * p * SSM_HEAD_DIM:(2 * p + 2) * SSM_HEAD_DIM])
            ht_ref[:, pcols] = (ht_ref[:, pcols] * edl[:, pcols]
                                + jnp.where(lane_n < SSM_HEAD_DIM, st[0], st[1]))
    hout_ref[0] = ht_ref[...]
    y = yb_ref[...] + dx_ref[...] * xs
    z = z_ref[...]
    if nv < L:
        z = jnp.concatenate([z, jnp.zeros((L - nv, D_SSM), F32)], axis=0)
    y = _rms(y * _silu(z), ng_ref[...])
    y_ref[...] = y[:nv].astype(y_ref.dtype)


def _ssd(u, small, prevc, h0t, cw, cb, dtb, alog, dx, ng, emat, *, bsz, t, L, nv):
    nt = t // nv
    row = lambda bi, ti: bi * nt + ti
    c512 = lambda c: (lambda bi, ti: (row(bi, ti), c))
    const2 = lambda bi, ti: (0, 0)
    return pl.pallas_call(
        functools.partial(_ssd_kernel, L=L, nv=nv),
        out_shape=(jax.ShapeDtypeStruct((bsz * t, D_SSM), BF16),
                   jax.ShapeDtypeStruct((bsz, SSM_STATE, D_SSM), F32),
                   jax.ShapeDtypeStruct((bsz, SSD_HALO, D_XBC), F32)),
        grid=(bsz, nt),
        in_specs=[pl.BlockSpec((nv, D_SSM), lambda bi, ti: (row(bi, ti), 0)),
                  pl.BlockSpec((nv, 512), c512(COL_XBC // 512)),
                  pl.BlockSpec((nv, 512), c512(COL_XBC // 512 + 1)),
                  pl.BlockSpec((nv, 512), c512(COL_XBC // 512 + 2)),
                  pl.BlockSpec((nv, LANES), c512(0)),
                  pl.BlockSpec((1, SSD_HALO, D_XBC), lambda bi, ti: (bi, 0, 0)),
                  pl.BlockSpec((1, SSM_STATE, D_SSM), lambda bi, ti: (bi, 0, 0)),
                  pl.BlockSpec((SSM_CONV, D_XBC), const2),
                  pl.BlockSpec((1, D_XBC), const2),
                  pl.BlockSpec((1, LANES), const2),
                  pl.BlockSpec((1, LANES), const2),
                  pl.BlockSpec((1, D_SSM), const2),
                  pl.BlockSpec((1, D_SSM), const2),
                  pl.BlockSpec((LANES, D_SSM), const2)],
        out_specs=(pl.BlockSpec((nv, D_SSM), lambda bi, ti: (row(bi, ti), 0)),
                   pl.BlockSpec((1, SSM_STATE, D_SSM), lambda bi, ti: (bi, 0, 0)),
                   pl.BlockSpec((1, SSD_HALO, D_XBC), lambda bi, ti: (bi, 0, 0))),
        scratch_shapes=[pltpu.VMEM((SSD_HALO + L, D_XBC), F32),
                        pltpu.VMEM((SSM_STATE, D_SSM), F32),
                        pltpu.VMEM((L, D_SSM), F32)],
        compiler_params=pltpu.CompilerParams(dimension_semantics=("arbitrary", "arbitrary"),
                                             vmem_limit_bytes=VMEM_LIMIT),
        name="ssd_mixer",
    )(u, u, u, u, small, prevc, h0t, cw, cb, dtb, alog, dx, ng, emat)


def _flip_negative(b):
    return b ^ ((b >> 31) & INT_MAX)


def _mono_key(x):
    return _flip_negative(lax.bitcast_convert_type(x, I32))


def _key_value(k):
    return lax.bitcast_convert_type(_flip_negative(k), F32)


F32_BIG = 3e38
MID_PERIOD = 12
BOOST_MAX = 2.0 ** 30
SEARCH_MAX_STEPS = 400


def _attn_body(*, tq, kt, nkt, topk, qf, qif, sm, kt_tile, v_tile, ki_tile, adm_fn, bias_fn,
               sc_ref, m_ref, l_ref, acc_ref, y_ref):
    nsl = kt // LANES
    lane = lax.broadcasted_iota(I32, (tq, LANES), 1)
    low = lane < HEAD_DIM
    qb = (qf * (HEAD_DIM ** -0.5)).astype(BF16)
    qib = qif.astype(BF16)
    wi = sm[:, SM_WI:SM_WI + H_IDX] * ((D_IDX ** -0.5) * (H_IDX ** -0.5))
    zero_b = jnp.zeros((tq, LANES), BF16)

    def head_window(x, h):
        win = x[:, (h // 2) * LANES:(h // 2 + 1) * LANES]
        return jnp.where(low if h % 2 == 0 else jnp.logical_not(low), win, zero_b)

    qim = [head_window(qib, h) for h in range(H_IDX)]
    wcol = [wi[:, h:h + 1] for h in range(H_IDX)]

    def p1(j, carry):
        ki = ki_tile(j)
        acc = jnp.zeros((tq, kt), F32)
        for h in range(H_IDX):
            acc = acc + jnp.maximum(_dot(qim[h], ki), 0.0) * wcol[h]
        for s in range(nsl):
            key = jnp.where(adm_fn(j, s), _mono_key(acc[:, s * LANES:(s + 1) * LANES]), INT_MIN)
            sc_ref[j, :, s * LANES:(s + 1) * LANES] = key
        return carry

    lax.fori_loop(0, nkt, p1, 0)

    def count(pred):
        def body(j, acc):
            tile = sc_ref[j]
            for s in range(nsl):
                acc = acc + jnp.where(pred(tile[:, s * LANES:(s + 1) * LANES], j, s), 1.0, 0.0)
            return acc
        acc = lax.fori_loop(0, nkt, body, jnp.zeros((tq, LANES), F32))
        return jnp.sum(acc, axis=1, keepdims=True)

    def count_ge(cand):
        cb = jnp.broadcast_to(cand, (tq, LANES))
        return count(lambda t, j, s: t >= cb)

    kf = float(topk)
    prefix = jnp.where(count_ge(jnp.zeros((tq, 1), I32)) >= kf, 0, INT_MIN).astype(I32)

    def bit_step(it, prefix):
        trial = prefix | lax.shift_left(jnp.int32(1), 30 - it)
        return jnp.where(count_ge(trial) >= kf, trial, prefix)

    thr = lax.fori_loop(0, 31, bit_step, prefix)
    thr = jnp.maximum(thr, INT_MIN + 1)
    thr_b = jnp.broadcast_to(thr, (tq, LANES))
    n_gt = count(lambda t, j, s: t > thr_b)
    n_eq = count(lambda t, j, s: t == thr_b)
    take = kf - n_gt

    def kpos(j, s):
        return j * kt + s * LANES + lane

    def tie_cut(_):
        def step(it, cut):
            trial = cut | lax.shift_left(jnp.int32(1), 30 - it)
            tb = jnp.broadcast_to(trial, (tq, LANES))
            c = count(lambda t, j, s: (t == thr_b) & (kpos(j, s) < tb))
            return jnp.where(c <= take, trial, cut)
        return lax.fori_loop(0, 31, step, jnp.zeros((tq, 1), I32))

    has_tie = jnp.max(jnp.where(n_gt + n_eq > kf, 1.0, 0.0)) > 0.0
    cut = lax.cond(has_tie, tie_cut, lambda _: jnp.full((tq, 1), INT_MAX, I32), 0)
    cut_b = jnp.broadcast_to(cut, (tq, LANES))

    qm = [head_window(qb, h) for h in range(H_ATT)]
    m_ref[...] = jnp.full(m_ref.shape, NEG_BIG, F32)
    l_ref[...] = jnp.zeros(l_ref.shape, F32)
    acc_ref[...] = jnp.zeros(acc_ref.shape, F32)

    def p3(j, carry):
        keyt = sc_ref[j]
        sel = []
        for s in range(nsl):
            ks = keyt[:, s * LANES:(s + 1) * LANES]
            sel.append((ks > thr_b) | ((ks == thr_b) & (kpos(j, s) < cut_b)))
        for p in range(H_ATT // 2):
            kp = kt_tile(j, p)
            vp = v_tile(j, p)
            for hsub in range(2):
                h = 2 * p + hsub
                s_all = _dot(qm[h], kp)
                parts = [jnp.where(sel[s], s_all[:, s * LANES:(s + 1) * LANES] + bias_fn(j, s, h), NEG_BIG)
                         for s in range(nsl)]
                mx = parts[0]
                for s in range(1, nsl):
                    mx = jnp.maximum(mx, parts[s])
                m_old = m_ref[h]
                m_new = jnp.maximum(m_old, jnp.max(mx, axis=1, keepdims=True))
                alpha = jnp.exp(m_old - m_new)
                pr = [jnp.exp(part - m_new) for part in parts]
                psum = pr[0]
                for s in range(1, nsl):
                    psum = psum + pr[s]
                l_ref[h] = alpha * l_ref[h] + psum
                pb = jnp.concatenate([x.astype(BF16) for x in pr], axis=1)
                acc_ref[h] = alpha * acc_ref[h] + _dot(pb, vp)
                m_ref[h] = m_new
        return carry

    lax.fori_loop(0, nkt, p3, 0)
    for p in range(H_ATT // 2):
        outs = []
        for hsub in range(2):
            h = 2 * p + hsub
            lsum = jnp.sum(l_ref[h], axis=1, keepdims=True)
            outs.append(acc_ref[h] / lsum)
        y_ref[:, p * LANES:(p + 1) * LANES] = jnp.where(low, outs[0], outs[1]).astype(y_ref.dtype)


N_BIAS_NEAR = 7


LOG2E = math.log2(math.e)
PV_ROWS = 256


def _attn_prompt_kernel(q_ref, qi_ref, sm_ref, k_ref, vt_ref, ki_ref, bt_ref, cf_ref, y_ref,
                        sc_ref, qt_ref, zb_ref, nm_ref, m_ref, l_ref, acc_ref, *, tq, kt, topk):
    i = pl.program_id(1)
    nql = tq // LANES
    nsl = kt // LANES
    nkt = lax.div((i + 1) * tq + (kt - 1), kt)

    def fold(x, op, chains=4):
        groups = x.shape[0] // SUBLANES
        accs = [x[a * SUBLANES:(a + 1) * SUBLANES] for a in range(chains)]
        for r in range(chains, groups):
            accs[r % chains] = op(accs[r % chains], x[r * SUBLANES:(r + 1) * SUBLANES])
        while len(accs) > 1:
            accs = [op(accs[a], accs[a + len(accs) // 2]) for a in range(len(accs) // 2)]
        return accs[0]

    eye = jnp.where(lax.broadcasted_iota(I32, (LANES, LANES), 0) == lax.broadcasted_iota(I32, (LANES, LANES), 1),
                    1.0, 0.0).astype(BF16)
    lane = lax.broadcasted_iota(I32, (tq, LANES), 1)
    low = lane < HEAD_DIM
    qb = (q_ref[...] * (HEAD_DIM ** -0.5 * LOG2E)).astype(BF16)
    qib = qi_ref[...].astype(BF16)
    zero_b = jnp.zeros((tq, LANES), BF16)
    for src, base in ((qib, 0), (qb, H_IDX)):
        for h in range(H_ATT):
            win = src[:, (h // 2) * LANES:(h // 2 + 1) * LANES]
            win = jnp.where(low if h % 2 == 0 else jnp.logical_not(low), win, zero_b)
            qt_ref[base + h] = _dot_nt(eye, win).astype(BF16)
    sh, smm, sl_ = _split3(sm_ref[...])
    sm_t = _dot_nt(eye, sh) + _dot_nt(eye, smm) + _dot_nt(eye, sl_)
    wrow = [sm_t[SM_WI + h:SM_WI + h + 1, :] * ((D_IDX ** -0.5) * (H_IDX ** -0.5)) for h in range(H_IDX)]

    qpos = i * tq + lax.broadcasted_iota(I32, (1, tq), 1)
    cend = (lax.shift_right_logical(qpos, int(math.log2(CHUNK))) + 1) * CHUNK
    krow = lax.broadcasted_iota(I32, (LANES, tq), 0)

    def p1(j, carry):
        vmax, vmin = carry
        ki2 = ki_ref[0, j]
        for h in range(H_IDX):
            t = jnp.maximum(_dot(ki2, qt_ref[h]), 0.0) * wrow[h]
            if h == 0:
                zb_ref[0] = t
            elif h < H_IDX - 1:
                zb_ref[0] += t
            else:
                for sl in range(nsl):
                    rows = slice(sl * LANES, (sl + 1) * LANES)
                    sc = zb_ref[0, rows, :] + t[rows]
                    sc = jnp.where(sc == 0.0, 0.0, sc)
                    adm = (j * nsl + sl) * LANES + krow < cend
                    sc_ref[j * nsl + sl] = jnp.where(adm, _mono_key(sc), INT_MIN)
                    vmax = jnp.maximum(vmax, fold(jnp.where(adm, sc, -F32_BIG), jnp.maximum))
                    vmin = jnp.minimum(vmin, fold(jnp.where(adm, sc, F32_BIG), jnp.minimum))
        return vmax, vmin

    vmax, vmin = lax.fori_loop(0, nkt, p1, (jnp.full((SUBLANES, tq), -F32_BIG, F32),
                                            jnp.full((SUBLANES, tq), F32_BIG, F32)))

    def count(pred):
        def body(j, acc):
            for sl in range(nsl):
                g = j * nsl + sl
                acc = acc + fold(jnp.where(pred(sc_ref[g], g), 1.0, 0.0), jnp.add)
            return acc
        acc = lax.fori_loop(0, nkt, body, jnp.zeros((SUBLANES, tq), F32))
        return jnp.sum(acc, axis=0, keepdims=True)

    kf = float(topk)

    def active_of(lo, hi, clo):
        return (clo > kf) & (hi - 1 > lo)

    def search_cond(st):
        it, lo, hi, clo, chi, side, boost = st
        return (it < SEARCH_MAX_STEPS) & (jnp.max(jnp.where(active_of(lo, hi, clo), 1.0, 0.0)) > 0.0)

    def search_step(st):
        it, lo, hi, clo, chi, side, boost = st
        active = active_of(lo, hi, clo)
        v_lo = _key_value(lo)
        v_hi = _key_value(hi)
        frac = (jnp.log(clo) - math.log(kf)) / (jnp.log(clo) - jnp.log(jnp.maximum(chi, 0.5)))
        frac = jnp.where(side > 0, jnp.minimum(frac * boost, 0.5),
                         jnp.where(side < 0, 1.0 - jnp.minimum((1.0 - frac) * boost, 0.5), frac))
        t_int = _mono_key(v_lo + (v_hi - v_lo) * frac)
        t_mid = (lo >> 1) + (hi >> 1) + (lo & hi & 1)
        trial = jnp.where(lax.rem(it, MID_PERIOD) == MID_PERIOD - 1, t_mid, t_int)
        trial = jnp.where(it == 0, 0, jnp.where((it == 1) & (lo == 0), 1, trial))
        trial = jnp.minimum(jnp.maximum(trial, lo + 1), hi - 1)
        c = count(lambda t, g: t >= trial)
        up = active & (c >= kf)
        dn = active & (c < kf)
        now = jnp.where(c >= kf, 1, -1)
        boost = jnp.where(active, jnp.where(now == side, jnp.minimum(boost * 2.0, BOOST_MAX), 1.0), boost)
        side = jnp.where(active, now, side)
        return (it + 1, jnp.where(up, trial, lo), jnp.where(dn, trial, hi),
                jnp.where(up, c, clo), jnp.where(dn, c, chi), side, boost)

    lo0 = _mono_key(jnp.min(vmin, axis=0, keepdims=True))
    hi0 = _mono_key(jnp.max(vmax, axis=0, keepdims=True)) + 1
    _, thr, _, n_ge, n_gt, _, _ = lax.while_loop(
        search_cond, search_step,
        (jnp.int32(0), lo0, hi0, cend.astype(F32), jnp.zeros((1, tq), F32),
         jnp.zeros((1, tq), I32), jnp.ones((1, tq), F32)))
    take = kf - n_gt

    def tie_cut(_):
        def step(it, cut):
            trial = cut | lax.shift_left(jnp.int32(1), 30 - it)
            c = count(lambda t, g: (t == thr) & (g * LANES + krow < trial))
            return jnp.where(c <= take, trial, cut)
        return lax.fori_loop(0, 31, step, jnp.zeros((1, tq), I32))

    has_tie = jnp.max(jnp.where(n_ge > kf, 1.0, 0.0)) > 0.0
    cut = lax.cond(has_tie, tie_cut, lambda _: jnp.full((1, tq), INT_MAX, I32), 0)

    m_ref[...] = jnp.full(m_ref.shape, NEG_BIG, F32)
    l_ref[...] = jnp.zeros(l_ref.shape, F32)
    acc_ref[...] = jnp.zeros(acc_ref.shape, F32)

    def tile_step(j, near):
        for sl in range(nsl):
            g = j * nsl + sl
            keyt = sc_ref[g]
            sel = (keyt > thr) | ((keyt == thr) & (g * LANES + krow < cut))
            nm_ref[sl * LANES:(sl + 1) * LANES, :] = jnp.where(sel, 0.0, NEG_BIG)

        def logits(h):
            mx = None
            for sl in range(nsl):
                rows = slice(sl * LANES, (sl + 1) * LANES)
                z = _dot(k_ref[0, j, rows, (h // 2) * LANES:(h // 2 + 1) * LANES], qt_ref[H_IDX + h]) + nm_ref[rows, :]
                if near:
                    z = z + jnp.concatenate(
                        [bt_ref[jnp.clip((i * nql + hf) - (j * nsl + sl), 0, N_BIAS_NEAR - 1), h]
                         for hf in range(nql)], axis=1)
                zb_ref[h % 2, rows, :] = z
                cm = fold(z, jnp.maximum)
                mx = cm if mx is None else jnp.maximum(mx, cm)
            return mx

        def accumulate(h, mx):
            shift = 0.0 if near else cf_ref[h]
            m_old = m_ref[h]
            m_new = jnp.maximum(m_old, jnp.max(mx, axis=0, keepdims=True) + shift)
            alpha = jnp.exp2(m_old - m_new)
            msub = m_new - shift
            lsum, pv = None, None
            for c in range(kt // PV_ROWS):
                rows = slice(c * PV_ROWS, (c + 1) * PV_ROWS)
                p = jnp.exp2(zb_ref[h % 2, rows, :] - msub)
                ls = fold(p, jnp.add)
                pc = _dot(vt_ref[0, j, (h // 2) * LANES:(h // 2 + 1) * LANES, rows], p.astype(BF16))
                lsum = ls if lsum is None else lsum + ls
                pv = pc if pv is None else pv + pc
            l_ref[h] = alpha * l_ref[h] + lsum
            acc_ref[h] = alpha * acc_ref[h] + pv
            m_ref[h] = m_new

        mx_next = logits(0)
        for h in range(H_ATT):
            mx_cur = mx_next
            if h + 1 < H_ATT:
                mx_next = logits(h + 1)
            accumulate(h, mx_cur)

    n_far = jnp.minimum(lax.div(jnp.maximum(nql * i - (N_BIAS_NEAR - 1) - (nsl - 1) + nsl, 0), nsl), nkt)

    def p3_far(j, carry):
        tile_step(j, False)
        return carry

    def p3_near(j, carry):
        tile_step(j, True)
        return carry

    lax.fori_loop(0, n_far, p3_far, 0)
    lax.fori_loop(n_far, nkt, p3_near, 0)

    eye_q = jnp.where(lax.broadcasted_iota(I32, (tq, tq), 0) == lax.broadcasted_iota(I32, (tq, tq), 1),
                      1.0, 0.0).astype(BF16)
    rowd = lax.broadcasted_iota(I32, (LANES, tq), 0)
    for p in range(H_ATT // 2):
        outs = []
        for hsub in range(2):
            h = 2 * p + hsub
            outs.append(acc_ref[h] / jnp.sum(l_ref[h], axis=0, keepdims=True))
        y_t = jnp.where(rowd < HEAD_DIM, outs[0], outs[1]).astype(BF16)
        y_ref[:, p * LANES:(p + 1) * LANES] = _dot_nt(eye_q, y_t).astype(y_ref.dtype)


def _attn_prompt(u, small, k_tiles, vt_tiles, ki_tiles, btab, cfar, *, bsz, t, tq, kt):
    assert t % kt == 0 and t % tq == 0 and tq % LANES == 0 and kt % LANES == 0
    nq = t // tq
    nk = t // kt
    row = lambda bi, qi: bi * nq + qi
    once = pl.Buffered(1)
    return pl.pallas_call(
        functools.partial(_attn_prompt_kernel, tq=tq, kt=kt, topk=min(TOPK, t // 4)),
        out_shape=jax.ShapeDtypeStruct((bsz * t, D_ATT), BF16),
        grid=(bsz, nq),
        in_specs=[pl.BlockSpec((tq, D_ATT), lambda bi, qi: (row(bi, qi), COL_Q // D_ATT)),
                  pl.BlockSpec((tq, D_ATT), lambda bi, qi: (row(bi, qi), COL_QI // D_ATT)),
                  pl.BlockSpec((tq, LANES), lambda bi, qi: (row(bi, qi), 0)),
                  pl.BlockSpec((1, nk, kt, D_ATT), lambda bi, qi: (bi, 0, 0, 0), pipeline_mode=once),
                  pl.BlockSpec((1, nk, D_ATT, kt), lambda bi, qi: (bi, 0, 0, 0), pipeline_mode=once),
                  pl.BlockSpec((1, nk, kt, LANES), lambda bi, qi: (bi, 0, 0, 0), pipeline_mode=once),
                  pl.BlockSpec((N_BIAS_NEAR, H_ATT, LANES, LANES), lambda bi, qi: (0, 0, 0, 0), pipeline_mode=once),
                  pl.BlockSpec((H_ATT, 1, tq), lambda bi, qi: (0, 0, 0), pipeline_mode=once)],
        out_specs=pl.BlockSpec((tq, D_ATT), lambda bi, qi: (row(bi, qi), 0)),
        scratch_shapes=[pltpu.VMEM((t // LANES, LANES, tq), I32),
                        pltpu.VMEM((H_IDX + H_ATT, LANES, tq), BF16),
                        pltpu.VMEM((2, kt, tq), F32),
                        pltpu.VMEM((kt, tq), F32),
                        pltpu.VMEM((H_ATT, 1, tq), F32),
                        pltpu.VMEM((H_ATT, SUBLANES, tq), F32),
                        pltpu.VMEM((H_ATT, LANES, tq), F32)],
        compiler_params=pltpu.CompilerParams(dimension_semantics=("arbitrary", "arbitrary"),
                                             vmem_limit_bytes=VMEM_LIMIT),
        name="sparse_attn_prompt",
    )(u, u, small, k_tiles, vt_tiles, ki_tiles, btab, cfar)


def _attn_sample_kernel(q_ref, qi_ref, sm_ref, kt_ref, v_ref, ki_ref, bt_ref, y_ref, sc_ref, m_ref, l_ref, acc_ref,
                        *, tq, kt, n_keys, topk):
    lane = lax.broadcasted_iota(I32, (tq, LANES), 1)

    def adm_fn(j, s):
        return (j * kt + s * LANES + lane) < n_keys

    def bias_fn(j, s, h):
        return bt_ref[0, h, :, s * LANES:(s + 1) * LANES]

    _attn_body(tq=tq, kt=kt, nkt=1, topk=topk, qf=q_ref[...], qif=qi_ref[...], sm=sm_ref[...],
               kt_tile=lambda j, p: kt_ref[0, p * LANES:(p + 1) * LANES, :],
               v_tile=lambda j, p: v_ref[0, :, p * LANES:(p + 1) * LANES],
               ki_tile=lambda j: ki_ref[0],
               adm_fn=adm_fn, bias_fn=bias_fn,
               sc_ref=sc_ref, m_ref=m_ref, l_ref=l_ref, acc_ref=acc_ref, y_ref=y_ref)


def _attn_sample(u, small, kt_all, v_all, ki_all, btab, *, bsz, tq, kt, n_keys):
    return pl.pallas_call(
        functools.partial(_attn_sample_kernel, tq=tq, kt=kt, n_keys=n_keys, topk=min(TOPK, n_keys // 4)),
        out_shape=jax.ShapeDtypeStruct((bsz * tq, D_ATT), BF16),
        grid=(bsz,),
        in_specs=[pl.BlockSpec((tq, D_ATT), lambda bi: (bi, COL_Q // D_ATT)),
                  pl.BlockSpec((tq, D_ATT), lambda bi: (bi, COL_QI // D_ATT)),
                  pl.BlockSpec((tq, LANES), lambda bi: (bi, 0)),
                  pl.BlockSpec((1, D_ATT, kt), lambda bi: (bi, 0, 0)),
                  pl.BlockSpec((1, kt, D_ATT), lambda bi: (bi, 0, 0)),
                  pl.BlockSpec((1, LANES, kt), lambda bi: (bi, 0, 0)),
                  pl.BlockSpec((1, H_ATT, tq, kt), lambda bi: (0, 0, 0, 0))],
        out_specs=pl.BlockSpec((tq, D_ATT), lambda bi: (bi, 0)),
        scratch_shapes=[pltpu.VMEM((1, tq, kt), I32),
                        pltpu.VMEM((H_ATT, tq, LANES), F32),
                        pltpu.VMEM((H_ATT, tq, LANES), F32),
                        pltpu.VMEM((H_ATT, tq, LANES), F32)],
        compiler_params=pltpu.CompilerParams(dimension_semantics=("arbitrary",),
                                             vmem_limit_bytes=VMEM_LIMIT),
        name="sparse_attn_sample",
    )(u, u, small, kt_all, v_all, ki_all, btab)


def _out_proj_kernel(x_ref, ys_ref, yc_ref, ya_ref, w_ref, g_ref, o_ref):
    acc = _dot(ys_ref[...], w_ref[0:D_SSM, :])
    acc = acc + _dot(yc_ref[...], w_ref[D_SSM:D_SSM + D_CONV, :])
    acc = acc + _dot(ya_ref[...], w_ref[D_SSM + D_CONV:D_MODEL, :])
    o_ref[...] = x_ref[...] + _rms(acc, g_ref[...])


def _out_proj(x, ys, yc, ya, w, g, *, tm):
    m = x.shape[0]
    return pl.pallas_call(
        _out_proj_kernel,
        out_shape=jax.ShapeDtypeStruct((m, D_MODEL), F32),
        grid=(m // tm,),
        in_specs=[pl.BlockSpec((tm, D_MODEL), lambda i: (i, 0)),
                  pl.BlockSpec((tm, D_SSM), lambda i: (i, 0)),
                  pl.BlockSpec((tm, D_CONV), lambda i: (i, 0)),
                  pl.BlockSpec((tm, D_ATT), lambda i: (i, 0)),
                  pl.BlockSpec((D_MODEL, D_MODEL), lambda i: (0, 0)),
                  pl.BlockSpec((1, D_MODEL), lambda i: (0, 0))],
        out_specs=pl.BlockSpec((tm, D_MODEL), lambda i: (i, 0)),
        compiler_params=pltpu.CompilerParams(dimension_semantics=("arbitrary",),
                                             vmem_limit_bytes=VMEM_LIMIT),
        name="out_proj",
    )(x, ys, yc, ya, w, g)


FFN_HALO = 8


def _ffn_kernel(x_ref, gpre_ref, wg_ref, wu_ref, wd_ref, cw_ref, cb_ref, gpost_ref, p1_ref, p2_ref,
                o_ref, aux_ref, h_ref, buf_ref, tail_ref, *, tm, tps, seq_len, chained):
    i = pl.program_id(0)
    j = pl.program_id(1)

    @pl.when(j == 0)
    def _():
        h_ref[...] = _rms(x_ref[...], gpre_ref[...]).astype(BF16)
        o_ref[...] = jnp.zeros(o_ref.shape, F32)

    h = h_ref[...]
    a_pre = _dot(h, wg_ref[...])
    buf_ref[FFN_HALO:FFN_HALO + tm, :] = a_pre
    if chained:
        seq_start = lax.rem(i, tps) == 0
        buf_ref[0:FFN_HALO, :] = jnp.where(seq_start, p1_ref[0], tail_ref[j])
        prev1 = buf_ref[FFN_HALO - 1:FFN_HALO - 1 + tm, :]
        prev2 = buf_ref[FFN_HALO - 2:FFN_HALO - 2 + tm, :]
        last = a_pre[tm - FFN_HALO:tm, :]
        tail_ref[j] = last
        aux_ref[0] = last
    else:
        buf_ref[0:FFN_HALO, :] = jnp.zeros((FFN_HALO, a_pre.shape[1]), F32)
        tpos = lax.rem(lax.broadcasted_iota(I32, a_pre.shape, 0), seq_len)
        prev1 = jnp.where(tpos >= 1, buf_ref[FFN_HALO - 1:FFN_HALO - 1 + tm, :], p1_ref[...])
        prev2 = jnp.where(tpos >= 2, buf_ref[FFN_HALO - 2:FFN_HALO - 2 + tm, :], p2_ref[...])
        aux_ref[...] = a_pre
    a = cw_ref[0:1, :] * prev2 + cw_ref[1:2, :] * prev1 + cw_ref[2:3, :] * a_pre + cb_ref[...]
    f = (_silu(a) * _dot(h, wu_ref[...])).astype(BF16)
    o_ref[...] += _dot(f, wd_ref[...])

    @pl.when(j == pl.num_programs(1) - 1)
    def _():
        o_ref[...] = x_ref[...] + _rms(o_ref[...], gpost_ref[...])


def _ffn(x, gpre, wg, wu, wd, cw, cb, gpost, p1, p2, *, tm, tf, seq_len, chained):
    m = x.shape[0]
    nf = D_FF // tf
    tps = max(seq_len // tm, 1)
    if chained:
        nseq = m // seq_len
        p_specs = [pl.BlockSpec((1, FFN_HALO, tf), lambda i, j: (i // tps, 0, j)),
                   pl.BlockSpec((1, FFN_HALO, tf), lambda i, j: (i // tps, 0, j))]
        aux_shape = jax.ShapeDtypeStruct((m // tm, FFN_HALO, D_FF), F32)
        aux_spec = pl.BlockSpec((1, FFN_HALO, tf), lambda i, j: (i, 0, j))
    else:
        p_specs = [pl.BlockSpec((tm, tf), lambda i, j: (i, j)), pl.BlockSpec((tm, tf), lambda i, j: (i, j))]
        aux_shape = jax.ShapeDtypeStruct((m, D_FF), F32)
        aux_spec = pl.BlockSpec((tm, tf), lambda i, j: (i, j))
    return pl.pallas_call(
        functools.partial(_ffn_kernel, tm=tm, tps=tps, seq_len=seq_len, chained=chained),
        out_shape=(jax.ShapeDtypeStruct((m, D_MODEL), F32), aux_shape),
        grid=(m // tm, nf),
        in_specs=[pl.BlockSpec((tm, D_MODEL), lambda i, j: (i, 0), pipeline_mode=pl.Buffered(1)),
                  pl.BlockSpec((1, D_MODEL), lambda i, j: (0, 0)),
                  pl.BlockSpec((D_MODEL, tf), lambda i, j: (0, j)),
                  pl.BlockSpec((D_MODEL, tf), lambda i, j: (0, j)),
                  pl.BlockSpec((tf, D_MODEL), lambda i, j: (j, 0)),
                  pl.BlockSpec((FFN_CONV, tf), lambda i, j: (0, j)),
                  pl.BlockSpec((1, tf), lambda i, j: (0, j)),
                  pl.BlockSpec((1, D_MODEL), lambda i, j: (0, 0))] + p_specs,
        out_specs=(pl.BlockSpec((tm, D_MODEL), lambda i, j: (i, 0)), aux_spec),
        scratch_shapes=[pltpu.VMEM((tm, D_MODEL), BF16),
                        pltpu.VMEM((FFN_HALO + tm, tf), F32),
                        pltpu.VMEM((nf, FFN_HALO, tf), F32)],
        compiler_params=pltpu.CompilerParams(dimension_semantics=("arbitrary", "arbitrary"),
                                             vmem_limit_bytes=VMEM_LIMIT),
        name="conv_ffn",
    )(x, gpre, wg, wu, wd, cw, cb, gpost, p1, p2)


def _prep_layer_weights(w):
    w_in = w["w_in"]
    o_dt = D_SSM + D_XBC
    o_glu = o_dt + H_SSM
    o_ki = o_glu + 2 * D_CONV + 4 * D_ATT
    o_wi = o_ki + D_IDX
    o_q = o_glu + 2 * D_CONV
    o_k = o_q + D_ATT
    o_qi = o_k + 2 * D_ATT
    pad = jnp.zeros((D_MODEL, LANES - H_SSM - H_IDX - D_IDX), w_in.dtype)
    w_r = jnp.concatenate([w_in[:, :o_dt], w_in[:, o_glu:o_k], w_in[:, o_qi:o_ki],
                           w_in[:, o_k:o_qi],
                           w_in[:, o_dt:o_glu], w_in[:, o_wi:o_wi + H_IDX], pad, w_in[:, o_ki:o_wi]],
                          axis=1).astype(BF16)
    assert w_r.shape[1] == D_PROJ and o_qi + H_IDX * D_IDX == o_ki
    row = lambda v: v.reshape(1, -1).astype(F32)
    padl = lambda v: jnp.pad(v.astype(F32), (0, LANES - v.shape[0])).reshape(1, LANES)
    return dict(
        w_in=w_r, g_mix_pre=row(w["g_mix_pre"]),
        ssm_conv_w=w["ssm_conv_w"].astype(F32), ssm_conv_b=row(w["ssm_conv_b"]),
        dt_bias=padl(w["ssm_dt_bias"]), a_log=padl(w["ssm_a_log"]),
        d_x=row(jnp.repeat(w["ssm_d"], SSM_HEAD_DIM)), ssm_norm_g=row(w["ssm_norm_g"]),
        cconv_w=w["cconv_w"].astype(F32), cconv_b=row(w["cconv_b"]),
        cconv_ln_g=row(w["cconv_ln_g"]), cconv_ln_b=row(w["cconv_ln_b"]),
        w_out=w["w_out"].astype(BF16), g_mix_post=row(w["g_mix_post"]), g_ffn_pre=row(w["g_ffn_pre"]),
        ffn_w_gate=w["ffn_w_gate"].astype(BF16), ffn_w_up=w["ffn_w_up"].astype(BF16),
        ffn_w_down=w["ffn_w_down"].astype(BF16), ffn_conv_w=w["ffn_conv_w"].astype(F32),
        ffn_conv_b=row(w["ffn_conv_b"]), g_ffn_post=row(w["g_ffn_post"]))


def _expand_matrix():
    e = np.zeros((LANES, D_SSM), np.float32)
    for h in range(H_SSM):
        e[h, h * SSM_HEAD_DIM:(h + 1) * SSM_HEAD_DIM] = 1.0
    return jnp.asarray(e, BF16)


def _front_pad(state, halo):
    return jnp.pad(state.astype(F32), ((0, 0), (halo - state.shape[1], 0), (0, 0)))


def _state_t(h):
    b = h.shape[0]
    return jnp.transpose(h.astype(F32), (0, 3, 1, 2)).reshape(b, SSM_STATE, D_SSM)


def _state_from_t(ht):
    b = ht.shape[0]
    return jnp.transpose(ht.reshape(b, SSM_STATE, H_SSM, SSM_HEAD_DIM), (0, 2, 3, 1))


def _mixer_common(x, lw, emat, ssm_conv_prev, ssm_h0, cconv_prev, *, bsz, t, tm_proj, kt, ssd_l, ssd_nv, cc_tm):
    u, k, v, small, *attn_ops = _in_proj(x, lw["g_mix_pre"], lw["w_in"], tm=tm_proj, kt=kt)
    y_ssm, ht, ctail = _ssd(u, small, _front_pad(ssm_conv_prev, SSD_HALO), _state_t(ssm_h0), lw["ssm_conv_w"],
                            lw["ssm_conv_b"], lw["dt_bias"], lw["a_log"], lw["d_x"], lw["ssm_norm_g"], emat,
                            bsz=bsz, t=t, L=ssd_l, nv=ssd_nv)
    y_conv, cctail = _cconv(u, _front_pad(cconv_prev, CC_HALO), lw["cconv_w"], lw["cconv_b"], lw["cconv_ln_g"],
                            lw["cconv_ln_b"], bsz=bsz, t=t, tm=cc_tm)
    ki = small[:, SM_KI:SM_KI + D_IDX]
    states = dict(k=k.reshape(bsz, t, H_ATT, HEAD_DIM), v=v.reshape(bsz, t, H_ATT, HEAD_DIM),
                  ki=ki.reshape(bsz, t, D_IDX), h=_state_from_t(ht),
                  ssm_conv=ctail[:, SSD_HALO - (SSM_CONV - 1):], cconv=cctail[:, CC_HALO - (CONV_WIDTH - 1):])
    return u, small, y_ssm, y_conv, k, v, ki, attn_ops, states


def _layer_prompt(x, lw, emat, btab, *, bsz, t, cfg):
    zeros = lambda *s: jnp.zeros(s, F32)
    kt = cfg["kt"]
    nk = t // kt
    u, small, y_ssm, y_conv, k, v, ki, (kb, vt, ki2), st = _mixer_common(
        x, lw, emat, zeros(bsz, SSM_CONV - 1, D_XBC), zeros(bsz, H_SSM, SSM_HEAD_DIM, SSM_STATE),
        zeros(bsz, CONV_WIDTH - 1, D_CONV), bsz=bsz, t=t, tm_proj=cfg["tm_proj"], kt=kt,
        ssd_l=cfg["ssd_l"], ssd_nv=cfg["ssd_l"], cc_tm=cfg["cc_tm"])
    tq = cfg["tq"]
    cfar = jnp.tile(btab[N_BIAS_NEAR - 1, :, 0:1, :], (1, 1, tq // LANES))
    y_att = _attn_prompt(u, small, kb.reshape(bsz, nk, kt, D_ATT), vt.reshape(bsz, nk, D_ATT, kt),
                         ki2.reshape(bsz, nk, kt, LANES), btab, cfar, bsz=bsz, t=t, tq=tq, kt=kt)
    x1 = _out_proj(x, y_ssm, y_conv, y_att, lw["w_out"], lw["g_mix_post"], tm=cfg["tm_out"])
    prev = zeros(bsz, FFN_HALO, D_FF)
    x2, ftail = _ffn(x1, lw["g_ffn_pre"], lw["ffn_w_gate"], lw["ffn_w_up"], lw["ffn_w_down"], lw["ffn_conv_w"],
                     lw["ffn_conv_b"], lw["g_ffn_post"], prev, prev, tm=cfg["tm_ffn"], tf=cfg["tf"], seq_len=t,
                     chained=True)
    ftail = ftail.reshape(bsz, t // cfg["tm_ffn"], FFN_HALO, D_FF)[:, -1]
    st["ffn_conv"] = ftail[:, FFN_HALO - (FFN_CONV - 1):]
    return x2, st


def _layer_sample(x, lw, emat, btab, past_k, past_v, past_ki, ssm_conv_prev, ssm_h0, cconv_prev, fconv_prev,
                  *, bsz, t, cfg):
    m = bsz * t
    u, small, y_ssm, y_conv, k, v, ki, _, st = _mixer_common(
        x, lw, emat, ssm_conv_prev, ssm_h0, cconv_prev, bsz=bsz, t=t, tm_proj=m, kt=None,
        ssd_l=LANES, ssd_nv=t, cc_tm=t)
    past = past_k.shape[1]
    n_keys = past + t
    ktp = cfg["kt_sample"]
    padk = lambda a: jnp.pad(a, ((0, 0), (0, ktp - n_keys), (0, 0)))
    k_all = padk(jnp.concatenate([past_k.reshape(bsz, past, D_ATT), k.reshape(bsz, t, D_ATT)], axis=1).astype(BF16))
    v_all = padk(jnp.concatenate([past_v.reshape(bsz, past, D_ATT), v.reshape(bsz, t, D_ATT)], axis=1).astype(BF16))
    ki_all = padk(jnp.concatenate([past_ki, ki.reshape(bsz, t, D_IDX)], axis=1).astype(BF16))
    kt_all = jnp.transpose(k_all, (0, 2, 1))
    kit = jnp.transpose(ki_all, (0, 2, 1))
    y_att = _attn_sample(u, small, kt_all, v_all, jnp.concatenate([kit, kit], axis=1), btab, bsz=bsz, tq=t, kt=ktp,
                         n_keys=n_keys)
    x1 = _out_proj(x, y_ssm, y_conv, y_att, lw["w_out"], lw["g_mix_post"], tm=m)
    fprev = fconv_prev.astype(F32)
    zrow = jnp.zeros((bsz, t - 1, D_FF), F32)
    p1 = jnp.concatenate([fprev[:, 1:2], zrow], axis=1).reshape(m, D_FF)
    p2 = jnp.concatenate([fprev, zrow[:, 1:]], axis=1).reshape(m, D_FF)
    x2, a_pre = _ffn(x1, lw["g_ffn_pre"], lw["ffn_w_gate"], lw["ffn_w_up"], lw["ffn_w_down"], lw["ffn_conv_w"],
                     lw["ffn_conv_b"], lw["g_ffn_post"], p1, p2, tm=m, tf=cfg["tf"], seq_len=t, chained=False)
    st["ffn_conv"] = a_pre.reshape(bsz, t, D_FF)[:, t - (FFN_CONV - 1):]
    return x2, st


_STATE_ORDER = ("k", "v", "ki", "h", "ssm_conv", "cconv", "ffn_conv")


def _prompt_cfg(t):
    big = t >= 4096
    return dict(tm_proj=1024 if big else 512, ssd_l=256, cc_tm=256, kt=512, tq=256,
                tm_out=512 if big else 256, tm_ffn=1024 if big else 256, tf=512)


def _forward(x_prompt, x_sample, cache_k, cache_v, cache_kidx, state_ssm, state_ssm_conv, state_cconv,
             state_ffn_conv, rel_bias, weights):
    bp, tp, _ = x_prompt.shape
    bs, ts, _ = x_sample.shape
    depth = weights["w_in"].shape[0]
    past = cache_k.shape[2]
    emat = _expand_matrix()
    cfg_p = _prompt_cfg(tp)
    kt_sample = -(-(past + ts) // LANES) * LANES
    cfg_s = dict(tf=512, kt_sample=kt_sample)
    rb = rel_bias.astype(F32)
    btab_p = _bias_table(rb, nd=N_BIAS_NEAR, rows=LANES, cols=LANES, off0=0, step=LANES, key_axis=0, scale=LOG2E)
    btab_s = _bias_table(rb, nd=1, rows=ts, cols=kt_sample, off0=-past, step=0)
    xp = x_prompt.reshape(bp * tp, D_MODEL)
    xs = x_sample.reshape(bs * ts, D_MODEL)
    p_states = {n: [] for n in _STATE_ORDER}
    s_states = {n: [] for n in _STATE_ORDER}
    for l in range(depth):
        lw = _prep_layer_weights({n: w[l] for n, w in weights.items()})
        xp, st_p = _layer_prompt(xp, lw, emat, btab_p, bsz=bp, t=tp, cfg=cfg_p)
        xs, st_s = _layer_sample(xs, lw, emat, btab_s, cache_k[l], cache_v[l], cache_kidx[l], state_ssm_conv[l],
                                 state_ssm[l], state_cconv[l], state_ffn_conv[l], bsz=bs, t=ts, cfg=cfg_s)
        for n in _STATE_ORDER:
            p_states[n].append(st_p[n])
            s_states[n].append(st_s[n])
    outs = [xp.reshape(bp, tp, D_MODEL), xs.reshape(bs, ts, D_MODEL)]
    outs += [jnp.stack(p_states[n]) for n in _STATE_ORDER]
    outs += [jnp.stack(s_states[n]) for n in _STATE_ORDER]
    return tuple(outs)


def kernel(x_prompt, x_sample, cache_k, cache_v, cache_kidx, state_ssm, state_ssm_conv, state_cconv, state_ffn_conv, rel_bias, g_mix_pre, w_in, ssm_conv_w, ssm_conv_b, ssm_dt_bias, ssm_a_log, ssm_d, ssm_norm_g, cconv_w, cconv_b, cconv_ln_g, cconv_ln_b, w_out, g_mix_post, g_ffn_pre, ffn_w_gate, ffn_w_up, ffn_conv_w, ffn_conv_b, ffn_w_down, g_ffn_post):
    weights = dict(g_mix_pre=g_mix_pre, w_in=w_in, ssm_conv_w=ssm_conv_w, ssm_conv_b=ssm_conv_b,
                   ssm_dt_bias=ssm_dt_bias, ssm_a_log=ssm_a_log, ssm_d=ssm_d, ssm_norm_g=ssm_norm_g,
                   cconv_w=cconv_w, cconv_b=cconv_b, cconv_ln_g=cconv_ln_g, cconv_ln_b=cconv_ln_b, w_out=w_out,
                   g_mix_post=g_mix_post, g_ffn_pre=g_ffn_pre, ffn_w_gate=ffn_w_gate, ffn_w_up=ffn_w_up,
                   ffn_conv_w=ffn_conv_w, ffn_conv_b=ffn_conv_b, ffn_w_down=ffn_w_down, g_ffn_post=g_ffn_post)
    return _forward(x_prompt, x_sample, cache_k, cache_v, cache_kidx, state_ssm, state_ssm_conv, state_cconv,
                    state_ffn_conv, rel_bias, weights)
```

```python
import functools
import math

import numpy as np
import jax
import jax.numpy as jnp
from jax import lax
from jax.experimental import pallas as pl
from jax.experimental.pallas import tpu as pltpu

F32 = jnp.float32
BF16 = jnp.bfloat16
I32 = jnp.int32

D_MODEL = 2048
D_SSM = 1024
SSM_HEAD_DIM = 64
H_SSM = 16
SSM_GROUPS = 2
SSM_STATE = 128
SSM_CONV = 4
D_XBC = D_SSM + 2 * SSM_GROUPS * SSM_STATE
D_CONV = 512
CONV_WIDTH = 31
D_ATT = 512
HEAD_DIM = 64
H_ATT = 8
H_IDX = 8
D_IDX = 64
TOPK = 256
CHUNK = 64
N_BUCKETS = 32
REL_MAX_DIST = 1024
D_FF = 5632
FFN_CONV = 3
EPS = 1e-6

LANES = 128
SUBLANES = 8

COL_Z, COL_XBC, COL_GLU, COL_Q, COL_QI = 0, 1024, 2560, 3584, 4096
D_U = 4608
PROJ_TILE = 1152
TAIL_K, TAIL_V, TAIL_SMALL = 0, 512, 1024
D_PROJ = D_U + PROJ_TILE
SM_DT, SM_WI, SM_KI = 0, 16, 64

INT_MIN = -(2 ** 31)
INT_MAX = 2 ** 31 - 1
NEG_BIG = -1e30
VMEM_LIMIT = 56 * 1024 * 1024


def _bucket_thresholds():
    nb = N_BUCKETS // 2
    max_exact = nb // 2
    n = np.arange(0, 4 * REL_MAX_DIST, dtype=np.int64)
    nf = np.maximum(n, 1).astype(np.float32)
    large = max_exact + (np.log(nf / np.float32(max_exact)) / np.float32(math.log(REL_MAX_DIST / max_exact))
                         * np.float32(nb - max_exact)).astype(np.int32)
    large = np.minimum(large, nb - 1)
    bucket = np.where(n < max_exact, n, large)
    steps = np.nonzero(np.diff(bucket))[0] + 1
    assert np.all(np.diff(bucket) >= 0) and np.all(np.diff(bucket) <= 1) and bucket[-1] == nb - 1
    return tuple(int(s) for s in steps)


BUCKET_STEPS = _bucket_thresholds()


def _sigmoid(x):
    return 1.0 / (1.0 + jnp.exp(-x))


def _silu(x):
    return x * _sigmoid(x)


def _split3(x):
    hi = x.astype(BF16)
    r1 = x - hi.astype(F32)
    mid = r1.astype(BF16)
    lo = (r1 - mid.astype(F32)).astype(BF16)
    return hi, mid, lo


def _dot(a, b):
    return jnp.dot(a, b, preferred_element_type=F32)


def _dot_nt(a, b):
    return lax.dot_general(a, b, (((1,), (1,)), ((), ())), preferred_element_type=F32)


def _exact_dot(sel_bf16, x_f32):
    hi, mid, lo = _split3(x_f32)
    return _dot(sel_bf16, hi) + _dot(sel_bf16, mid) + _dot(sel_bf16, lo)


def _exact_dot_r(x_f32, sel_bf16):
    hi, mid, lo = _split3(x_f32)
    return _dot(hi, sel_bf16) + _dot(mid, sel_bf16) + _dot(lo, sel_bf16)


def _rms(x, g):
    ms = jnp.mean(x * x, axis=-1, keepdims=True)
    return x * lax.rsqrt(ms + EPS) * g


def _in_proj_kernel(x_ref, g_ref, w_ref, u_ref, k_ref, v_ref, sm_ref, *rest, kt):
    h_ref = rest[-1]
    j = pl.program_id(1)
    last = pl.num_programs(1) - 1

    @pl.when(j == 0)
    def _():
        h_ref[...] = _rms(x_ref[...], g_ref[...]).astype(BF16)

    y = _dot(h_ref[...], w_ref[...])

    @pl.when(j < last)
    def _():
        u_ref[...] = y

    @pl.when(j == last)
    def _():
        k = y[:, TAIL_K:TAIL_K + D_ATT]
        v = y[:, TAIL_V:TAIL_V + D_ATT]
        sm = y[:, TAIL_SMALL:TAIL_SMALL + LANES]
        k_ref[...] = k
        v_ref[...] = v
        sm_ref[...] = sm
        if kt is not None:
            kb_ref, vt_ref, ki2_ref = rest[:3]
            kb_ref[...] = k.astype(BF16)
            vb = v.astype(BF16)
            eye = jnp.where(lax.broadcasted_iota(I32, (LANES, LANES), 0)
                            == lax.broadcasted_iota(I32, (LANES, LANES), 1), 1.0, 0.0).astype(BF16)
            for c in range(vb.shape[0] // kt):
                for p in range(D_ATT // LANES):
                    blk = vb[c * kt:(c + 1) * kt, p * LANES:(p + 1) * LANES]
                    vt_ref[c, p * LANES:(p + 1) * LANES, :] = _dot_nt(eye, blk).astype(BF16)
            kib = sm[:, SM_KI:SM_KI + D_IDX].astype(BF16)
            ki2_ref[...] = jnp.concatenate([kib, kib], axis=1)


def _in_proj(x, g, w, *, tm, kt=None):
    m, d = x.shape
    nj = D_PROJ // PROJ_TILE
    assert m % tm == 0 and w.shape[1] == D_PROJ and (kt is None or tm % kt == 0)
    row = lambda i, j: (i, 0)
    out_shape = [jax.ShapeDtypeStruct((m, D_U), F32), jax.ShapeDtypeStruct((m, D_ATT), F32),
                 jax.ShapeDtypeStruct((m, D_ATT), F32), jax.ShapeDtypeStruct((m, LANES), F32)]
    out_specs = [pl.BlockSpec((tm, PROJ_TILE), lambda i, j: (i, jnp.minimum(j, nj - 2))),
                 pl.BlockSpec((tm, D_ATT), row), pl.BlockSpec((tm, D_ATT), row), pl.BlockSpec((tm, LANES), row)]
    if kt is not None:
        out_shape += [jax.ShapeDtypeStruct((m, D_ATT), BF16), jax.ShapeDtypeStruct((m // kt, D_ATT, kt), BF16),
                      jax.ShapeDtypeStruct((m, LANES), BF16)]
        out_specs += [pl.BlockSpec((tm, D_ATT), row), pl.BlockSpec((tm // kt, D_ATT, kt), lambda i, j: (i, 0, 0)),
                      pl.BlockSpec((tm, LANES), row)]
    return pl.pallas_call(
        functools.partial(_in_proj_kernel, kt=kt),
        out_shape=tuple(out_shape),
        grid=(m // tm, nj),
        in_specs=[pl.BlockSpec((tm, d), lambda i, j: (i, 0)),
                  pl.BlockSpec((1, d), lambda i, j: (0, 0)),
                  pl.BlockSpec((d, PROJ_TILE), lambda i, j: (0, j))],
        out_specs=tuple(out_specs),
        scratch_shapes=[pltpu.VMEM((tm, d), BF16)],
        compiler_params=pltpu.CompilerParams(dimension_semantics=("arbitrary", "arbitrary"),
                                             vmem_limit_bytes=VMEM_LIMIT),
        name="rms_in_proj",
    )(x, g, w)


def _bias_table_kernel(rb_ref, o_ref, *, off0, step, key_axis, scale):
    d = pl.program_id(0)
    h = pl.program_id(1)
    rows, cols = o_ref.shape[2], o_ref.shape[3]
    rel = (lax.broadcasted_iota(I32, (rows, cols), key_axis) - lax.broadcasted_iota(I32, (rows, cols), 1 - key_axis)
           + (off0 - d * step))
    n = jnp.abs(rel)
    bucket = jnp.where(rel > 0, N_BUCKETS // 2, 0)
    for s in BUCKET_STEPS:
        bucket = bucket + jnp.where(n >= s, 1, 0)
    acc = jnp.zeros((rows, cols), F32)
    for b in range(N_BUCKETS):
        acc = jnp.where(bucket == b, rb_ref[b, h], acc)
    o_ref[0, 0] = acc * scale


def _bias_table(rel_bias, *, nd, rows, cols, off0, step, key_axis=1, scale=1.0):
    return pl.pallas_call(
        functools.partial(_bias_table_kernel, off0=off0, step=step, key_axis=key_axis, scale=scale),
        out_shape=jax.ShapeDtypeStruct((nd, H_ATT, rows, cols), F32),
        grid=(nd, H_ATT),
        in_specs=[pl.BlockSpec(memory_space=pltpu.SMEM)],
        out_specs=pl.BlockSpec((1, 1, rows, cols), lambda d, h: (d, h, 0, 0)),
        name="bias_table",
    )(rel_bias)


CC_HALO = 32


def _cconv_kernel(val_ref, gate_ref, prev_ref, w_ref, b_ref, lg_ref, lb_ref, y_ref, tail_ref, buf_ref, sh_ref, *, tm):
    @pl.when(pl.program_id(1) == 0)
    def _():
        buf_ref[0:CC_HALO, :] = prev_ref[0]

    buf_ref[CC_HALO:CC_HALO + tm, :] = val_ref[...] * _sigmoid(gate_ref[...])
    span = CC_HALO + tm - SUBLANES
    for b in range(1, SUBLANES):
        sh_ref[b - 1, 0:span, :] = buf_ref[b:b + span, :]
    first = CC_HALO - (CONV_WIDTH - 1)
    acc = jnp.zeros((tm, D_CONV), F32) + b_ref[...]
    for k in range(CONV_WIDTH):
        base, b = (first + k) // SUBLANES * SUBLANES, (first + k) % SUBLANES
        rows = buf_ref[base:base + tm, :] if b == 0 else sh_ref[b - 1, base:base + tm, :]
        acc = acc + w_ref[k:k + 1, :] * rows
    mu = jnp.mean(acc, axis=-1, keepdims=True)
    xc = acc - mu
    var = jnp.mean(xc * xc, axis=-1, keepdims=True)
    y = xc * lax.rsqrt(var + EPS) * lg_ref[...] + lb_ref[...]
    y_ref[...] = _silu(y).astype(y_ref.dtype)
    tail = buf_ref[tm:tm + CC_HALO, :]
    tail_ref[0] = tail
    buf_ref[0:CC_HALO, :] = tail


def _cconv(u, prev, w, b, lg, lb, *, bsz, t, tm):
    nt = t // tm
    row = lambda bi, ti: bi * nt + ti
    return pl.pallas_call(
        functools.partial(_cconv_kernel, tm=tm),
        out_shape=(jax.ShapeDtypeStruct((bsz * t, D_CONV), BF16),
                   jax.ShapeDtypeStruct((bsz, CC_HALO, D_CONV), F32)),
        grid=(bsz, nt),
        in_specs=[pl.BlockSpec((tm, D_CONV), lambda bi, ti: (row(bi, ti), COL_GLU // D_CONV)),
                  pl.BlockSpec((tm, D_CONV), lambda bi, ti: (row(bi, ti), COL_GLU // D_CONV + 1)),
                  pl.BlockSpec((1, CC_HALO, D_CONV), lambda bi, ti: (bi, 0, 0)),
                  pl.BlockSpec((CONV_WIDTH, D_CONV), lambda bi, ti: (0, 0)),
                  pl.BlockSpec((1, D_CONV), lambda bi, ti: (0, 0)),
                  pl.BlockSpec((1, D_CONV), lambda bi, ti: (0, 0)),
                  pl.BlockSpec((1, D_CONV), lambda bi, ti: (0, 0))],
        out_specs=(pl.BlockSpec((tm, D_CONV), lambda bi, ti: (row(bi, ti), 0)),
                   pl.BlockSpec((1, CC_HALO, D_CONV), lambda bi, ti: (bi, 0, 0))),
        scratch_shapes=[pltpu.VMEM((CC_HALO + tm, D_CONV), F32),
                        pltpu.VMEM((SUBLANES - 1, CC_HALO + tm, D_CONV), F32)],
        compiler_params=pltpu.CompilerParams(dimension_semantics=("arbitrary", "arbitrary"),
                                             vmem_limit_bytes=VMEM_LIMIT),
        name="conformer_conv",
    )(u, u, prev, w, b, lg, lb)


SSD_HALO = 8


def _ssd_kernel(z_ref, x0_ref, x1_ref, x2_ref, sm_ref, prevc_ref, h0_ref, cw_ref, cb_ref, dtb_ref, alog_ref,
                dx_ref, ng_ref, e_ref, y_ref, hout_ref, ctail_ref, buf_ref, ht_ref, yb_ref, *, L, nv):
    @pl.when(pl.program_id(1) == 0)
    def _():
        buf_ref[0:SSD_HALO, :] = prevc_ref[0]
        ht_ref[...] = h0_ref[0]

    if nv < L:
        buf_ref[SSD_HALO + nv:SSD_HALO + L, :] = jnp.zeros((L - nv, D_XBC), F32)
    for c, r in enumerate((x0_ref, x1_ref, x2_ref)):
        buf_ref[SSD_HALO:SSD_HALO + nv, c * 512:(c + 1) * 512] = r[...]
    first = SSD_HALO - (SSM_CONV - 1)
    acc = jnp.zeros((L, D_XBC), F32) + cb_ref[...]
    for k in range(SSM_CONV):
        acc = acc + cw_ref[k:k + 1, :] * buf_ref[first + k:first + k + L, :]
    xbc = _silu(acc)
    tail = buf_ref[nv:nv + SSD_HALO, :]
    ctail_ref[0] = tail
    buf_ref[0:SSD_HALO, :] = tail

    xs = xbc[:, :D_SSM]
    lane = lax.broadcasted_iota(I32, (L, LANES), 1)
    rowi = lax.broadcasted_iota(I32, (L, LANES), 0)
    sm = sm_ref[...]
    if nv < L:
        sm = jnp.concatenate([sm, jnp.zeros((L - nv, LANES), F32)], axis=0)
    dtr = sm + dtb_ref[...]
    dt = jnp.maximum(dtr, 0.0) + jnp.log(1.0 + jnp.exp(-jnp.abs(dtr)))
    dt = jnp.where((lane < H_SSM) & (rowi < nv), dt, 0.0)
    a = -jnp.exp(alog_ref[...])
    da = dt * a
    ri = lax.broadcasted_iota(I32, (L, L), 0)
    ci = lax.broadcasted_iota(I32, (L, L), 1)
    causal = ri >= ci
    tril = jnp.where(causal, 1.0, 0.0).astype(BF16)
    cum = _exact_dot(tril, da)
    eye = jnp.where(lax.broadcasted_iota(I32, (LANES, LANES), 0) == lax.broadcasted_iota(I32, (LANES, LANES), 1),
                    1.0, 0.0).astype(BF16)
    ch, cm, cl = _split3(cum)
    cum_t = _dot_nt(eye, ch) + _dot_nt(eye, cm) + _dot_nt(eye, cl)
    e = e_ref[...]
    ecx = _exact_dot_r(jnp.exp(cum), e)
    dtx = _exact_dot_r(dt, e)
    xdt = (xs * dtx).astype(BF16)
    edl = ecx[L - 1:L, :]
    dend_t = jnp.exp(cum_t[:, L - 1:L] - cum_t)
    lane_l = lax.broadcasted_iota(I32, (L, LANES), 1)
    lane_n = lax.broadcasted_iota(I32, (SSM_STATE, LANES), 1)
    hpg = H_SSM // SSM_GROUPS
    for g in range(SSM_GROUPS):
        bg = xbc[:, D_SSM + g * SSM_STATE:D_SSM + (g + 1) * SSM_STATE].astype(BF16)
        cg = xbc[:, D_SSM + (SSM_GROUPS + g) * SSM_STATE:D_SSM + (SSM_GROUPS + g + 1) * SSM_STATE].astype(BF16)
        cbt = _dot_nt(cg, bg)
        bg_t = _dot_nt(eye, bg)
        gcols = slice(g * hpg * SSM_HEAD_DIM, (g + 1) * hpg * SSM_HEAD_DIM)
        yoff = _dot(cg, ht_ref[:, gcols].astype(BF16)) * ecx[:, gcols]
        for p in range(hpg // 2):
            h0 = g * hpg + 2 * p
            pcols = slice(h0 * SSM_HEAD_DIM, (h0 + 2) * SSM_HEAD_DIM)
            xpair = xdt[:, pcols]
            res, st = [], []
            for hh in (h0, h0 + 1):
                seg = cum[:, hh:hh + 1] - cum_t[hh:hh + 1, :]
                dec = jnp.where(causal, jnp.exp(seg), 0.0)
                res.append(_dot((cbt * dec).astype(BF16), xpair))
                st.append(_dot((bg_t * dend_t[hh:hh + 1, :]).astype(BF16), xpair))
            yb_ref[:, pcols] = (jnp.where(lane_l < SSM_HEAD_DIM, res[0], res[1])
                                + yoff[:, 2 * p * SSM_HEAD_DIM:(2 * p + 2) * SSM_HEAD_DIM])
            ht_ref[:, pcols] = (ht_ref[:, pcols] * edl[:, pcols]
                                + jnp.where(lane_n < SSM_HEAD_DIM, st[0], st[1]))
    hout_ref[0] = ht_ref[...]
    y = yb_ref[...] + dx_ref[...] * xs
    z = z_ref[...]
    if nv < L:
        z = jnp.concatenate([z, jnp.zeros((L - nv, D_SSM), F32)], axis=0)
    y = _rms(y * _silu(z), ng_ref[...])
    y_ref[...] = y[:nv].astype(y_ref.dtype)


def _ssd(u, small, prevc, h0t, cw, cb, dtb, alog, dx, ng, emat, *, bsz, t, L, nv):
    nt = t // nv
    row = lambda bi, ti: bi * nt + ti
    c512 = lambda c: (lambda bi, ti: (row(bi, ti), c))
    const2 = lambda bi, ti: (0, 0)
    return pl.pallas_call(
        functools.partial(_ssd_kernel, L=L, nv=nv),
        out_shape=(jax.ShapeDtypeStruct((bsz * t, D_SSM), BF16),
                   jax.ShapeDtypeStruct((bsz, SSM_STATE, D_SSM), F32),
                   jax.ShapeDtypeStruct((bsz, SSD_HALO, D_XBC), F32)),
        grid=(bsz, nt),
        in_specs=[pl.BlockSpec((nv, D_SSM), lambda bi, ti: (row(bi, ti), 0)),
                  pl.BlockSpec((nv, 512), c512(COL_XBC // 512)),
                  pl.BlockSpec((nv, 512), c512(COL_XBC // 512 + 1)),
                  pl.BlockSpec((nv, 512), c512(COL_XBC // 512 + 2)),
                  pl.BlockSpec((nv, LANES), c512(0)),
                  pl.BlockSpec((1, SSD_HALO, D_XBC), lambda bi, ti: (bi, 0, 0)),
                  pl.BlockSpec((1, SSM_STATE, D_SSM), lambda bi, ti: (bi, 0, 0)),
                  pl.BlockSpec((SSM_CONV, D_XBC), const2),
                  pl.BlockSpec((1, D_XBC), const2),
                  pl.BlockSpec((1, LANES), const2),
                  pl.BlockSpec((1, LANES), const2),
                  pl.BlockSpec((1, D_SSM), const2),
                  pl.BlockSpec((1, D_SSM), const2),
                  pl.BlockSpec((LANES, D_SSM), const2)],
        out_specs=(pl.BlockSpec((nv, D_SSM), lambda bi, ti: (row(bi, ti), 0)),
                   pl.BlockSpec((1, SSM_STATE, D_SSM), lambda bi, ti: (bi, 0, 0)),
                   pl.BlockSpec((1, SSD_HALO, D_XBC), lambda bi, ti: (bi, 0, 0))),
        scratch_shapes=[pltpu.VMEM((SSD_HALO + L, D_XBC), F32),
                        pltpu.VMEM((SSM_STATE, D_SSM), F32),
                        pltpu.VMEM((L, D_SSM), F32)],
        compiler_params=pltpu.CompilerParams(dimension_semantics=("arbitrary", "arbitrary"),
                                             vmem_limit_bytes=VMEM_LIMIT),
        name="ssd_mixer",
    )(u, u, u, u, small, prevc, h0t, cw, cb, dtb, alog, dx, ng, emat)


def _flip_negative(b):
    return b ^ ((b >> 31) & INT_MAX)


def _mono_key(x):
    return _flip_negative(lax.bitcast_convert_type(x, I32))


def _key_value(k):
    return lax.bitcast_convert_type(_flip_negative(k), F32)


F32_BIG = 3e38
MID_PERIOD = 12
BOOST_MAX = 2.0 ** 30
SEARCH_MAX_STEPS = 400


def _attn_body(*, tq, kt, nkt, topk, qf, qif, sm, kt_tile, v_tile, ki_tile, adm_fn, bias_fn,
               sc_ref, m_ref, l_ref, acc_ref, y_ref):
    nsl = kt // LANES
    lane = lax.broadcasted_iota(I32, (tq, LANES), 1)
    low = lane < HEAD_DIM
    qb = (qf * (HEAD_DIM ** -0.5)).astype(BF16)
    qib = qif.astype(BF16)
    wi = sm[:, SM_WI:SM_WI + H_IDX] * ((D_IDX ** -0.5) * (H_IDX ** -0.5))
    zero_b = jnp.zeros((tq, LANES), BF16)

    def head_window(x, h):
        win = x[:, (h // 2) * LANES:(h // 2 + 1) * LANES]
        return jnp.where(low if h % 2 == 0 else jnp.logical_not(low), win, zero_b)

    qim = [head_window(qib, h) for h in range(H_IDX)]
    wcol = [wi[:, h:h + 1] for h in range(H_IDX)]

    def p1(j, carry):
        ki = ki_tile(j)
        acc = jnp.zeros((tq, kt), F32)
        for h in range(H_IDX):
            acc = acc + jnp.maximum(_dot(qim[h], ki), 0.0) * wcol[h]
        for s in range(nsl):
            key = jnp.where(adm_fn(j, s), _mono_key(acc[:, s * LANES:(s + 1) * LANES]), INT_MIN)
            sc_ref[j, :, s * LANES:(s + 1) * LANES] = key
        return carry

    lax.fori_loop(0, nkt, p1, 0)

    def count(pred):
        def body(j, acc):
            tile = sc_ref[j]
            for s in range(nsl):
                acc = acc + jnp.where(pred(tile[:, s * LANES:(s + 1) * LANES], j, s), 1.0, 0.0)
            return acc
        acc = lax.fori_loop(0, nkt, body, jnp.zeros((tq, LANES), F32))
        return jnp.sum(acc, axis=1, keepdims=True)

    def count_ge(cand):
        cb = jnp.broadcast_to(cand, (tq, LANES))
        return count(lambda t, j, s: t >= cb)

    kf = float(topk)
    prefix = jnp.where(count_ge(jnp.zeros((tq, 1), I32)) >= kf, 0, INT_MIN).astype(I32)

    def bit_step(it, prefix):
        trial = prefix | lax.shift_left(jnp.int32(1), 30 - it)
        return jnp.where(count_ge(trial) >= kf, trial, prefix)

    thr = lax.fori_loop(0, 31, bit_step, prefix)
    thr = jnp.maximum(thr, INT_MIN + 1)
    thr_b = jnp.broadcast_to(thr, (tq, LANES))
    n_gt = count(lambda t, j, s: t > thr_b)
    n_eq = count(lambda t, j, s: t == thr_b)
    take = kf - n_gt

    def kpos(j, s):
        return j * kt + s * LANES + lane

    def tie_cut(_):
        def step(it, cut):
            trial = cut | lax.shift_left(jnp.int32(1), 30 - it)
            tb = jnp.broadcast_to(trial, (tq, LANES))
            c = count(lambda t, j, s: (t == thr_b) & (kpos(j, s) < tb))
            return jnp.where(c <= take, trial, cut)
        return lax.fori_loop(0, 31, step, jnp.zeros((tq, 1), I32))

    has_tie = jnp.max(jnp.where(n_gt + n_eq > kf, 1.0, 0.0)) > 0.0
    cut = lax.cond(has_tie, tie_cut, lambda _: jnp.full((tq, 1), INT_MAX, I32), 0)
    cut_b = jnp.broadcast_to(cut, (tq, LANES))

    qm = [head_window(qb, h) for h in range(H_ATT)]
    m_ref[...] = jnp.full(m_ref.shape, NEG_BIG, F32)
    l_ref[...] = jnp.zeros(l_ref.shape, F32)
    acc_ref[...] = jnp.zeros(acc_ref.shape, F32)

    def p3(j, carry):
        keyt = sc_ref[j]
        sel = []
        for s in range(nsl):
            ks = keyt[:, s * LANES:(s + 1) * LANES]
            sel.append((ks > thr_b) | ((ks == thr_b) & (kpos(j, s) < cut_b)))
        for p in range(H_ATT // 2):
            kp = kt_tile(j, p)
            vp = v_tile(j, p)
            for hsub in range(2):
                h = 2 * p + hsub
                s_all = _dot(qm[h], kp)
                parts = [jnp.where(sel[s], s_all[:, s * LANES:(s + 1) * LANES] + bias_fn(j, s, h), NEG_BIG)
                         for s in range(nsl)]
                mx = parts[0]
                for s in range(1, nsl):
                    mx = jnp.maximum(mx, parts[s])
                m_old = m_ref[h]
                m_new = jnp.maximum(m_old, jnp.max(mx, axis=1, keepdims=True))
                alpha = jnp.exp(m_old - m_new)
                pr = [jnp.exp(part - m_new) for part in parts]
                psum = pr[0]
                for s in range(1, nsl):
                    psum = psum + pr[s]
                l_ref[h] = alpha * l_ref[h] + psum
                pb = jnp.concatenate([x.astype(BF16) for x in pr], axis=1)
                acc_ref[h] = alpha * acc_ref[h] + _dot(pb, vp)
                m_ref[h] = m_new
        return carry

    lax.fori_loop(0, nkt, p3, 0)
    for p in range(H_ATT // 2):
        outs = []
        for hsub in range(2):
            h = 2 * p + hsub
            lsum = jnp.sum(l_ref[h], axis=1, keepdims=True)
            outs.append(acc_ref[h] / lsum)
        y_ref[:, p * LANES:(p + 1) * LANES] = jnp.where(low, outs[0], outs[1]).astype(y_ref.dtype)


N_BIAS_NEAR = 7


LOG2E = math.log2(math.e)
PV_ROWS = 256


def _attn_prompt_kernel(q_ref, qi_ref, sm_ref, k_ref, vt_ref, ki_ref, bt_ref, cf_ref, y_ref,
                        sc_ref, qt_ref, zb_ref, nm_ref, m_ref, l_ref, acc_ref, *, tq, kt, topk):
    i = pl.program_id(1)
    nql = tq // LANES
    nsl = kt // LANES
    nkt = lax.div((i + 1) * tq + (kt - 1), kt)

    def fold(x, op, chains=4):
        groups = x.shape[0] // SUBLANES
        accs = [x[a * SUBLANES:(a + 1) * SUBLANES] for a in range(chains)]
        for r in range(chains, groups):
            accs[r % chains] = op(accs[r % chains], x[r * SUBLANES:(r + 1) * SUBLANES])
        while len(accs) > 1:
            accs = [op(accs[a], accs[a + len(accs) // 2]) for a in range(len(accs) // 2)]
        return accs[0]

    eye = jnp.where(lax.broadcasted_iota(I32, (LANES, LANES), 0) == lax.broadcasted_iota(I32, (LANES, LANES), 1),
                    1.0, 0.0).astype(BF16)
    lane = lax.broadcasted_iota(I32, (tq, LANES), 1)
    low = lane < HEAD_DIM
    qb = (q_ref[...] * (HEAD_DIM ** -0.5 * LOG2E)).astype(BF16)
    qib = qi_ref[...].astype(BF16)
    zero_b = jnp.zeros((tq, LANES), BF16)
    for src, base in ((qib, 0), (qb, H_IDX)):
        for h in range(H_ATT):
            win = src[:, (h // 2) * LANES:(h // 2 + 1) * LANES]
            win = jnp.where(low if h % 2 == 0 else jnp.logical_not(low), win, zero_b)
            qt_ref[base + h] = _dot_nt(eye, win).astype(BF16)
    sh, smm, sl_ = _split3(sm_ref[...])
    sm_t = _dot_nt(eye, sh) + _dot_nt(eye, smm) + _dot_nt(eye, sl_)
    wrow = [sm_t[SM_WI + h:SM_WI + h + 1, :] * ((D_IDX ** -0.5) * (H_IDX ** -0.5)) for h in range(H_IDX)]

    qpos = i * tq + lax.broadcasted_iota(I32, (1, tq), 1)
    cend = (lax.shift_right_logical(qpos, int(math.log2(CHUNK))) + 1) * CHUNK
    krow = lax.broadcasted_iota(I32, (LANES, tq), 0)

    def p1(j, carry):
        vmax, vmin = carry
        ki2 = ki_ref[0, j]
        for h in range(H_IDX):
            t = jnp.maximum(_dot(ki2, qt_ref[h]), 0.0) * wrow[h]
            if h == 0:
                zb_ref[0] = t
            elif h < H_IDX - 1:
                zb_ref[0] += t
            else:
                for sl in range(nsl):
                    rows = slice(sl * LANES, (sl + 1) * LANES)
                    sc = zb_ref[0, rows, :] + t[rows]
                    sc = jnp.where(sc == 0.0, 0.0, sc)
                    adm = (j * nsl + sl) * LANES + krow < cend
                    sc_ref[j * nsl + sl] = jnp.where(adm, _mono_key(sc), INT_MIN)
                    vmax = jnp.maximum(vmax, fold(jnp.where(adm, sc, -F32_BIG), jnp.maximum))
                    vmin = jnp.minimum(vmin, fold(jnp.where(adm, sc, F32_BIG), jnp.minimum))
        return vmax, vmin

    vmax, vmin = lax.fori_loop(0, nkt, p1, (jnp.full((SUBLANES, tq), -F32_BIG, F32),
                                            jnp.full((SUBLANES, tq), F32_BIG, F32)))

    def count(pred):
        def body(j, acc):
            for sl in range(nsl):
                g = j * nsl + sl
                acc = acc + fold(jnp.where(pred(sc_ref[g], g), 1.0, 0.0), jnp.add)
            return acc
        acc = lax.fori_loop(0, nkt, body, jnp.zeros((SUBLANES, tq), F32))
        return jnp.sum(acc, axis=0, keepdims=True)

    kf = float(topk)

    def active_of(lo, hi, clo):
        return (clo > kf) & (hi - 1 > lo)

    def search_cond(st):
        it, lo, hi, clo, chi, side, boost = st
        return (it < SEARCH_MAX_STEPS) & (jnp.max(jnp.where(active_of(lo, hi, clo), 1.0, 0.0)) > 0.0)

    def search_step(st):
        it, lo, hi, clo, chi, side, boost = st
        active = active_of(lo, hi, clo)
        v_lo = _key_value(lo)
        v_hi = _key_value(hi)
        frac = (jnp.log(clo) - math.log(kf)) / (jnp.log(clo) - jnp.log(jnp.maximum(chi, 0.5)))
        frac = jnp.where(side > 0, jnp.minimum(frac * boost, 0.5),
                         jnp.where(side < 0, 1.0 - jnp.minimum((1.0 - frac) * boost, 0.5), frac))
        t_int = _mono_key(v_lo + (v_hi - v_lo) * frac)
        t_mid = (lo >> 1) + (hi >> 1) + (lo & hi & 1)
        trial = jnp.where(lax.rem(it, MID_PERIOD) == MID_PERIOD - 1, t_mid, t_int)
        trial = jnp.where(it == 0, 0, jnp.where((it == 1) & (lo == 0), 1, trial))
        trial = jnp.minimum(jnp.maximum(trial, lo + 1), hi - 1)
        c = count(lambda t, g: t >= trial)
        up = active & (c >= kf)
        dn = active & (c < kf)
        now = jnp.where(c >= kf, 1, -1)
        boost = jnp.where(active, jnp.where(now == side, jnp.minimum(boost * 2.0, BOOST_MAX), 1.0), boost)
        side = jnp.where(active, now, side)
        return (it + 1, jnp.where(up, trial, lo), jnp.where(dn, trial, hi),
                jnp.where(up, c, clo), jnp.where(dn, c, chi), side, boost)

    lo0 = _mono_key(jnp.min(vmin, axis=0, keepdims=True))
    hi0 = _mono_key(jnp.max(vmax, axis=0, keepdims=True)) + 1
    _, thr, _, n_ge, n_gt, _, _ = lax.while_loop(
        search_cond, search_step,
        (jnp.int32(0), lo0, hi0, cend.astype(F32), jnp.zeros((1, tq), F32),
         jnp.zeros((1, tq), I32), jnp.ones((1, tq), F32)))
    take = kf - n_gt

    def tie_cut(_):
        def step(it, cut):
            trial = cut | lax.shift_left(jnp.int32(1), 30 - it)
            c = count(lambda t, g: (t == thr) & (g * LANES + krow < trial))
            return jnp.where(c <= take, trial, cut)
        return lax.fori_loop(0, 31, step, jnp.zeros((1, tq), I32))

    has_tie = jnp.max(jnp.where(n_ge > kf, 1.0, 0.0)) > 0.0
    cut = lax.cond(has_tie, tie_cut, lambda _: jnp.full((1, tq), INT_MAX, I32), 0)

    m_ref[...] = jnp.full(m_ref.shape, NEG_BIG, F32)
    l_ref[...] = jnp.zeros(l_ref.shape, F32)
    acc_ref[...] = jnp.zeros(acc_ref.shape, F32)

    def tile_step(j, near):
        for sl in range(nsl):
            g = j * nsl + sl
            keyt = sc_ref[g]
            sel = (keyt > thr) | ((keyt == thr) & (g * LANES + krow < cut))
            nm_ref[sl * LANES:(sl + 1) * LANES, :] = jnp.where(sel, 0.0, NEG_BIG)

        def logits(h):
            mx = None
            for sl in range(nsl):
                rows = slice(sl * LANES, (sl + 1) * LANES)
                z = _dot(k_ref[0, j, rows, (h // 2) * LANES:(h // 2 + 1) * LANES], qt_ref[H_IDX + h]) + nm_ref[rows, :]
                if near:
                    z = z + jnp.concatenate(
                        [bt_ref[jnp.clip((i * nql + hf) - (j * nsl + sl), 0, N_BIAS_NEAR - 1), h]
                         for hf in range(nql)], axis=1)
                zb_ref[h % 2, rows, :] = z
                cm = fold(z, jnp.maximum)
                mx = cm if mx is None else jnp.maximum(mx, cm)
            return mx

        def accumulate(h, mx):
            shift = 0.0 if near else cf_ref[h]
            m_old = m_ref[h]
            m_new = jnp.maximum(m_old, jnp.max(mx, axis=0, keepdims=True) + shift)
            alpha = jnp.exp2(m_old - m_new)
            msub = m_new - shift
            lsum, pv = None, None
            for c in range(kt // PV_ROWS):
                rows = slice(c * PV_ROWS, (c + 1) * PV_ROWS)
                p = jnp.exp2(zb_ref[h % 2, rows, :] - msub)
                ls = fold(p, jnp.add)
                pc = _dot(vt_ref[0, j, (h // 2) * LANES:(h // 2 + 1) * LANES, rows], p.astype(BF16))
                lsum = ls if lsum is None else lsum + ls
                pv = pc if pv is None else pv + pc
            l_ref[h] = alpha * l_ref[h] + lsum
            acc_ref[h] = alpha * acc_ref[h] + pv
            m_ref[h] = m_new

        mx_next = logits(0)
        for h in range(H_ATT):
            mx_cur = mx_next
            if h + 1 < H_ATT:
                mx_next = logits(h + 1)
            accumulate(h, mx_cur)

    n_far = jnp.minimum(lax.div(jnp.maximum(nql * i - (N_BIAS_NEAR - 1) - (nsl - 1) + nsl, 0), nsl), nkt)

    def p3_far(j, carry):
        tile_step(j, False)
        return carry

    def p3_near(j, carry):
        tile_step(j, True)
        return carry

    lax.fori_loop(0, n_far, p3_far, 0)
    lax.fori_loop(n_far, nkt, p3_near, 0)

    eye_q = jnp.where(lax.broadcasted_iota(I32, (tq, tq), 0) == lax.broadcasted_iota(I32, (tq, tq), 1),
                      1.0, 0.0).astype(BF16)
    rowd = lax.broadcasted_iota(I32, (LANES, tq), 0)
    for p in range(H_ATT // 2):
        outs = []
        for hsub in range(2):
            h = 2 * p + hsub
            outs.append(acc_ref[h] / jnp.sum(l_ref[h], axis=0, keepdims=True))
        y_t = jnp.where(rowd < HEAD_DIM, outs[0], outs[1]).astype(BF16)
        y_ref[:, p * LANES:(p + 1) * LANES] = _dot_nt(eye_q, y_t).astype(y_ref.dtype)


def _attn_prompt(u, small, k_tiles, vt_tiles, ki_tiles, btab, cfar, *, bsz, t, tq, kt):
    assert t % kt == 0 and t % tq == 0 and tq % LANES == 0 and kt % LANES == 0
    nq = t // tq
    nk = t // kt
    row = lambda bi, qi: bi * nq + qi
    once = pl.Buffered(1)
    return pl.pallas_call(
        functools.partial(_attn_prompt_kernel, tq=tq, kt=kt, topk=min(TOPK, t // 4)),
        out_shape=jax.ShapeDtypeStruct((bsz * t, D_ATT), BF16),
        grid=(bsz, nq),
        in_specs=[pl.BlockSpec((tq, D_ATT), lambda bi, qi: (row(bi, qi), COL_Q // D_ATT)),
                  pl.BlockSpec((tq, D_ATT), lambda bi, qi: (row(bi, qi), COL_QI // D_ATT)),
                  pl.BlockSpec((tq, LANES), lambda bi, qi: (row(bi, qi), 0)),
                  pl.BlockSpec((1, nk, kt, D_ATT), lambda bi, qi: (bi, 0, 0, 0), pipeline_mode=once),
                  pl.BlockSpec((1, nk, D_ATT, kt), lambda bi, qi: (bi, 0, 0, 0), pipeline_mode=once),
                  pl.BlockSpec((1, nk, kt, LANES), lambda bi, qi: (bi, 0, 0, 0), pipeline_mode=once),
                  pl.BlockSpec((N_BIAS_NEAR, H_ATT, LANES, LANES), lambda bi, qi: (0, 0, 0, 0), pipeline_mode=once),
                  pl.BlockSpec((H_ATT, 1, tq), lambda bi, qi: (0, 0, 0), pipeline_mode=once)],
        out_specs=pl.BlockSpec((tq, D_ATT), lambda bi, qi: (row(bi, qi), 0)),
        scratch_shapes=[pltpu.VMEM((t // LANES, LANES, tq), I32),
                        pltpu.VMEM((H_IDX + H_ATT, LANES, tq), BF16),
                        pltpu.VMEM((2, kt, tq), F32),
                        pltpu.VMEM((kt, tq), F32),
                        pltpu.VMEM((H_ATT, 1, tq), F32),
                        pltpu.VMEM((H_ATT, SUBLANES, tq), F32),
                        pltpu.VMEM((H_ATT, LANES, tq), F32)],
        compiler_params=pltpu.CompilerParams(dimension_semantics=("arbitrary", "arbitrary"),
                                             vmem_limit_bytes=VMEM_LIMIT),
        name="sparse_attn_prompt",
    )(u, u, small, k_tiles, vt_tiles, ki_tiles, btab, cfar)


def _attn_sample_kernel(q_ref, qi_ref, sm_ref, kt_ref, v_ref, ki_ref, bt_ref, y_ref, sc_ref, m_ref, l_ref, acc_ref,
                        *, tq, kt, n_keys, topk):
    lane = lax.broadcasted_iota(I32, (tq, LANES), 1)

    def adm_fn(j, s):
        return (j * kt + s * LANES + lane) < n_keys

    def bias_fn(j, s, h):
        return bt_ref[0, h, :, s * LANES:(s + 1) * LANES]

    _attn_body(tq=tq, kt=kt, nkt=1, topk=topk, qf=q_ref[...], qif=qi_ref[...], sm=sm_ref[...],
               kt_tile=lambda j, p: kt_ref[0, p * LANES:(p + 1) * LANES, :],
               v_tile=lambda j, p: v_ref[0, :, p * LANES:(p + 1) * LANES],
               ki_tile=lambda j: ki_ref[0],
               adm_fn=adm_fn, bias_fn=bias_fn,
               sc_ref=sc_ref, m_ref=m_ref, l_ref=l_ref, acc_ref=acc_ref, y_ref=y_ref)


def _attn_sample(u, small, kt_all, v_all, ki_all, btab, *, bsz, tq, kt, n_keys):
    return pl.pallas_call(
        functools.partial(_attn_sample_kernel, tq=tq, kt=kt, n_keys=n_keys, topk=min(TOPK, n_keys // 4)),
        out_shape=jax.ShapeDtypeStruct((bsz * tq, D_ATT), BF16),
        grid=(bsz,),
        in_specs=[pl.BlockSpec((tq, D_ATT), lambda bi: (bi, COL_Q // D_ATT)),
                  pl.BlockSpec((tq, D_ATT), lambda bi: (bi, COL_QI // D_ATT)),
                  pl.BlockSpec((tq, LANES), lambda bi: (bi, 0)),
                  pl.BlockSpec((1, D_ATT, kt), lambda bi: (bi, 0, 0)),
                  pl.BlockSpec((1, kt, D_ATT), lambda bi: (bi, 0, 0)),
                  pl.BlockSpec((1, LANES, kt), lambda bi: (bi, 0, 0)),
                  pl.BlockSpec((1, H_ATT, tq, kt), lambda bi: (0, 0, 0, 0))],
        out_specs=pl.BlockSpec((tq, D_ATT), lambda bi: (bi, 0)),
        scratch_shapes=[pltpu.VMEM((1, tq, kt), I32),
                        pltpu.VMEM((H_ATT, tq, LANES), F32),
                        pltpu.VMEM((H_ATT, tq, LANES), F32),
                        pltpu.VMEM((H_ATT, tq, LANES), F32)],
        compiler_params=pltpu.CompilerParams(dimension_semantics=("arbitrary",),
                                             vmem_limit_bytes=VMEM_LIMIT),
        name="sparse_attn_sample",
    )(u, u, small, kt_all, v_all, ki_all, btab)


def _out_proj_kernel(x_ref, ys_ref, yc_ref, ya_ref, w_ref, g_ref, o_ref):
    acc = _dot(ys_ref[...], w_ref[0:D_SSM, :])
    acc = acc + _dot(yc_ref[...], w_ref[D_SSM:D_SSM + D_CONV, :])
    acc = acc + _dot(ya_ref[...], w_ref[D_SSM + D_CONV:D_MODEL, :])
    o_ref[...] = x_ref[...] + _rms(acc, g_ref[...])


def _out_proj(x, ys, yc, ya, w, g, *, tm):
    m = x.shape[0]
    return pl.pallas_call(
        _out_proj_kernel,
        out_shape=jax.ShapeDtypeStruct((m, D_MODEL), F32),
        grid=(m // tm,),
        in_specs=[pl.BlockSpec((tm, D_MODEL), lambda i: (i, 0)),
                  pl.BlockSpec((tm, D_SSM), lambda i: (i, 0)),
                  pl.BlockSpec((tm, D_CONV), lambda i: (i, 0)),
                  pl.BlockSpec((tm, D_ATT), lambda i: (i, 0)),
                  pl.BlockSpec((D_MODEL, D_MODEL), lambda i: (0, 0)),
                  pl.BlockSpec((1, D_MODEL), lambda i: (0, 0))],
        out_specs=pl.BlockSpec((tm, D_MODEL), lambda i: (i, 0)),
        compiler_params=pltpu.CompilerParams(dimension_semantics=("arbitrary",),
                                             vmem_limit_bytes=VMEM_LIMIT),
        name="out_proj",
    )(x, ys, yc, ya, w, g)


FFN_HALO = 8


def _ffn_kernel(x_ref, gpre_ref, wg_ref, wu_ref, wd_ref, cw_ref, cb_ref, gpost_ref, p1_ref, p2_ref,
                o_ref, aux_ref, h_ref, buf_ref, tail_ref, *, tm, tps, seq_len, chained):
    i = pl.program_id(0)
    j = pl.program_id(1)

    @pl.when(j == 0)
    def _():
        h_ref[...] = _rms(x_ref[...], gpre_ref[...]).astype(BF16)
        o_ref[...] = jnp.zeros(o_ref.shape, F32)

    h = h_ref[...]
    a_pre = _dot(h, wg_ref[...])
    buf_ref[FFN_HALO:FFN_HALO + tm, :] = a_pre
    if chained:
        seq_start = lax.rem(i, tps) == 0
        buf_ref[0:FFN_HALO, :] = jnp.where(seq_start, p1_ref[0], tail_ref[j])
        prev1 = buf_ref[FFN_HALO - 1:FFN_HALO - 1 + tm, :]
        prev2 = buf_ref[FFN_HALO - 2:FFN_HALO - 2 + tm, :]
        last = a_pre[tm - FFN_HALO:tm, :]
        tail_ref[j] = last
        aux_ref[0] = last
    else:
        buf_ref[0:FFN_HALO, :] = jnp.zeros((FFN_HALO, a_pre.shape[1]), F32)
        tpos = lax.rem(lax.broadcasted_iota(I32, a_pre.shape, 0), seq_len)
        prev1 = jnp.where(tpos >= 1, buf_ref[FFN_HALO - 1:FFN_HALO - 1 + tm, :], p1_ref[...])
        prev2 = jnp.where(tpos >= 2, buf_ref[FFN_HALO - 2:FFN_HALO - 2 + tm, :], p2_ref[...])
        aux_ref[...] = a_pre
    a = cw_ref[0:1, :] * prev2 + cw_ref[1:2, :] * prev1 + cw_ref[2:3, :] * a_pre + cb_ref[...]
    f = (_silu(a) * _dot(h, wu_ref[...])).astype(BF16)
    o_ref[...] += _dot(f, wd_ref[...])

    @pl.when(j == pl.num_programs(1) - 1)
    def _():
        o_ref[...] = x_ref[...] + _rms(o_ref[...], gpost_ref[...])


def _ffn(x, gpre, wg, wu, wd, cw, cb, gpost, p1, p2, *, tm, tf, seq_len, chained):
    m = x.shape[0]
    nf = D_FF // tf
    tps = max(seq_len // tm, 1)
    if chained:
        nseq = m // seq_len
        p_specs = [pl.BlockSpec((1, FFN_HALO, tf), lambda i, j: (i // tps, 0, j)),
                   pl.BlockSpec((1, FFN_HALO, tf), lambda i, j: (i // tps, 0, j))]
        aux_shape = jax.ShapeDtypeStruct((m // tm, FFN_HALO, D_FF), F32)
        aux_spec = pl.BlockSpec((1, FFN_HALO, tf), lambda i, j: (i, 0, j))
    else:
        p_specs = [pl.BlockSpec((tm, tf), lambda i, j: (i, j)), pl.BlockSpec((tm, tf), lambda i, j: (i, j))]
        aux_shape = jax.ShapeDtypeStruct((m, D_FF), F32)
        aux_spec = pl.BlockSpec((tm, tf), lambda i, j: (i, j))
    return pl.pallas_call(
        functools.partial(_ffn_kernel, tm=tm, tps=tps, seq_len=seq_len, chained=chained),
        out_shape=(jax.ShapeDtypeStruct((m, D_MODEL), F32), aux_shape),
        grid=(m // tm, nf),
        in_specs=[pl.BlockSpec((tm, D_MODEL), lambda i, j: (i, 0), pipeline_mode=pl.Buffered(1)),
                  pl.BlockSpec((1, D_MODEL), lambda i, j: (0, 0)),
                  pl.BlockSpec((D_MODEL, tf), lambda i, j: (0, j)),
                  pl.BlockSpec((D_MODEL, tf), lambda i, j: (0, j)),
                  pl.BlockSpec((tf, D_MODEL), lambda i, j: (j, 0)),
                  pl.BlockSpec((FFN_CONV, tf), lambda i, j: (0, j)),
                  pl.BlockSpec((1, tf), lambda i, j: (0, j)),
                  pl.BlockSpec((1, D_MODEL), lambda i, j: (0, 0))] + p_specs,
        out_specs=(pl.BlockSpec((tm, D_MODEL), lambda i, j: (i, 0)), aux_spec),
        scratch_shapes=[pltpu.VMEM((tm, D_MODEL), BF16),
                        pltpu.VMEM((FFN_HALO + tm, tf), F32),
                        pltpu.VMEM((nf, FFN_HALO, tf), F32)],
        compiler_params=pltpu.CompilerParams(dimension_semantics=("arbitrary", "arbitrary"),
                                             vmem_limit_bytes=VMEM_LIMIT),
        name="conv_ffn",
    )(x, gpre, wg, wu, wd, cw, cb, gpost, p1, p2)


def _prep_layer_weights(w):
    w_in = w["w_in"]
    o_dt = D_SSM + D_XBC
    o_glu = o_dt + H_SSM
    o_ki = o_glu + 2 * D_CONV + 4 * D_ATT
    o_wi = o_ki + D_IDX
    o_q = o_glu + 2 * D_CONV
    o_k = o_q + D_ATT
    o_qi = o_k + 2 * D_ATT
    pad = jnp.zeros((D_MODEL, LANES - H_SSM - H_IDX - D_IDX), w_in.dtype)
    w_r = jnp.concatenate([w_in[:, :o_dt], w_in[:, o_glu:o_k], w_in[:, o_qi:o_ki],
                           w_in[:, o_k:o_qi],
                           w_in[:, o_dt:o_glu], w_in[:, o_wi:o_wi + H_IDX], pad, w_in[:, o_ki:o_wi]],
                          axis=1).astype(BF16)
    assert w_r.shape[1] == D_PROJ and o_qi + H_IDX * D_IDX == o_ki
    row = lambda v: v.reshape(1, -1).astype(F32)
    padl = lambda v: jnp.pad(v.astype(F32), (0, LANES - v.shape[0])).reshape(1, LANES)
    return dict(
        w_in=w_r, g_mix_pre=row(w["g_mix_pre"]),
        ssm_conv_w=w["ssm_conv_w"].astype(F32), ssm_conv_b=row(w["ssm_conv_b"]),
        dt_bias=padl(w["ssm_dt_bias"]), a_log=padl(w["ssm_a_log"]),
        d_x=row(jnp.repeat(w["ssm_d"], SSM_HEAD_DIM)), ssm_norm_g=row(w["ssm_norm_g"]),
        cconv_w=w["cconv_w"].astype(F32), cconv_b=row(w["cconv_b"]),
        cconv_ln_g=row(w["cconv_ln_g"]), cconv_ln_b=row(w["cconv_ln_b"]),
        w_out=w["w_out"].astype(BF16), g_mix_post=row(w["g_mix_post"]), g_ffn_pre=row(w["g_ffn_pre"]),
        ffn_w_gate=w["ffn_w_gate"].astype(BF16), ffn_w_up=w["ffn_w_up"].astype(BF16),
        ffn_w_down=w["ffn_w_down"].astype(BF16), ffn_conv_w=w["ffn_conv_w"].astype(F32),
        ffn_conv_b=row(w["ffn_conv_b"]), g_ffn_post=row(w["g_ffn_post"]))


def _expand_matrix():
    e = np.zeros((LANES, D_SSM), np.float32)
    for h in range(H_SSM):
        e[h, h * SSM_HEAD_DIM:(h + 1) * SSM_HEAD_DIM] = 1.0
    return jnp.asarray(e, BF16)


def _front_pad(state, halo):
    return jnp.pad(state.astype(F32), ((0, 0), (halo - state.shape[1], 0), (0, 0)))


def _state_t(h):
    b = h.shape[0]
    return jnp.transpose(h.astype(F32), (0, 3, 1, 2)).reshape(b, SSM_STATE, D_SSM)


def _state_from_t(ht):
    b = ht.shape[0]
    return jnp.transpose(ht.reshape(b, SSM_STATE, H_SSM, SSM_HEAD_DIM), (0, 2, 3, 1))


def _mixer_common(x, lw, emat, ssm_conv_prev, ssm_h0, cconv_prev, *, bsz, t, tm_proj, kt, ssd_l, ssd_nv, cc_tm):
    u, k, v, small, *attn_ops = _in_proj(x, lw["g_mix_pre"], lw["w_in"], tm=tm_proj, kt=kt)
    y_ssm, ht, ctail = _ssd(u, small, _front_pad(ssm_conv_prev, SSD_HALO), _state_t(ssm_h0), lw["ssm_conv_w"],
                            lw["ssm_conv_b"], lw["dt_bias"], lw["a_log"], lw["d_x"], lw["ssm_norm_g"], emat,
                            bsz=bsz, t=t, L=ssd_l, nv=ssd_nv)
    y_conv, cctail = _cconv(u, _front_pad(cconv_prev, CC_HALO), lw["cconv_w"], lw["cconv_b"], lw["cconv_ln_g"],
                            lw["cconv_ln_b"], bsz=bsz, t=t, tm=cc_tm)
    ki = small[:, SM_KI:SM_KI + D_IDX]
    states = dict(k=k.reshape(bsz, t, H_ATT, HEAD_DIM), v=v.reshape(bsz, t, H_ATT, HEAD_DIM),
                  ki=ki.reshape(bsz, t, D_IDX), h=_state_from_t(ht),
                  ssm_conv=ctail[:, SSD_HALO - (SSM_CONV - 1):], cconv=cctail[:, CC_HALO - (CONV_WIDTH - 1):])
    return u, small, y_ssm, y_conv, k, v, ki, attn_ops, states


def _layer_prompt(x, lw, emat, btab, *, bsz, t, cfg):
    zeros = lambda *s: jnp.zeros(s, F32)
    kt = cfg["kt"]
    nk = t // kt
    u, small, y_ssm, y_conv, k, v, ki, (kb, vt, ki2), st = _mixer_common(
        x, lw, emat, zeros(bsz, SSM_CONV - 1, D_XBC), zeros(bsz, H_SSM, SSM_HEAD_DIM, SSM_STATE),
        zeros(bsz, CONV_WIDTH - 1, D_CONV), bsz=bsz, t=t, tm_proj=cfg["tm_proj"], kt=kt,
        ssd_l=cfg["ssd_l"], ssd_nv=cfg["ssd_l"], cc_tm=cfg["cc_tm"])
    tq = cfg["tq"]
    cfar = jnp.tile(btab[N_BIAS_NEAR - 1, :, 0:1, :], (1, 1, tq // LANES))
    y_att = _attn_prompt(u, small, kb.reshape(bsz, nk, kt, D_ATT), vt.reshape(bsz, nk, D_ATT, kt),
                         ki2.reshape(bsz, nk, kt, LANES), btab, cfar, bsz=bsz, t=t, tq=tq, kt=kt)
    x1 = _out_proj(x, y_ssm, y_conv, y_att, lw["w_out"], lw["g_mix_post"], tm=cfg["tm_out"])
    prev = zeros(bsz, FFN_HALO, D_FF)
    x2, ftail = _ffn(x1, lw["g_ffn_pre"], lw["ffn_w_gate"], lw["ffn_w_up"], lw["ffn_w_down"], lw["ffn_conv_w"],
                     lw["ffn_conv_b"], lw["g_ffn_post"], prev, prev, tm=cfg["tm_ffn"], tf=cfg["tf"], seq_len=t,
                     chained=True)
    ftail = ftail.reshape(bsz, t // cfg["tm_ffn"], FFN_HALO, D_FF)[:, -1]
    st["ffn_conv"] = ftail[:, FFN_HALO - (FFN_CONV - 1):]
    return x2, st


def _layer_sample(x, lw, emat, btab, past_k, past_v, past_ki, ssm_conv_prev, ssm_h0, cconv_prev, fconv_prev,
                  *, bsz, t, cfg):
    m = bsz * t
    u, small, y_ssm, y_conv, k, v, ki, _, st = _mixer_common(
        x, lw, emat, ssm_conv_prev, ssm_h0, cconv_prev, bsz=bsz, t=t, tm_proj=m, kt=None,
        ssd_l=LANES, ssd_nv=t, cc_tm=t)
    past = past_k.shape[1]
    n_keys = past + t
    ktp = cfg["kt_sample"]
    padk = lambda a: jnp.pad(a, ((0, 0), (0, ktp - n_keys), (0, 0)))
    k_all = padk(jnp.concatenate([past_k.reshape(bsz, past, D_ATT), k.reshape(bsz, t, D_ATT)], axis=1).astype(BF16))
    v_all = padk(jnp.concatenate([past_v.reshape(bsz, past, D_ATT), v.reshape(bsz, t, D_ATT)], axis=1).astype(BF16))
    ki_all = padk(jnp.concatenate([past_ki, ki.reshape(bsz, t, D_IDX)], axis=1).astype(BF16))
    kt_all = jnp.transpose(k_all, (0, 2, 1))
    kit = jnp.transpose(ki_all, (0, 2, 1))
    y_att = _attn_sample(u, small, kt_all, v_all, jnp.concatenate([kit, kit], axis=1), btab, bsz=bsz, tq=t, kt=ktp,
                         n_keys=n_keys)
    x1 = _out_proj(x, y_ssm, y_conv, y_att, lw["w_out"], lw["g_mix_post"], tm=m)
    fprev = fconv_prev.astype(F32)
    zrow = jnp.zeros((bsz, t - 1, D_FF), F32)
    p1 = jnp.concatenate([fprev[:, 1:2], zrow], axis=1).reshape(m, D_FF)
    p2 = jnp.concatenate([fprev, zrow[:, 1:]], axis=1).reshape(m, D_FF)
    x2, a_pre = _ffn(x1, lw["g_ffn_pre"], lw["ffn_w_gate"], lw["ffn_w_up"], lw["ffn_w_down"], lw["ffn_conv_w"],
                     lw["ffn_conv_b"], lw["g_ffn_post"], p1, p2, tm=m, tf=cfg["tf"], seq_len=t, chained=False)
    st["ffn_conv"] = a_pre.reshape(bsz, t, D_FF)[:, t - (FFN_CONV - 1):]
    return x2, st


_STATE_ORDER = ("k", "v", "ki", "h", "ssm_conv", "cconv", "ffn_conv")


def _prompt_cfg(t):
    big = t >= 4096
    return dict(tm_proj=512, ssd_l=256, cc_tm=256, kt=512, tq=256,
                tm_out=512 if big else 256, tm_ffn=1024 if big else 256, tf=512)


def _forward(x_prompt, x_sample, cache_k, cache_v, cache_kidx, state_ssm, state_ssm_conv, state_cconv,
             state_ffn_conv, rel_bias, weights):
    bp, tp, _ = x_prompt.shape
    bs, ts, _ = x_sample.shape
    depth = weights["w_in"].shape[0]
    past = cache_k.shape[2]
    emat = _expand_matrix()
    cfg_p = _prompt_cfg(tp)
    kt_sample = -(-(past + ts) // LANES) * LANES
    cfg_s = dict(tf=512, kt_sample=kt_sample)
    rb = rel_bias.astype(F32)
    btab_p = _bias_table(rb, nd=N_BIAS_NEAR, rows=LANES, cols=LANES, off0=0, step=LANES, key_axis=0, scale=LOG2E)
    btab_s = _bias_table(rb, nd=1, rows=ts, cols=kt_sample, off0=-past, step=0)
    xp = x_prompt.reshape(bp * tp, D_MODEL)
    xs = x_sample.reshape(bs * ts, D_MODEL)
    p_states = {n: [] for n in _STATE_ORDER}
    s_states = {n: [] for n in _STATE_ORDER}
    for l in range(depth):
        lw = _prep_layer_weights({n: w[l] for n, w in weights.items()})
        xp, st_p = _layer_prompt(xp, lw, emat, btab_p, bsz=bp, t=tp, cfg=cfg_p)
        xs, st_s = _layer_sample(xs, lw, emat, btab_s, cache_k[l], cache_v[l], cache_kidx[l], state_ssm_conv[l],
                                 state_ssm[l], state_cconv[l], state_ffn_conv[l], bsz=bs, t=ts, cfg=cfg_s)
        for n in _STATE_ORDER:
            p_states[n].append(st_p[n])
            s_states[n].append(st_s[n])
    outs = [xp.reshape(bp, tp, D_MODEL), xs.reshape(bs, ts, D_MODEL)]
    outs += [jnp.stack(p_states[n]) for n in _STATE_ORDER]
    outs += [jnp.stack(s_states[n]) for n in _STATE_ORDER]
    return tuple(outs)


def kernel(x_prompt, x_sample, cache_k, cache_v, cache_kidx, state_ssm, state_ssm_conv, state_cconv, state_ffn_conv, rel_bias, g_mix_pre, w_in, ssm_conv_w, ssm_conv_b, ssm_dt_bias, ssm_a_log, ssm_d, ssm_norm_g, cconv_w, cconv_b, cconv_ln_g, cconv_ln_b, w_out, g_mix_post, g_ffn_pre, ffn_w_gate, ffn_w_up, ffn_conv_w, ffn_conv_b, ffn_w_down, g_ffn_post):
    weights = dict(g_mix_pre=g_mix_pre, w_in=w_in, ssm_conv_w=ssm_conv_w, ssm_conv_b=ssm_conv_b,
                   ssm_dt_bias=ssm_dt_bias, ssm_a_log=ssm_a_log, ssm_d=ssm_d, ssm_norm_g=ssm_norm_g,
                   cconv_w=cconv_w, cconv_b=cconv_b, cconv_ln_g=cconv_ln_g, cconv_ln_b=cconv_ln_b, w_out=w_out,
                   g_mix_post=g_mix_post, g_ffn_pre=g_ffn_pre, ffn_w_gate=ffn_w_gate, ffn_w_up=ffn_w_up,
                   ffn_conv_w=ffn_conv_w, ffn_conv_b=ffn_conv_b, ffn_w_down=ffn_w_down, g_ffn_post=g_ffn_post)
    return _forward(x_prompt, x_sample, cache_k, cache_v, cache_kidx, state_ssm, state_ssm_conv, state_cconv,
                    state_ffn_conv, rel_bias, weights)
```

```python
import functools
import math

import numpy as np
import jax
import jax.numpy as jnp
from jax import lax
from jax.experimental import pallas as pl
from jax.experimental.pallas import tpu as pltpu

F32 = jnp.float32
BF16 = jnp.bfloat16
I32 = jnp.int32

D_MODEL = 2048
D_SSM = 1024
SSM_HEAD_DIM = 64
H_SSM = 16
SSM_GROUPS = 2
SSM_STATE = 128
SSM_CONV = 4
D_XBC = D_SSM + 2 * SSM_GROUPS * SSM_STATE
D_CONV = 512
CONV_WIDTH = 31
D_ATT = 512
HEAD_DIM = 64
H_ATT = 8
H_IDX = 8
D_IDX = 64
TOPK = 256
CHUNK = 64
N_BUCKETS = 32
REL_MAX_DIST = 1024
D_FF = 5632
FFN_CONV = 3
EPS = 1e-6

LANES = 128
SUBLANES = 8

COL_Z, COL_XBC, COL_GLU, COL_Q, COL_QI = 0, 1024, 2560, 3584, 4096
D_U = 4608
PROJ_TILE = 1152
TAIL_K, TAIL_V, TAIL_SMALL = 0, 512, 1024
D_PROJ = D_U + PROJ_TILE
SM_DT, SM_WI, SM_KI = 0, 16, 64

INT_MIN = -(2 ** 31)
INT_MAX = 2 ** 31 - 1
NEG_BIG = -1e30
VMEM_LIMIT = 56 * 1024 * 1024


def _bucket_thresholds():
    nb = N_BUCKETS // 2
    max_exact = nb // 2
    n = np.arange(0, 4 * REL_MAX_DIST, dtype=np.int64)
    nf = np.maximum(n, 1).astype(np.float32)
    large = max_exact + (np.log(nf / np.float32(max_exact)) / np.float32(math.log(REL_MAX_DIST / max_exact))
                         * np.float32(nb - max_exact)).astype(np.int32)
    large = np.minimum(large, nb - 1)
    bucket = np.where(n < max_exact, n, large)
    steps = np.nonzero(np.diff(bucket))[0] + 1
    assert np.all(np.diff(bucket) >= 0) and np.all(np.diff(bucket) <= 1) and bucket[-1] == nb - 1
    return tuple(int(s) for s in steps)


BUCKET_STEPS = _bucket_thresholds()


def _sigmoid(x):
    return 1.0 / (1.0 + jnp.exp(-x))


def _silu(x):
    return x * _sigmoid(x)


def _split3(x):
    hi = x.astype(BF16)
    r1 = x - hi.astype(F32)
    mid = r1.astype(BF16)
    lo = (r1 - mid.astype(F32)).astype(BF16)
    return hi, mid, lo


def _dot(a, b):
    return jnp.dot(a, b, preferred_element_type=F32)


def _dot_nt(a, b):
    return lax.dot_general(a, b, (((1,), (1,)), ((), ())), preferred_element_type=F32)


def _exact_dot(sel_bf16, x_f32):
    hi, mid, lo = _split3(x_f32)
    return _dot(sel_bf16, hi) + _dot(sel_bf16, mid) + _dot(sel_bf16, lo)


def _exact_dot_r(x_f32, sel_bf16):
    hi, mid, lo = _split3(x_f32)
    return _dot(hi, sel_bf16) + _dot(mid, sel_bf16) + _dot(lo, sel_bf16)


def _rms(x, g):
    ms = jnp.mean(x * x, axis=-1, keepdims=True)
    return x * lax.rsqrt(ms + EPS) * g


def _in_proj_u_kernel(x_ref, g_ref, w_ref, u_ref, h_ref):
    @pl.when(pl.program_id(1) == 0)
    def _():
        h_ref[...] = _rms(x_ref[...], g_ref[...]).astype(BF16)

    u_ref[...] = _dot(h_ref[...], w_ref[...])


def _in_proj_tail_kernel(x_ref, g_ref, w_ref, k_ref, v_ref, sm_ref, *rest, kt):
    y = _dot(_rms(x_ref[...], g_ref[...]).astype(BF16), w_ref[...])
    k = y[:, TAIL_K:TAIL_K + D_ATT]
    v = y[:, TAIL_V:TAIL_V + D_ATT]
    sm = y[:, TAIL_SMALL:TAIL_SMALL + LANES]
    k_ref[...] = k
    v_ref[...] = v
    sm_ref[...] = sm
    if kt is not None:
        kb_ref, vt_ref, ki2_ref = rest
        kb_ref[...] = k.astype(BF16)
        vb = v.astype(BF16)
        eye = jnp.where(lax.broadcasted_iota(I32, (LANES, LANES), 0)
                        == lax.broadcasted_iota(I32, (LANES, LANES), 1), 1.0, 0.0).astype(BF16)
        for c in range(vb.shape[0] // kt):
            for p in range(D_ATT // LANES):
                blk = vb[c * kt:(c + 1) * kt, p * LANES:(p + 1) * LANES]
                vt_ref[c, p * LANES:(p + 1) * LANES, :] = _dot_nt(eye, blk).astype(BF16)
        kib = sm[:, SM_KI:SM_KI + D_IDX].astype(BF16)
        ki2_ref[...] = jnp.concatenate([kib, kib], axis=1)


def _in_proj(x, g, w, *, tm, kt=None):
    m, d = x.shape
    nj = D_U // PROJ_TILE
    assert m % tm == 0 and w.shape[1] == D_PROJ and (kt is None or tm % kt == 0)
    u = pl.pallas_call(
        _in_proj_u_kernel,
        out_shape=jax.ShapeDtypeStruct((m, D_U), F32),
        grid=(m // tm, nj),
        in_specs=[pl.BlockSpec((tm, d), lambda i, j: (i, 0)),
                  pl.BlockSpec((1, d), lambda i, j: (0, 0)),
                  pl.BlockSpec((d, PROJ_TILE), lambda i, j: (0, j))],
        out_specs=pl.BlockSpec((tm, PROJ_TILE), lambda i, j: (i, j)),
        scratch_shapes=[pltpu.VMEM((tm, d), BF16)],
        compiler_params=pltpu.CompilerParams(dimension_semantics=("arbitrary", "arbitrary"),
                                             vmem_limit_bytes=VMEM_LIMIT),
        name="rms_in_proj",
    )(x, g, w)
    row = lambda i: (i, 0)
    out_shape = [jax.ShapeDtypeStruct((m, D_ATT), F32), jax.ShapeDtypeStruct((m, D_ATT), F32),
                 jax.ShapeDtypeStruct((m, LANES), F32)]
    out_specs = [pl.BlockSpec((tm, D_ATT), row), pl.BlockSpec((tm, D_ATT), row), pl.BlockSpec((tm, LANES), row)]
    if kt is not None:
        out_shape += [jax.ShapeDtypeStruct((m, D_ATT), BF16), jax.ShapeDtypeStruct((m // kt, D_ATT, kt), BF16),
                      jax.ShapeDtypeStruct((m, LANES), BF16)]
        out_specs += [pl.BlockSpec((tm, D_ATT), row), pl.BlockSpec((tm // kt, D_ATT, kt), lambda i: (i, 0, 0)),
                      pl.BlockSpec((tm, LANES), row)]
    tail = pl.pallas_call(
        functools.partial(_in_proj_tail_kernel, kt=kt),
        out_shape=tuple(out_shape),
        grid=(m // tm,),
        in_specs=[pl.BlockSpec((tm, d), lambda i: (i, 0)),
                  pl.BlockSpec((1, d), lambda i: (0, 0)),
                  pl.BlockSpec((d, PROJ_TILE), lambda i: (0, nj))],
        out_specs=tuple(out_specs),
        compiler_params=pltpu.CompilerParams(dimension_semantics=("arbitrary",),
                                             vmem_limit_bytes=VMEM_LIMIT),
        name="rms_in_proj_tail",
    )(x, g, w)
    return (u,) + tuple(tail)


def _bias_table_kernel(rb_ref, o_ref, *, off0, step, key_axis, scale):
    d = pl.program_id(0)
    h = pl.program_id(1)
    rows, cols = o_ref.shape[2], o_ref.shape[3]
    rel = (lax.broadcasted_iota(I32, (rows, cols), key_axis) - lax.broadcasted_iota(I32, (rows, cols), 1 - key_axis)
           + (off0 - d * step))
    n = jnp.abs(rel)
    bucket = jnp.where(rel > 0, N_BUCKETS // 2, 0)
    for s in BUCKET_STEPS:
        bucket = bucket + jnp.where(n >= s, 1, 0)
    acc = jnp.zeros((rows, cols), F32)
    for b in range(N_BUCKETS):
        acc = jnp.where(bucket == b, rb_ref[b, h], acc)
    o_ref[0, 0] = acc * scale


def _bias_table(rel_bias, *, nd, rows, cols, off0, step, key_axis=1, scale=1.0):
    return pl.pallas_call(
        functools.partial(_bias_table_kernel, off0=off0, step=step, key_axis=key_axis, scale=scale),
        out_shape=jax.ShapeDtypeStruct((nd, H_ATT, rows, cols), F32),
        grid=(nd, H_ATT),
        in_specs=[pl.BlockSpec(memory_space=pltpu.SMEM)],
        out_specs=pl.BlockSpec((1, 1, rows, cols), lambda d, h: (d, h, 0, 0)),
        name="bias_table",
    )(rel_bias)


CC_HALO = 32


def _cconv_kernel(val_ref, gate_ref, prev_ref, w_ref, b_ref, lg_ref, lb_ref, y_ref, tail_ref, buf_ref, sh_ref, *, tm):
    @pl.when(pl.program_id(1) == 0)
    def _():
        buf_ref[0:CC_HALO, :] = prev_ref[0]

    buf_ref[CC_HALO:CC_HALO + tm, :] = val_ref[...] * _sigmoid(gate_ref[...])
    span = CC_HALO + tm - SUBLANES
    for b in range(1, SUBLANES):
        sh_ref[b - 1, 0:span, :] = buf_ref[b:b + span, :]
    first = CC_HALO - (CONV_WIDTH - 1)
    acc = jnp.zeros((tm, D_CONV), F32) + b_ref[...]
    for k in range(CONV_WIDTH):
        base, b = (first + k) // SUBLANES * SUBLANES, (first + k) % SUBLANES
        rows = buf_ref[base:base + tm, :] if b == 0 else sh_ref[b - 1, base:base + tm, :]
        acc = acc + w_ref[k:k + 1, :] * rows
    mu = jnp.mean(acc, axis=-1, keepdims=True)
    xc = acc - mu
    var = jnp.mean(xc * xc, axis=-1, keepdims=True)
    y = xc * lax.rsqrt(var + EPS) * lg_ref[...] + lb_ref[...]
    y_ref[...] = _silu(y).astype(y_ref.dtype)
    tail = buf_ref[tm:tm + CC_HALO, :]
    tail_ref[0] = tail
    buf_ref[0:CC_HALO, :] = tail


def _cconv(u, prev, w, b, lg, lb, *, bsz, t, tm):
    nt = t // tm
    row = lambda bi, ti: bi * nt + ti
    return pl.pallas_call(
        functools.partial(_cconv_kernel, tm=tm),
        out_shape=(jax.ShapeDtypeStruct((bsz * t, D_CONV), BF16),
                   jax.ShapeDtypeStruct((bsz, CC_HALO, D_CONV), F32)),
        grid=(bsz, nt),
        in_specs=[pl.BlockSpec((tm, D_CONV), lambda bi, ti: (row(bi, ti), COL_GLU // D_CONV)),
                  pl.BlockSpec((tm, D_CONV), lambda bi, ti: (row(bi, ti), COL_GLU // D_CONV + 1)),
                  pl.BlockSpec((1, CC_HALO, D_CONV), lambda bi, ti: (bi, 0, 0)),
                  pl.BlockSpec((CONV_WIDTH, D_CONV), lambda bi, ti: (0, 0)),
                  pl.BlockSpec((1, D_CONV), lambda bi, ti: (0, 0)),
                  pl.BlockSpec((1, D_CONV), lambda bi, ti: (0, 0)),
                  pl.BlockSpec((1, D_CONV), lambda bi, ti: (0, 0))],
        out_specs=(pl.BlockSpec((tm, D_CONV), lambda bi, ti: (row(bi, ti), 0)),
                   pl.BlockSpec((1, CC_HALO, D_CONV), lambda bi, ti: (bi, 0, 0))),
        scratch_shapes=[pltpu.VMEM((CC_HALO + tm, D_CONV), F32),
                        pltpu.VMEM((SUBLANES - 1, CC_HALO + tm, D_CONV), F32)],
        compiler_params=pltpu.CompilerParams(dimension_semantics=("arbitrary", "arbitrary"),
                                             vmem_limit_bytes=VMEM_LIMIT),
        name="conformer_conv",
    )(u, u, prev, w, b, lg, lb)


SSD_HALO = 8


def _ssd_kernel(z_ref, x0_ref, x1_ref, x2_ref, sm_ref, prevc_ref, h0_ref, cw_ref, cb_ref, dtb_ref, alog_ref,
                dx_ref, ng_ref, e_ref, y_ref, hout_ref, ctail_ref, buf_ref, ht_ref, yb_ref, *, L, nv):
    @pl.when(pl.program_id(1) == 0)
    def _():
        buf_ref[0:SSD_HALO, :] = prevc_ref[0]
        ht_ref[...] = h0_ref[0]

    if nv < L:
        buf_ref[SSD_HALO + nv:SSD_HALO + L, :] = jnp.zeros((L - nv, D_XBC), F32)
    for c, r in enumerate((x0_ref, x1_ref, x2_ref)):
        buf_ref[SSD_HALO:SSD_HALO + nv, c * 512:(c + 1) * 512] = r[...]
    first = SSD_HALO - (SSM_CONV - 1)
    acc = jnp.zeros((L, D_XBC), F32) + cb_ref[...]
    for k in range(SSM_CONV):
        acc = acc + cw_ref[k:k + 1, :] * buf_ref[first + k:first + k + L, :]
    xbc = _silu(acc)
    tail = buf_ref[nv:nv + SSD_HALO, :]
    ctail_ref[0] = tail
    buf_ref[0:SSD_HALO, :] = tail

    xs = xbc[:, :D_SSM]
    lane = lax.broadcasted_iota(I32, (L, LANES), 1)
    rowi = lax.broadcasted_iota(I32, (L, LANES), 0)
    sm = sm_ref[...]
    if nv < L:
        sm = jnp.concatenate([sm, jnp.zeros((L - nv, LANES), F32)], axis=0)
    dtr = sm + dtb_ref[...]
    dt = jnp.maximum(dtr, 0.0) + jnp.log(1.0 + jnp.exp(-jnp.abs(dtr)))
    dt = jnp.where((lane < H_SSM) & (rowi < nv), dt, 0.0)
    a = -jnp.exp(alog_ref[...])
    da = dt * a
    ri = lax.broadcasted_iota(I32, (L, L), 0)
    ci = lax.broadcasted_iota(I32, (L, L), 1)
    causal = ri >= ci
    tril = jnp.where(causal, 1.0, 0.0).astype(BF16)
    cum = _exact_dot(tril, da)
    eye = jnp.where(lax.broadcasted_iota(I32, (LANES, LANES), 0) == lax.broadcasted_iota(I32, (LANES, LANES), 1),
                    1.0, 0.0).astype(BF16)
    ch, cm, cl = _split3(cum)
    cum_t = _dot_nt(eye, ch) + _dot_nt(eye, cm) + _dot_nt(eye, cl)
    e = e_ref[...]
    ecx = _exact_dot_r(jnp.exp(cum), e)
    dtx = _exact_dot_r(dt, e)
    xdt = (xs * dtx).astype(BF16)
    edl = ecx[L - 1:L, :]
    dend_t = jnp.exp(cum_t[:, L - 1:L] - cum_t)
    lane_l = lax.broadcasted_iota(I32, (L, LANES), 1)
    lane_n = lax.broadcasted_iota(I32, (SSM_STATE, LANES), 1)
    hpg = H_SSM // SSM_GROUPS
    for g in range(SSM_GROUPS):
        bg = xbc[:, D_SSM + g * SSM_STATE:D_SSM + (g + 1) * SSM_STATE].astype(BF16)
        cg = xbc[:, D_SSM + (SSM_GROUPS + g) * SSM_STATE:D_SSM + (SSM_GROUPS + g + 1) * SSM_STATE].astype(BF16)
        cbt = _dot_nt(cg, bg)
        bg_t = _dot_nt(eye, bg)
        gcols = slice(g * hpg * SSM_HEAD_DIM, (g + 1) * hpg * SSM_HEAD_DIM)
        yoff = _dot(cg, ht_ref[:, gcols].astype(BF16)) * ecx[:, gcols]
        for p in range(hpg // 2):
            h0 = g * hpg + 2 * p
            pcols = slice(h0 * SSM_HEAD_DIM, (h0 + 2) * SSM_HEAD_DIM)
            xpair = xdt[:, pcols]
            res, st = [], []
            for hh in (h0, h0 + 1):
                seg = cum[:, hh:hh + 1] - cum_t[hh:hh + 1, :]
                dec = jnp.where(causal, jnp.exp(seg), 0.0)
                res.append(_dot((cbt * dec).astype(BF16), xpair))
                st.append(_dot((bg_t * dend_t[hh:hh + 1, :]).astype(BF16), xpair))
            yb_ref[:, pcols] = (jnp.where(lane_l < SSM_HEAD_DIM, res[0], res[1])
                                + yoff[:, 2 * p * SSM_HEAD_DIM:(2 * p + 2) * SSM_HEAD_DIM])
            ht_ref[:, pcols] = (ht_ref[:, pcols] * edl[:, pcols]
                                + jnp.where(lane_n < SSM_HEAD_DIM, st[0], st[1]))
    hout_ref[0] = ht_ref[...]
    y = yb_ref[...] + dx_ref[...] * xs
    z = z_ref[...]
    if nv < L:
        z = jnp.concatenate([z, jnp.zeros((L - nv, D_SSM), F32)], axis=0)
    y = _rms(y * _silu(z), ng_ref[...])
    y_ref[...] = y[:nv].astype(y_ref.dtype)


def _ssd(u, small, prevc, h0t, cw, cb, dtb, alog, dx, ng, emat, *, bsz, t, L, nv):
    nt = t // nv
    row = lambda bi, ti: bi * nt + ti
    c512 = lambda c: (lambda bi, ti: (row(bi, ti), c))
    const2 = lambda bi, ti: (0, 0)
    return pl.pallas_call(
        functools.partial(_ssd_kernel, L=L, nv=nv),
        out_shape=(jax.ShapeDtypeStruct((bsz * t, D_SSM), BF16),
                   jax.ShapeDtypeStruct((bsz, SSM_STATE, D_SSM), F32),
                   jax.ShapeDtypeStruct((bsz, SSD_HALO, D_XBC), F32)),
        grid=(bsz, nt),
        in_specs=[pl.BlockSpec((nv, D_SSM), lambda bi, ti: (row(bi, ti), 0)),
                  pl.BlockSpec((nv, 512), c512(COL_XBC // 512)),
                  pl.BlockSpec((nv, 512), c512(COL_XBC // 512 + 1)),
                  pl.BlockSpec((nv, 512), c512(COL_XBC // 512 + 2)),
                  pl.BlockSpec((nv, LANES), c512(0)),
                  pl.BlockSpec((1, SSD_HALO, D_XBC), lambda bi, ti: (bi, 0, 0)),
                  pl.BlockSpec((1, SSM_STATE, D_SSM), lambda bi, ti: (bi, 0, 0)),
                  pl.BlockSpec((SSM_CONV, D_XBC), const2),
                  pl.BlockSpec((1, D_XBC), const2),
                  pl.BlockSpec((1, LANES), const2),
                  pl.BlockSpec((1, LANES), const2),
                  pl.BlockSpec((1, D_SSM), const2),
                  pl.BlockSpec((1, D_SSM), const2),
                  pl.BlockSpec((LANES, D_SSM), const2)],
        out_specs=(pl.BlockSpec((nv, D_SSM), lambda bi, ti: (row(bi, ti), 0)),
                   pl.BlockSpec((1, SSM_STATE, D_SSM), lambda bi, ti: (bi, 0, 0)),
                   pl.BlockSpec((1, SSD_HALO, D_XBC), lambda bi, ti: (bi, 0, 0))),
        scratch_shapes=[pltpu.VMEM((SSD_HALO + L, D_XBC), F32),
                        pltpu.VMEM((SSM_STATE, D_SSM), F32),
                        pltpu.VMEM((L, D_SSM), F32)],
        compiler_params=pltpu.CompilerParams(dimension_semantics=("arbitrary", "arbitrary"),
                                             vmem_limit_bytes=VMEM_LIMIT),
        name="ssd_mixer",
    )(u, u, u, u, small, prevc, h0t, cw, cb, dtb, alog, dx, ng, emat)


def _flip_negative(b):
    return b ^ ((b >> 31) & INT_MAX)


def _mono_key(x):
    return _flip_negative(lax.bitcast_convert_type(x, I32))


def _key_value(k):
    return lax.bitcast_convert_type(_flip_negative(k), F32)


F32_BIG = 3e38
MID_PERIOD = 12
BOOST_MAX = 2.0 ** 30
SEARCH_MAX_STEPS = 400


def _attn_body(*, tq, kt, nkt, topk, qf, qif, sm, kt_tile, v_tile, ki_tile, adm_fn, bias_fn,
               sc_ref, m_ref, l_ref, acc_ref, y_ref):
    nsl = kt // LANES
    lane = lax.broadcasted_iota(I32, (tq, LANES), 1)
    low = lane < HEAD_DIM
    qb = (qf * (HEAD_DIM ** -0.5)).astype(BF16)
    qib = qif.astype(BF16)
    wi = sm[:, SM_WI:SM_WI + H_IDX] * ((D_IDX ** -0.5) * (H_IDX ** -0.5))
    zero_b = jnp.zeros((tq, LANES), BF16)

    def head_window(x, h):
        win = x[:, (h // 2) * LANES:(h // 2 + 1) * LANES]
        return jnp.where(low if h % 2 == 0 else jnp.logical_not(low), win, zero_b)

    qim = [head_window(qib, h) for h in range(H_IDX)]
    wcol = [wi[:, h:h + 1] for h in range(H_IDX)]

    def p1(j, carry):
        ki = ki_tile(j)
        acc = jnp.zeros((tq, kt), F32)
        for h in range(H_IDX):
            acc = acc + jnp.maximum(_dot(qim[h], ki), 0.0) * wcol[h]
        for s in range(nsl):
            key = jnp.where(adm_fn(j, s), _mono_key(acc[:, s * LANES:(s + 1) * LANES]), INT_MIN)
            sc_ref[j, :, s * LANES:(s + 1) * LANES] = key
        return carry

    lax.fori_loop(0, nkt, p1, 0)

    def count(pred):
        def body(j, acc):
            tile = sc_ref[j]
            for s in range(nsl):
                acc = acc + jnp.where(pred(tile[:, s * LANES:(s + 1) * LANES], j, s), 1.0, 0.0)
            return acc
        acc = lax.fori_loop(0, nkt, body, jnp.zeros((tq, LANES), F32))
        return jnp.sum(acc, axis=1, keepdims=True)

    def count_ge(cand):
        cb = jnp.broadcast_to(cand, (tq, LANES))
        return count(lambda t, j, s: t >= cb)

    kf = float(topk)
    prefix = jnp.where(count_ge(jnp.zeros((tq, 1), I32)) >= kf, 0, INT_MIN).astype(I32)

    def bit_step(it, prefix):
        trial = prefix | lax.shift_left(jnp.int32(1), 30 - it)
        return jnp.where(count_ge(trial) >= kf, trial, prefix)

    thr = lax.fori_loop(0, 31, bit_step, prefix)
    thr = jnp.maximum(thr, INT_MIN + 1)
    thr_b = jnp.broadcast_to(thr, (tq, LANES))
    n_gt = count(lambda t, j, s: t > thr_b)
    n_eq = count(lambda t, j, s: t == thr_b)
    take = kf - n_gt

    def kpos(j, s):
        return j * kt + s * LANES + lane

    def tie_cut(_):
        def step(it, cut):
            trial = cut | lax.shift_left(jnp.int32(1), 30 - it)
            tb = jnp.broadcast_to(trial, (tq, LANES))
            c = count(lambda t, j, s: (t == thr_b) & (kpos(j, s) < tb))
            return jnp.where(c <= take, trial, cut)
        return lax.fori_loop(0, 31, step, jnp.zeros((tq, 1), I32))

    has_tie = jnp.max(jnp.where(n_gt + n_eq > kf, 1.0, 0.0)) > 0.0
    cut = lax.cond(has_tie, tie_cut, lambda _: jnp.full((tq, 1), INT_MAX, I32), 0)
    cut_b = jnp.broadcast_to(cut, (tq, LANES))

    qm = [head_window(qb, h) for h in range(H_ATT)]
    m_ref[...] = jnp.full(m_ref.shape, NEG_BIG, F32)
    l_ref[...] = jnp.zeros(l_ref.shape, F32)
    acc_ref[...] = jnp.zeros(acc_ref.shape, F32)

    def p3(j, carry):
        keyt = sc_ref[j]
        sel = []
        for s in range(nsl):
            ks = keyt[:, s * LANES:(s + 1) * LANES]
            sel.append((ks > thr_b) | ((ks == thr_b) & (kpos(j, s) < cut_b)))
        for p in range(H_ATT // 2):
            kp = kt_tile(j, p)
            vp = v_tile(j, p)
            for hsub in range(2):
                h = 2 * p + hsub
                s_all = _dot(qm[h], kp)
                parts = [jnp.where(sel[s], s_all[:, s * LANES:(s + 1) * LANES] + bias_fn(j, s, h), NEG_BIG)
                         for s in range(nsl)]
                mx = parts[0]
                for s in range(1, nsl):
                    mx = jnp.maximum(mx, parts[s])
                m_old = m_ref[h]
                m_new = jnp.maximum(m_old, jnp.max(mx, axis=1, keepdims=True))
                alpha = jnp.exp(m_old - m_new)
                pr = [jnp.exp(part - m_new) for part in parts]
                psum = pr[0]
                for s in range(1, nsl):
                    psum = psum + pr[s]
                l_ref[h] = alpha * l_ref[h] + psum
                pb = jnp.concatenate([x.astype(BF16) for x in pr], axis=1)
                acc_ref[h] = alpha * acc_ref[h] + _dot(pb, vp)
                m_ref[h] = m_new
        return carry

    lax.fori_loop(0, nkt, p3, 0)
    for p in range(H_ATT // 2):
        outs = []
        for hsub in range(2):
            h = 2 * p + hsub
            lsum = jnp.sum(l_ref[h], axis=1, keepdims=True)
            outs.append(acc_ref[h] / lsum)
        y_ref[:, p * LANES:(p + 1) * LANES] = jnp.where(low, outs[0], outs[1]).astype(y_ref.dtype)


N_BIAS_NEAR = 7


LOG2E = math.log2(math.e)
PV_ROWS = 256


def _attn_prompt_kernel(q_ref, qi_ref, sm_ref, k_ref, vt_ref, ki_ref, bt_ref, cf_ref, y_ref,
                        sc_ref, qt_ref, zb_ref, nm_ref, m_ref, l_ref, acc_ref, *, tq, kt, topk):
    i = pl.program_id(1)
    nql = tq // LANES
    nsl = kt // LANES
    nkt = lax.div((i + 1) * tq + (kt - 1), kt)

    def fold(x, op, chains=4):
        groups = x.shape[0] // SUBLANES
        accs = [x[a * SUBLANES:(a + 1) * SUBLANES] for a in range(chains)]
        for r in range(chains, groups):
            accs[r % chains] = op(accs[r % chains], x[r * SUBLANES:(r + 1) * SUBLANES])
        while len(accs) > 1:
            accs = [op(accs[a], accs[a + len(accs) // 2]) for a in range(len(accs) // 2)]
        return accs[0]

    eye = jnp.where(lax.broadcasted_iota(I32, (LANES, LANES), 0) == lax.broadcasted_iota(I32, (LANES, LANES), 1),
                    1.0, 0.0).astype(BF16)
    lane = lax.broadcasted_iota(I32, (tq, LANES), 1)
    low = lane < HEAD_DIM
    qb = (q_ref[...] * (HEAD_DIM ** -0.5 * LOG2E)).astype(BF16)
    qib = qi_ref[...].astype(BF16)
    zero_b = jnp.zeros((tq, LANES), BF16)
    for src, base in ((qib, 0), (qb, H_IDX)):
        for h in range(H_ATT):
            win = src[:, (h // 2) * LANES:(h // 2 + 1) * LANES]
            win = jnp.where(low if h % 2 == 0 else jnp.logical_not(low), win, zero_b)
            qt_ref[base + h] = _dot_nt(eye, win).astype(BF16)
    sh, smm, sl_ = _split3(sm_ref[...])
    sm_t = _dot_nt(eye, sh) + _dot_nt(eye, smm) + _dot_nt(eye, sl_)
    wrow = [sm_t[SM_WI + h:SM_WI + h + 1, :] * ((D_IDX ** -0.5) * (H_IDX ** -0.5)) for h in range(H_IDX)]

    qpos = i * tq + lax.broadcasted_iota(I32, (1, tq), 1)
    cend = (lax.shift_right_logical(qpos, int(math.log2(CHUNK))) + 1) * CHUNK
    krow = lax.broadcasted_iota(I32, (LANES, tq), 0)

    def p1(j, carry):
        vmax, vmin = carry
        ki2 = ki_ref[0, j]
        for h in range(H_IDX):
            t = jnp.maximum(_dot(ki2, qt_ref[h]), 0.0) * wrow[h]
            if h == 0:
                zb_ref[0] = t
            elif h < H_IDX - 1:
                zb_ref[0] += t
            else:
                for sl in range(nsl):
                    rows = slice(sl * LANES, (sl + 1) * LANES)
                    sc = zb_ref[0, rows, :] + t[rows]
                    sc = jnp.where(sc == 0.0, 0.0, sc)
                    adm = (j * nsl + sl) * LANES + krow < cend
                    sc_ref[j * nsl + sl] = jnp.where(adm, _mono_key(sc), INT_MIN)
                    vmax = jnp.maximum(vmax, fold(jnp.where(adm, sc, -F32_BIG), jnp.maximum))
                    vmin = jnp.minimum(vmin, fold(jnp.where(adm, sc, F32_BIG), jnp.minimum))
        return vmax, vmin

    vmax, vmin = lax.fori_loop(0, nkt, p1, (jnp.full((SUBLANES, tq), -F32_BIG, F32),
                                            jnp.full((SUBLANES, tq), F32_BIG, F32)))

    def count(pred):
        def body(j, acc):
            for sl in range(nsl):
                g = j * nsl + sl
                acc = acc + fold(jnp.where(pred(sc_ref[g], g), 1.0, 0.0), jnp.add)
            return acc
        acc = lax.fori_loop(0, nkt, body, jnp.zeros((SUBLANES, tq), F32))
        return jnp.sum(acc, axis=0, keepdims=True)

    kf = float(topk)

    def active_of(lo, hi, clo):
        return (clo > kf) & (hi - 1 > lo)

    def search_cond(st):
        it, lo, hi, clo, chi, side, boost = st
        return (it < SEARCH_MAX_STEPS) & (jnp.max(jnp.where(active_of(lo, hi, clo), 1.0, 0.0)) > 0.0)

    def search_step(st):
        it, lo, hi, clo, chi, side, boost = st
        active = active_of(lo, hi, clo)
        v_lo = _key_value(lo)
        v_hi = _key_value(hi)
        frac = (jnp.log(clo) - math.log(kf)) / (jnp.log(clo) - jnp.log(jnp.maximum(chi, 0.5)))
        frac = jnp.where(side > 0, jnp.minimum(frac * boost, 0.5),
                         jnp.where(side < 0, 1.0 - jnp.minimum((1.0 - frac) * boost, 0.5), frac))
        t_int = _mono_key(v_lo + (v_hi - v_lo) * frac)
        t_mid = (lo >> 1) + (hi >> 1) + (lo & hi & 1)
        trial = jnp.where(lax.rem(it, MID_PERIOD) == MID_PERIOD - 1, t_mid, t_int)
        trial = jnp.where(it == 0, 0, jnp.where((it == 1) & (lo == 0), 1, trial))
        trial = jnp.minimum(jnp.maximum(trial, lo + 1), hi - 1)
        c = count(lambda t, g: t >= trial)
        up = active & (c >= kf)
        dn = active & (c < kf)
        now = jnp.where(c >= kf, 1, -1)
        boost = jnp.where(active, jnp.where(now == side, jnp.minimum(boost * 2.0, BOOST_MAX), 1.0), boost)
        side = jnp.where(active, now, side)
        return (it + 1, jnp.where(up, trial, lo), jnp.where(dn, trial, hi),
                jnp.where(up, c, clo), jnp.where(dn, c, chi), side, boost)

    lo0 = _mono_key(jnp.min(vmin, axis=0, keepdims=True))
    hi0 = _mono_key(jnp.max(vmax, axis=0, keepdims=True)) + 1
    _, thr, _, n_ge, n_gt, _, _ = lax.while_loop(
        search_cond, search_step,
        (jnp.int32(0), lo0, hi0, cend.astype(F32), jnp.zeros((1, tq), F32),
         jnp.zeros((1, tq), I32), jnp.ones((1, tq), F32)))
    take = kf - n_gt

    def tie_cut(_):
        def step(it, cut):
            trial = cut | lax.shift_left(jnp.int32(1), 30 - it)
            c = count(lambda t, g: (t == thr) & (g * LANES + krow < trial))
            return jnp.where(c <= take, trial, cut)
        return lax.fori_loop(0, 31, step, jnp.zeros((1, tq), I32))

    has_tie = jnp.max(jnp.where(n_ge > kf, 1.0, 0.0)) > 0.0
    cut = lax.cond(has_tie, tie_cut, lambda _: jnp.full((1, tq), INT_MAX, I32), 0)

    m_ref[...] = jnp.full(m_ref.shape, NEG_BIG, F32)
    l_ref[...] = jnp.zeros(l_ref.shape, F32)
    acc_ref[...] = jnp.zeros(acc_ref.shape, F32)

    def tile_step(j, near):
        for sl in range(nsl):
            g = j * nsl + sl
            keyt = sc_ref[g]
            sel = (keyt > thr) | ((keyt == thr) & (g * LANES + krow < cut))
            nm_ref[sl * LANES:(sl + 1) * LANES, :] = jnp.where(sel, 0.0, NEG_BIG)

        def logits(h):
            mx = None
            for sl in range(nsl):
                rows = slice(sl * LANES, (sl + 1) * LANES)
                z = _dot(k_ref[0, j, rows, (h // 2) * LANES:(h // 2 + 1) * LANES], qt_ref[H_IDX + h]) + nm_ref[rows, :]
                if near:
                    z = z + jnp.concatenate(
                        [bt_ref[jnp.clip((i * nql + hf) - (j * nsl + sl), 0, N_BIAS_NEAR - 1), h]
                         for hf in range(nql)], axis=1)
                zb_ref[h % 2, rows, :] = z
                cm = fold(z, jnp.maximum)
                mx = cm if mx is None else jnp.maximum(mx, cm)
            return mx

        def accumulate(h, mx):
            shift = 0.0 if near else cf_ref[h]
            m_old = m_ref[h]
            m_new = jnp.maximum(m_old, jnp.max(mx, axis=0, keepdims=True) + shift)
            alpha = jnp.exp2(m_old - m_new)
            msub = m_new - shift
            lsum, pv = None, None
            for c in range(kt // PV_ROWS):
                rows = slice(c * PV_ROWS, (c + 1) * PV_ROWS)
                p = jnp.exp2(zb_ref[h % 2, rows, :] - msub)
                ls = fold(p, jnp.add)
                pc = _dot(vt_ref[0, j, (h // 2) * LANES:(h // 2 + 1) * LANES, rows], p.astype(BF16))
                lsum = ls if lsum is None else lsum + ls
                pv = pc if pv is None else pv + pc
            l_ref[h] = alpha * l_ref[h] + lsum
            acc_ref[h] = alpha * acc_ref[h] + pv
            m_ref[h] = m_new

        mx_next = logits(0)
        for h in range(H_ATT):
            mx_cur = mx_next
            if h + 1 < H_ATT:
                mx_next = logits(h + 1)
            accumulate(h, mx_cur)

    n_far = jnp.minimum(lax.div(jnp.maximum(nql * i - (N_BIAS_NEAR - 1) - (nsl - 1) + nsl, 0), nsl), nkt)

    def p3_far(j, carry):
        tile_step(j, False)
        return carry

    def p3_near(j, carry):
        tile_step(j, True)
        return carry

    lax.fori_loop(0, n_far, p3_far, 0)
    lax.fori_loop(n_far, nkt, p3_near, 0)

    eye_q = jnp.where(lax.broadcasted_iota(I32, (tq, tq), 0) == lax.broadcasted_iota(I32, (tq, tq), 1),
                      1.0, 0.0).astype(BF16)
    rowd = lax.broadcasted_iota(I32, (LANES, tq), 0)
    for p in range(H_ATT // 2):
        outs = []
        for hsub in range(2):
            h = 2 * p + hsub
            outs.append(acc_ref[h] / jnp.sum(l_ref[h], axis=0, keepdims=True))
        y_t = jnp.where(rowd < HEAD_DIM, outs[0], outs[1]).astype(BF16)
        y_ref[:, p * LANES:(p + 1) * LANES] = _dot_nt(eye_q, y_t).astype(y_ref.dtype)


def _attn_prompt(u, small, k_tiles, vt_tiles, ki_tiles, btab, cfar, *, bsz, t, tq, kt):
    assert t % kt == 0 and t % tq == 0 and tq % LANES == 0 and kt % LANES == 0
    nq = t // tq
    nk = t // kt
    row = lambda bi, qi: bi * nq + qi
    once = pl.Buffered(1)
    return pl.pallas_call(
        functools.partial(_attn_prompt_kernel, tq=tq, kt=kt, topk=min(TOPK, t // 4)),
        out_shape=jax.ShapeDtypeStruct((bsz * t, D_ATT), BF16),
        grid=(bsz, nq),
        in_specs=[pl.BlockSpec((tq, D_ATT), lambda bi, qi: (row(bi, qi), COL_Q // D_ATT)),
                  pl.BlockSpec((tq, D_ATT), lambda bi, qi: (row(bi, qi), COL_QI // D_ATT)),
                  pl.BlockSpec((tq, LANES), lambda bi, qi: (row(bi, qi), 0)),
                  pl.BlockSpec((1, nk, kt, D_ATT), lambda bi, qi: (bi, 0, 0, 0), pipeline_mode=once),
                  pl.BlockSpec((1, nk, D_ATT, kt), lambda bi, qi: (bi, 0, 0, 0), pipeline_mode=once),
                  pl.BlockSpec((1, nk, kt, LANES), lambda bi, qi: (bi, 0, 0, 0), pipeline_mode=once),
                  pl.BlockSpec((N_BIAS_NEAR, H_ATT, LANES, LANES), lambda bi, qi: (0, 0, 0, 0), pipeline_mode=once),
                  pl.BlockSpec((H_ATT, 1, tq), lambda bi, qi: (0, 0, 0), pipeline_mode=once)],
        out_specs=pl.BlockSpec((tq, D_ATT), lambda bi, qi: (row(bi, qi), 0)),
        scratch_shapes=[pltpu.VMEM((t // LANES, LANES, tq), I32),
                        pltpu.VMEM((H_IDX + H_ATT, LANES, tq), BF16),
                        pltpu.VMEM((2, kt, tq), F32),
                        pltpu.VMEM((kt, tq), F32),
                        pltpu.VMEM((H_ATT, 1, tq), F32),
                        pltpu.VMEM((H_ATT, SUBLANES, tq), F32),
                        pltpu.VMEM((H_ATT, LANES, tq), F32)],
        compiler_params=pltpu.CompilerParams(dimension_semantics=("arbitrary", "arbitrary"),
                                             vmem_limit_bytes=VMEM_LIMIT),
        name="sparse_attn_prompt",
    )(u, u, small, k_tiles, vt_tiles, ki_tiles, btab, cfar)


def _attn_sample_kernel(q_ref, qi_ref, sm_ref, kt_ref, v_ref, ki_ref, bt_ref, y_ref, sc_ref, m_ref, l_ref, acc_ref,
                        *, tq, kt, n_keys, topk):
    lane = lax.broadcasted_iota(I32, (tq, LANES), 1)

    def adm_fn(j, s):
        return (j * kt + s * LANES + lane) < n_keys

    def bias_fn(j, s, h):
        return bt_ref[0, h, :, s * LANES:(s + 1) * LANES]

    _attn_body(tq=tq, kt=kt, nkt=1, topk=topk, qf=q_ref[...], qif=qi_ref[...], sm=sm_ref[...],
               kt_tile=lambda j, p: kt_ref[0, p * LANES:(p + 1) * LANES, :],
               v_tile=lambda j, p: v_ref[0, :, p * LANES:(p + 1) * LANES],
               ki_tile=lambda j: ki_ref[0],
               adm_fn=adm_fn, bias_fn=bias_fn,
               sc_ref=sc_ref, m_ref=m_ref, l_ref=l_ref, acc_ref=acc_ref, y_ref=y_ref)


def _attn_sample(u, small, kt_all, v_all, ki_all, btab, *, bsz, tq, kt, n_keys):
    return pl.pallas_call(
        functools.partial(_attn_sample_kernel, tq=tq, kt=kt, n_keys=n_keys, topk=min(TOPK, n_keys // 4)),
        out_shape=jax.ShapeDtypeStruct((bsz * tq, D_ATT), BF16),
        grid=(bsz,),
        in_specs=[pl.BlockSpec((tq, D_ATT), lambda bi: (bi, COL_Q // D_ATT)),
                  pl.BlockSpec((tq, D_ATT), lambda bi: (bi, COL_QI // D_ATT)),
                  pl.BlockSpec((tq, LANES), lambda bi: (bi, 0)),
                  pl.BlockSpec((1, D_ATT, kt), lambda bi: (bi, 0, 0)),
                  pl.BlockSpec((1, kt, D_ATT), lambda bi: (bi, 0, 0)),
                  pl.BlockSpec((1, LANES, kt), lambda bi: (bi, 0, 0)),
                  pl.BlockSpec((1, H_ATT, tq, kt), lambda bi: (0, 0, 0, 0))],
        out_specs=pl.BlockSpec((tq, D_ATT), lambda bi: (bi, 0)),
        scratch_shapes=[pltpu.VMEM((1, tq, kt), I32),
                        pltpu.VMEM((H_ATT, tq, LANES), F32),
                        pltpu.VMEM((H_ATT, tq, LANES), F32),
                        pltpu.VMEM((H_ATT, tq, LANES), F32)],
        compiler_params=pltpu.CompilerParams(dimension_semantics=("arbitrary",),
                                             vmem_limit_bytes=VMEM_LIMIT),
        name="sparse_attn_sample",
    )(u, u, small, kt_all, v_all, ki_all, btab)


def _out_proj_kernel(x_ref, ys_ref, yc_ref, ya_ref, w_ref, g_ref, o_ref):
    acc = _dot(ys_ref[...], w_ref[0:D_SSM, :])
    acc = acc + _dot(yc_ref[...], w_ref[D_SSM:D_SSM + D_CONV, :])
    acc = acc + _dot(ya_ref[...], w_ref[D_SSM + D_CONV:D_MODEL, :])
    o_ref[...] = x_ref[...] + _rms(acc, g_ref[...])


def _out_proj(x, ys, yc, ya, w, g, *, tm):
    m = x.shape[0]
    return pl.pallas_call(
        _out_proj_kernel,
        out_shape=jax.ShapeDtypeStruct((m, D_MODEL), F32),
        grid=(m // tm,),
        in_specs=[pl.BlockSpec((tm, D_MODEL), lambda i: (i, 0)),
                  pl.BlockSpec((tm, D_SSM), lambda i: (i, 0)),
                  pl.BlockSpec((tm, D_CONV), lambda i: (i, 0)),
                  pl.BlockSpec((tm, D_ATT), lambda i: (i, 0)),
                  pl.BlockSpec((D_MODEL, D_MODEL), lambda i: (0, 0)),
                  pl.BlockSpec((1, D_MODEL), lambda i: (0, 0))],
        out_specs=pl.BlockSpec((tm, D_MODEL), lambda i: (i, 0)),
        compiler_params=pltpu.CompilerParams(dimension_semantics=("arbitrary",),
                                             vmem_limit_bytes=VMEM_LIMIT),
        name="out_proj",
    )(x, ys, yc, ya, w, g)


FFN_HALO = 8


def _ffn_kernel(x_ref, gpre_ref, wg_ref, wu_ref, wd_ref, cw_ref, cb_ref, gpost_ref, p1_ref, p2_ref,
                o_ref, aux_ref, h_ref, buf_ref, tail_ref, *, tm, tps, seq_len, chained):
    i = pl.program_id(0)
    j = pl.program_id(1)

    @pl.when(j == 0)
    def _():
        h_ref[...] = _rms(x_ref[...], gpre_ref[...]).astype(BF16)
        o_ref[...] = jnp.zeros(o_ref.shape, F32)

    h = h_ref[...]
    a_pre = _dot(h, wg_ref[...])
    buf_ref[FFN_HALO:FFN_HALO + tm, :] = a_pre
    if chained:
        seq_start = lax.rem(i, tps) == 0
        buf_ref[0:FFN_HALO, :] = jnp.where(seq_start, p1_ref[0], tail_ref[j])
        prev1 = buf_ref[FFN_HALO - 1:FFN_HALO - 1 + tm, :]
        prev2 = buf_ref[FFN_HALO - 2:FFN_HALO - 2 + tm, :]
        last = a_pre[tm - FFN_HALO:tm, :]
        tail_ref[j] = last
        aux_ref[0] = last
    else:
        buf_ref[0:FFN_HALO, :] = jnp.zeros((FFN_HALO, a_pre.shape[1]), F32)
        tpos = lax.rem(lax.broadcasted_iota(I32, a_pre.shape, 0), seq_len)
        prev1 = jnp.where(tpos >= 1, buf_ref[FFN_HALO - 1:FFN_HALO - 1 + tm, :], p1_ref[...])
        prev2 = jnp.where(tpos >= 2, buf_ref[FFN_HALO - 2:FFN_HALO - 2 + tm, :], p2_ref[...])
        aux_ref[...] = a_pre
    a = cw_ref[0:1, :] * prev2 + cw_ref[1:2, :] * prev1 + cw_ref[2:3, :] * a_pre + cb_ref[...]
    f = (_silu(a) * _dot(h, wu_ref[...])).astype(BF16)
    o_ref[...] += _dot(f, wd_ref[...])

    @pl.when(j == pl.num_programs(1) - 1)
    def _():
        o_ref[...] = x_ref[...] + _rms(o_ref[...], gpost_ref[...])


def _ffn(x, gpre, wg, wu, wd, cw, cb, gpost, p1, p2, *, tm, tf, seq_len, chained):
    m = x.shape[0]
    nf = D_FF // tf
    tps = max(seq_len // tm, 1)
    if chained:
        nseq = m // seq_len
        p_specs = [pl.BlockSpec((1, FFN_HALO, tf), lambda i, j: (i // tps, 0, j)),
                   pl.BlockSpec((1, FFN_HALO, tf), lambda i, j: (i // tps, 0, j))]
        aux_shape = jax.ShapeDtypeStruct((m // tm, FFN_HALO, D_FF), F32)
        aux_spec = pl.BlockSpec((1, FFN_HALO, tf), lambda i, j: (i, 0, j))
    else:
        p_specs = [pl.BlockSpec((tm, tf), lambda i, j: (i, j)), pl.BlockSpec((tm, tf), lambda i, j: (i, j))]
        aux_shape = jax.ShapeDtypeStruct((m, D_FF), F32)
        aux_spec = pl.BlockSpec((tm, tf), lambda i, j: (i, j))
    return pl.pallas_call(
        functools.partial(_ffn_kernel, tm=tm, tps=tps, seq_len=seq_len, chained=chained),
        out_shape=(jax.ShapeDtypeStruct((m, D_MODEL), F32), aux_shape),
        grid=(m // tm, nf),
        in_specs=[pl.BlockSpec((tm, D_MODEL), lambda i, j: (i, 0), pipeline_mode=pl.Buffered(1)),
                  pl.BlockSpec((1, D_MODEL), lambda i, j: (0, 0)),
                  pl.BlockSpec((D_MODEL, tf), lambda i, j: (0, j)),
                  pl.BlockSpec((D_MODEL, tf), lambda i, j: (0, j)),
                  pl.BlockSpec((tf, D_MODEL), lambda i, j: (j, 0)),
                  pl.BlockSpec((FFN_CONV, tf), lambda i, j: (0, j)),
                  pl.BlockSpec((1, tf), lambda i, j: (0, j)),
                  pl.BlockSpec((1, D_MODEL), lambda i, j: (0, 0))] + p_specs,
        out_specs=(pl.BlockSpec((tm, D_MODEL), lambda i, j: (i, 0)), aux_spec),
        scratch_shapes=[pltpu.VMEM((tm, D_MODEL), BF16),
                        pltpu.VMEM((FFN_HALO + tm, tf), F32),
                        pltpu.VMEM((nf, FFN_HALO, tf), F32)],
        compiler_params=pltpu.CompilerParams(dimension_semantics=("arbitrary", "arbitrary"),
                                             vmem_limit_bytes=VMEM_LIMIT),
        name="conv_ffn",
    )(x, gpre, wg, wu, wd, cw, cb, gpost, p1, p2)


def _prep_layer_weights(w):
    w_in = w["w_in"]
    o_dt = D_SSM + D_XBC
    o_glu = o_dt + H_SSM
    o_ki = o_glu + 2 * D_CONV + 4 * D_ATT
    o_wi = o_ki + D_IDX
    o_q = o_glu + 2 * D_CONV
    o_k = o_q + D_ATT
    o_qi = o_k + 2 * D_ATT
    pad = jnp.zeros((D_MODEL, LANES - H_SSM - H_IDX - D_IDX), w_in.dtype)
    w_r = jnp.concatenate([w_in[:, :o_dt], w_in[:, o_glu:o_k], w_in[:, o_qi:o_ki],
                           w_in[:, o_k:o_qi],
                           w_in[:, o_dt:o_glu], w_in[:, o_wi:o_wi + H_IDX], pad, w_in[:, o_ki:o_wi]],
                          axis=1).astype(BF16)
    assert w_r.shape[1] == D_PROJ and o_qi + H_IDX * D_IDX == o_ki
    row = lambda v: v.reshape(1, -1).astype(F32)
    padl = lambda v: jnp.pad(v.astype(F32), (0, LANES - v.shape[0])).reshape(1, LANES)
    return dict(
        w_in=w_r, g_mix_pre=row(w["g_mix_pre"]),
        ssm_conv_w=w["ssm_conv_w"].astype(F32), ssm_conv_b=row(w["ssm_conv_b"]),
        dt_bias=padl(w["ssm_dt_bias"]), a_log=padl(w["ssm_a_log"]),
        d_x=row(jnp.repeat(w["ssm_d"], SSM_HEAD_DIM)), ssm_norm_g=row(w["ssm_norm_g"]),
        cconv_w=w["cconv_w"].astype(F32), cconv_b=row(w["cconv_b"]),
        cconv_ln_g=row(w["cconv_ln_g"]), cconv_ln_b=row(w["cconv_ln_b"]),
        w_out=w["w_out"].astype(BF16), g_mix_post=row(w["g_mix_post"]), g_ffn_pre=row(w["g_ffn_pre"]),
        ffn_w_gate=w["ffn_w_gate"].astype(BF16), ffn_w_up=w["ffn_w_up"].astype(BF16),
        ffn_w_down=w["ffn_w_down"].astype(BF16), ffn_conv_w=w["ffn_conv_w"].astype(F32),
        ffn_conv_b=row(w["ffn_conv_b"]), g_ffn_post=row(w["g_ffn_post"]))


def _expand_matrix():
    e = np.zeros((LANES, D_SSM), np.float32)
    for h in range(H_SSM):
        e[h, h * SSM_HEAD_DIM:(h + 1) * SSM_HEAD_DIM] = 1.0
    return jnp.asarray(e, BF16)


def _front_pad(state, halo):
    return jnp.pad(state.astype(F32), ((0, 0), (halo - state.shape[1], 0), (0, 0)))


def _state_t(h):
    b = h.shape[0]
    return jnp.transpose(h.astype(F32), (0, 3, 1, 2)).reshape(b, SSM_STATE, D_SSM)


def _state_from_t(ht):
    b = ht.shape[0]
    return jnp.transpose(ht.reshape(b, SSM_STATE, H_SSM, SSM_HEAD_DIM), (0, 2, 3, 1))


def _mixer_common(x, lw, emat, ssm_conv_prev, ssm_h0, cconv_prev, *, bsz, t, tm_proj, kt, ssd_l, ssd_nv, cc_tm):
    u, k, v, small, *attn_ops = _in_proj(x, lw["g_mix_pre"], lw["w_in"], tm=tm_proj, kt=kt)
    y_ssm, ht, ctail = _ssd(u, small, _front_pad(ssm_conv_prev, SSD_HALO), _state_t(ssm_h0), lw["ssm_conv_w"],
                            lw["ssm_conv_b"], lw["dt_bias"], lw["a_log"], lw["d_x"], lw["ssm_norm_g"], emat,
                            bsz=bsz, t=t, L=ssd_l, nv=ssd_nv)
    y_conv, cctail = _cconv(u, _front_pad(cconv_prev, CC_HALO), lw["cconv_w"], lw["cconv_b"], lw["cconv_ln_g"],
                            lw["cconv_ln_b"], bsz=bsz, t=t, tm=cc_tm)
    ki = small[:, SM_KI:SM_KI + D_IDX]
    states = dict(k=k.reshape(bsz, t, H_ATT, HEAD_DIM), v=v.reshape(bsz, t, H_ATT, HEAD_DIM),
                  ki=ki.reshape(bsz, t, D_IDX), h=_state_from_t(ht),
                  ssm_conv=ctail[:, SSD_HALO - (SSM_CONV - 1):], cconv=cctail[:, CC_HALO - (CONV_WIDTH - 1):])
    return u, small, y_ssm, y_conv, k, v, ki, attn_ops, states


def _layer_prompt(x, lw, emat, btab, *, bsz, t, cfg):
    zeros = lambda *s: jnp.zeros(s, F32)
    kt = cfg["kt"]
    nk = t // kt
    u, small, y_ssm, y_conv, k, v, ki, (kb, vt, ki2), st = _mixer_common(
        x, lw, emat, zeros(bsz, SSM_CONV - 1, D_XBC), zeros(bsz, H_SSM, SSM_HEAD_DIM, SSM_STATE),
        zeros(bsz, CONV_WIDTH - 1, D_CONV), bsz=bsz, t=t, tm_proj=cfg["tm_proj"], kt=kt,
        ssd_l=cfg["ssd_l"], ssd_nv=cfg["ssd_l"], cc_tm=cfg["cc_tm"])
    tq = cfg["tq"]
    cfar = jnp.tile(btab[N_BIAS_NEAR - 1, :, 0:1, :], (1, 1, tq // LANES))
    y_att = _attn_prompt(u, small, kb.reshape(bsz, nk, kt, D_ATT), vt.reshape(bsz, nk, D_ATT, kt),
                         ki2.reshape(bsz, nk, kt, LANES), btab, cfar, bsz=bsz, t=t, tq=tq, kt=kt)
    x1 = _out_proj(x, y_ssm, y_conv, y_att, lw["w_out"], lw["g_mix_post"], tm=cfg["tm_out"])
    prev = zeros(bsz, FFN_HALO, D_FF)
    x2, ftail = _ffn(x1, lw["g_ffn_pre"], lw["ffn_w_gate"], lw["ffn_w_up"], lw["ffn_w_down"], lw["ffn_conv_w"],
                     lw["ffn_conv_b"], lw["g_ffn_post"], prev, prev, tm=cfg["tm_ffn"], tf=cfg["tf"], seq_len=t,
                     chained=True)
    ftail = ftail.reshape(bsz, t // cfg["tm_ffn"], FFN_HALO, D_FF)[:, -1]
    st["ffn_conv"] = ftail[:, FFN_HALO - (FFN_CONV - 1):]
    return x2, st


def _layer_sample(x, lw, emat, btab, past_k, past_v, past_ki, ssm_conv_prev, ssm_h0, cconv_prev, fconv_prev,
                  *, bsz, t, cfg):
    m = bsz * t
    u, small, y_ssm, y_conv, k, v, ki, _, st = _mixer_common(
        x, lw, emat, ssm_conv_prev, ssm_h0, cconv_prev, bsz=bsz, t=t, tm_proj=m, kt=None,
        ssd_l=LANES, ssd_nv=t, cc_tm=t)
    past = past_k.shape[1]
    n_keys = past + t
    ktp = cfg["kt_sample"]
    padk = lambda a: jnp.pad(a, ((0, 0), (0, ktp - n_keys), (0, 0)))
    k_all = padk(jnp.concatenate([past_k.reshape(bsz, past, D_ATT), k.reshape(bsz, t, D_ATT)], axis=1).astype(BF16))
    v_all = padk(jnp.concatenate([past_v.reshape(bsz, past, D_ATT), v.reshape(bsz, t, D_ATT)], axis=1).astype(BF16))
    ki_all = padk(jnp.concatenate([past_ki, ki.reshape(bsz, t, D_IDX)], axis=1).astype(BF16))
    kt_all = jnp.transpose(k_all, (0, 2, 1))
    kit = jnp.transpose(ki_all, (0, 2, 1))
    y_att = _attn_sample(u, small, kt_all, v_all, jnp.concatenate([kit, kit], axis=1), btab, bsz=bsz, tq=t, kt=ktp,
                         n_keys=n_keys)
    x1 = _out_proj(x, y_ssm, y_conv, y_att, lw["w_out"], lw["g_mix_post"], tm=m)
    fprev = fconv_prev.astype(F32)
    zrow = jnp.zeros((bsz, t - 1, D_FF), F32)
    p1 = jnp.concatenate([fprev[:, 1:2], zrow], axis=1).reshape(m, D_FF)
    p2 = jnp.concatenate([fprev, zrow[:, 1:]], axis=1).reshape(m, D_FF)
    x2, a_pre = _ffn(x1, lw["g_ffn_pre"], lw["ffn_w_gate"], lw["ffn_w_up"], lw["ffn_w_down"], lw["ffn_conv_w"],
                     lw["ffn_conv_b"], lw["g_ffn_post"], p1, p2, tm=m, tf=cfg["tf"], seq_len=t, chained=False)
    st["ffn_conv"] = a_pre.reshape(bsz, t, D_FF)[:, t - (FFN_CONV - 1):]
    return x2, st


_STATE_ORDER = ("k", "v", "ki", "h", "ssm_conv", "cconv", "ffn_conv")


def _prompt_cfg(t):
    big = t >= 4096
    return dict(tm_proj=1024 if big else 512, ssd_l=256, cc_tm=256, kt=512, tq=256,
                tm_out=512 if big else 256, tm_ffn=1024 if big else 256, tf=512)


def _forward(x_prompt, x_sample, cache_k, cache_v, cache_kidx, state_ssm, state_ssm_conv, state_cconv,
             state_ffn_conv, rel_bias, weights):
    bp, tp, _ = x_prompt.shape
    bs, ts, _ = x_sample.shape
    depth = weights["w_in"].shape[0]
    past = cache_k.shape[2]
    emat = _expand_matrix()
    cfg_p = _prompt_cfg(tp)
    kt_sample = -(-(past + ts) // LANES) * LANES
    cfg_s = dict(tf=512, kt_sample=kt_sample)
    rb = rel_bias.astype(F32)
    btab_p = _bias_table(rb, nd=N_BIAS_NEAR, rows=LANES, cols=LANES, off0=0, step=LANES, key_axis=0, scale=LOG2E)
    btab_s = _bias_table(rb, nd=1, rows=ts, cols=kt_sample, off0=-past, step=0)
    xp = x_prompt.reshape(bp * tp, D_MODEL)
    xs = x_sample.reshape(bs * ts, D_MODEL)
    p_states = {n: [] for n in _STATE_ORDER}
    s_states = {n: [] for n in _STATE_ORDER}
    for l in range(depth):
        lw = _prep_layer_weights({n: w[l] for n, w in weights.items()})
        xp, st_p = _layer_prompt(xp, lw, emat, btab_p, bsz=bp, t=tp, cfg=cfg_p)
        xs, st_s = _layer_sample(xs, lw, emat, btab_s, cache_k[l], cache_v[l], cache_kidx[l], state_ssm_conv[l],
                                 state_ssm[l], state_cconv[l], state_ffn_conv[l], bsz=bs, t=ts, cfg=cfg_s)
        for n in _STATE_ORDER:
            p_states[n].append(st_p[n])
            s_states[n].append(st_s[n])
    outs = [xp.reshape(bp, tp, D_MODEL), xs.reshape(bs, ts, D_MODEL)]
    outs += [jnp.stack(p_states[n]) for n in _STATE_ORDER]
    outs += [jnp.stack(s_states[n]) for n in _STATE_ORDER]
    return tuple(outs)


def kernel(x_prompt, x_sample, cache_k, cache_v, cache_kidx, state_ssm, state_ssm_conv, state_cconv, state_ffn_conv, rel_bias, g_mix_pre, w_in, ssm_conv_w, ssm_conv_b, ssm_dt_bias, ssm_a_log, ssm_d, ssm_norm_g, cconv_w, cconv_b, cconv_ln_g, cconv_ln_b, w_out, g_mix_post, g_ffn_pre, ffn_w_gate, ffn_w_up, ffn_conv_w, ffn_conv_b, ffn_w_down, g_ffn_post):
    weights = dict(g_mix_pre=g_mix_pre, w_in=w_in, ssm_conv_w=ssm_conv_w, ssm_conv_b=ssm_conv_b,
                   ssm_dt_bias=ssm_dt_bias, ssm_a_log=ssm_a_log, ssm_d=ssm_d, ssm_norm_g=ssm_norm_g,
                   cconv_w=cconv_w, cconv_b=cconv_b, cconv_ln_g=cconv_ln_g, cconv_ln_b=cconv_ln_b, w_out=w_out,
                   g_mix_post=g_mix_post, g_ffn_pre=g_ffn_pre, ffn_w_gate=ffn_w_gate, ffn_w_up=ffn_w_up,
                   ffn_conv_w=ffn_conv_w, ffn_conv_b=ffn_conv_b, ffn_w_down=ffn_w_down, g_ffn_post=g_ffn_post)
    return _forward(x_prompt, x_sample, cache_k, cache_v, cache_kidx, state_ssm, state_ssm_conv, state_cconv,
                    state_ffn_conv, rel_bias, weights)
```

```python
import functools
import math

import numpy as np
import jax
import jax.numpy as jnp
from jax import lax
from jax.experimental import pallas as pl
from jax.experimental.pallas import tpu as pltpu

F32 = jnp.float32
BF16 = jnp.bfloat16
I32 = jnp.int32

D_MODEL = 2048
D_SSM = 1024
SSM_HEAD_DIM = 64
H_SSM = 16
SSM_GROUPS = 2
SSM_STATE = 128
SSM_CONV = 4
D_XBC = D_SSM + 2 * SSM_GROUPS * SSM_STATE
D_CONV = 512
CONV_WIDTH = 31
D_ATT = 512
HEAD_DIM = 64
H_ATT = 8
H_IDX = 8
D_IDX = 64
TOPK = 256
CHUNK = 64
N_BUCKETS = 32
REL_MAX_DIST = 1024
D_FF = 5632
FFN_CONV = 3
EPS = 1e-6

LANES = 128
SUBLANES = 8

COL_Z, COL_XBC, COL_GLU, COL_Q, COL_QI = 0, 1024, 2560, 3584, 4096
D_U = 4608
PROJ_TILE = 1152
TAIL_K, TAIL_V, TAIL_SMALL = 0, 512, 1024
D_PROJ = D_U + PROJ_TILE
SM_DT, SM_WI, SM_KI = 0, 16, 64

INT_MIN = -(2 ** 31)
INT_MAX = 2 ** 31 - 1
NEG_BIG = -1e30
VMEM_LIMIT = 56 * 1024 * 1024


def _bucket_thresholds():
    nb = N_BUCKETS // 2
    max_exact = nb // 2
    n = np.arange(0, 4 * REL_MAX_DIST, dtype=np.int64)
    nf = np.maximum(n, 1).astype(np.float32)
    large = max_exact + (np.log(nf / np.float32(max_exact)) / np.float32(math.log(REL_MAX_DIST / max_exact))
                         * np.float32(nb - max_exact)).astype(np.int32)
    large = np.minimum(large, nb - 1)
    bucket = np.where(n < max_exact, n, large)
    steps = np.nonzero(np.diff(bucket))[0] + 1
    assert np.all(np.diff(bucket) >= 0) and np.all(np.diff(bucket) <= 1) and bucket[-1] == nb - 1
    return tuple(int(s) for s in steps)


BUCKET_STEPS = _bucket_thresholds()


def _sigmoid(x):
    return 1.0 / (1.0 + jnp.exp(-x))


def _silu(x):
    return x * _sigmoid(x)


def _split3(x):
    hi = x.astype(BF16)
    r1 = x - hi.astype(F32)
    mid = r1.astype(BF16)
    lo = (r1 - mid.astype(F32)).astype(BF16)
    return hi, mid, lo


def _dot(a, b):
    return jnp.dot(a, b, preferred_element_type=F32)


def _dot_nt(a, b):
    return lax.dot_general(a, b, (((1,), (1,)), ((), ())), preferred_element_type=F32)


def _exact_dot(sel_bf16, x_f32):
    hi, mid, lo = _split3(x_f32)
    return _dot(sel_bf16, hi) + _dot(sel_bf16, mid) + _dot(sel_bf16, lo)


def _exact_dot_r(x_f32, sel_bf16):
    hi, mid, lo = _split3(x_f32)
    return _dot(hi, sel_bf16) + _dot(mid, sel_bf16) + _dot(lo, sel_bf16)


def _rms(x, g):
    ms = jnp.mean(x * x, axis=-1, keepdims=True)
    return x * lax.rsqrt(ms + EPS) * g


def _in_proj_u_kernel(x_ref, g_ref, w_ref, u_ref, h_ref):
    @pl.when(pl.program_id(1) == 0)
    def _():
        h_ref[...] = _rms(x_ref[...], g_ref[...]).astype(BF16)

    u_ref[...] = _dot(h_ref[...], w_ref[...])


def _in_proj_tail_kernel(x_ref, g_ref, w_ref, k_ref, v_ref, sm_ref, *rest, kt):
    y = _dot(_rms(x_ref[...], g_ref[...]).astype(BF16), w_ref[...])
    k = y[:, TAIL_K:TAIL_K + D_ATT]
    v = y[:, TAIL_V:TAIL_V + D_ATT]
    sm = y[:, TAIL_SMALL:TAIL_SMALL + LANES]
    k_ref[...] = k
    v_ref[...] = v
    sm_ref[...] = sm
    if kt is not None:
        kb_ref, vt_ref, ki2_ref = rest
        kb_ref[...] = k.astype(BF16)
        vb = v.astype(BF16)
        eye = jnp.where(lax.broadcasted_iota(I32, (LANES, LANES), 0)
                        == lax.broadcasted_iota(I32, (LANES, LANES), 1), 1.0, 0.0).astype(BF16)
        for c in range(vb.shape[0] // kt):
            for p in range(D_ATT // LANES):
                blk = vb[c * kt:(c + 1) * kt, p * LANES:(p + 1) * LANES]
                vt_ref[c, p * LANES:(p + 1) * LANES, :] = _dot_nt(eye, blk).astype(BF16)
        kib = sm[:, SM_KI:SM_KI + D_IDX].astype(BF16)
        ki2_ref[...] = jnp.concatenate([kib, kib], axis=1)


def _in_proj(x, g, w, *, tm, kt=None):
    m, d = x.shape
    nj = D_U // PROJ_TILE
    assert m % tm == 0 and w.shape[1] == D_PROJ and (kt is None or tm % kt == 0)
    u = pl.pallas_call(
        _in_proj_u_kernel,
        out_shape=jax.ShapeDtypeStruct((m, D_U), F32),
        grid=(m // tm, nj),
        in_specs=[pl.BlockSpec((tm, d), lambda i, j: (i, 0)),
                  pl.BlockSpec((1, d), lambda i, j: (0, 0)),
                  pl.BlockSpec((d, PROJ_TILE), lambda i, j: (0, j))],
        out_specs=pl.BlockSpec((tm, PROJ_TILE), lambda i, j: (i, j)),
        scratch_shapes=[pltpu.VMEM((tm, d), BF16)],
        compiler_params=pltpu.CompilerParams(dimension_semantics=("arbitrary", "arbitrary"),
                                             vmem_limit_bytes=VMEM_LIMIT),
        name="rms_in_proj",
    )(x, g, w)
    row = lambda i: (i, 0)
    out_shape = [jax.ShapeDtypeStruct((m, D_ATT), F32), jax.ShapeDtypeStruct((m, D_ATT), F32),
                 jax.ShapeDtypeStruct((m, LANES), F32)]
    out_specs = [pl.BlockSpec((tm, D_ATT), row), pl.BlockSpec((tm, D_ATT), row), pl.BlockSpec((tm, LANES), row)]
    if kt is not None:
        out_shape += [jax.ShapeDtypeStruct((m, D_ATT), BF16), jax.ShapeDtypeStruct((m // kt, D_ATT, kt), BF16),
                      jax.ShapeDtypeStruct((m, LANES), BF16)]
        out_specs += [pl.BlockSpec((tm, D_ATT), row), pl.BlockSpec((tm // kt, D_ATT, kt), lambda i: (i, 0, 0)),
                      pl.BlockSpec((tm, LANES), row)]
    tail = pl.pallas_call(
        functools.partial(_in_proj_tail_kernel, kt=kt),
        out_shape=tuple(out_shape),
        grid=(m // tm,),
        in_specs=[pl.BlockSpec((tm, d), lambda i: (i, 0)),
                  pl.BlockSpec((1, d), lambda i: (0, 0)),
                  pl.BlockSpec((d, PROJ_TILE), lambda i: (0, nj))],
        out_specs=tuple(out_specs),
        compiler_params=pltpu.CompilerParams(dimension_semantics=("arbitrary",),
                                             vmem_limit_bytes=VMEM_LIMIT),
        name="rms_in_proj_tail",
    )(x, g, w)
    return (u,) + tuple(tail)


def _bias_table_kernel(rb_ref, o_ref, *, off0, step, key_axis, scale):
    d = pl.program_id(0)
    h = pl.program_id(1)
    rows, cols = o_ref.shape[2], o_ref.shape[3]
    rel = (lax.broadcasted_iota(I32, (rows, cols), key_axis) - lax.broadcasted_iota(I32, (rows, cols), 1 - key_axis)
           + (off0 - d * step))
    n = jnp.abs(rel)
    bucket = jnp.where(rel > 0, N_BUCKETS // 2, 0)
    for s in BUCKET_STEPS:
        bucket = bucket + jnp.where(n >= s, 1, 0)
    acc = jnp.zeros((rows, cols), F32)
    for b in range(N_BUCKETS):
        acc = jnp.where(bucket == b, rb_ref[b, h], acc)
    o_ref[0, 0] = acc * scale


def _bias_table(rel_bias, *, nd, rows, cols, off0, step, key_axis=1, scale=1.0):
    return pl.pallas_call(
        functools.partial(_bias_table_kernel, off0=off0, step=step, key_axis=key_axis, scale=scale),
        out_shape=jax.ShapeDtypeStruct((nd, H_ATT, rows, cols), F32),
        grid=(nd, H_ATT),
        in_specs=[pl.BlockSpec(memory_space=pltpu.SMEM)],
        out_specs=pl.BlockSpec((1, 1, rows, cols), lambda d, h: (d, h, 0, 0)),
        name="bias_table",
    )(rel_bias)


CC_HALO = 32


def _cconv_kernel(val_ref, gate_ref, prev_ref, w_ref, b_ref, lg_ref, lb_ref, y_ref, tail_ref, buf_ref, sh_ref, *, tm):
    @pl.when(pl.program_id(1) == 0)
    def _():
        buf_ref[0:CC_HALO, :] = prev_ref[0]

    buf_ref[CC_HALO:CC_HALO + tm, :] = val_ref[...] * _sigmoid(gate_ref[...])
    span = CC_HALO + tm - SUBLANES
    for b in range(1, SUBLANES):
        sh_ref[b - 1, 0:span, :] = buf_ref[b:b + span, :]
    first = CC_HALO - (CONV_WIDTH - 1)
    acc = jnp.zeros((tm, D_CONV), F32) + b_ref[...]
    for k in range(CONV_WIDTH):
        base, b = (first + k) // SUBLANES * SUBLANES, (first + k) % SUBLANES
        rows = buf_ref[base:base + tm, :] if b == 0 else sh_ref[b - 1, base:base + tm, :]
        acc = acc + w_ref[k:k + 1, :] * rows
    mu = jnp.mean(acc, axis=-1, keepdims=True)
    xc = acc - mu
    var = jnp.mean(xc * xc, axis=-1, keepdims=True)
    y = xc * lax.rsqrt(var + EPS) * lg_ref[...] + lb_ref[...]
    y_ref[...] = _silu(y).astype(y_ref.dtype)
    tail = buf_ref[tm:tm + CC_HALO, :]
    tail_ref[0] = tail
    buf_ref[0:CC_HALO, :] = tail


def _cconv(u, prev, w, b, lg, lb, *, bsz, t, tm):
    nt = t // tm
    row = lambda bi, ti: bi * nt + ti
    return pl.pallas_call(
        functools.partial(_cconv_kernel, tm=tm),
        out_shape=(jax.ShapeDtypeStruct((bsz * t, D_CONV), BF16),
                   jax.ShapeDtypeStruct((bsz, CC_HALO, D_CONV), F32)),
        grid=(bsz, nt),
        in_specs=[pl.BlockSpec((tm, D_CONV), lambda bi, ti: (row(bi, ti), COL_GLU // D_CONV)),
                  pl.BlockSpec((tm, D_CONV), lambda bi, ti: (row(bi, ti), COL_GLU // D_CONV + 1)),
                  pl.BlockSpec((1, CC_HALO, D_CONV), lambda bi, ti: (bi, 0, 0)),
                  pl.BlockSpec((CONV_WIDTH, D_CONV), lambda bi, ti: (0, 0)),
                  pl.BlockSpec((1, D_CONV), lambda bi, ti: (0, 0)),
                  pl.BlockSpec((1, D_CONV), lambda bi, ti: (0, 0)),
                  pl.BlockSpec((1, D_CONV), lambda bi, ti: (0, 0))],
        out_specs=(pl.BlockSpec((tm, D_CONV), lambda bi, ti: (row(bi, ti), 0)),
                   pl.BlockSpec((1, CC_HALO, D_CONV), lambda bi, ti: (bi, 0, 0))),
        scratch_shapes=[pltpu.VMEM((CC_HALO + tm, D_CONV), F32),
                        pltpu.VMEM((SUBLANES - 1, CC_HALO + tm, D_CONV), F32)],
        compiler_params=pltpu.CompilerParams(dimension_semantics=("arbitrary", "arbitrary"),
                                             vmem_limit_bytes=VMEM_LIMIT),
        name="conformer_conv",
    )(u, u, prev, w, b, lg, lb)


SSD_HALO = 8


def _ssd_kernel(z_ref, x0_ref, x1_ref, x2_ref, sm_ref, prevc_ref, h0_ref, cw_ref, cb_ref, dtb_ref, alog_ref,
                dx_ref, ng_ref, e_ref, y_ref, hout_ref, ctail_ref, buf_ref, ht_ref, yb_ref, *, L, nv):
    @pl.when(pl.program_id(1) == 0)
    def _():
        buf_ref[0:SSD_HALO, :] = prevc_ref[0]
        ht_ref[...] = h0_ref[0]

    if nv < L:
        buf_ref[SSD_HALO + nv:SSD_HALO + L, :] = jnp.zeros((L - nv, D_XBC), F32)
    for c, r in enumerate((x0_ref, x1_ref, x2_ref)):
        buf_ref[SSD_HALO:SSD_HALO + nv, c * 512:(c + 1) * 512] = r[...]
    first = SSD_HALO - (SSM_CONV - 1)
    acc = jnp.zeros((L, D_XBC), F32) + cb_ref[...]
    for k in range(SSM_CONV):
        acc = acc + cw_ref[k:k + 1, :] * buf_ref[first + k:first + k + L, :]
    xbc = _silu(acc)
    tail = buf_ref[nv:nv + SSD_HALO, :]
    ctail_ref[0] = tail
    buf_ref[0:SSD_HALO, :] = tail

    xs = xbc[:, :D_SSM]
    lane = lax.broadcasted_iota(I32, (L, LANES), 1)
    rowi = lax.broadcasted_iota(I32, (L, LANES), 0)
    sm = sm_ref[...]
    if nv < L:
        sm = jnp.concatenate([sm, jnp.zeros((L - nv, LANES), F32)], axis=0)
    dtr = sm + dtb_ref[...]
    dt = jnp.maximum(dtr, 0.0) + jnp.log(1.0 + jnp.exp(-jnp.abs(dtr)))
    dt = jnp.where((lane < H_SSM) & (rowi < nv), dt, 0.0)
    a = -jnp.exp(alog_ref[...])
    da = dt * a
    ri = lax.broadcasted_iota(I32, (L, L), 0)
    ci = lax.broadcasted_iota(I32, (L, L), 1)
    causal = ri >= ci
    tril = jnp.where(causal, 1.0, 0.0).astype(BF16)
    cum = _exact_dot(tril, da)
    eye = jnp.where(lax.broadcasted_iota(I32, (LANES, LANES), 0) == lax.broadcasted_iota(I32, (LANES, LANES), 1),
                    1.0, 0.0).astype(BF16)
    ch, cm, cl = _split3(cum)
    cum_t = _dot_nt(eye, ch) + _dot_nt(eye, cm) + _dot_nt(eye, cl)
    e = e_ref[...]
    ecx = _exact_dot_r(jnp.exp(cum), e)
    dtx = _exact_dot_r(dt, e)
    xdt = (xs * dtx).astype(BF16)
    edl = ecx[L - 1:L, :]
    dend_t = jnp.exp(cum_t[:, L - 1:L] - cum_t)
    lane_l = lax.broadcasted_iota(I32, (L, LANES), 1)
    lane_n = lax.broadcasted_iota(I32, (SSM_STATE, LANES), 1)
    hpg = H_SSM // SSM_GROUPS
    for g in range(SSM_GROUPS):
        bg = xbc[:, D_SSM + g * SSM_STATE:D_SSM + (g + 1) * SSM_STATE].astype(BF16)
        cg = xbc[:, D_SSM + (SSM_GROUPS + g) * SSM_STATE:D_SSM + (SSM_GROUPS + g + 1) * SSM_STATE].astype(BF16)
        cbt = _dot_nt(cg, bg)
        bg_t = _dot_nt(eye, bg)
        gcols = slice(g * hpg * SSM_HEAD_DIM, (g + 1) * hpg * SSM_HEAD_DIM)
        yoff = _dot(cg, ht_ref[:, gcols].astype(BF16)) * ecx[:, gcols]
        for p in range(hpg // 2):
            h0 = g * hpg + 2 * p
            pcols = slice(h0 * SSM_HEAD_DIM, (h0 + 2) * SSM_HEAD_DIM)
            xpair = xdt[:, pcols]
            res, st = [], []
            for hh in (h0, h0 + 1):
                seg = cum[:, hh:hh + 1] - cum_t[hh:hh + 1, :]
                dec = jnp.where(causal, jnp.exp(seg), 0.0)
                res.append(_dot((cbt * dec).astype(BF16), xpair))
                st.append(_dot((bg_t * dend_t[hh:hh + 1, :]).astype(BF16), xpair))
            yb_ref[:, pcols] = (jnp.where(lane_l < SSM_HEAD_DIM, res[0], res[1])
                                + yoff[:, 2 * p * SSM_HEAD_DIM:(2 * p + 2) * SSM_HEAD_DIM])
            ht_ref[:, pcols] = (ht_ref[:, pcols] * edl[:, pcols]
                                + jnp.where(lane_n < SSM_HEAD_DIM, st[0], st[1]))
    hout_ref[0] = ht_ref[...]
    y = yb_ref[...] + dx_ref[...] * xs
    z = z_ref[...]
    if nv < L:
        z = jnp.concatenate([z, jnp.zeros((L - nv, D_SSM), F32)], axis=0)
    y = _rms(y * _silu(z), ng_ref[...])
    y_ref[...] = y[:nv].astype(y_ref.dtype)


def _ssd(u, small, prevc, h0t, cw, cb, dtb, alog, dx, ng, emat, *, bsz, t, L, nv):
    nt = t // nv
    row = lambda bi, ti: bi * nt + ti
    c512 = lambda c: (lambda bi, ti: (row(bi, ti), c))
    const2 = lambda bi, ti: (0, 0)
    return pl.pallas_call(
        functools.partial(_ssd_kernel, L=L, nv=nv),
        out_shape=(jax.ShapeDtypeStruct((bsz * t, D_SSM), BF16),
                   jax.ShapeDtypeStruct((bsz, SSM_STATE, D_SSM), F32),
                   jax.ShapeDtypeStruct((bsz, SSD_HALO, D_XBC), F32)),
        grid=(bsz, nt),
        in_specs=[pl.BlockSpec((nv, D_SSM), lambda bi, ti: (row(bi, ti), 0)),
                  pl.BlockSpec((nv, 512), c512(COL_XBC // 512)),
                  pl.BlockSpec((nv, 512), c512(COL_XBC // 512 + 1)),
                  pl.BlockSpec((nv, 512), c512(COL_XBC // 512 + 2)),
                  pl.BlockSpec((nv, LANES), c512(0)),
                  pl.BlockSpec((1, SSD_HALO, D_XBC), lambda bi, ti: (bi, 0, 0)),
                  pl.BlockSpec((1, SSM_STATE, D_SSM), lambda bi, ti: (bi, 0, 0)),
                  pl.BlockSpec((SSM_CONV, D_XBC), const2),
                  pl.BlockSpec((1, D_XBC), const2),
                  pl.BlockSpec((1, LANES), const2),
                  pl.BlockSpec((1, LANES), const2),
                  pl.BlockSpec((1, D_SSM), const2),
                  pl.BlockSpec((1, D_SSM), const2),
                  pl.BlockSpec((LANES, D_SSM), const2)],
        out_specs=(pl.BlockSpec((nv, D_SSM), lambda bi, ti: (row(bi, ti), 0)),
                   pl.BlockSpec((1, SSM_STATE, D_SSM), lambda bi, ti: (bi, 0, 0)),
                   pl.BlockSpec((1, SSD_HALO, D_XBC), lambda bi, ti: (bi, 0, 0))),
        scratch_shapes=[pltpu.VMEM((SSD_HALO + L, D_XBC), F32),
                        pltpu.VMEM((SSM_STATE, D_SSM), F32),
                        pltpu.VMEM((L, D_SSM), F32)],
        compiler_params=pltpu.CompilerParams(dimension_semantics=("arbitrary", "arbitrary"),
                                             vmem_limit_bytes=VMEM_LIMIT),
        name="ssd_mixer",
    )(u, u, u, u, small, prevc, h0t, cw, cb, dtb, alog, dx, ng, emat)


def _flip_negative(b):
    return b ^ ((b >> 31) & INT_MAX)


def _mono_key(x):
    return _flip_negative(lax.bitcast_convert_type(x, I32))


def _key_value(k):
    return lax.bitcast_convert_type(_flip_negative(k), F32)


F32_BIG = 3e38
MID_PERIOD = 12
BOOST_MAX = 2.0 ** 30
SEARCH_FIXED_STEPS = 14
SEARCH_MAX_STEPS = 400


def _attn_body(*, tq, kt, nkt, topk, qf, qif, sm, kt_tile, v_tile, ki_tile, adm_fn, bias_fn,
               sc_ref, m_ref, l_ref, acc_ref, y_ref):
    nsl = kt // LANES
    lane = lax.broadcasted_iota(I32, (tq, LANES), 1)
    low = lane < HEAD_DIM
    qb = (qf * (HEAD_DIM ** -0.5)).astype(BF16)
    qib = qif.astype(BF16)
    wi = sm[:, SM_WI:SM_WI + H_IDX] * ((D_IDX ** -0.5) * (H_IDX ** -0.5))
    zero_b = jnp.zeros((tq, LANES), BF16)

    def head_window(x, h):
        win = x[:, (h // 2) * LANES:(h // 2 + 1) * LANES]
        return jnp.where(low if h % 2 == 0 else jnp.logical_not(low), win, zero_b)

    qim = [head_window(qib, h) for h in range(H_IDX)]
    wcol = [wi[:, h:h + 1] for h in range(H_IDX)]

    def p1(j, carry):
        ki = ki_tile(j)
        acc = jnp.zeros((tq, kt), F32)
        for h in range(H_IDX):
            acc = acc + jnp.maximum(_dot(qim[h], ki), 0.0) * wcol[h]
        for s in range(nsl):
            key = jnp.where(adm_fn(j, s), _mono_key(acc[:, s * LANES:(s + 1) * LANES]), INT_MIN)
            sc_ref[j, :, s * LANES:(s + 1) * LANES] = key
        return carry

    lax.fori_loop(0, nkt, p1, 0)

    def count(pred):
        def body(j, acc):
            tile = sc_ref[j]
            for s in range(nsl):
                acc = acc + jnp.where(pred(tile[:, s * LANES:(s + 1) * LANES], j, s), 1.0, 0.0)
            return acc
        acc = lax.fori_loop(0, nkt, body, jnp.zeros((tq, LANES), F32))
        return jnp.sum(acc, axis=1, keepdims=True)

    def count_ge(cand):
        cb = jnp.broadcast_to(cand, (tq, LANES))
        return count(lambda t, j, s: t >= cb)

    kf = float(topk)
    prefix = jnp.where(count_ge(jnp.zeros((tq, 1), I32)) >= kf, 0, INT_MIN).astype(I32)

    def bit_step(it, prefix):
        trial = prefix | lax.shift_left(jnp.int32(1), 30 - it)
        return jnp.where(count_ge(trial) >= kf, trial, prefix)

    thr = lax.fori_loop(0, 31, bit_step, prefix)
    thr = jnp.maximum(thr, INT_MIN + 1)
    thr_b = jnp.broadcast_to(thr, (tq, LANES))
    n_gt = count(lambda t, j, s: t > thr_b)
    n_eq = count(lambda t, j, s: t == thr_b)
    take = kf - n_gt

    def kpos(j, s):
        return j * kt + s * LANES + lane

    def tie_cut(_):
        def step(it, cut):
            trial = cut | lax.shift_left(jnp.int32(1), 30 - it)
            tb = jnp.broadcast_to(trial, (tq, LANES))
            c = count(lambda t, j, s: (t == thr_b) & (kpos(j, s) < tb))
            return jnp.where(c <= take, trial, cut)
        return lax.fori_loop(0, 31, step, jnp.zeros((tq, 1), I32))

    has_tie = jnp.max(jnp.where(n_gt + n_eq > kf, 1.0, 0.0)) > 0.0
    cut = lax.cond(has_tie, tie_cut, lambda _: jnp.full((tq, 1), INT_MAX, I32), 0)
    cut_b = jnp.broadcast_to(cut, (tq, LANES))

    qm = [head_window(qb, h) for h in range(H_ATT)]
    m_ref[...] = jnp.full(m_ref.shape, NEG_BIG, F32)
    l_ref[...] = jnp.zeros(l_ref.shape, F32)
    acc_ref[...] = jnp.zeros(acc_ref.shape, F32)

    def p3(j, carry):
        keyt = sc_ref[j]
        sel = []
        for s in range(nsl):
            ks = keyt[:, s * LANES:(s + 1) * LANES]
            sel.append((ks > thr_b) | ((ks == thr_b) & (kpos(j, s) < cut_b)))
        for p in range(H_ATT // 2):
            kp = kt_tile(j, p)
            vp = v_tile(j, p)
            for hsub in range(2):
                h = 2 * p + hsub
                s_all = _dot(qm[h], kp)
                parts = [jnp.where(sel[s], s_all[:, s * LANES:(s + 1) * LANES] + bias_fn(j, s, h), NEG_BIG)
                         for s in range(nsl)]
                mx = parts[0]
                for s in range(1, nsl):
                    mx = jnp.maximum(mx, parts[s])
                m_old = m_ref[h]
                m_new = jnp.maximum(m_old, jnp.max(mx, axis=1, keepdims=True))
                alpha = jnp.exp(m_old - m_new)
                pr = [jnp.exp(part - m_new) for part in parts]
                psum = pr[0]
                for s in range(1, nsl):
                    psum = psum + pr[s]
                l_ref[h] = alpha * l_ref[h] + psum
                pb = jnp.concatenate([x.astype(BF16) for x in pr], axis=1)
                acc_ref[h] = alpha * acc_ref[h] + _dot(pb, vp)
                m_ref[h] = m_new
        return carry

    lax.fori_loop(0, nkt, p3, 0)
    for p in range(H_ATT // 2):
        outs = []
        for hsub in range(2):
            h = 2 * p + hsub
            lsum = jnp.sum(l_ref[h], axis=1, keepdims=True)
            outs.append(acc_ref[h] / lsum)
        y_ref[:, p * LANES:(p + 1) * LANES] = jnp.where(low, outs[0], outs[1]).astype(y_ref.dtype)


N_BIAS_NEAR = 7


LOG2E = math.log2(math.e)
PV_ROWS = 256


def _attn_prompt_kernel(q_ref, qi_ref, sm_ref, k_ref, vt_ref, ki_ref, bt_ref, cf_ref, y_ref,
                        sc_ref, qt_ref, zb_ref, nm_ref, m_ref, l_ref, acc_ref, *, tq, kt, topk):
    i = pl.program_id(1)
    nql = tq // LANES
    nsl = kt // LANES
    nkt = lax.div((i + 1) * tq + (kt - 1), kt)

    def fold(x, op, chains=4):
        groups = x.shape[0] // SUBLANES
        accs = [x[a * SUBLANES:(a + 1) * SUBLANES] for a in range(chains)]
        for r in range(chains, groups):
            accs[r % chains] = op(accs[r % chains], x[r * SUBLANES:(r + 1) * SUBLANES])
        while len(accs) > 1:
            accs = [op(accs[a], accs[a + len(accs) // 2]) for a in range(len(accs) // 2)]
        return accs[0]

    eye = jnp.where(lax.broadcasted_iota(I32, (LANES, LANES), 0) == lax.broadcasted_iota(I32, (LANES, LANES), 1),
                    1.0, 0.0).astype(BF16)
    lane = lax.broadcasted_iota(I32, (tq, LANES), 1)
    low = lane < HEAD_DIM
    qb = (q_ref[...] * (HEAD_DIM ** -0.5 * LOG2E)).astype(BF16)
    qib = qi_ref[...].astype(BF16)
    zero_b = jnp.zeros((tq, LANES), BF16)
    for src, base in ((qib, 0), (qb, H_IDX)):
        for h in range(H_ATT):
            win = src[:, (h // 2) * LANES:(h // 2 + 1) * LANES]
            win = jnp.where(low if h % 2 == 0 else jnp.logical_not(low), win, zero_b)
            qt_ref[base + h] = _dot_nt(eye, win).astype(BF16)
    sh, smm, sl_ = _split3(sm_ref[...])
    sm_t = _dot_nt(eye, sh) + _dot_nt(eye, smm) + _dot_nt(eye, sl_)
    wrow = [sm_t[SM_WI + h:SM_WI + h + 1, :] * ((D_IDX ** -0.5) * (H_IDX ** -0.5)) for h in range(H_IDX)]

    qpos = i * tq + lax.broadcasted_iota(I32, (1, tq), 1)
    cend = (lax.shift_right_logical(qpos, int(math.log2(CHUNK))) + 1) * CHUNK
    krow = lax.broadcasted_iota(I32, (LANES, tq), 0)

    def p1(j, carry):
        vmax, vmin = carry
        ki2 = ki_ref[0, j]
        for h in range(H_IDX):
            t = jnp.maximum(_dot(ki2, qt_ref[h]), 0.0) * wrow[h]
            if h == 0:
                zb_ref[0] = t
            elif h < H_IDX - 1:
                zb_ref[0] += t
            else:
                for sl in range(nsl):
                    rows = slice(sl * LANES, (sl + 1) * LANES)
                    sc = zb_ref[0, rows, :] + t[rows]
                    sc = jnp.where(sc == 0.0, 0.0, sc)
                    adm = (j * nsl + sl) * LANES + krow < cend
                    sc_ref[j * nsl + sl] = jnp.where(adm, _mono_key(sc), INT_MIN)
                    vmax = jnp.maximum(vmax, fold(jnp.where(adm, sc, -F32_BIG), jnp.maximum))
                    vmin = jnp.minimum(vmin, fold(jnp.where(adm, sc, F32_BIG), jnp.minimum))
        return vmax, vmin

    vmax, vmin = lax.fori_loop(0, nkt, p1, (jnp.full((SUBLANES, tq), -F32_BIG, F32),
                                            jnp.full((SUBLANES, tq), F32_BIG, F32)))

    def count(pred):
        def body(j, acc):
            for sl in range(nsl):
                g = j * nsl + sl
                acc = acc + fold(jnp.where(pred(sc_ref[g], g), 1.0, 0.0), jnp.add)
            return acc
        acc = lax.fori_loop(0, nkt, body, jnp.zeros((SUBLANES, tq), F32))
        return jnp.sum(acc, axis=0, keepdims=True)

    kf = float(topk)

    def active_of(lo, hi, clo):
        return (clo > kf) & (hi - 1 > lo)

    def search_cond(st):
        it, lo, hi, clo, chi, side, boost = st
        return (it < SEARCH_MAX_STEPS) & (jnp.max(jnp.where(active_of(lo, hi, clo), 1.0, 0.0)) > 0.0)

    def search_step(st):
        it, lo, hi, clo, chi, side, boost = st
        active = active_of(lo, hi, clo)
        v_lo = _key_value(lo)
        v_hi = _key_value(hi)
        frac = (jnp.log(clo) - math.log(kf)) / (jnp.log(clo) - jnp.log(jnp.maximum(chi, 0.5)))
        frac = jnp.where(side > 0, jnp.minimum(frac * boost, 0.5),
                         jnp.where(side < 0, 1.0 - jnp.minimum((1.0 - frac) * boost, 0.5), frac))
        t_int = _mono_key(v_lo + (v_hi - v_lo) * frac)
        t_mid = (lo >> 1) + (hi >> 1) + (lo & hi & 1)
        trial = jnp.where(lax.rem(it, MID_PERIOD) == MID_PERIOD - 1, t_mid, t_int)
        trial = jnp.where(it == 0, 0, jnp.where((it == 1) & (lo == 0), 1, trial))
        trial = jnp.minimum(jnp.maximum(trial, lo + 1), hi - 1)
        c = count(lambda t, g: t >= trial)
        up = active & (c >= kf)
        dn = active & (c < kf)
        now = jnp.where(c >= kf, 1, -1)
        boost = jnp.where(active, jnp.where(now == side, jnp.minimum(boost * 2.0, BOOST_MAX), 1.0), boost)
        side = jnp.where(active, now, side)
        return (it + 1, jnp.where(up, trial, lo), jnp.where(dn, trial, hi),
                jnp.where(up, c, clo), jnp.where(dn, c, chi), side, boost)

    lo0 = _mono_key(jnp.min(vmin, axis=0, keepdims=True))
    hi0 = _mono_key(jnp.max(vmax, axis=0, keepdims=True)) + 1
    state = (jnp.int32(0), lo0, hi0, cend.astype(F32), jnp.zeros((1, tq), F32),
             jnp.zeros((1, tq), I32), jnp.ones((1, tq), F32))
    state = lax.fori_loop(0, SEARCH_FIXED_STEPS, lambda _, st: search_step(st), state)
    _, thr, _, n_ge, n_gt, _, _ = lax.while_loop(search_cond, search_step, state)
    take = kf - n_gt

    def tie_cut(_):
        def step(it, cut):
            trial = cut | lax.shift_left(jnp.int32(1), 30 - it)
            c = count(lambda t, g: (t == thr) & (g * LANES + krow < trial))
            return jnp.where(c <= take, trial, cut)
        return lax.fori_loop(0, 31, step, jnp.zeros((1, tq), I32))

    has_tie = jnp.max(jnp.where(n_ge > kf, 1.0, 0.0)) > 0.0
    cut = lax.cond(has_tie, tie_cut, lambda _: jnp.full((1, tq), INT_MAX, I32), 0)

    m_ref[...] = jnp.full(m_ref.shape, NEG_BIG, F32)
    l_ref[...] = jnp.zeros(l_ref.shape, F32)
    acc_ref[...] = jnp.zeros(acc_ref.shape, F32)

    def tile_step(j, near):
        for sl in range(nsl):
            g = j * nsl + sl
            keyt = sc_ref[g]
            sel = (keyt > thr) | ((keyt == thr) & (g * LANES + krow < cut))
            nm_ref[sl * LANES:(sl + 1) * LANES, :] = jnp.where(sel, 0.0, NEG_BIG)

        def logits(h):
            mx = None
            for sl in range(nsl):
                rows = slice(sl * LANES, (sl + 1) * LANES)
                z = _dot(k_ref[0, j, rows, (h // 2) * LANES:(h // 2 + 1) * LANES], qt_ref[H_IDX + h]) + nm_ref[rows, :]
                if near:
                    z = z + jnp.concatenate(
                        [bt_ref[jnp.clip((i * nql + hf) - (j * nsl + sl), 0, N_BIAS_NEAR - 1), h]
                         for hf in range(nql)], axis=1)
                zb_ref[h % 2, rows, :] = z
                cm = fold(z, jnp.maximum)
                mx = cm if mx is None else jnp.maximum(mx, cm)
            return mx

        def accumulate(h, mx):
            shift = 0.0 if near else cf_ref[h]
            m_old = m_ref[h]
            m_new = jnp.maximum(m_old, jnp.max(mx, axis=0, keepdims=True) + shift)
            alpha = jnp.exp2(m_old - m_new)
            msub = m_new - shift
            lsum, pv = None, None
            for c in range(kt // PV_ROWS):
                rows = slice(c * PV_ROWS, (c + 1) * PV_ROWS)
                p = jnp.exp2(zb_ref[h % 2, rows, :] - msub)
                ls = fold(p, jnp.add)
                pc = _dot(vt_ref[0, j, (h // 2) * LANES:(h // 2 + 1) * LANES, rows], p.astype(BF16))
                lsum = ls if lsum is None else lsum + ls
                pv = pc if pv is None else pv + pc
            l_ref[h] = alpha * l_ref[h] + lsum
            acc_ref[h] = alpha * acc_ref[h] + pv
            m_ref[h] = m_new

        mx_next = logits(0)
        for h in range(H_ATT):
            mx_cur = mx_next
            if h + 1 < H_ATT:
                mx_next = logits(h + 1)
            accumulate(h, mx_cur)

    n_far = jnp.minimum(lax.div(jnp.maximum(nql * i - (N_BIAS_NEAR - 1) - (nsl - 1) + nsl, 0), nsl), nkt)

    def p3_far(j, carry):
        tile_step(j, False)
        return carry

    def p3_near(j, carry):
        tile_step(j, True)
        return carry

    lax.fori_loop(0, n_far, p3_far, 0)
    lax.fori_loop(n_far, nkt, p3_near, 0)

    eye_q = jnp.where(lax.broadcasted_iota(I32, (tq, tq), 0) == lax.broadcasted_iota(I32, (tq, tq), 1),
                      1.0, 0.0).astype(BF16)
    rowd = lax.broadcasted_iota(I32, (LANES, tq), 0)
    for p in range(H_ATT // 2):
        outs = []
        for hsub in range(2):
            h = 2 * p + hsub
            outs.append(acc_ref[h] / jnp.sum(l_ref[h], axis=0, keepdims=True))
        y_t = jnp.where(rowd < HEAD_DIM, outs[0], outs[1]).astype(BF16)
        y_ref[:, p * LANES:(p + 1) * LANES] = _dot_nt(eye_q, y_t).astype(y_ref.dtype)


def _attn_prompt(u, small, k_tiles, vt_tiles, ki_tiles, btab, cfar, *, bsz, t, tq, kt):
    assert t % kt == 0 and t % tq == 0 and tq % LANES == 0 and kt % LANES == 0
    nq = t // tq
    nk = t // kt
    row = lambda bi, qi: bi * nq + qi
    once = pl.Buffered(1)
    return pl.pallas_call(
        functools.partial(_attn_prompt_kernel, tq=tq, kt=kt, topk=min(TOPK, t // 4)),
        out_shape=jax.ShapeDtypeStruct((bsz * t, D_ATT), BF16),
        grid=(bsz, nq),
        in_specs=[pl.BlockSpec((tq, D_ATT), lambda bi, qi: (row(bi, qi), COL_Q // D_ATT)),
                  pl.BlockSpec((tq, D_ATT), lambda bi, qi: (row(bi, qi), COL_QI // D_ATT)),
                  pl.BlockSpec((tq, LANES), lambda bi, qi: (row(bi, qi), 0)),
                  pl.BlockSpec((1, nk, kt, D_ATT), lambda bi, qi: (bi, 0, 0, 0), pipeline_mode=once),
                  pl.BlockSpec((1, nk, D_ATT, kt), lambda bi, qi: (bi, 0, 0, 0), pipeline_mode=once),
                  pl.BlockSpec((1, nk, kt, LANES), lambda bi, qi: (bi, 0, 0, 0), pipeline_mode=once),
                  pl.BlockSpec((N_BIAS_NEAR, H_ATT, LANES, LANES), lambda bi, qi: (0, 0, 0, 0), pipeline_mode=once),
                  pl.BlockSpec((H_ATT, 1, tq), lambda bi, qi: (0, 0, 0), pipeline_mode=once)],
        out_specs=pl.BlockSpec((tq, D_ATT), lambda bi, qi: (row(bi, qi), 0)),
        scratch_shapes=[pltpu.VMEM((t // LANES, LANES, tq), I32),
                        pltpu.VMEM((H_IDX + H_ATT, LANES, tq), BF16),
                        pltpu.VMEM((2, kt, tq), F32),
                        pltpu.VMEM((kt, tq), F32),
                        pltpu.VMEM((H_ATT, 1, tq), F32),
                        pltpu.VMEM((H_ATT, SUBLANES, tq), F32),
                        pltpu.VMEM((H_ATT, LANES, tq), F32)],
        compiler_params=pltpu.CompilerParams(dimension_semantics=("arbitrary", "arbitrary"),
                                             vmem_limit_bytes=VMEM_LIMIT),
        name="sparse_attn_prompt",
    )(u, u, small, k_tiles, vt_tiles, ki_tiles, btab, cfar)


def _attn_sample_kernel(q_ref, qi_ref, sm_ref, kt_ref, v_ref, ki_ref, bt_ref, y_ref, sc_ref, m_ref, l_ref, acc_ref,
                        *, tq, kt, n_keys, topk):
    lane = lax.broadcasted_iota(I32, (tq, LANES), 1)

    def adm_fn(j, s):
        return (j * kt + s * LANES + lane) < n_keys

    def bias_fn(j, s, h):
        return bt_ref[0, h, :, s * LANES:(s + 1) * LANES]

    _attn_body(tq=tq, kt=kt, nkt=1, topk=topk, qf=q_ref[...], qif=qi_ref[...], sm=sm_ref[...],
               kt_tile=lambda j, p: kt_ref[0, p * LANES:(p + 1) * LANES, :],
               v_tile=lambda j, p: v_ref[0, :, p * LANES:(p + 1) * LANES],
               ki_tile=lambda j: ki_ref[0],
               adm_fn=adm_fn, bias_fn=bias_fn,
               sc_ref=sc_ref, m_ref=m_ref, l_ref=l_ref, acc_ref=acc_ref, y_ref=y_ref)


def _attn_sample(u, small, kt_all, v_all, ki_all, btab, *, bsz, tq, kt, n_keys):
    return pl.pallas_call(
        functools.partial(_attn_sample_kernel, tq=tq, kt=kt, n_keys=n_keys, topk=min(TOPK, n_keys // 4)),
        out_shape=jax.ShapeDtypeStruct((bsz * tq, D_ATT), BF16),
        grid=(bsz,),
        in_specs=[pl.BlockSpec((tq, D_ATT), lambda bi: (bi, COL_Q // D_ATT)),
                  pl.BlockSpec((tq, D_ATT), lambda bi: (bi, COL_QI // D_ATT)),
                  pl.BlockSpec((tq, LANES), lambda bi: (bi, 0)),
                  pl.BlockSpec((1, D_ATT, kt), lambda bi: (bi, 0, 0)),
                  pl.BlockSpec((1, kt, D_ATT), lambda bi: (bi, 0, 0)),
                  pl.BlockSpec((1, LANES, kt), lambda bi: (bi, 0, 0)),
                  pl.BlockSpec((1, H_ATT, tq, kt), lambda bi: (0, 0, 0, 0))],
        out_specs=pl.BlockSpec((tq, D_ATT), lambda bi: (bi, 0)),
        scratch_shapes=[pltpu.VMEM((1, tq, kt), I32),
                        pltpu.VMEM((H_ATT, tq, LANES), F32),
                        pltpu.VMEM((H_ATT, tq, LANES), F32),
                        pltpu.VMEM((H_ATT, tq, LANES), F32)],
        compiler_params=pltpu.CompilerParams(dimension_semantics=("arbitrary",),
                                             vmem_limit_bytes=VMEM_LIMIT),
        name="sparse_attn_sample",
    )(u, u, small, kt_all, v_all, ki_all, btab)


def _out_proj_kernel(x_ref, ys_ref, yc_ref, ya_ref, w_ref, g_ref, o_ref):
    acc = _dot(ys_ref[...], w_ref[0:D_SSM, :])
    acc = acc + _dot(yc_ref[...], w_ref[D_SSM:D_SSM + D_CONV, :])
    acc = acc + _dot(ya_ref[...], w_ref[D_SSM + D_CONV:D_MODEL, :])
    o_ref[...] = x_ref[...] + _rms(acc, g_ref[...])


def _out_proj(x, ys, yc, ya, w, g, *, tm):
    m = x.shape[0]
    return pl.pallas_call(
        _out_proj_kernel,
        out_shape=jax.ShapeDtypeStruct((m, D_MODEL), F32),
        grid=(m // tm,),
        in_specs=[pl.BlockSpec((tm, D_MODEL), lambda i: (i, 0)),
                  pl.BlockSpec((tm, D_SSM), lambda i: (i, 0)),
                  pl.BlockSpec((tm, D_CONV), lambda i: (i, 0)),
                  pl.BlockSpec((tm, D_ATT), lambda i: (i, 0)),
                  pl.BlockSpec((D_MODEL, D_MODEL), lambda i: (0, 0)),
                  pl.BlockSpec((1, D_MODEL), lambda i: (0, 0))],
        out_specs=pl.BlockSpec((tm, D_MODEL), lambda i: (i, 0)),
        compiler_params=pltpu.CompilerParams(dimension_semantics=("arbitrary",),
                                             vmem_limit_bytes=VMEM_LIMIT),
        name="out_proj",
    )(x, ys, yc, ya, w, g)


FFN_HALO = 8


def _ffn_kernel(x_ref, gpre_ref, wg_ref, wu_ref, wd_ref, cw_ref, cb_ref, gpost_ref, p1_ref, p2_ref,
                o_ref, aux_ref, h_ref, buf_ref, tail_ref, *, tm, tps, seq_len, chained):
    i = pl.program_id(0)
    j = pl.program_id(1)

    @pl.when(j == 0)
    def _():
        h_ref[...] = _rms(x_ref[...], gpre_ref[...]).astype(BF16)
        o_ref[...] = jnp.zeros(o_ref.shape, F32)

    h = h_ref[...]
    a_pre = _dot(h, wg_ref[...])
    buf_ref[FFN_HALO:FFN_HALO + tm, :] = a_pre
    if chained:
        seq_start = lax.rem(i, tps) == 0
        buf_ref[0:FFN_HALO, :] = jnp.where(seq_start, p1_ref[0], tail_ref[j])
        prev1 = buf_ref[FFN_HALO - 1:FFN_HALO - 1 + tm, :]
        prev2 = buf_ref[FFN_HALO - 2:FFN_HALO - 2 + tm, :]
        last = a_pre[tm - FFN_HALO:tm, :]
        tail_ref[j] = last
        aux_ref[0] = last
    else:
        buf_ref[0:FFN_HALO, :] = jnp.zeros((FFN_HALO, a_pre.shape[1]), F32)
        tpos = lax.rem(lax.broadcasted_iota(I32, a_pre.shape, 0), seq_len)
        prev1 = jnp.where(tpos >= 1, buf_ref[FFN_HALO - 1:FFN_HALO - 1 + tm, :], p1_ref[...])
        prev2 = jnp.where(tpos >= 2, buf_ref[FFN_HALO - 2:FFN_HALO - 2 + tm, :], p2_ref[...])
        aux_ref[...] = a_pre
    a = cw_ref[0:1, :] * prev2 + cw_ref[1:2, :] * prev1 + cw_ref[2:3, :] * a_pre + cb_ref[...]
    f = (_silu(a) * _dot(h, wu_ref[...])).astype(BF16)
    o_ref[...] += _dot(f, wd_ref[...])

    @pl.when(j == pl.num_programs(1) - 1)
    def _():
        o_ref[...] = x_ref[...] + _rms(o_ref[...], gpost_ref[...])


def _ffn(x, gpre, wg, wu, wd, cw, cb, gpost, p1, p2, *, tm, tf, seq_len, chained):
    m = x.shape[0]
    nf = D_FF // tf
    tps = max(seq_len // tm, 1)
    if chained:
        nseq = m // seq_len
        p_specs = [pl.BlockSpec((1, FFN_HALO, tf), lambda i, j: (i // tps, 0, j)),
                   pl.BlockSpec((1, FFN_HALO, tf), lambda i, j: (i // tps, 0, j))]
        aux_shape = jax.ShapeDtypeStruct((m // tm, FFN_HALO, D_FF), F32)
        aux_spec = pl.BlockSpec((1, FFN_HALO, tf), lambda i, j: (i, 0, j))
    else:
        p_specs = [pl.BlockSpec((tm, tf), lambda i, j: (i, j)), pl.BlockSpec((tm, tf), lambda i, j: (i, j))]
        aux_shape = jax.ShapeDtypeStruct((m, D_FF), F32)
        aux_spec = pl.BlockSpec((tm, tf), lambda i, j: (i, j))
    return pl.pallas_call(
        functools.partial(_ffn_kernel, tm=tm, tps=tps, seq_len=seq_len, chained=chained),
        out_shape=(jax.ShapeDtypeStruct((m, D_MODEL), F32), aux_shape),
        grid=(m // tm, nf),
        in_specs=[pl.BlockSpec((tm, D_MODEL), lambda i, j: (i, 0), pipeline_mode=pl.Buffered(1)),
                  pl.BlockSpec((1, D_MODEL), lambda i, j: (0, 0)),
                  pl.BlockSpec((D_MODEL, tf), lambda i, j: (0, j)),
                  pl.BlockSpec((D_MODEL, tf), lambda i, j: (0, j)),
                  pl.BlockSpec((tf, D_MODEL), lambda i, j: (j, 0)),
                  pl.BlockSpec((FFN_CONV, tf), lambda i, j: (0, j)),
                  pl.BlockSpec((1, tf), lambda i, j: (0, j)),
                  pl.BlockSpec((1, D_MODEL), lambda i, j: (0, 0))] + p_specs,
        out_specs=(pl.BlockSpec((tm, D_MODEL), lambda i, j: (i, 0)), aux_spec),
        scratch_shapes=[pltpu.VMEM((tm, D_MODEL), BF16),
                        pltpu.VMEM((FFN_HALO + tm, tf), F32),
                        pltpu.VMEM((nf, FFN_HALO, tf), F32)],
        compiler_params=pltpu.CompilerParams(dimension_semantics=("arbitrary", "arbitrary"),
                                             vmem_limit_bytes=VMEM_LIMIT),
        name="conv_ffn",
    )(x, gpre, wg, wu, wd, cw, cb, gpost, p1, p2)


def _prep_layer_weights(w):
    w_in = w["w_in"]
    o_dt = D_SSM + D_XBC
    o_glu = o_dt + H_SSM
    o_ki = o_glu + 2 * D_CONV + 4 * D_ATT
    o_wi = o_ki + D_IDX
    o_q = o_glu + 2 * D_CONV
    o_k = o_q + D_ATT
    o_qi = o_k + 2 * D_ATT
    pad = jnp.zeros((D_MODEL, LANES - H_SSM - H_IDX - D_IDX), w_in.dtype)
    w_r = jnp.concatenate([w_in[:, :o_dt], w_in[:, o_glu:o_k], w_in[:, o_qi:o_ki],
                           w_in[:, o_k:o_qi],
                           w_in[:, o_dt:o_glu], w_in[:, o_wi:o_wi + H_IDX], pad, w_in[:, o_ki:o_wi]],
                          axis=1).astype(BF16)
    assert w_r.shape[1] == D_PROJ and o_qi + H_IDX * D_IDX == o_ki
    row = lambda v: v.reshape(1, -1).astype(F32)
    padl = lambda v: jnp.pad(v.astype(F32), (0, LANES - v.shape[0])).reshape(1, LANES)
    return dict(
        w_in=w_r, g_mix_pre=row(w["g_mix_pre"]),
        ssm_conv_w=w["ssm_conv_w"].astype(F32), ssm_conv_b=row(w["ssm_conv_b"]),
        dt_bias=padl(w["ssm_dt_bias"]), a_log=padl(w["ssm_a_log"]),
        d_x=row(jnp.repeat(w["ssm_d"], SSM_HEAD_DIM)), ssm_norm_g=row(w["ssm_norm_g"]),
        cconv_w=w["cconv_w"].astype(F32), cconv_b=row(w["cconv_b"]),
        cconv_ln_g=row(w["cconv_ln_g"]), cconv_ln_b=row(w["cconv_ln_b"]),
        w_out=w["w_out"].astype(BF16), g_mix_post=row(w["g_mix_post"]), g_ffn_pre=row(w["g_ffn_pre"]),
        ffn_w_gate=w["ffn_w_gate"].astype(BF16), ffn_w_up=w["ffn_w_up"].astype(BF16),
        ffn_w_down=w["ffn_w_down"].astype(BF16), ffn_conv_w=w["ffn_conv_w"].astype(F32),
        ffn_conv_b=row(w["ffn_conv_b"]), g_ffn_post=row(w["g_ffn_post"]))


def _expand_matrix():
    e = np.zeros((LANES, D_SSM), np.float32)
    for h in range(H_SSM):
        e[h, h * SSM_HEAD_DIM:(h + 1) * SSM_HEAD_DIM] = 1.0
    return jnp.asarray(e, BF16)


def _front_pad(state, halo):
    return jnp.pad(state.astype(F32), ((0, 0), (halo - state.shape[1], 0), (0, 0)))


def _state_t(h):
    b = h.shape[0]
    return jnp.transpose(h.astype(F32), (0, 3, 1, 2)).reshape(b, SSM_STATE, D_SSM)


def _state_from_t(ht):
    b = ht.shape[0]
    return jnp.transpose(ht.reshape(b, SSM_STATE, H_SSM, SSM_HEAD_DIM), (0, 2, 3, 1))


def _mixer_common(x, lw, emat, ssm_conv_prev, ssm_h0, cconv_prev, *, bsz, t, tm_proj, kt, ssd_l, ssd_nv, cc_tm):
    u, k, v, small, *attn_ops = _in_proj(x, lw["g_mix_pre"], lw["w_in"], tm=tm_proj, kt=kt)
    y_ssm, ht, ctail = _ssd(u, small, _front_pad(ssm_conv_prev, SSD_HALO), _state_t(ssm_h0), lw["ssm_conv_w"],
                            lw["ssm_conv_b"], lw["dt_bias"], lw["a_log"], lw["d_x"], lw["ssm_norm_g"], emat,
                            bsz=bsz, t=t, L=ssd_l, nv=ssd_nv)
    y_conv, cctail = _cconv(u, _front_pad(cconv_prev, CC_HALO), lw["cconv_w"], lw["cconv_b"], lw["cconv_ln_g"],
                            lw["cconv_ln_b"], bsz=bsz, t=t, tm=cc_tm)
    ki = small[:, SM_KI:SM_KI + D_IDX]
    states = dict(k=k.reshape(bsz, t, H_ATT, HEAD_DIM), v=v.reshape(bsz, t, H_ATT, HEAD_DIM),
                  ki=ki.reshape(bsz, t, D_IDX), h=_state_from_t(ht),
                  ssm_conv=ctail[:, SSD_HALO - (SSM_CONV - 1):], cconv=cctail[:, CC_HALO - (CONV_WIDTH - 1):])
    return u, small, y_ssm, y_conv, k, v, ki, attn_ops, states


def _layer_prompt(x, lw, emat, btab, *, bsz, t, cfg):
    zeros = lambda *s: jnp.zeros(s, F32)
    kt = cfg["kt"]
    nk = t // kt
    u, small, y_ssm, y_conv, k, v, ki, (kb, vt, ki2), st = _mixer_common(
        x, lw, emat, zeros(bsz, SSM_CONV - 1, D_XBC), zeros(bsz, H_SSM, SSM_HEAD_DIM, SSM_STATE),
        zeros(bsz, CONV_WIDTH - 1, D_CONV), bsz=bsz, t=t, tm_proj=cfg["tm_proj"], kt=kt,
        ssd_l=cfg["ssd_l"], ssd_nv=cfg["ssd_l"], cc_tm=cfg["cc_tm"])
    tq = cfg["tq"]
    cfar = jnp.tile(btab[N_BIAS_NEAR - 1, :, 0:1, :], (1, 1, tq // LANES))
    y_att = _attn_prompt(u, small, kb.reshape(bsz, nk, kt, D_ATT), vt.reshape(bsz, nk, D_ATT, kt),
                         ki2.reshape(bsz, nk, kt, LANES), btab, cfar, bsz=bsz, t=t, tq=tq, kt=kt)
    x1 = _out_proj(x, y_ssm, y_conv, y_att, lw["w_out"], lw["g_mix_post"], tm=cfg["tm_out"])
    prev = zeros(bsz, FFN_HALO, D_FF)
    x2, ftail = _ffn(x1, lw["g_ffn_pre"], lw["ffn_w_gate"], lw["ffn_w_up"], lw["ffn_w_down"], lw["ffn_conv_w"],
                     lw["ffn_conv_b"], lw["g_ffn_post"], prev, prev, tm=cfg["tm_ffn"], tf=cfg["tf"], seq_len=t,
                     chained=True)
    ftail = ftail.reshape(bsz, t // cfg["tm_ffn"], FFN_HALO, D_FF)[:, -1]
    st["ffn_conv"] = ftail[:, FFN_HALO - (FFN_CONV - 1):]
    return x2, st


def _layer_sample(x, lw, emat, btab, past_k, past_v, past_ki, ssm_conv_prev, ssm_h0, cconv_prev, fconv_prev,
                  *, bsz, t, cfg):
    m = bsz * t
    u, small, y_ssm, y_conv, k, v, ki, _, st = _mixer_common(
        x, lw, emat, ssm_conv_prev, ssm_h0, cconv_prev, bsz=bsz, t=t, tm_proj=m, kt=None,
        ssd_l=LANES, ssd_nv=t, cc_tm=t)
    past = past_k.shape[1]
    n_keys = past + t
    ktp = cfg["kt_sample"]
    padk = lambda a: jnp.pad(a, ((0, 0), (0, ktp - n_keys), (0, 0)))
    k_all = padk(jnp.concatenate([past_k.reshape(bsz, past, D_ATT), k.reshape(bsz, t, D_ATT)], axis=1).astype(BF16))
    v_all = padk(jnp.concatenate([past_v.reshape(bsz, past, D_ATT), v.reshape(bsz, t, D_ATT)], axis=1).astype(BF16))
    ki_all = padk(jnp.concatenate([past_ki, ki.reshape(bsz, t, D_IDX)], axis=1).astype(BF16))
    kt_all = jnp.transpose(k_all, (0, 2, 1))
    kit = jnp.transpose(ki_all, (0, 2, 1))
    y_att = _attn_sample(u, small, kt_all, v_all, jnp.concatenate([kit, kit], axis=1), btab, bsz=bsz, tq=t, kt=ktp,
                         n_keys=n_keys)
    x1 = _out_proj(x, y_ssm, y_conv, y_att, lw["w_out"], lw["g_mix_post"], tm=m)
    fprev = fconv_prev.astype(F32)
    zrow = jnp.zeros((bsz, t - 1, D_FF), F32)
    p1 = jnp.concatenate([fprev[:, 1:2], zrow], axis=1).reshape(m, D_FF)
    p2 = jnp.concatenate([fprev, zrow[:, 1:]], axis=1).reshape(m, D_FF)
    x2, a_pre = _ffn(x1, lw["g_ffn_pre"], lw["ffn_w_gate"], lw["ffn_w_up"], lw["ffn_w_down"], lw["ffn_conv_w"],
                     lw["ffn_conv_b"], lw["g_ffn_post"], p1, p2, tm=m, tf=cfg["tf"], seq_len=t, chained=False)
    st["ffn_conv"] = a_pre.reshape(bsz, t, D_FF)[:, t - (FFN_CONV - 1):]
    return x2, st


_STATE_ORDER = ("k", "v", "ki", "h", "ssm_conv", "cconv", "ffn_conv")


def _prompt_cfg(t):
    big = t >= 4096
    return dict(tm_proj=1024 if big else 512, ssd_l=256, cc_tm=256, kt=512, tq=256,
                tm_out=512 if big else 256, tm_ffn=1024 if big else 256, tf=512)


def _forward(x_prompt, x_sample, cache_k, cache_v, cache_kidx, state_ssm, state_ssm_conv, state_cconv,
             state_ffn_conv, rel_bias, weights):
    bp, tp, _ = x_prompt.shape
    bs, ts, _ = x_sample.shape
    depth = weights["w_in"].shape[0]
    past = cache_k.shape[2]
    emat = _expand_matrix()
    cfg_p = _prompt_cfg(tp)
    kt_sample = -(-(past + ts) // LANES) * LANES
    cfg_s = dict(tf=512, kt_sample=kt_sample)
    rb = rel_bias.astype(F32)
    btab_p = _bias_table(rb, nd=N_BIAS_NEAR, rows=LANES, cols=LANES, off0=0, step=LANES, key_axis=0, scale=LOG2E)
    btab_s = _bias_table(rb, nd=1, rows=ts, cols=kt_sample, off0=-past, step=0)
    xp = x_prompt.reshape(bp * tp, D_MODEL)
    xs = x_sample.reshape(bs * ts, D_MODEL)
    p_states = {n: [] for n in _STATE_ORDER}
    s_states = {n: [] for n in _STATE_ORDER}
    for l in range(depth):
        lw = _prep_layer_weights({n: w[l] for n, w in weights.items()})
        xp, st_p = _layer_prompt(xp, lw, emat, btab_p, bsz=bp, t=tp, cfg=cfg_p)
        xs, st_s = _layer_sample(xs, lw, emat, btab_s, cache_k[l], cache_v[l], cache_kidx[l], state_ssm_conv[l],
                                 state_ssm[l], state_cconv[l], state_ffn_conv[l], bsz=bs, t=ts, cfg=cfg_s)
        for n in _STATE_ORDER:
            p_states[n].append(st_p[n])
            s_states[n].append(st_s[n])
    outs = [xp.reshape(bp, tp, D_MODEL), xs.reshape(bs, ts, D_MODEL)]
    outs += [jnp.stack(p_states[n]) for n in _STATE_ORDER]
    outs += [jnp.stack(s_states[n]) for n in _STATE_ORDER]
    return tuple(outs)


def kernel(x_prompt, x_sample, cache_k, cache_v, cache_kidx, state_ssm, state_ssm_conv, state_cconv, state_ffn_conv, rel_bias, g_mix_pre, w_in, ssm_conv_w, ssm_conv_b, ssm_dt_bias, ssm_a_log, ssm_d, ssm_norm_g, cconv_w, cconv_b, cconv_ln_g, cconv_ln_b, w_out, g_mix_post, g_ffn_pre, ffn_w_gate, ffn_w_up, ffn_conv_w, ffn_conv_b, ffn_w_down, g_ffn_post):
    weights = dict(g_mix_pre=g_mix_pre, w_in=w_in, ssm_conv_w=ssm_conv_w, ssm_conv_b=ssm_conv_b,
                   ssm_dt_bias=ssm_dt_bias, ssm_a_log=ssm_a_log, ssm_d=ssm_d, ssm_norm_g=ssm_norm_g,
                   cconv_w=cconv_w, cconv_b=cconv_b, cconv_ln_g=cconv_ln_g, cconv_ln_b=cconv_ln_b, w_out=w_out,
                   g_mix_post=g_mix_post, g_ffn_pre=g_ffn_pre, ffn_w_gate=ffn_w_gate, ffn_w_up=ffn_w_up,
                   ffn_conv_w=ffn_conv_w, ffn_conv_b=ffn_conv_b, ffn_w_down=ffn_w_down, g_ffn_post=g_ffn_post)
    return _forward(x_prompt, x_sample, cache_k, cache_v, cache_kidx, state_ssm, state_ssm_conv, state_cconv,
                    state_ffn_conv, rel_bias, weights)
```

```python
import functools
import math

import numpy as np
import jax
import jax.numpy as jnp
from jax import lax
from jax.experimental import pallas as pl
from jax.experimental.pallas import tpu as pltpu

F32 = jnp.float32
BF16 = jnp.bfloat16
I32 = jnp.int32

D_MODEL = 2048
D_SSM = 1024
SSM_HEAD_DIM = 64
H_SSM = 16
SSM_GROUPS = 2
SSM_STATE = 128
SSM_CONV = 4
D_XBC = D_SSM + 2 * SSM_GROUPS * SSM_STATE
D_CONV = 512
CONV_WIDTH = 31
D_ATT = 512
HEAD_DIM = 64
H_ATT = 8
H_IDX = 8
D_IDX = 64
TOPK = 256
CHUNK = 64
N_BUCKETS = 32
REL_MAX_DIST = 1024
D_FF = 5632
FFN_CONV = 3
EPS = 1e-6

LANES = 128
SUBLANES = 8

COL_Z, COL_XBC, COL_GLU, COL_Q, COL_QI = 0, 1024, 2560, 3584, 4096
D_U = 4608
PROJ_TILE = 1152
TAIL_K, TAIL_V, TAIL_SMALL = 0, 512, 1024
TAIL_TM = 512
D_PROJ = D_U + PROJ_TILE
SM_DT, SM_WI, SM_KI = 0, 16, 64

INT_MIN = -(2 ** 31)
INT_MAX = 2 ** 31 - 1
NEG_BIG = -1e30
VMEM_LIMIT = 56 * 1024 * 1024


def _bucket_thresholds():
    nb = N_BUCKETS // 2
    max_exact = nb // 2
    n = np.arange(0, 4 * REL_MAX_DIST, dtype=np.int64)
    nf = np.maximum(n, 1).astype(np.float32)
    large = max_exact + (np.log(nf / np.float32(max_exact)) / np.float32(math.log(REL_MAX_DIST / max_exact))
                         * np.float32(nb - max_exact)).astype(np.int32)
    large = np.minimum(large, nb - 1)
    bucket = np.where(n < max_exact, n, large)
    steps = np.nonzero(np.diff(bucket))[0] + 1
    assert np.all(np.diff(bucket) >= 0) and np.all(np.diff(bucket) <= 1) and bucket[-1] == nb - 1
    return tuple(int(s) for s in steps)


BUCKET_STEPS = _bucket_thresholds()


def _sigmoid(x):
    return 1.0 / (1.0 + jnp.exp(-x))


def _silu(x):
    return x * _sigmoid(x)


def _split3(x):
    hi = x.astype(BF16)
    r1 = x - hi.astype(F32)
    mid = r1.astype(BF16)
    lo = (r1 - mid.astype(F32)).astype(BF16)
    return hi, mid, lo


def _dot(a, b):
    return jnp.dot(a, b, preferred_element_type=F32)


def _dot_nt(a, b):
    return lax.dot_general(a, b, (((1,), (1,)), ((), ())), preferred_element_type=F32)


def _exact_dot(sel_bf16, x_f32):
    hi, mid, lo = _split3(x_f32)
    return _dot(sel_bf16, hi) + _dot(sel_bf16, mid) + _dot(sel_bf16, lo)


def _exact_dot_r(x_f32, sel_bf16):
    hi, mid, lo = _split3(x_f32)
    return _dot(hi, sel_bf16) + _dot(mid, sel_bf16) + _dot(lo, sel_bf16)


def _rms(x, g):
    ms = jnp.mean(x * x, axis=-1, keepdims=True)
    return x * lax.rsqrt(ms + EPS) * g


def _in_proj_u_kernel(x_ref, g_ref, w_ref, u_ref, h_ref):
    @pl.when(pl.program_id(1) == 0)
    def _():
        h_ref[...] = _rms(x_ref[...], g_ref[...]).astype(BF16)

    u_ref[...] = _dot(h_ref[...], w_ref[...])


def _in_proj_tail_kernel(x_ref, g_ref, w_ref, k_ref, v_ref, sm_ref, *rest, kt):
    y = _dot(_rms(x_ref[...], g_ref[...]).astype(BF16), w_ref[...])
    k = y[:, TAIL_K:TAIL_K + D_ATT]
    v = y[:, TAIL_V:TAIL_V + D_ATT]
    sm = y[:, TAIL_SMALL:TAIL_SMALL + LANES]
    sm_ref[...] = sm
    if kt is None:
        k_ref[...] = k
        v_ref[...] = v
    else:
        for o_ref, val in ((k_ref, k), (v_ref, v)):
            for h in range(H_ATT):
                o_ref[pl.ds(h, val.shape[0], stride=H_ATT), :] = val[:, h * HEAD_DIM:(h + 1) * HEAD_DIM]
    if kt is not None:
        kb_ref, vt_ref, ki2_ref = rest
        kb_ref[...] = k.astype(BF16)
        vb = v.astype(BF16)
        eye = jnp.where(lax.broadcasted_iota(I32, (LANES, LANES), 0)
                        == lax.broadcasted_iota(I32, (LANES, LANES), 1), 1.0, 0.0).astype(BF16)
        for c in range(vb.shape[0] // kt):
            for p in range(D_ATT // LANES):
                blk = vb[c * kt:(c + 1) * kt, p * LANES:(p + 1) * LANES]
                vt_ref[c, p * LANES:(p + 1) * LANES, :] = _dot_nt(eye, blk).astype(BF16)
        kib = sm[:, SM_KI:SM_KI + D_IDX].astype(BF16)
        ki2_ref[...] = jnp.concatenate([kib, kib], axis=1)


def _in_proj(x, g, w, *, tm, kt=None):
    m, d = x.shape
    nj = D_U // PROJ_TILE
    assert m % tm == 0 and w.shape[1] == D_PROJ and (kt is None or tm % kt == 0)
    u = pl.pallas_call(
        _in_proj_u_kernel,
        out_shape=jax.ShapeDtypeStruct((m, D_U), F32),
        grid=(m // tm, nj),
        in_specs=[pl.BlockSpec((tm, d), lambda i, j: (i, 0)),
                  pl.BlockSpec((1, d), lambda i, j: (0, 0)),
                  pl.BlockSpec((d, PROJ_TILE), lambda i, j: (0, j))],
        out_specs=pl.BlockSpec((tm, PROJ_TILE), lambda i, j: (i, j)),
        scratch_shapes=[pltpu.VMEM((tm, d), BF16)],
        compiler_params=pltpu.CompilerParams(dimension_semantics=("arbitrary", "arbitrary"),
                                             vmem_limit_bytes=VMEM_LIMIT),
        name="rms_in_proj",
    )(x, g, w)
    row = lambda i: (i, 0)
    if kt is None:
        kv_shape, kv_spec = jax.ShapeDtypeStruct((m, D_ATT), F32), pl.BlockSpec((tm, D_ATT), row)
    else:
        tm = min(tm, TAIL_TM)
        kv_shape = jax.ShapeDtypeStruct((m * H_ATT, HEAD_DIM), F32)
        kv_spec = pl.BlockSpec((tm * H_ATT, HEAD_DIM), row)
    out_shape = [kv_shape, kv_shape, jax.ShapeDtypeStruct((m, LANES), F32)]
    out_specs = [kv_spec, kv_spec, pl.BlockSpec((tm, LANES), row)]
    if kt is not None:
        out_shape += [jax.ShapeDtypeStruct((m, D_ATT), BF16), jax.ShapeDtypeStruct((m // kt, D_ATT, kt), BF16),
                      jax.ShapeDtypeStruct((m, LANES), BF16)]
        out_specs += [pl.BlockSpec((tm, D_ATT), row), pl.BlockSpec((tm // kt, D_ATT, kt), lambda i: (i, 0, 0)),
                      pl.BlockSpec((tm, LANES), row)]
    tail = pl.pallas_call(
        functools.partial(_in_proj_tail_kernel, kt=kt),
        out_shape=tuple(out_shape),
        grid=(m // tm,),
        in_specs=[pl.BlockSpec((tm, d), lambda i: (i, 0)),
                  pl.BlockSpec((1, d), lambda i: (0, 0)),
                  pl.BlockSpec((d, PROJ_TILE), lambda i: (0, nj))],
        out_specs=tuple(out_specs),
        compiler_params=pltpu.CompilerParams(dimension_semantics=("arbitrary",),
                                             vmem_limit_bytes=VMEM_LIMIT),
        name="rms_in_proj_tail",
    )(x, g, w)
    return (u,) + tuple(tail)


def _bias_table_kernel(rb_ref, o_ref, *, off0, step, key_axis, scale):
    d = pl.program_id(0)
    h = pl.program_id(1)
    rows, cols = o_ref.shape[2], o_ref.shape[3]
    rel = (lax.broadcasted_iota(I32, (rows, cols), key_axis) - lax.broadcasted_iota(I32, (rows, cols), 1 - key_axis)
           + (off0 - d * step))
    n = jnp.abs(rel)
    bucket = jnp.where(rel > 0, N_BUCKETS // 2, 0)
    for s in BUCKET_STEPS:
        bucket = bucket + jnp.where(n >= s, 1, 0)
    acc = jnp.zeros((rows, cols), F32)
    for b in range(N_BUCKETS):
        acc = jnp.where(bucket == b, rb_ref[b, h], acc)
    o_ref[0, 0] = acc * scale


def _bias_table(rel_bias, *, nd, rows, cols, off0, step, key_axis=1, scale=1.0):
    return pl.pallas_call(
        functools.partial(_bias_table_kernel, off0=off0, step=step, key_axis=key_axis, scale=scale),
        out_shape=jax.ShapeDtypeStruct((nd, H_ATT, rows, cols), F32),
        grid=(nd, H_ATT),
        in_specs=[pl.BlockSpec(memory_space=pltpu.SMEM)],
        out_specs=pl.BlockSpec((1, 1, rows, cols), lambda d, h: (d, h, 0, 0)),
        name="bias_table",
    )(rel_bias)


CC_HALO = 32


def _cconv_kernel(val_ref, gate_ref, prev_ref, w_ref, b_ref, lg_ref, lb_ref, y_ref, tail_ref, buf_ref, sh_ref, *, tm):
    @pl.when(pl.program_id(1) == 0)
    def _():
        buf_ref[0:CC_HALO, :] = prev_ref[0]

    buf_ref[CC_HALO:CC_HALO + tm, :] = val_ref[...] * _sigmoid(gate_ref[...])
    span = CC_HALO + tm - SUBLANES
    for b in range(1, SUBLANES):
        sh_ref[b - 1, 0:span, :] = buf_ref[b:b + span, :]
    first = CC_HALO - (CONV_WIDTH - 1)
    acc = jnp.zeros((tm, D_CONV), F32) + b_ref[...]
    for k in range(CONV_WIDTH):
        base, b = (first + k) // SUBLANES * SUBLANES, (first + k) % SUBLANES
        rows = buf_ref[base:base + tm, :] if b == 0 else sh_ref[b - 1, base:base + tm, :]
        acc = acc + w_ref[k:k + 1, :] * rows
    mu = jnp.mean(acc, axis=-1, keepdims=True)
    xc = acc - mu
    var = jnp.mean(xc * xc, axis=-1, keepdims=True)
    y = xc * lax.rsqrt(var + EPS) * lg_ref[...] + lb_ref[...]
    y_ref[...] = _silu(y).astype(y_ref.dtype)
    tail = buf_ref[tm:tm + CC_HALO, :]
    tail_ref[0] = tail
    buf_ref[0:CC_HALO, :] = tail


def _cconv(u, prev, w, b, lg, lb, *, bsz, t, tm):
    nt = t // tm
    row = lambda bi, ti: bi * nt + ti
    return pl.pallas_call(
        functools.partial(_cconv_kernel, tm=tm),
        out_shape=(jax.ShapeDtypeStruct((bsz * t, D_CONV), BF16),
                   jax.ShapeDtypeStruct((bsz, CC_HALO, D_CONV), F32)),
        grid=(bsz, nt),
        in_specs=[pl.BlockSpec((tm, D_CONV), lambda bi, ti: (row(bi, ti), COL_GLU // D_CONV)),
                  pl.BlockSpec((tm, D_CONV), lambda bi, ti: (row(bi, ti), COL_GLU // D_CONV + 1)),
                  pl.BlockSpec((1, CC_HALO, D_CONV), lambda bi, ti: (bi, 0, 0)),
                  pl.BlockSpec((CONV_WIDTH, D_CONV), lambda bi, ti: (0, 0)),
                  pl.BlockSpec((1, D_CONV), lambda bi, ti: (0, 0)),
                  pl.BlockSpec((1, D_CONV), lambda bi, ti: (0, 0)),
                  pl.BlockSpec((1, D_CONV), lambda bi, ti: (0, 0))],
        out_specs=(pl.BlockSpec((tm, D_CONV), lambda bi, ti: (row(bi, ti), 0)),
                   pl.BlockSpec((1, CC_HALO, D_CONV), lambda bi, ti: (bi, 0, 0))),
        scratch_shapes=[pltpu.VMEM((CC_HALO + tm, D_CONV), F32),
                        pltpu.VMEM((SUBLANES - 1, CC_HALO + tm, D_CONV), F32)],
        compiler_params=pltpu.CompilerParams(dimension_semantics=("arbitrary", "arbitrary"),
                                             vmem_limit_bytes=VMEM_LIMIT),
        name="conformer_conv",
    )(u, u, prev, w, b, lg, lb)


SSD_HALO = 8


def _ssd_kernel(z_ref, x0_ref, x1_ref, x2_ref, sm_ref, prevc_ref, h0_ref, cw_ref, cb_ref, dtb_ref, alog_ref,
                dx_ref, ng_ref, e_ref, y_ref, hout_ref, ctail_ref, buf_ref, ht_ref, yb_ref, *, L, nv):
    @pl.when(pl.program_id(1) == 0)
    def _():
        buf_ref[0:SSD_HALO, :] = prevc_ref[0]
        ht_ref[...] = h0_ref[0]

    if nv < L:
        buf_ref[SSD_HALO + nv:SSD_HALO + L, :] = jnp.zeros((L - nv, D_XBC), F32)
    for c, r in enumerate((x0_ref, x1_ref, x2_ref)):
        buf_ref[SSD_HALO:SSD_HALO + nv, c * 512:(c + 1) * 512] = r[...]
    first = SSD_HALO - (SSM_CONV - 1)
    acc = jnp.zeros((L, D_XBC), F32) + cb_ref[...]
    for k in range(SSM_CONV):
        acc = acc + cw_ref[k:k + 1, :] * buf_ref[first + k:first + k + L, :]
    xbc = _silu(acc)
    tail = buf_ref[nv:nv + SSD_HALO, :]
    ctail_ref[0] = tail
    buf_ref[0:SSD_HALO, :] = tail

    xs = xbc[:, :D_SSM]
    lane = lax.broadcasted_iota(I32, (L, LANES), 1)
    rowi = lax.broadcasted_iota(I32, (L, LANES), 0)
    sm = sm_ref[...]
    if nv < L:
        sm = jnp.concatenate([sm, jnp.zeros((L - nv, LANES), F32)], axis=0)
    dtr = sm + dtb_ref[...]
    dt = jnp.maximum(dtr, 0.0) + jnp.log(1.0 + jnp.exp(-jnp.abs(dtr)))
    dt = jnp.where((lane < H_SSM) & (rowi < nv), dt, 0.0)
    a = -jnp.exp(alog_ref[...])
    da = dt * a
    ri = lax.broadcasted_iota(I32, (L, L), 0)
    ci = lax.broadcasted_iota(I32, (L, L), 1)
    causal = ri >= ci
    tril = jnp.where(causal, 1.0, 0.0).astype(BF16)
    cum = _exact_dot(tril, da)
    eye = jnp.where(lax.broadcasted_iota(I32, (LANES, LANES), 0) == lax.broadcasted_iota(I32, (LANES, LANES), 1),
                    1.0, 0.0).astype(BF16)
    ch, cm, cl = _split3(cum)
    cum_t = _dot_nt(eye, ch) + _dot_nt(eye, cm) + _dot_nt(eye, cl)
    e = e_ref[...]
    ecx = _exact_dot_r(jnp.exp(cum), e)
    dtx = _exact_dot_r(dt, e)
    xdt = (xs * dtx).astype(BF16)
    edl = ecx[L - 1:L, :]
    dend_t = jnp.exp(cum_t[:, L - 1:L] - cum_t)
    lane_l = lax.broadcasted_iota(I32, (L, LANES), 1)
    lane_n = lax.broadcasted_iota(I32, (SSM_STATE, LANES), 1)
    hpg = H_SSM // SSM_GROUPS
    for g in range(SSM_GROUPS):
        bg = xbc[:, D_SSM + g * SSM_STATE:D_SSM + (g + 1) * SSM_STATE].astype(BF16)
        cg = xbc[:, D_SSM + (SSM_GROUPS + g) * SSM_STATE:D_SSM + (SSM_GROUPS + g + 1) * SSM_STATE].astype(BF16)
        cbt = _dot_nt(cg, bg)
        bg_t = _dot_nt(eye, bg)
        gcols = slice(g * hpg * SSM_HEAD_DIM, (g + 1) * hpg * SSM_HEAD_DIM)
        yoff = _dot(cg, ht_ref[:, gcols].astype(BF16)) * ecx[:, gcols]
        for p in range(hpg // 2):
            h0 = g * hpg + 2 * p
            pcols = slice(h0 * SSM_HEAD_DIM, (h0 + 2) * SSM_HEAD_DIM)
            xpair = xdt[:, pcols]
            res, st = [], []
            for hh in (h0, h0 + 1):
                seg = cum[:, hh:hh + 1] - cum_t[hh:hh + 1, :]
                dec = jnp.where(causal, jnp.exp(seg), 0.0)
                res.append(_dot((cbt * dec).astype(BF16), xpair))
                st.append(_dot((bg_t * dend_t[hh:hh + 1, :]).astype(BF16), xpair))
            yb_ref[:, pcols] = (jnp.where(lane_l < SSM_HEAD_DIM, res[0], res[1])
                                + yoff[:, 2 * p * SSM_HEAD_DIM:(2 * p + 2) * SSM_HEAD_DIM])
            ht_ref[:, pcols] = (ht_ref[:, pcols] * edl[:, pcols]
                                + jnp.where(lane_n < SSM_HEAD_DIM, st[0], st[1]))
    hout_ref[0] = ht_ref[...]
    y = yb_ref[...] + dx_ref[...] * xs
    z = z_ref[...]
    if nv < L:
        z = jnp.concatenate([z, jnp.zeros((L - nv, D_SSM), F32)], axis=0)
    y = _rms(y * _silu(z), ng_ref[...])
    y_ref[...] = y[:nv].astype(y_ref.dtype)


def _ssd(u, small, prevc, h0t, cw, cb, dtb, alog, dx, ng, emat, *, bsz, t, L, nv):
    nt = t // nv
    row = lambda bi, ti: bi * nt + ti
    c512 = lambda c: (lambda bi, ti: (row(bi, ti), c))
    const2 = lambda bi, ti: (0, 0)
    return pl.pallas_call(
        functools.partial(_ssd_kernel, L=L, nv=nv),
        out_shape=(jax.ShapeDtypeStruct((bsz * t, D_SSM), BF16),
                   jax.ShapeDtypeStruct((bsz, SSM_STATE, D_SSM), F32),
                   jax.ShapeDtypeStruct((bsz, SSD_HALO, D_XBC), F32)),
        grid=(bsz, nt),
        in_specs=[pl.BlockSpec((nv, D_SSM), lambda bi, ti: (row(bi, ti), 0)),
                  pl.BlockSpec((nv, 512), c512(COL_XBC // 512)),
                  pl.BlockSpec((nv, 512), c512(COL_XBC // 512 + 1)),
                  pl.BlockSpec((nv, 512), c512(COL_XBC // 512 + 2)),
                  pl.BlockSpec((nv, LANES), c512(0)),
                  pl.BlockSpec((1, SSD_HALO, D_XBC), lambda bi, ti: (bi, 0, 0)),
                  pl.BlockSpec((1, SSM_STATE, D_SSM), lambda bi, ti: (bi, 0, 0)),
                  pl.BlockSpec((SSM_CONV, D_XBC), const2),
                  pl.BlockSpec((1, D_XBC), const2),
                  pl.BlockSpec((1, LANES), const2),
                  pl.BlockSpec((1, LANES), const2),
                  pl.BlockSpec((1, D_SSM), const2),
                  pl.BlockSpec((1, D_SSM), const2),
                  pl.BlockSpec((LANES, D_SSM), const2)],
        out_specs=(pl.BlockSpec((nv, D_SSM), lambda bi, ti: (row(bi, ti), 0)),
                   pl.BlockSpec((1, SSM_STATE, D_SSM), lambda bi, ti: (bi, 0, 0)),
                   pl.BlockSpec((1, SSD_HALO, D_XBC), lambda bi, ti: (bi, 0, 0))),
        scratch_shapes=[pltpu.VMEM((SSD_HALO + L, D_XBC), F32),
                        pltpu.VMEM((SSM_STATE, D_SSM), F32),
                        pltpu.VMEM((L, D_SSM), F32)],
        compiler_params=pltpu.CompilerParams(dimension_semantics=("arbitrary", "arbitrary"),
                                             vmem_limit_bytes=VMEM_LIMIT),
        name="ssd_mixer",
    )(u, u, u, u, small, prevc, h0t, cw, cb, dtb, alog, dx, ng, emat)


def _flip_negative(b):
    return b ^ ((b >> 31) & INT_MAX)


def _mono_key(x):
    return _flip_negative(lax.bitcast_convert_type(x, I32))


def _key_value(k):
    return lax.bitcast_convert_type(_flip_negative(k), F32)


F32_BIG = 3e38
MID_PERIOD = 12
BOOST_MAX = 2.0 ** 30
SEARCH_FIXED_STEPS = 14
SEARCH_MAX_STEPS = 400


def _attn_body(*, tq, kt, nkt, topk, qf, qif, sm, kt_tile, v_tile, ki_tile, adm_fn, bias_fn,
               sc_ref, m_ref, l_ref, acc_ref, y_ref):
    nsl = kt // LANES
    lane = lax.broadcasted_iota(I32, (tq, LANES), 1)
    low = lane < HEAD_DIM
    qb = (qf * (HEAD_DIM ** -0.5)).astype(BF16)
    qib = qif.astype(BF16)
    wi = sm[:, SM_WI:SM_WI + H_IDX] * ((D_IDX ** -0.5) * (H_IDX ** -0.5))
    zero_b = jnp.zeros((tq, LANES), BF16)

    def head_window(x, h):
        win = x[:, (h // 2) * LANES:(h // 2 + 1) * LANES]
        return jnp.where(low if h % 2 == 0 else jnp.logical_not(low), win, zero_b)

    qim = [head_window(qib, h) for h in range(H_IDX)]
    wcol = [wi[:, h:h + 1] for h in range(H_IDX)]

    def p1(j, carry):
        ki = ki_tile(j)
        acc = jnp.zeros((tq, kt), F32)
        for h in range(H_IDX):
            acc = acc + jnp.maximum(_dot(qim[h], ki), 0.0) * wcol[h]
        for s in range(nsl):
            key = jnp.where(adm_fn(j, s), _mono_key(acc[:, s * LANES:(s + 1) * LANES]), INT_MIN)
            sc_ref[j, :, s * LANES:(s + 1) * LANES] = key
        return carry

    lax.fori_loop(0, nkt, p1, 0)

    def count(pred):
        def body(j, acc):
            tile = sc_ref[j]
            for s in range(nsl):
                acc = acc + jnp.where(pred(tile[:, s * LANES:(s + 1) * LANES], j, s), 1.0, 0.0)
            return acc
        acc = lax.fori_loop(0, nkt, body, jnp.zeros((tq, LANES), F32))
        return jnp.sum(acc, axis=1, keepdims=True)

    def count_ge(cand):
        cb = jnp.broadcast_to(cand, (tq, LANES))
        return count(lambda t, j, s: t >= cb)

    kf = float(topk)
    prefix = jnp.where(count_ge(jnp.zeros((tq, 1), I32)) >= kf, 0, INT_MIN).astype(I32)

    def bit_step(it, prefix):
        trial = prefix | lax.shift_left(jnp.int32(1), 30 - it)
        return jnp.where(count_ge(trial) >= kf, trial, prefix)

    thr = lax.fori_loop(0, 31, bit_step, prefix)
    thr = jnp.maximum(thr, INT_MIN + 1)
    thr_b = jnp.broadcast_to(thr, (tq, LANES))
    n_gt = count(lambda t, j, s: t > thr_b)
    n_eq = count(lambda t, j, s: t == thr_b)
    take = kf - n_gt

    def kpos(j, s):
        return j * kt + s * LANES + lane

    def tie_cut(_):
        def step(it, cut):
            trial = cut | lax.shift_left(jnp.int32(1), 30 - it)
            tb = jnp.broadcast_to(trial, (tq, LANES))
            c = count(lambda t, j, s: (t == thr_b) & (kpos(j, s) < tb))
            return jnp.where(c <= take, trial, cut)
        return lax.fori_loop(0, 31, step, jnp.zeros((tq, 1), I32))

    has_tie = jnp.max(jnp.where(n_gt + n_eq > kf, 1.0, 0.0)) > 0.0
    cut = lax.cond(has_tie, tie_cut, lambda _: jnp.full((tq, 1), INT_MAX, I32), 0)
    cut_b = jnp.broadcast_to(cut, (tq, LANES))

    qm = [head_window(qb, h) for h in range(H_ATT)]
    m_ref[...] = jnp.full(m_ref.shape, NEG_BIG, F32)
    l_ref[...] = jnp.zeros(l_ref.shape, F32)
    acc_ref[...] = jnp.zeros(acc_ref.shape, F32)

    def p3(j, carry):
        keyt = sc_ref[j]
        sel = []
        for s in range(nsl):
            ks = keyt[:, s * LANES:(s + 1) * LANES]
            sel.append((ks > thr_b) | ((ks == thr_b) & (kpos(j, s) < cut_b)))
        for p in range(H_ATT // 2):
            kp = kt_tile(j, p)
            vp = v_tile(j, p)
            for hsub in range(2):
                h = 2 * p + hsub
                s_all = _dot(qm[h], kp)
                parts = [jnp.where(sel[s], s_all[:, s * LANES:(s + 1) * LANES] + bias_fn(j, s, h), NEG_BIG)
                         for s in range(nsl)]
                mx = parts[0]
                for s in range(1, nsl):
                    mx = jnp.maximum(mx, parts[s])
                m_old = m_ref[h]
                m_new = jnp.maximum(m_old, jnp.max(mx, axis=1, keepdims=True))
                alpha = jnp.exp(m_old - m_new)
                pr = [jnp.exp(part - m_new) for part in parts]
                psum = pr[0]
                for s in range(1, nsl):
                    psum = psum + pr[s]
                l_ref[h] = alpha * l_ref[h] + psum
                pb = jnp.concatenate([x.astype(BF16) for x in pr], axis=1)
                acc_ref[h] = alpha * acc_ref[h] + _dot(pb, vp)
                m_ref[h] = m_new
        return carry

    lax.fori_loop(0, nkt, p3, 0)
    for p in range(H_ATT // 2):
        outs = []
        for hsub in range(2):
            h = 2 * p + hsub
            lsum = jnp.sum(l_ref[h], axis=1, keepdims=True)
            outs.append(acc_ref[h] / lsum)
        y_ref[:, p * LANES:(p + 1) * LANES] = jnp.where(low, outs[0], outs[1]).astype(y_ref.dtype)


N_BIAS_NEAR = 7


LOG2E = math.log2(math.e)
PV_ROWS = 256


def _attn_prompt_kernel(q_ref, qi_ref, sm_ref, k_ref, vt_ref, ki_ref, bt_ref, cf_ref, y_ref,
                        sc_ref, qt_ref, zb_ref, nm_ref, m_ref, l_ref, acc_ref, *, tq, kt, topk):
    i = pl.program_id(1)
    nql = tq // LANES
    nsl = kt // LANES
    nkt = lax.div((i + 1) * tq + (kt - 1), kt)

    def fold(x, op, chains=4):
        groups = x.shape[0] // SUBLANES
        accs = [x[a * SUBLANES:(a + 1) * SUBLANES] for a in range(chains)]
        for r in range(chains, groups):
            accs[r % chains] = op(accs[r % chains], x[r * SUBLANES:(r + 1) * SUBLANES])
        while len(accs) > 1:
            accs = [op(accs[a], accs[a + len(accs) // 2]) for a in range(len(accs) // 2)]
        return accs[0]

    eye = jnp.where(lax.broadcasted_iota(I32, (LANES, LANES), 0) == lax.broadcasted_iota(I32, (LANES, LANES), 1),
                    1.0, 0.0).astype(BF16)
    lane = lax.broadcasted_iota(I32, (tq, LANES), 1)
    low = lane < HEAD_DIM
    qb = (q_ref[...] * (HEAD_DIM ** -0.5 * LOG2E)).astype(BF16)
    qib = qi_ref[...].astype(BF16)
    zero_b = jnp.zeros((tq, LANES), BF16)
    for src, base in ((qib, 0), (qb, H_IDX)):
        for h in range(H_ATT):
            win = src[:, (h // 2) * LANES:(h // 2 + 1) * LANES]
            win = jnp.where(low if h % 2 == 0 else jnp.logical_not(low), win, zero_b)
            qt_ref[base + h] = _dot_nt(eye, win).astype(BF16)
    sh, smm, sl_ = _split3(sm_ref[...])
    sm_t = _dot_nt(eye, sh) + _dot_nt(eye, smm) + _dot_nt(eye, sl_)
    wrow = [sm_t[SM_WI + h:SM_WI + h + 1, :] * ((D_IDX ** -0.5) * (H_IDX ** -0.5)) for h in range(H_IDX)]

    qpos = i * tq + lax.broadcasted_iota(I32, (1, tq), 1)
    cend = (lax.shift_right_logical(qpos, int(math.log2(CHUNK))) + 1) * CHUNK
    krow = lax.broadcasted_iota(I32, (LANES, tq), 0)

    def p1(j, carry):
        vmax, vmin = carry
        ki2 = ki_ref[0, j]
        for h in range(H_IDX):
            t = jnp.maximum(_dot(ki2, qt_ref[h]), 0.0) * wrow[h]
            if h == 0:
                zb_ref[0] = t
            elif h < H_IDX - 1:
                zb_ref[0] += t
            else:
                for sl in range(nsl):
                    rows = slice(sl * LANES, (sl + 1) * LANES)
                    sc = zb_ref[0, rows, :] + t[rows]
                    sc = jnp.where(sc == 0.0, 0.0, sc)
                    adm = (j * nsl + sl) * LANES + krow < cend
                    sc_ref[j * nsl + sl] = jnp.where(adm, _mono_key(sc), INT_MIN)
                    vmax = jnp.maximum(vmax, fold(jnp.where(adm, sc, -F32_BIG), jnp.maximum))
                    vmin = jnp.minimum(vmin, fold(jnp.where(adm, sc, F32_BIG), jnp.minimum))
        return vmax, vmin

    vmax, vmin = lax.fori_loop(0, nkt, p1, (jnp.full((SUBLANES, tq), -F32_BIG, F32),
                                            jnp.full((SUBLANES, tq), F32_BIG, F32)))

    def count(pred):
        def body(j, acc):
            for sl in range(nsl):
                g = j * nsl + sl
                acc = acc + fold(jnp.where(pred(sc_ref[g], g), 1.0, 0.0), jnp.add)
            return acc
        acc = lax.fori_loop(0, nkt, body, jnp.zeros((SUBLANES, tq), F32))
        return jnp.sum(acc, axis=0, keepdims=True)

    kf = float(topk)

    def active_of(lo, hi, clo):
        return (clo > kf) & (hi - 1 > lo)

    def search_cond(st):
        it, lo, hi, clo, chi, side, boost = st
        return (it < SEARCH_MAX_STEPS) & (jnp.max(jnp.where(active_of(lo, hi, clo), 1.0, 0.0)) > 0.0)

    def search_step(st):
        it, lo, hi, clo, chi, side, boost = st
        active = active_of(lo, hi, clo)
        v_lo = _key_value(lo)
        v_hi = _key_value(hi)
        frac = (jnp.log(clo) - math.log(kf)) / (jnp.log(clo) - jnp.log(jnp.maximum(chi, 0.5)))
        frac = jnp.where(side > 0, jnp.minimum(frac * boost, 0.5),
                         jnp.where(side < 0, 1.0 - jnp.minimum((1.0 - frac) * boost, 0.5), frac))
        t_int = _mono_key(v_lo + (v_hi - v_lo) * frac)
        t_mid = (lo >> 1) + (hi >> 1) + (lo & hi & 1)
        trial = jnp.where(lax.rem(it, MID_PERIOD) == MID_PERIOD - 1, t_mid, t_int)
        trial = jnp.where(it == 0, 0, jnp.where((it == 1) & (lo == 0), 1, trial))
        trial = jnp.minimum(jnp.maximum(trial, lo + 1), hi - 1)
        c = count(lambda t, g: t >= trial)
        up = active & (c >= kf)
        dn = active & (c < kf)
        now = jnp.where(c >= kf, 1, -1)
        boost = jnp.where(active, jnp.where(now == side, jnp.minimum(boost * 2.0, BOOST_MAX), 1.0), boost)
        side = jnp.where(active, now, side)
        return (it + 1, jnp.where(up, trial, lo), jnp.where(dn, trial, hi),
                jnp.where(up, c, clo), jnp.where(dn, c, chi), side, boost)

    lo0 = _mono_key(jnp.min(vmin, axis=0, keepdims=True))
    hi0 = _mono_key(jnp.max(vmax, axis=0, keepdims=True)) + 1
    state = (jnp.int32(0), lo0, hi0, cend.astype(F32), jnp.zeros((1, tq), F32),
             jnp.zeros((1, tq), I32), jnp.ones((1, tq), F32))
    state = lax.fori_loop(0, SEARCH_FIXED_STEPS, lambda _, st: search_step(st), state)
    _, thr, _, n_ge, n_gt, _, _ = lax.while_loop(search_cond, search_step, state)
    take = kf - n_gt

    def tie_cut(_):
        def step(it, cut):
            trial = cut | lax.shift_left(jnp.int32(1), 30 - it)
            c = count(lambda t, g: (t == thr) & (g * LANES + krow < trial))
            return jnp.where(c <= take, trial, cut)
        return lax.fori_loop(0, 31, step, jnp.zeros((1, tq), I32))

    has_tie = jnp.max(jnp.where(n_ge > kf, 1.0, 0.0)) > 0.0
    cut = lax.cond(has_tie, tie_cut, lambda _: jnp.full((1, tq), INT_MAX, I32), 0)

    m_ref[...] = jnp.full(m_ref.shape, NEG_BIG, F32)
    l_ref[...] = jnp.zeros(l_ref.shape, F32)
    acc_ref[...] = jnp.zeros(acc_ref.shape, F32)

    def tile_step(j, near):
        for sl in range(nsl):
            g = j * nsl + sl
            keyt = sc_ref[g]
            sel = (keyt > thr) | ((keyt == thr) & (g * LANES + krow < cut))
            nm_ref[sl * LANES:(sl + 1) * LANES, :] = jnp.where(sel, 0.0, NEG_BIG)

        def logits(h):
            mx = None
            for sl in range(nsl):
                rows = slice(sl * LANES, (sl + 1) * LANES)
                z = _dot(k_ref[0, j, rows, (h // 2) * LANES:(h // 2 + 1) * LANES], qt_ref[H_IDX + h]) + nm_ref[rows, :]
                if near:
                    z = z + jnp.concatenate(
                        [bt_ref[jnp.clip((i * nql + hf) - (j * nsl + sl), 0, N_BIAS_NEAR - 1), h]
                         for hf in range(nql)], axis=1)
                zb_ref[h % 2, rows, :] = z
                cm = fold(z, jnp.maximum)
                mx = cm if mx is None else jnp.maximum(mx, cm)
            return mx

        def accumulate(h, mx):
            shift = 0.0 if near else cf_ref[h]
            m_old = m_ref[h]
            m_new = jnp.maximum(m_old, jnp.max(mx, axis=0, keepdims=True) + shift)
            alpha = jnp.exp2(m_old - m_new)
            msub = m_new - shift
            lsum, pv = None, None
            for c in range(kt // PV_ROWS):
                rows = slice(c * PV_ROWS, (c + 1) * PV_ROWS)
                p = jnp.exp2(zb_ref[h % 2, rows, :] - msub)
                ls = fold(p, jnp.add)
                pc = _dot(vt_ref[0, j, (h // 2) * LANES:(h // 2 + 1) * LANES, rows], p.astype(BF16))
                lsum = ls if lsum is None else lsum + ls
                pv = pc if pv is None else pv + pc
            l_ref[h] = alpha * l_ref[h] + lsum
            acc_ref[h] = alpha * acc_ref[h] + pv
            m_ref[h] = m_new

        mx_next = logits(0)
        for h in range(H_ATT):
            mx_cur = mx_next
            if h + 1 < H_ATT:
                mx_next = logits(h + 1)
            accumulate(h, mx_cur)

    n_far = jnp.minimum(lax.div(jnp.maximum(nql * i - (N_BIAS_NEAR - 1) - (nsl - 1) + nsl, 0), nsl), nkt)

    def p3_far(j, carry):
        tile_step(j, False)
        return carry

    def p3_near(j, carry):
        tile_step(j, True)
        return carry

    lax.fori_loop(0, n_far, p3_far, 0)
    lax.fori_loop(n_far, nkt, p3_near, 0)

    eye_q = jnp.where(lax.broadcasted_iota(I32, (tq, tq), 0) == lax.broadcasted_iota(I32, (tq, tq), 1),
                      1.0, 0.0).astype(BF16)
    rowd = lax.broadcasted_iota(I32, (LANES, tq), 0)
    for p in range(H_ATT // 2):
        outs = []
        for hsub in range(2):
            h = 2 * p + hsub
            outs.append(acc_ref[h] / jnp.sum(l_ref[h], axis=0, keepdims=True))
        y_t = jnp.where(rowd < HEAD_DIM, outs[0], outs[1]).astype(BF16)
        y_ref[:, p * LANES:(p + 1) * LANES] = _dot_nt(eye_q, y_t).astype(y_ref.dtype)


def _attn_prompt(u, small, k_tiles, vt_tiles, ki_tiles, btab, cfar, *, bsz, t, tq, kt):
    assert t % kt == 0 and t % tq == 0 and tq % LANES == 0 and kt % LANES == 0
    nq = t // tq
    nk = t // kt
    row = lambda bi, qi: bi * nq + qi
    once = pl.Buffered(1)
    return pl.pallas_call(
        functools.partial(_attn_prompt_kernel, tq=tq, kt=kt, topk=min(TOPK, t // 4)),
        out_shape=jax.ShapeDtypeStruct((bsz * t, D_ATT), BF16),
        grid=(bsz, nq),
        in_specs=[pl.BlockSpec((tq, D_ATT), lambda bi, qi: (row(bi, qi), COL_Q // D_ATT)),
                  pl.BlockSpec((tq, D_ATT), lambda bi, qi: (row(bi, qi), COL_QI // D_ATT)),
                  pl.BlockSpec((tq, LANES), lambda bi, qi: (row(bi, qi), 0)),
                  pl.BlockSpec((1, nk, kt, D_ATT), lambda bi, qi: (bi, 0, 0, 0), pipeline_mode=once),
                  pl.BlockSpec((1, nk, D_ATT, kt), lambda bi, qi: (bi, 0, 0, 0), pipeline_mode=once),
                  pl.BlockSpec((1, nk, kt, LANES), lambda bi, qi: (bi, 0, 0, 0), pipeline_mode=once),
                  pl.BlockSpec((N_BIAS_NEAR, H_ATT, LANES, LANES), lambda bi, qi: (0, 0, 0, 0), pipeline_mode=once),
                  pl.BlockSpec((H_ATT, 1, tq), lambda bi, qi: (0, 0, 0), pipeline_mode=once)],
        out_specs=pl.BlockSpec((tq, D_ATT), lambda bi, qi: (row(bi, qi), 0)),
        scratch_shapes=[pltpu.VMEM((t // LANES, LANES, tq), I32),
                        pltpu.VMEM((H_IDX + H_ATT, LANES, tq), BF16),
                        pltpu.VMEM((2, kt, tq), F32),
                        pltpu.VMEM((kt, tq), F32),
                        pltpu.VMEM((H_ATT, 1, tq), F32),
                        pltpu.VMEM((H_ATT, SUBLANES, tq), F32),
                        pltpu.VMEM((H_ATT, LANES, tq), F32)],
        compiler_params=pltpu.CompilerParams(dimension_semantics=("arbitrary", "arbitrary"),
                                             vmem_limit_bytes=VMEM_LIMIT),
        name="sparse_attn_prompt",
    )(u, u, small, k_tiles, vt_tiles, ki_tiles, btab, cfar)


def _attn_sample_kernel(q_ref, qi_ref, sm_ref, kt_ref, v_ref, ki_ref, bt_ref, y_ref, sc_ref, m_ref, l_ref, acc_ref,
                        *, tq, kt, n_keys, topk):
    lane = lax.broadcasted_iota(I32, (tq, LANES), 1)

    def adm_fn(j, s):
        return (j * kt + s * LANES + lane) < n_keys

    def bias_fn(j, s, h):
        return bt_ref[0, h, :, s * LANES:(s + 1) * LANES]

    _attn_body(tq=tq, kt=kt, nkt=1, topk=topk, qf=q_ref[...], qif=qi_ref[...], sm=sm_ref[...],
               kt_tile=lambda j, p: kt_ref[0, p * LANES:(p + 1) * LANES, :],
               v_tile=lambda j, p: v_ref[0, :, p * LANES:(p + 1) * LANES],
               ki_tile=lambda j: ki_ref[0],
               adm_fn=adm_fn, bias_fn=bias_fn,
               sc_ref=sc_ref, m_ref=m_ref, l_ref=l_ref, acc_ref=acc_ref, y_ref=y_ref)


def _attn_sample(u, small, kt_all, v_all, ki_all, btab, *, bsz, tq, kt, n_keys):
    return pl.pallas_call(
        functools.partial(_attn_sample_kernel, tq=tq, kt=kt, n_keys=n_keys, topk=min(TOPK, n_keys // 4)),
        out_shape=jax.ShapeDtypeStruct((bsz * tq, D_ATT), BF16),
        grid=(bsz,),
        in_specs=[pl.BlockSpec((tq, D_ATT), lambda bi: (bi, COL_Q // D_ATT)),
                  pl.BlockSpec((tq, D_ATT), lambda bi: (bi, COL_QI // D_ATT)),
                  pl.BlockSpec((tq, LANES), lambda bi: (bi, 0)),
                  pl.BlockSpec((1, D_ATT, kt), lambda bi: (bi, 0, 0)),
                  pl.BlockSpec((1, kt, D_ATT), lambda bi: (bi, 0, 0)),
                  pl.BlockSpec((1, LANES, kt), lambda bi: (bi, 0, 0)),
                  pl.BlockSpec((1, H_ATT, tq, kt), lambda bi: (0, 0, 0, 0))],
        out_specs=pl.BlockSpec((tq, D_ATT), lambda bi: (bi, 0)),
        scratch_shapes=[pltpu.VMEM((1, tq, kt), I32),
                        pltpu.VMEM((H_ATT, tq, LANES), F32),
                        pltpu.VMEM((H_ATT, tq, LANES), F32),
                        pltpu.VMEM((H_ATT, tq, LANES), F32)],
        compiler_params=pltpu.CompilerParams(dimension_semantics=("arbitrary",),
                                             vmem_limit_bytes=VMEM_LIMIT),
        name="sparse_attn_sample",
    )(u, u, small, kt_all, v_all, ki_all, btab)


def _out_proj_kernel(x_ref, ys_ref, yc_ref, ya_ref, w_ref, g_ref, o_ref):
    acc = _dot(ys_ref[...], w_ref[0:D_SSM, :])
    acc = acc + _dot(yc_ref[...], w_ref[D_SSM:D_SSM + D_CONV, :])
    acc = acc + _dot(ya_ref[...], w_ref[D_SSM + D_CONV:D_MODEL, :])
    o_ref[...] = x_ref[...] + _rms(acc, g_ref[...])


def _out_proj(x, ys, yc, ya, w, g, *, tm):
    m = x.shape[0]
    return pl.pallas_call(
        _out_proj_kernel,
        out_shape=jax.ShapeDtypeStruct((m, D_MODEL), F32),
        grid=(m // tm,),
        in_specs=[pl.BlockSpec((tm, D_MODEL), lambda i: (i, 0)),
                  pl.BlockSpec((tm, D_SSM), lambda i: (i, 0)),
                  pl.BlockSpec((tm, D_CONV), lambda i: (i, 0)),
                  pl.BlockSpec((tm, D_ATT), lambda i: (i, 0)),
                  pl.BlockSpec((D_MODEL, D_MODEL), lambda i: (0, 0)),
                  pl.BlockSpec((1, D_MODEL), lambda i: (0, 0))],
        out_specs=pl.BlockSpec((tm, D_MODEL), lambda i: (i, 0)),
        compiler_params=pltpu.CompilerParams(dimension_semantics=("arbitrary",),
                                             vmem_limit_bytes=VMEM_LIMIT),
        name="out_proj",
    )(x, ys, yc, ya, w, g)


FFN_HALO = 8


def _ffn_kernel(x_ref, gpre_ref, wg_ref, wu_ref, wd_ref, cw_ref, cb_ref, gpost_ref, p1_ref, p2_ref,
                o_ref, aux_ref, h_ref, buf_ref, tail_ref, *, tm, tps, seq_len, chained):
    i = pl.program_id(0)
    j = pl.program_id(1)

    @pl.when(j == 0)
    def _():
        h_ref[...] = _rms(x_ref[...], gpre_ref[...]).astype(BF16)
        o_ref[...] = jnp.zeros(o_ref.shape, F32)

    h = h_ref[...]
    a_pre = _dot(h, wg_ref[...])
    buf_ref[FFN_HALO:FFN_HALO + tm, :] = a_pre
    if chained:
        seq_start = lax.rem(i, tps) == 0
        buf_ref[0:FFN_HALO, :] = jnp.where(seq_start, p1_ref[0], tail_ref[j])
        prev1 = buf_ref[FFN_HALO - 1:FFN_HALO - 1 + tm, :]
        prev2 = buf_ref[FFN_HALO - 2:FFN_HALO - 2 + tm, :]
        last = a_pre[tm - FFN_HALO:tm, :]
        tail_ref[j] = last
        aux_ref[0] = last
    else:
        buf_ref[0:FFN_HALO, :] = jnp.zeros((FFN_HALO, a_pre.shape[1]), F32)
        tpos = lax.rem(lax.broadcasted_iota(I32, a_pre.shape, 0), seq_len)
        prev1 = jnp.where(tpos >= 1, buf_ref[FFN_HALO - 1:FFN_HALO - 1 + tm, :], p1_ref[...])
        prev2 = jnp.where(tpos >= 2, buf_ref[FFN_HALO - 2:FFN_HALO - 2 + tm, :], p2_ref[...])
        aux_ref[...] = a_pre
    a = cw_ref[0:1, :] * prev2 + cw_ref[1:2, :] * prev1 + cw_ref[2:3, :] * a_pre + cb_ref[...]
    f = (_silu(a) * _dot(h, wu_ref[...])).astype(BF16)
    o_ref[...] += _dot(f, wd_ref[...])

    @pl.when(j == pl.num_programs(1) - 1)
    def _():
        o_ref[...] = x_ref[...] + _rms(o_ref[...], gpost_ref[...])


def _ffn(x, gpre, wg, wu, wd, cw, cb, gpost, p1, p2, *, tm, tf, seq_len, chained):
    m = x.shape[0]
    nf = D_FF // tf
    tps = max(seq_len // tm, 1)
    if chained:
        nseq = m // seq_len
        p_specs = [pl.BlockSpec((1, FFN_HALO, tf), lambda i, j: (i // tps, 0, j)),
                   pl.BlockSpec((1, FFN_HALO, tf), lambda i, j: (i // tps, 0, j))]
        aux_shape = jax.ShapeDtypeStruct((m // tm, FFN_HALO, D_FF), F32)
        aux_spec = pl.BlockSpec((1, FFN_HALO, tf), lambda i, j: (i, 0, j))
    else:
        p_specs = [pl.BlockSpec((tm, tf), lambda i, j: (i, j)), pl.BlockSpec((tm, tf), lambda i, j: (i, j))]
        aux_shape = jax.ShapeDtypeStruct((m, D_FF), F32)
        aux_spec = pl.BlockSpec((tm, tf), lambda i, j: (i, j))
    return pl.pallas_call(
        functools.partial(_ffn_kernel, tm=tm, tps=tps, seq_len=seq_len, chained=chained),
        out_shape=(jax.ShapeDtypeStruct((m, D_MODEL), F32), aux_shape),
        grid=(m // tm, nf),
        in_specs=[pl.BlockSpec((tm, D_MODEL), lambda i, j: (i, 0), pipeline_mode=pl.Buffered(1)),
                  pl.BlockSpec((1, D_MODEL), lambda i, j: (0, 0)),
                  pl.BlockSpec((D_MODEL, tf), lambda i, j: (0, j)),
                  pl.BlockSpec((D_MODEL, tf), lambda i, j: (0, j)),
                  pl.BlockSpec((tf, D_MODEL), lambda i, j: (j, 0)),
                  pl.BlockSpec((FFN_CONV, tf), lambda i, j: (0, j)),
                  pl.BlockSpec((1, tf), lambda i, j: (0, j)),
                  pl.BlockSpec((1, D_MODEL), lambda i, j: (0, 0))] + p_specs,
        out_specs=(pl.BlockSpec((tm, D_MODEL), lambda i, j: (i, 0)), aux_spec),
        scratch_shapes=[pltpu.VMEM((tm, D_MODEL), BF16),
                        pltpu.VMEM((FFN_HALO + tm, tf), F32),
                        pltpu.VMEM((nf, FFN_HALO, tf), F32)],
        compiler_params=pltpu.CompilerParams(dimension_semantics=("arbitrary", "arbitrary"),
                                             vmem_limit_bytes=VMEM_LIMIT),
        name="conv_ffn",
    )(x, gpre, wg, wu, wd, cw, cb, gpost, p1, p2)


def _prep_layer_weights(w):
    w_in = w["w_in"]
    o_dt = D_SSM + D_XBC
    o_glu = o_dt + H_SSM
    o_ki = o_glu + 2 * D_CONV + 4 * D_ATT
    o_wi = o_ki + D_IDX
    o_q = o_glu + 2 * D_CONV
    o_k = o_q + D_ATT
    o_qi = o_k + 2 * D_ATT
    pad = jnp.zeros((D_MODEL, LANES - H_SSM - H_IDX - D_IDX), w_in.dtype)
    w_r = jnp.concatenate([w_in[:, :o_dt], w_in[:, o_glu:o_k], w_in[:, o_qi:o_ki],
                           w_in[:, o_k:o_qi],
                           w_in[:, o_dt:o_glu], w_in[:, o_wi:o_wi + H_IDX], pad, w_in[:, o_ki:o_wi]],
                          axis=1).astype(BF16)
    assert w_r.shape[1] == D_PROJ and o_qi + H_IDX * D_IDX == o_ki
    row = lambda v: v.reshape(1, -1).astype(F32)
    padl = lambda v: jnp.pad(v.astype(F32), (0, LANES - v.shape[0])).reshape(1, LANES)
    return dict(
        w_in=w_r, g_mix_pre=row(w["g_mix_pre"]),
        ssm_conv_w=w["ssm_conv_w"].astype(F32), ssm_conv_b=row(w["ssm_conv_b"]),
        dt_bias=padl(w["ssm_dt_bias"]), a_log=padl(w["ssm_a_log"]),
        d_x=row(jnp.repeat(w["ssm_d"], SSM_HEAD_DIM)), ssm_norm_g=row(w["ssm_norm_g"]),
        cconv_w=w["cconv_w"].astype(F32), cconv_b=row(w["cconv_b"]),
        cconv_ln_g=row(w["cconv_ln_g"]), cconv_ln_b=row(w["cconv_ln_b"]),
        w_out=w["w_out"].astype(BF16), g_mix_post=row(w["g_mix_post"]), g_ffn_pre=row(w["g_ffn_pre"]),
        ffn_w_gate=w["ffn_w_gate"].astype(BF16), ffn_w_up=w["ffn_w_up"].astype(BF16),
        ffn_w_down=w["ffn_w_down"].astype(BF16), ffn_conv_w=w["ffn_conv_w"].astype(F32),
        ffn_conv_b=row(w["ffn_conv_b"]), g_ffn_post=row(w["g_ffn_post"]))


def _expand_matrix():
    e = np.zeros((LANES, D_SSM), np.float32)
    for h in range(H_SSM):
        e[h, h * SSM_HEAD_DIM:(h + 1) * SSM_HEAD_DIM] = 1.0
    return jnp.asarray(e, BF16)


def _front_pad(state, halo):
    return jnp.pad(state.astype(F32), ((0, 0), (halo - state.shape[1], 0), (0, 0)))


def _state_t(h):
    b = h.shape[0]
    return jnp.transpose(h.astype(F32), (0, 3, 1, 2)).reshape(b, SSM_STATE, D_SSM)


def _state_from_t(ht):
    b = ht.shape[0]
    return jnp.transpose(ht.reshape(b, SSM_STATE, H_SSM, SSM_HEAD_DIM), (0, 2, 3, 1))


def _mixer_common(x, lw, emat, ssm_conv_prev, ssm_h0, cconv_prev, *, bsz, t, tm_proj, kt, ssd_l, ssd_nv, cc_tm):
    u, k, v, small, *attn_ops = _in_proj(x, lw["g_mix_pre"], lw["w_in"], tm=tm_proj, kt=kt)
    y_ssm, ht, ctail = _ssd(u, small, _front_pad(ssm_conv_prev, SSD_HALO), _state_t(ssm_h0), lw["ssm_conv_w"],
                            lw["ssm_conv_b"], lw["dt_bias"], lw["a_log"], lw["d_x"], lw["ssm_norm_g"], emat,
                            bsz=bsz, t=t, L=ssd_l, nv=ssd_nv)
    y_conv, cctail = _cconv(u, _front_pad(cconv_prev, CC_HALO), lw["cconv_w"], lw["cconv_b"], lw["cconv_ln_g"],
                            lw["cconv_ln_b"], bsz=bsz, t=t, tm=cc_tm)
    ki = small[:, SM_KI:SM_KI + D_IDX]
    states = dict(k=k.reshape(bsz, t, H_ATT, HEAD_DIM), v=v.reshape(bsz, t, H_ATT, HEAD_DIM),
                  ki=ki.reshape(bsz, t, D_IDX), h=_state_from_t(ht),
                  ssm_conv=ctail[:, SSD_HALO - (SSM_CONV - 1):], cconv=cctail[:, CC_HALO - (CONV_WIDTH - 1):])
    return u, small, y_ssm, y_conv, k, v, ki, attn_ops, states


def _layer_prompt(x, lw, emat, btab, *, bsz, t, cfg):
    zeros = lambda *s: jnp.zeros(s, F32)
    kt = cfg["kt"]
    nk = t // kt
    u, small, y_ssm, y_conv, k, v, ki, (kb, vt, ki2), st = _mixer_common(
        x, lw, emat, zeros(bsz, SSM_CONV - 1, D_XBC), zeros(bsz, H_SSM, SSM_HEAD_DIM, SSM_STATE),
        zeros(bsz, CONV_WIDTH - 1, D_CONV), bsz=bsz, t=t, tm_proj=cfg["tm_proj"], kt=kt,
        ssd_l=cfg["ssd_l"], ssd_nv=cfg["ssd_l"], cc_tm=cfg["cc_tm"])
    tq = cfg["tq"]
    cfar = jnp.tile(btab[N_BIAS_NEAR - 1, :, 0:1, :], (1, 1, tq // LANES))
    y_att = _attn_prompt(u, small, kb.reshape(bsz, nk, kt, D_ATT), vt.reshape(bsz, nk, D_ATT, kt),
                         ki2.reshape(bsz, nk, kt, LANES), btab, cfar, bsz=bsz, t=t, tq=tq, kt=kt)
    x1 = _out_proj(x, y_ssm, y_conv, y_att, lw["w_out"], lw["g_mix_post"], tm=cfg["tm_out"])
    prev = zeros(bsz, FFN_HALO, D_FF)
    x2, ftail = _ffn(x1, lw["g_ffn_pre"], lw["ffn_w_gate"], lw["ffn_w_up"], lw["ffn_w_down"], lw["ffn_conv_w"],
                     lw["ffn_conv_b"], lw["g_ffn_post"], prev, prev, tm=cfg["tm_ffn"], tf=cfg["tf"], seq_len=t,
                     chained=True)
    ftail = ftail.reshape(bsz, t // cfg["tm_ffn"], FFN_HALO, D_FF)[:, -1]
    st["ffn_conv"] = ftail[:, FFN_HALO - (FFN_CONV - 1):]
    return x2, st


def _layer_sample(x, lw, emat, btab, past_k, past_v, past_ki, ssm_conv_prev, ssm_h0, cconv_prev, fconv_prev,
                  *, bsz, t, cfg):
    m = bsz * t
    u, small, y_ssm, y_conv, k, v, ki, _, st = _mixer_common(
        x, lw, emat, ssm_conv_prev, ssm_h0, cconv_prev, bsz=bsz, t=t, tm_proj=m, kt=None,
        ssd_l=LANES, ssd_nv=t, cc_tm=t)
    past = past_k.shape[1]
    n_keys = past + t
    ktp = cfg["kt_sample"]
    padk = lambda a: jnp.pad(a, ((0, 0), (0, ktp - n_keys), (0, 0)))
    k_all = padk(jnp.concatenate([past_k.reshape(bsz, past, D_ATT), k.reshape(bsz, t, D_ATT)], axis=1).astype(BF16))
    v_all = padk(jnp.concatenate([past_v.reshape(bsz, past, D_ATT), v.reshape(bsz, t, D_ATT)], axis=1).astype(BF16))
    ki_all = padk(jnp.concatenate([past_ki, ki.reshape(bsz, t, D_IDX)], axis=1).astype(BF16))
    kt_all = jnp.transpose(k_all, (0, 2, 1))
    kit = jnp.transpose(ki_all, (0, 2, 1))
    y_att = _attn_sample(u, small, kt_all, v_all, jnp.concatenate([kit, kit], axis=1), btab, bsz=bsz, tq=t, kt=ktp,
                         n_keys=n_keys)
    x1 = _out_proj(x, y_ssm, y_conv, y_att, lw["w_out"], lw["g_mix_post"], tm=m)
    fprev = fconv_prev.astype(F32)
    zrow = jnp.zeros((bsz, t - 1, D_FF), F32)
    p1 = jnp.concatenate([fprev[:, 1:2], zrow], axis=1).reshape(m, D_FF)
    p2 = jnp.concatenate([fprev, zrow[:, 1:]], axis=1).reshape(m, D_FF)
    x2, a_pre = _ffn(x1, lw["g_ffn_pre"], lw["ffn_w_gate"], lw["ffn_w_up"], lw["ffn_w_down"], lw["ffn_conv_w"],
                     lw["ffn_conv_b"], lw["g_ffn_post"], p1, p2, tm=m, tf=cfg["tf"], seq_len=t, chained=False)
    st["ffn_conv"] = a_pre.reshape(bsz, t, D_FF)[:, t - (FFN_CONV - 1):]
    return x2, st


_STATE_ORDER = ("k", "v", "ki", "h", "ssm_conv", "cconv", "ffn_conv")


def _prompt_cfg(t):
    big = t >= 4096
    return dict(tm_proj=1024 if big else 512, ssd_l=256, cc_tm=256, kt=512, tq=256,
                tm_out=512 if big else 256, tm_ffn=1024 if big else 256, tf=512)


def _forward(x_prompt, x_sample, cache_k, cache_v, cache_kidx, state_ssm, state_ssm_conv, state_cconv,
             state_ffn_conv, rel_bias, weights):
    bp, tp, _ = x_prompt.shape
    bs, ts, _ = x_sample.shape
    depth = weights["w_in"].shape[0]
    past = cache_k.shape[2]
    emat = _expand_matrix()
    cfg_p = _prompt_cfg(tp)
    kt_sample = -(-(past + ts) // LANES) * LANES
    cfg_s = dict(tf=512, kt_sample=kt_sample)
    rb = rel_bias.astype(F32)
    btab_p = _bias_table(rb, nd=N_BIAS_NEAR, rows=LANES, cols=LANES, off0=0, step=LANES, key_axis=0, scale=LOG2E)
    btab_s = _bias_table(rb, nd=1, rows=ts, cols=kt_sample, off0=-past, step=0)
    xp = x_prompt.reshape(bp * tp, D_MODEL)
    xs = x_sample.reshape(bs * ts, D_MODEL)
    p_states = {n: [] for n in _STATE_ORDER}
    s_states = {n: [] for n in _STATE_ORDER}
    for l in range(depth):
        lw = _prep_layer_weights({n: w[l] for n, w in weights.items()})
        xp, st_p = _layer_prompt(xp, lw, emat, btab_p, bsz=bp, t=tp, cfg=cfg_p)
        xs, st_s = _layer_sample(xs, lw, emat, btab_s, cache_k[l], cache_v[l], cache_kidx[l], state_ssm_conv[l],
                                 state_ssm[l], state_cconv[l], state_ffn_conv[l], bsz=bs, t=ts, cfg=cfg_s)
        for n in _STATE_ORDER:
            p_states[n].append(st_p[n])
            s_states[n].append(st_s[n])
    outs = [xp.reshape(bp, tp, D_MODEL), xs.reshape(bs, ts, D_MODEL)]
    outs += [jnp.stack(p_states[n]) for n in _STATE_ORDER]
    outs += [jnp.stack(s_states[n]) for n in _STATE_ORDER]
    return tuple(outs)


def kernel(x_prompt, x_sample, cache_k, cache_v, cache_kidx, state_ssm, state_ssm_conv, state_cconv, state_ffn_conv, rel_bias, g_mix_pre, w_in, ssm_conv_w, ssm_conv_b, ssm_dt_bias, ssm_a_log, ssm_d, ssm_norm_g, cconv_w, cconv_b, cconv_ln_g, cconv_ln_b, w_out, g_mix_post, g_ffn_pre, ffn_w_gate, ffn_w_up, ffn_conv_w, ffn_conv_b, ffn_w_down, g_ffn_post):
    weights = dict(g_mix_pre=g_mix_pre, w_in=w_in, ssm_conv_w=ssm_conv_w, ssm_conv_b=ssm_conv_b,
                   ssm_dt_bias=ssm_dt_bias, ssm_a_log=ssm_a_log, ssm_d=ssm_d, ssm_norm_g=ssm_norm_g,
                   cconv_w=cconv_w, cconv_b=cconv_b, cconv_ln_g=cconv_ln_g, cconv_ln_b=cconv_ln_b, w_out=w_out,
                   g_mix_post=g_mix_post, g_ffn_pre=g_ffn_pre, ffn_w_gate=ffn_w_gate, ffn_w_up=ffn_w_up,
                   ffn_conv_w=ffn_conv_w, ffn_conv_b=ffn_conv_b, ffn_w_down=ffn_w_down, g_ffn_post=g_ffn_post)
    return _forward(x_prompt, x_sample, cache_k, cache_v, cache_kidx, state_ssm, state_ssm_conv, state_cconv,
                    state_ffn_conv, rel_bias, weights)
```

```python
import functools
import math

import numpy as np
import jax
import jax.numpy as jnp
from jax import lax
from jax.experimental import pallas as pl
from jax.experimental.pallas import tpu as pltpu

F32 = jnp.float32
BF16 = jnp.bfloat16
I32 = jnp.int32

D_MODEL = 2048
D_SSM = 1024
SSM_HEAD_DIM = 64
H_SSM = 16
SSM_GROUPS = 2
SSM_STATE = 128
SSM_CONV = 4
D_XBC = D_SSM + 2 * SSM_GROUPS * SSM_STATE
D_CONV = 512
CONV_WIDTH = 31
D_ATT = 512
HEAD_DIM = 64
H_ATT = 8
H_IDX = 8
D_IDX = 64
TOPK = 256
CHUNK = 64
N_BUCKETS = 32
REL_MAX_DIST = 1024
D_FF = 5632
FFN_CONV = 3
EPS = 1e-6

LANES = 128
SUBLANES = 8

COL_Z, COL_XBC, COL_GLU, COL_Q, COL_QI = 0, 1024, 2560, 3584, 4096
D_U = 4608
PROJ_TILE = 1152
TAIL_K, TAIL_V, TAIL_SMALL = 0, 512, 1024
TAIL_TM = 512
D_PROJ = D_U + PROJ_TILE
SM_DT, SM_WI, SM_KI = 0, 16, 64

INT_MIN = -(2 ** 31)
INT_MAX = 2 ** 31 - 1
NEG_BIG = -1e30
VMEM_LIMIT = 56 * 1024 * 1024


def _bucket_thresholds():
    nb = N_BUCKETS // 2
    max_exact = nb // 2
    n = np.arange(0, 4 * REL_MAX_DIST, dtype=np.int64)
    nf = np.maximum(n, 1).astype(np.float32)
    large = max_exact + (np.log(nf / np.float32(max_exact)) / np.float32(math.log(REL_MAX_DIST / max_exact))
                         * np.float32(nb - max_exact)).astype(np.int32)
    large = np.minimum(large, nb - 1)
    bucket = np.where(n < max_exact, n, large)
    steps = np.nonzero(np.diff(bucket))[0] + 1
    assert np.all(np.diff(bucket) >= 0) and np.all(np.diff(bucket) <= 1) and bucket[-1] == nb - 1
    return tuple(int(s) for s in steps)


BUCKET_STEPS = _bucket_thresholds()


def _sigmoid(x):
    return 1.0 / (1.0 + jnp.exp(-x))


def _silu(x):
    return x * _sigmoid(x)


def _split3(x):
    hi = x.astype(BF16)
    r1 = x - hi.astype(F32)
    mid = r1.astype(BF16)
    lo = (r1 - mid.astype(F32)).astype(BF16)
    return hi, mid, lo


def _dot(a, b):
    return jnp.dot(a, b, preferred_element_type=F32)


def _dot_nt(a, b):
    return lax.dot_general(a, b, (((1,), (1,)), ((), ())), preferred_element_type=F32)


def _exact_dot(sel_bf16, x_f32):
    hi, mid, lo = _split3(x_f32)
    return _dot(sel_bf16, hi) + _dot(sel_bf16, mid) + _dot(sel_bf16, lo)


def _exact_dot_r(x_f32, sel_bf16):
    hi, mid, lo = _split3(x_f32)
    return _dot(hi, sel_bf16) + _dot(mid, sel_bf16) + _dot(lo, sel_bf16)


def _rms(x, g):
    ms = jnp.mean(x * x, axis=-1, keepdims=True)
    return x * lax.rsqrt(ms + EPS) * g


def _in_proj_u_kernel(x_ref, g_ref, w_ref, u_ref, h_ref):
    @pl.when(pl.program_id(1) == 0)
    def _():
        h_ref[...] = _rms(x_ref[...], g_ref[...]).astype(BF16)

    u_ref[...] = _dot(h_ref[...], w_ref[...])


def _in_proj_tail_kernel(x_ref, g_ref, w_ref, k_ref, v_ref, sm_ref, *rest, kt):
    y = _dot(_rms(x_ref[...], g_ref[...]).astype(BF16), w_ref[...])
    k = y[:, TAIL_K:TAIL_K + D_ATT]
    v = y[:, TAIL_V:TAIL_V + D_ATT]
    sm = y[:, TAIL_SMALL:TAIL_SMALL + LANES]
    sm_ref[...] = sm
    if kt is None:
        k_ref[...] = k
        v_ref[...] = v
    else:
        for o_ref, val in ((k_ref, k), (v_ref, v)):
            for h in range(H_ATT):
                o_ref[pl.ds(h, val.shape[0], stride=H_ATT), :] = val[:, h * HEAD_DIM:(h + 1) * HEAD_DIM]
    if kt is not None:
        kb_ref, vt_ref, ki2_ref = rest
        kb_ref[...] = k.astype(BF16)
        vb = v.astype(BF16)
        eye = jnp.where(lax.broadcasted_iota(I32, (LANES, LANES), 0)
                        == lax.broadcasted_iota(I32, (LANES, LANES), 1), 1.0, 0.0).astype(BF16)
        for c in range(vb.shape[0] // kt):
            for p in range(D_ATT // LANES):
                blk = vb[c * kt:(c + 1) * kt, p * LANES:(p + 1) * LANES]
                vt_ref[c, p * LANES:(p + 1) * LANES, :] = _dot_nt(eye, blk).astype(BF16)
        kib = sm[:, SM_KI:SM_KI + D_IDX].astype(BF16)
        ki2_ref[...] = jnp.concatenate([kib, kib], axis=1)


def _in_proj(x, g, w, *, tm, kt=None):
    m, d = x.shape
    nj = D_U // PROJ_TILE
    assert m % tm == 0 and w.shape[1] == D_PROJ and (kt is None or tm % kt == 0)
    u = pl.pallas_call(
        _in_proj_u_kernel,
        out_shape=jax.ShapeDtypeStruct((m, D_U), F32),
        grid=(m // tm, nj),
        in_specs=[pl.BlockSpec((tm, d), lambda i, j: (i, 0)),
                  pl.BlockSpec((1, d), lambda i, j: (0, 0)),
                  pl.BlockSpec((d, PROJ_TILE), lambda i, j: (0, j))],
        out_specs=pl.BlockSpec((tm, PROJ_TILE), lambda i, j: (i, j)),
        scratch_shapes=[pltpu.VMEM((tm, d), BF16)],
        compiler_params=pltpu.CompilerParams(dimension_semantics=("arbitrary", "arbitrary"),
                                             vmem_limit_bytes=VMEM_LIMIT),
        name="rms_in_proj",
    )(x, g, w)
    row = lambda i: (i, 0)
    if kt is None:
        kv_shape, kv_spec = jax.ShapeDtypeStruct((m, D_ATT), F32), pl.BlockSpec((tm, D_ATT), row)
    else:
        tm = min(tm, TAIL_TM)
        kv_shape = jax.ShapeDtypeStruct((m * H_ATT, HEAD_DIM), F32)
        kv_spec = pl.BlockSpec((tm * H_ATT, HEAD_DIM), row)
    out_shape = [kv_shape, kv_shape, jax.ShapeDtypeStruct((m, LANES), F32)]
    out_specs = [kv_spec, kv_spec, pl.BlockSpec((tm, LANES), row)]
    if kt is not None:
        out_shape += [jax.ShapeDtypeStruct((m, D_ATT), BF16), jax.ShapeDtypeStruct((m // kt, D_ATT, kt), BF16),
                      jax.ShapeDtypeStruct((m, LANES), BF16)]
        out_specs += [pl.BlockSpec((tm, D_ATT), row), pl.BlockSpec((tm // kt, D_ATT, kt), lambda i: (i, 0, 0)),
                      pl.BlockSpec((tm, LANES), row)]
    tail = pl.pallas_call(
        functools.partial(_in_proj_tail_kernel, kt=kt),
        out_shape=tuple(out_shape),
        grid=(m // tm,),
        in_specs=[pl.BlockSpec((tm, d), lambda i: (i, 0)),
                  pl.BlockSpec((1, d), lambda i: (0, 0)),
                  pl.BlockSpec((d, PROJ_TILE), lambda i: (0, nj))],
        out_specs=tuple(out_specs),
        compiler_params=pltpu.CompilerParams(dimension_semantics=("arbitrary",),
                                             vmem_limit_bytes=VMEM_LIMIT),
        name="rms_in_proj_tail",
    )(x, g, w)
    return (u,) + tuple(tail)


def _bias_table_kernel(rb_ref, o_ref, *, off0, step, key_axis, scale):
    d = pl.program_id(0)
    h = pl.program_id(1)
    rows, cols = o_ref.shape[2], o_ref.shape[3]
    rel = (lax.broadcasted_iota(I32, (rows, cols), key_axis) - lax.broadcasted_iota(I32, (rows, cols), 1 - key_axis)
           + (off0 - d * step))
    n = jnp.abs(rel)
    bucket = jnp.where(rel > 0, N_BUCKETS // 2, 0)
    for s in BUCKET_STEPS:
        bucket = bucket + jnp.where(n >= s, 1, 0)
    acc = jnp.zeros((rows, cols), F32)
    for b in range(N_BUCKETS):
        acc = jnp.where(bucket == b, rb_ref[b, h], acc)
    o_ref[0, 0] = acc * scale


def _bias_table(rel_bias, *, nd, rows, cols, off0, step, key_axis=1, scale=1.0):
    return pl.pallas_call(
        functools.partial(_bias_table_kernel, off0=off0, step=step, key_axis=key_axis, scale=scale),
        out_shape=jax.ShapeDtypeStruct((nd, H_ATT, rows, cols), F32),
        grid=(nd, H_ATT),
        in_specs=[pl.BlockSpec(memory_space=pltpu.SMEM)],
        out_specs=pl.BlockSpec((1, 1, rows, cols), lambda d, h: (d, h, 0, 0)),
        name="bias_table",
    )(rel_bias)


CC_HALO = 32


def _cconv_kernel(val_ref, gate_ref, prev_ref, w_ref, b_ref, lg_ref, lb_ref, y_ref, tail_ref, buf_ref, sh_ref, *, tm):
    @pl.when(pl.program_id(1) == 0)
    def _():
        buf_ref[0:CC_HALO, :] = prev_ref[0]

    buf_ref[CC_HALO:CC_HALO + tm, :] = val_ref[...] * _sigmoid(gate_ref[...])
    span = CC_HALO + tm - SUBLANES
    for b in range(1, SUBLANES):
        sh_ref[b - 1, 0:span, :] = buf_ref[b:b + span, :]
    first = CC_HALO - (CONV_WIDTH - 1)
    acc = jnp.zeros((tm, D_CONV), F32) + b_ref[...]
    for k in range(CONV_WIDTH):
        base, b = (first + k) // SUBLANES * SUBLANES, (first + k) % SUBLANES
        rows = buf_ref[base:base + tm, :] if b == 0 else sh_ref[b - 1, base:base + tm, :]
        acc = acc + w_ref[k:k + 1, :] * rows
    mu = jnp.mean(acc, axis=-1, keepdims=True)
    xc = acc - mu
    var = jnp.mean(xc * xc, axis=-1, keepdims=True)
    y = xc * lax.rsqrt(var + EPS) * lg_ref[...] + lb_ref[...]
    y_ref[...] = _silu(y).astype(y_ref.dtype)
    tail = buf_ref[tm:tm + CC_HALO, :]
    tail_ref[0] = tail
    buf_ref[0:CC_HALO, :] = tail


def _cconv(u, prev, w, b, lg, lb, *, bsz, t, tm):
    nt = t // tm
    row = lambda bi, ti: bi * nt + ti
    return pl.pallas_call(
        functools.partial(_cconv_kernel, tm=tm),
        out_shape=(jax.ShapeDtypeStruct((bsz * t, D_CONV), BF16),
                   jax.ShapeDtypeStruct((bsz, CC_HALO, D_CONV), F32)),
        grid=(bsz, nt),
        in_specs=[pl.BlockSpec((tm, D_CONV), lambda bi, ti: (row(bi, ti), COL_GLU // D_CONV)),
                  pl.BlockSpec((tm, D_CONV), lambda bi, ti: (row(bi, ti), COL_GLU // D_CONV + 1)),
                  pl.BlockSpec((1, CC_HALO, D_CONV), lambda bi, ti: (bi, 0, 0)),
                  pl.BlockSpec((CONV_WIDTH, D_CONV), lambda bi, ti: (0, 0)),
                  pl.BlockSpec((1, D_CONV), lambda bi, ti: (0, 0)),
                  pl.BlockSpec((1, D_CONV), lambda bi, ti: (0, 0)),
                  pl.BlockSpec((1, D_CONV), lambda bi, ti: (0, 0))],
        out_specs=(pl.BlockSpec((tm, D_CONV), lambda bi, ti: (row(bi, ti), 0)),
                   pl.BlockSpec((1, CC_HALO, D_CONV), lambda bi, ti: (bi, 0, 0))),
        scratch_shapes=[pltpu.VMEM((CC_HALO + tm, D_CONV), F32),
                        pltpu.VMEM((SUBLANES - 1, CC_HALO + tm, D_CONV), F32)],
        compiler_params=pltpu.CompilerParams(dimension_semantics=("arbitrary", "arbitrary"),
                                             vmem_limit_bytes=VMEM_LIMIT),
        name="conformer_conv",
    )(u, u, prev, w, b, lg, lb)


SSD_HALO = 8


def _ssd_kernel(z_ref, x0_ref, x1_ref, x2_ref, sm_ref, prevc_ref, h0_ref, cw_ref, cb_ref, dtb_ref, alog_ref,
                dx_ref, ng_ref, e_ref, y_ref, hout_ref, ctail_ref, buf_ref, ht_ref, yb_ref, *, L, nv):
    @pl.when(pl.program_id(1) == 0)
    def _():
        buf_ref[0:SSD_HALO, :] = prevc_ref[0]
        ht_ref[...] = h0_ref[0]

    if nv < L:
        buf_ref[SSD_HALO + nv:SSD_HALO + L, :] = jnp.zeros((L - nv, D_XBC), F32)
    for c, r in enumerate((x0_ref, x1_ref, x2_ref)):
        buf_ref[SSD_HALO:SSD_HALO + nv, c * 512:(c + 1) * 512] = r[...]
    first = SSD_HALO - (SSM_CONV - 1)
    acc = jnp.zeros((L, D_XBC), F32) + cb_ref[...]
    for k in range(SSM_CONV):
        acc = acc + cw_ref[k:k + 1, :] * buf_ref[first + k:first + k + L, :]
    xbc = _silu(acc)
    tail = buf_ref[nv:nv + SSD_HALO, :]
    ctail_ref[0] = tail
    buf_ref[0:SSD_HALO, :] = tail

    xs = xbc[:, :D_SSM]
    lane = lax.broadcasted_iota(I32, (L, LANES), 1)
    rowi = lax.broadcasted_iota(I32, (L, LANES), 0)
    sm = sm_ref[...]
    if nv < L:
        sm = jnp.concatenate([sm, jnp.zeros((L - nv, LANES), F32)], axis=0)
    dtr = sm + dtb_ref[...]
    dt = jnp.maximum(dtr, 0.0) + jnp.log(1.0 + jnp.exp(-jnp.abs(dtr)))
    dt = jnp.where((lane < H_SSM) & (rowi < nv), dt, 0.0)
    a = -jnp.exp(alog_ref[...])
    da = dt * a
    ri = lax.broadcasted_iota(I32, (L, L), 0)
    ci = lax.broadcasted_iota(I32, (L, L), 1)
    causal = ri >= ci
    tril = jnp.where(causal, 1.0, 0.0).astype(BF16)
    cum = _exact_dot(tril, da)
    eye = jnp.where(lax.broadcasted_iota(I32, (LANES, LANES), 0) == lax.broadcasted_iota(I32, (LANES, LANES), 1),
                    1.0, 0.0).astype(BF16)
    ch, cm, cl = _split3(cum)
    cum_t = _dot_nt(eye, ch) + _dot_nt(eye, cm) + _dot_nt(eye, cl)
    e = e_ref[...]
    ecx = _exact_dot_r(jnp.exp(cum), e)
    dtx = _exact_dot_r(dt, e)
    xdt = (xs * dtx).astype(BF16)
    edl = ecx[L - 1:L, :]
    dend_t = jnp.exp(cum_t[:, L - 1:L] - cum_t)
    lane_l = lax.broadcasted_iota(I32, (L, LANES), 1)
    lane_n = lax.broadcasted_iota(I32, (SSM_STATE, LANES), 1)
    hpg = H_SSM // SSM_GROUPS
    for g in range(SSM_GROUPS):
        bg = xbc[:, D_SSM + g * SSM_STATE:D_SSM + (g + 1) * SSM_STATE].astype(BF16)
        cg = xbc[:, D_SSM + (SSM_GROUPS + g) * SSM_STATE:D_SSM + (SSM_GROUPS + g + 1) * SSM_STATE].astype(BF16)
        cbt = _dot_nt(cg, bg)
        bg_t = _dot_nt(eye, bg)
        gcols = slice(g * hpg * SSM_HEAD_DIM, (g + 1) * hpg * SSM_HEAD_DIM)
        yoff = _dot(cg, ht_ref[:, gcols].astype(BF16)) * ecx[:, gcols]
        for p in range(hpg // 2):
            h0 = g * hpg + 2 * p
            pcols = slice(h0 * SSM_HEAD_DIM, (h0 + 2) * SSM_HEAD_DIM)
            xpair = xdt[:, pcols]
            res, st = [], []
            for hh in (h0, h0 + 1):
                seg = cum[:, hh:hh + 1] - cum_t[hh:hh + 1, :]
                dec = jnp.where(causal, jnp.exp(seg), 0.0)
                res.append(_dot((cbt * dec).astype(BF16), xpair))
                st.append(_dot((bg_t * dend_t[hh:hh + 1, :]).astype(BF16), xpair))
            yb_ref[:, pcols] = (jnp.where(lane_l < SSM_HEAD_DIM, res[0], res[1])
                                + yoff[:, 2 * p * SSM_HEAD_DIM:(2 * p + 2) * SSM_HEAD_DIM])
            ht_ref[:, pcols] = (ht_ref[:, pcols] * edl[:, pcols]
                                + jnp.where(lane_n < SSM_HEAD_DIM, st[0], st[1]))
    hout_ref[0] = ht_ref[...]
    y = yb_ref[...] + dx_ref[...] * xs
    z = z_ref[...]
    if nv < L:
        z = jnp.concatenate([z, jnp.zeros((L - nv, D_SSM), F32)], axis=0)
    y = _rms(y * _silu(z), ng_ref[...])
    y_ref[...] = y[:nv].astype(y_ref.dtype)


def _ssd(u, small, prevc, h0t, cw, cb, dtb, alog, dx, ng, emat, *, bsz, t, L, nv):
    nt = t // nv
    row = lambda bi, ti: bi * nt + ti
    c512 = lambda c: (lambda bi, ti: (row(bi, ti), c))
    const2 = lambda bi, ti: (0, 0)
    return pl.pallas_call(
        functools.partial(_ssd_kernel, L=L, nv=nv),
        out_shape=(jax.ShapeDtypeStruct((bsz * t, D_SSM), BF16),
                   jax.ShapeDtypeStruct((bsz, SSM_STATE, D_SSM), F32),
                   jax.ShapeDtypeStruct((bsz, SSD_HALO, D_XBC), F32)),
        grid=(bsz, nt),
        in_specs=[pl.BlockSpec((nv, D_SSM), lambda bi, ti: (row(bi, ti), 0)),
                  pl.BlockSpec((nv, 512), c512(COL_XBC // 512)),
                  pl.BlockSpec((nv, 512), c512(COL_XBC // 512 + 1)),
                  pl.BlockSpec((nv, 512), c512(COL_XBC // 512 + 2)),
                  pl.BlockSpec((nv, LANES), c512(0)),
                  pl.BlockSpec((1, SSD_HALO, D_XBC), lambda bi, ti: (bi, 0, 0)),
                  pl.BlockSpec((1, SSM_STATE, D_SSM), lambda bi, ti: (bi, 0, 0)),
                  pl.BlockSpec((SSM_CONV, D_XBC), const2),
                  pl.BlockSpec((1, D_XBC), const2),
                  pl.BlockSpec((1, LANES), const2),
                  pl.BlockSpec((1, LANES), const2),
                  pl.BlockSpec((1, D_SSM), const2),
                  pl.BlockSpec((1, D_SSM), const2),
                  pl.BlockSpec((LANES, D_SSM), const2)],
        out_specs=(pl.BlockSpec((nv, D_SSM), lambda bi, ti: (row(bi, ti), 0)),
                   pl.BlockSpec((1, SSM_STATE, D_SSM), lambda bi, ti: (bi, 0, 0)),
                   pl.BlockSpec((1, SSD_HALO, D_XBC), lambda bi, ti: (bi, 0, 0))),
        scratch_shapes=[pltpu.VMEM((SSD_HALO + L, D_XBC), F32),
                        pltpu.VMEM((SSM_STATE, D_SSM), F32),
                        pltpu.VMEM((L, D_SSM), F32)],
        compiler_params=pltpu.CompilerParams(dimension_semantics=("arbitrary", "arbitrary"),
                                             vmem_limit_bytes=VMEM_LIMIT),
        name="ssd_mixer",
    )(u, u, u, u, small, prevc, h0t, cw, cb, dtb, alog, dx, ng, emat)


def _flip_negative(b):
    return b ^ ((b >> 31) & INT_MAX)


def _mono_key(x):
    return _flip_negative(lax.bitcast_convert_type(x, I32))


def _key_value(k):
    return lax.bitcast_convert_type(_flip_negative(k), F32)


F32_BIG = 3e38
MID_PERIOD = 12
BOOST_MAX = 2.0 ** 30
SEARCH_FIXED_STEPS = 14
SEARCH_MAX_STEPS = 400


def _attn_body(*, tq, kt, nkt, topk, qf, qif, sm, kt_tile, v_tile, ki_tile, adm_fn, bias_fn,
               sc_ref, m_ref, l_ref, acc_ref, y_ref):
    nsl = kt // LANES
    lane = lax.broadcasted_iota(I32, (tq, LANES), 1)
    low = lane < HEAD_DIM
    qb = (qf * (HEAD_DIM ** -0.5)).astype(BF16)
    qib = qif.astype(BF16)
    wi = sm[:, SM_WI:SM_WI + H_IDX] * ((D_IDX ** -0.5) * (H_IDX ** -0.5))
    zero_b = jnp.zeros((tq, LANES), BF16)

    def head_window(x, h):
        win = x[:, (h // 2) * LANES:(h // 2 + 1) * LANES]
        return jnp.where(low if h % 2 == 0 else jnp.logical_not(low), win, zero_b)

    qim = [head_window(qib, h) for h in range(H_IDX)]
    wcol = [wi[:, h:h + 1] for h in range(H_IDX)]

    def p1(j, carry):
        ki = ki_tile(j)
        acc = jnp.zeros((tq, kt), F32)
        for h in range(H_IDX):
            acc = acc + jnp.maximum(_dot(qim[h], ki), 0.0) * wcol[h]
        for s in range(nsl):
            key = jnp.where(adm_fn(j, s), _mono_key(acc[:, s * LANES:(s + 1) * LANES]), INT_MIN)
            sc_ref[j, :, s * LANES:(s + 1) * LANES] = key
        return carry

    lax.fori_loop(0, nkt, p1, 0)

    def count(pred):
        def body(j, acc):
            tile = sc_ref[j]
            for s in range(nsl):
                acc = acc + jnp.where(pred(tile[:, s * LANES:(s + 1) * LANES], j, s), 1.0, 0.0)
            return acc
        acc = lax.fori_loop(0, nkt, body, jnp.zeros((tq, LANES), F32))
        return jnp.sum(acc, axis=1, keepdims=True)

    def count_ge(cand):
        cb = jnp.broadcast_to(cand, (tq, LANES))
        return count(lambda t, j, s: t >= cb)

    kf = float(topk)
    prefix = jnp.where(count_ge(jnp.zeros((tq, 1), I32)) >= kf, 0, INT_MIN).astype(I32)

    def bit_step(it, prefix):
        trial = prefix | lax.shift_left(jnp.int32(1), 30 - it)
        return jnp.where(count_ge(trial) >= kf, trial, prefix)

    thr = lax.fori_loop(0, 31, bit_step, prefix)
    thr = jnp.maximum(thr, INT_MIN + 1)
    thr_b = jnp.broadcast_to(thr, (tq, LANES))
    n_gt = count(lambda t, j, s: t > thr_b)
    n_eq = count(lambda t, j, s: t == thr_b)
    take = kf - n_gt

    def kpos(j, s):
        return j * kt + s * LANES + lane

    def tie_cut(_):
        def step(it, cut):
            trial = cut | lax.shift_left(jnp.int32(1), 30 - it)
            tb = jnp.broadcast_to(trial, (tq, LANES))
            c = count(lambda t, j, s: (t == thr_b) & (kpos(j, s) < tb))
            return jnp.where(c <= take, trial, cut)
        return lax.fori_loop(0, 31, step, jnp.zeros((tq, 1), I32))

    has_tie = jnp.max(jnp.where(n_gt + n_eq > kf, 1.0, 0.0)) > 0.0
    cut = lax.cond(has_tie, tie_cut, lambda _: jnp.full((tq, 1), INT_MAX, I32), 0)
    cut_b = jnp.broadcast_to(cut, (tq, LANES))

    qm = [head_window(qb, h) for h in range(H_ATT)]
    m_ref[...] = jnp.full(m_ref.shape, NEG_BIG, F32)
    l_ref[...] = jnp.zeros(l_ref.shape, F32)
    acc_ref[...] = jnp.zeros(acc_ref.shape, F32)

    def p3(j, carry):
        keyt = sc_ref[j]
        sel = []
        for s in range(nsl):
            ks = keyt[:, s * LANES:(s + 1) * LANES]
            sel.append((ks > thr_b) | ((ks == thr_b) & (kpos(j, s) < cut_b)))
        for p in range(H_ATT // 2):
            kp = kt_tile(j, p)
            vp = v_tile(j, p)
            for hsub in range(2):
                h = 2 * p + hsub
                s_all = _dot(qm[h], kp)
                parts = [jnp.where(sel[s], s_all[:, s * LANES:(s + 1) * LANES] + bias_fn(j, s, h), NEG_BIG)
                         for s in range(nsl)]
                mx = parts[0]
                for s in range(1, nsl):
                    mx = jnp.maximum(mx, parts[s])
                m_old = m_ref[h]
                m_new = jnp.maximum(m_old, jnp.max(mx, axis=1, keepdims=True))
                alpha = jnp.exp(m_old - m_new)
                pr = [jnp.exp(part - m_new) for part in parts]
                psum = pr[0]
                for s in range(1, nsl):
                    psum = psum + pr[s]
                l_ref[h] = alpha * l_ref[h] + psum
                pb = jnp.concatenate([x.astype(BF16) for x in pr], axis=1)
                acc_ref[h] = alpha * acc_ref[h] + _dot(pb, vp)
                m_ref[h] = m_new
        return carry

    lax.fori_loop(0, nkt, p3, 0)
    for p in range(H_ATT // 2):
        outs = []
        for hsub in range(2):
            h = 2 * p + hsub
            lsum = jnp.sum(l_ref[h], axis=1, keepdims=True)
            outs.append(acc_ref[h] / lsum)
        y_ref[:, p * LANES:(p + 1) * LANES] = jnp.where(low, outs[0], outs[1]).astype(y_ref.dtype)


N_BIAS_NEAR = 7


LOG2E = math.log2(math.e)
PV_ROWS = 256


def _attn_prompt_kernel(q_ref, qi_ref, sm_ref, k_ref, vt_ref, ki_ref, bt_ref, cf_ref, y_ref,
                        sc_ref, qt_ref, zb_ref, nm_ref, m_ref, l_ref, acc_ref, *, tq, kt, topk):
    i = pl.program_id(1)
    nql = tq // LANES
    nsl = kt // LANES
    nkt = lax.div((i + 1) * tq + (kt - 1), kt)

    def fold(x, op, chains=4):
        groups = x.shape[0] // SUBLANES
        accs = [x[a * SUBLANES:(a + 1) * SUBLANES] for a in range(chains)]
        for r in range(chains, groups):
            accs[r % chains] = op(accs[r % chains], x[r * SUBLANES:(r + 1) * SUBLANES])
        while len(accs) > 1:
            accs = [op(accs[a], accs[a + len(accs) // 2]) for a in range(len(accs) // 2)]
        return accs[0]

    eye = jnp.where(lax.broadcasted_iota(I32, (LANES, LANES), 0) == lax.broadcasted_iota(I32, (LANES, LANES), 1),
                    1.0, 0.0).astype(BF16)
    lane = lax.broadcasted_iota(I32, (tq, LANES), 1)
    low = lane < HEAD_DIM
    qb = (q_ref[...] * (HEAD_DIM ** -0.5 * LOG2E)).astype(BF16)
    qib = qi_ref[...].astype(BF16)
    zero_b = jnp.zeros((tq, LANES), BF16)
    for src, base in ((qib, 0), (qb, H_IDX)):
        for h in range(H_ATT):
            win = src[:, (h // 2) * LANES:(h // 2 + 1) * LANES]
            win = jnp.where(low if h % 2 == 0 else jnp.logical_not(low), win, zero_b)
            qt_ref[base + h] = _dot_nt(eye, win).astype(BF16)
    sh, smm, sl_ = _split3(sm_ref[...])
    sm_t = _dot_nt(eye, sh) + _dot_nt(eye, smm) + _dot_nt(eye, sl_)
    wrow = [sm_t[SM_WI + h:SM_WI + h + 1, :] * ((D_IDX ** -0.5) * (H_IDX ** -0.5)) for h in range(H_IDX)]

    qpos = i * tq + lax.broadcasted_iota(I32, (1, tq), 1)
    cend = (lax.shift_right_logical(qpos, int(math.log2(CHUNK))) + 1) * CHUNK
    krow = lax.broadcasted_iota(I32, (LANES, tq), 0)

    def p1(j, carry):
        vmax, vmin = carry
        ki2 = ki_ref[0, j]
        for h in range(H_IDX):
            t = jnp.maximum(_dot(ki2, qt_ref[h]), 0.0) * wrow[h]
            if h == 0:
                zb_ref[0] = t
            elif h < H_IDX - 1:
                zb_ref[0] += t
            else:
                for sl in range(nsl):
                    rows = slice(sl * LANES, (sl + 1) * LANES)
                    sc = zb_ref[0, rows, :] + t[rows]
                    sc = jnp.where(sc == 0.0, 0.0, sc)
                    adm = (j * nsl + sl) * LANES + krow < cend
                    sc_ref[j * nsl + sl] = jnp.where(adm, _mono_key(sc), INT_MIN)
                    vmax = jnp.maximum(vmax, fold(jnp.where(adm, sc, -F32_BIG), jnp.maximum))
                    vmin = jnp.minimum(vmin, fold(jnp.where(adm, sc, F32_BIG), jnp.minimum))
        return vmax, vmin

    vmax, vmin = lax.fori_loop(0, nkt, p1, (jnp.full((SUBLANES, tq), -F32_BIG, F32),
                                            jnp.full((SUBLANES, tq), F32_BIG, F32)))

    def count(pred):
        def body(j, acc):
            for sl in range(nsl):
                g = j * nsl + sl
                acc = acc + fold(jnp.where(pred(sc_ref[g], g), 1.0, 0.0), jnp.add)
            return acc
        acc = lax.fori_loop(0, nkt, body, jnp.zeros((SUBLANES, tq), F32))
        return jnp.sum(acc, axis=0, keepdims=True)

    kf = float(topk)

    def active_of(lo, hi, clo):
        return (clo > kf) & (hi - 1 > lo)

    def search_cond(st):
        it, lo, hi, clo, chi, side, boost = st
        return (it < SEARCH_MAX_STEPS) & (jnp.max(jnp.where(active_of(lo, hi, clo), 1.0, 0.0)) > 0.0)

    def search_step(st):
        it, lo, hi, clo, chi, side, boost = st
        active = active_of(lo, hi, clo)
        v_lo = _key_value(lo)
        v_hi = _key_value(hi)
        frac = (jnp.log(clo) - math.log(kf)) / (jnp.log(clo) - jnp.log(jnp.maximum(chi, 0.5)))
        frac = jnp.where(side > 0, jnp.minimum(frac * boost, 0.5),
                         jnp.where(side < 0, 1.0 - jnp.minimum((1.0 - frac) * boost, 0.5), frac))
        t_int = _mono_key(v_lo + (v_hi - v_lo) * frac)
        t_mid = (lo >> 1) + (hi >> 1) + (lo & hi & 1)
        trial = jnp.where(lax.rem(it, MID_PERIOD) == MID_PERIOD - 1, t_mid, t_int)
        trial = jnp.where(it == 0, 0, jnp.where((it == 1) & (lo == 0), 1, trial))
        trial = jnp.minimum(jnp.maximum(trial, lo + 1), hi - 1)
        c = count(lambda t, g: t >= trial)
        up = active & (c >= kf)
        dn = active & (c < kf)
        now = jnp.where(c >= kf, 1, -1)
        boost = jnp.where(active, jnp.where(now == side, jnp.minimum(boost * 2.0, BOOST_MAX), 1.0), boost)
        side = jnp.where(active, now, side)
        return (it + 1, jnp.where(up, trial, lo), jnp.where(dn, trial, hi),
                jnp.where(up, c, clo), jnp.where(dn, c, chi), side, boost)

    lo0 = _mono_key(jnp.min(vmin, axis=0, keepdims=True))
    hi0 = _mono_key(jnp.max(vmax, axis=0, keepdims=True)) + 1
    state = (jnp.int32(0), lo0, hi0, cend.astype(F32), jnp.zeros((1, tq), F32),
             jnp.zeros((1, tq), I32), jnp.ones((1, tq), F32))
    state = lax.fori_loop(0, SEARCH_FIXED_STEPS, lambda _, st: search_step(st), state)
    _, thr, _, n_ge, n_gt, _, _ = lax.while_loop(search_cond, search_step, state)
    take = kf - n_gt

    def tie_cut(_):
        def step(it, cut):
            trial = cut | lax.shift_left(jnp.int32(1), 30 - it)
            c = count(lambda t, g: (t == thr) & (g * LANES + krow < trial))
            return jnp.where(c <= take, trial, cut)
        return lax.fori_loop(0, 31, step, jnp.zeros((1, tq), I32))

    has_tie = jnp.max(jnp.where(n_ge > kf, 1.0, 0.0)) > 0.0
    cut = lax.cond(has_tie, tie_cut, lambda _: jnp.full((1, tq), INT_MAX, I32), 0)

    m_ref[...] = jnp.full(m_ref.shape, NEG_BIG, F32)
    l_ref[...] = jnp.zeros(l_ref.shape, F32)
    acc_ref[...] = jnp.zeros(acc_ref.shape, F32)

    def tile_step(j, near):
        for sl in range(nsl):
            g = j * nsl + sl
            keyt = sc_ref[g]
            sel = (keyt > thr) | ((keyt == thr) & (g * LANES + krow < cut))
            nm_ref[sl * LANES:(sl + 1) * LANES, :] = jnp.where(sel, 0.0, NEG_BIG)

        def logits(h):
            mx = None
            for sl in range(nsl):
                rows = slice(sl * LANES, (sl + 1) * LANES)
                z = _dot(k_ref[0, j, rows, (h // 2) * LANES:(h // 2 + 1) * LANES], qt_ref[H_IDX + h]) + nm_ref[rows, :]
                if near:
                    z = z + jnp.concatenate(
                        [bt_ref[jnp.clip((i * nql + hf) - (j * nsl + sl), 0, N_BIAS_NEAR - 1), h]
                         for hf in range(nql)], axis=1)
                zb_ref[h % 2, rows, :] = z
                cm = fold(z, jnp.maximum)
                mx = cm if mx is None else jnp.maximum(mx, cm)
            return mx

        def accumulate(h, mx):
            shift = 0.0 if near else cf_ref[h]
            m_old = m_ref[h]
            m_new = jnp.maximum(m_old, jnp.max(mx, axis=0, keepdims=True) + shift)
            alpha = jnp.exp2(m_old - m_new)
            msub = m_new - shift
            lsum, pv = None, None
            for c in range(kt // PV_ROWS):
                rows = slice(c * PV_ROWS, (c + 1) * PV_ROWS)
                p = jnp.exp2(zb_ref[h % 2, rows, :] - msub)
                ls = fold(p, jnp.add)
                pc = _dot(vt_ref[0, j, (h // 2) * LANES:(h // 2 + 1) * LANES, rows], p.astype(BF16))
                lsum = ls if lsum is None else lsum + ls
                pv = pc if pv is None else pv + pc
            l_ref[h] = alpha * l_ref[h] + lsum
            acc_ref[h] = alpha * acc_ref[h] + pv
            m_ref[h] = m_new

        mx_next = logits(0)
        for h in range(H_ATT):
            mx_cur = mx_next
            if h + 1 < H_ATT:
                mx_next = logits(h + 1)
            accumulate(h, mx_cur)

    n_far = jnp.minimum(lax.div(jnp.maximum(nql * i - (N_BIAS_NEAR - 1) - (nsl - 1) + nsl, 0), nsl), nkt)

    def p3_far(j, carry):
        tile_step(j, False)
        return carry

    def p3_near(j, carry):
        tile_step(j, True)
        return carry

    lax.fori_loop(0, n_far, p3_far, 0)
    lax.fori_loop(n_far, nkt, p3_near, 0)

    eye_q = jnp.where(lax.broadcasted_iota(I32, (tq, tq), 0) == lax.broadcasted_iota(I32, (tq, tq), 1),
                      1.0, 0.0).astype(BF16)
    rowd = lax.broadcasted_iota(I32, (LANES, tq), 0)
    for p in range(H_ATT // 2):
        outs = []
        for hsub in range(2):
            h = 2 * p + hsub
            outs.append(acc_ref[h] / jnp.sum(l_ref[h], axis=0, keepdims=True))
        y_t = jnp.where(rowd < HEAD_DIM, outs[0], outs[1]).astype(BF16)
        y_ref[:, p * LANES:(p + 1) * LANES] = _dot_nt(eye_q, y_t).astype(y_ref.dtype)


def _attn_prompt(u, small, k_tiles, vt_tiles, ki_tiles, btab, cfar, *, bsz, t, tq, kt):
    assert t % kt == 0 and t % tq == 0 and tq % LANES == 0 and kt % LANES == 0
    nq = t // tq
    nk = t // kt
    row = lambda bi, qi: bi * nq + qi
    once = pl.Buffered(1)
    return pl.pallas_call(
        functools.partial(_attn_prompt_kernel, tq=tq, kt=kt, topk=min(TOPK, t // 4)),
        out_shape=jax.ShapeDtypeStruct((bsz * t, D_ATT), BF16),
        grid=(bsz, nq),
        in_specs=[pl.BlockSpec((tq, D_ATT), lambda bi, qi: (row(bi, qi), COL_Q // D_ATT)),
                  pl.BlockSpec((tq, D_ATT), lambda bi, qi: (row(bi, qi), COL_QI // D_ATT)),
                  pl.BlockSpec((tq, LANES), lambda bi, qi: (row(bi, qi), 0)),
                  pl.BlockSpec((1, nk, kt, D_ATT), lambda bi, qi: (bi, 0, 0, 0), pipeline_mode=once),
                  pl.BlockSpec((1, nk, D_ATT, kt), lambda bi, qi: (bi, 0, 0, 0), pipeline_mode=once),
                  pl.BlockSpec((1, nk, kt, LANES), lambda bi, qi: (bi, 0, 0, 0), pipeline_mode=once),
                  pl.BlockSpec((N_BIAS_NEAR, H_ATT, LANES, LANES), lambda bi, qi: (0, 0, 0, 0), pipeline_mode=once),
                  pl.BlockSpec((H_ATT, 1, tq), lambda bi, qi: (0, 0, 0), pipeline_mode=once)],
        out_specs=pl.BlockSpec((tq, D_ATT), lambda bi, qi: (row(bi, qi), 0)),
        scratch_shapes=[pltpu.VMEM((t // LANES, LANES, tq), I32),
                        pltpu.VMEM((H_IDX + H_ATT, LANES, tq), BF16),
                        pltpu.VMEM((2, kt, tq), F32),
                        pltpu.VMEM((kt, tq), F32),
                        pltpu.VMEM((H_ATT, 1, tq), F32),
                        pltpu.VMEM((H_ATT, SUBLANES, tq), F32),
                        pltpu.VMEM((H_ATT, LANES, tq), F32)],
        compiler_params=pltpu.CompilerParams(dimension_semantics=("arbitrary", "arbitrary"),
                                             vmem_limit_bytes=VMEM_LIMIT),
        name="sparse_attn_prompt",
    )(u, u, small, k_tiles, vt_tiles, ki_tiles, btab, cfar)


def _attn_sample_kernel(q_ref, qi_ref, sm_ref, kt_ref, v_ref, ki_ref, bt_ref, y_ref, sc_ref, m_ref, l_ref, acc_ref,
                        *, tq, kt, n_keys, topk):
    lane = lax.broadcasted_iota(I32, (tq, LANES), 1)

    def adm_fn(j, s):
        return (j * kt + s * LANES + lane) < n_keys

    def bias_fn(j, s, h):
        return bt_ref[0, h, :, s * LANES:(s + 1) * LANES]

    _attn_body(tq=tq, kt=kt, nkt=1, topk=topk, qf=q_ref[...], qif=qi_ref[...], sm=sm_ref[...],
               kt_tile=lambda j, p: kt_ref[0, p * LANES:(p + 1) * LANES, :],
               v_tile=lambda j, p: v_ref[0, :, p * LANES:(p + 1) * LANES],
               ki_tile=lambda j: ki_ref[0],
               adm_fn=adm_fn, bias_fn=bias_fn,
               sc_ref=sc_ref, m_ref=m_ref, l_ref=l_ref, acc_ref=acc_ref, y_ref=y_ref)


def _attn_sample(u, small, kt_all, v_all, ki_all, btab, *, bsz, tq, kt, n_keys):
    return pl.pallas_call(
        functools.partial(_attn_sample_kernel, tq=tq, kt=kt, n_keys=n_keys, topk=min(TOPK, n_keys // 4)),
        out_shape=jax.ShapeDtypeStruct((bsz * tq, D_ATT), BF16),
        grid=(bsz,),
        in_specs=[pl.BlockSpec((tq, D_ATT), lambda bi: (bi, COL_Q // D_ATT)),
                  pl.BlockSpec((tq, D_ATT), lambda bi: (bi, COL_QI // D_ATT)),
                  pl.BlockSpec((tq, LANES), lambda bi: (bi, 0)),
                  pl.BlockSpec((1, D_ATT, kt), lambda bi: (bi, 0, 0)),
                  pl.BlockSpec((1, kt, D_ATT), lambda bi: (bi, 0, 0)),
                  pl.BlockSpec((1, LANES, kt), lambda bi: (bi, 0, 0)),
                  pl.BlockSpec((1, H_ATT, tq, kt), lambda bi: (0, 0, 0, 0))],
        out_specs=pl.BlockSpec((tq, D_ATT), lambda bi: (bi, 0)),
        scratch_shapes=[pltpu.VMEM((1, tq, kt), I32),
                        pltpu.VMEM((H_ATT, tq, LANES), F32),
                        pltpu.VMEM((H_ATT, tq, LANES), F32),
                        pltpu.VMEM((H_ATT, tq, LANES), F32)],
        compiler_params=pltpu.CompilerParams(dimension_semantics=("arbitrary",),
                                             vmem_limit_bytes=VMEM_LIMIT),
        name="sparse_attn_sample",
    )(u, u, small, kt_all, v_all, ki_all, btab)


def _out_proj_kernel(x_ref, ys_ref, yc_ref, ya_ref, w_ref, g_ref, o_ref):
    acc = _dot(ys_ref[...], w_ref[0:D_SSM, :])
    acc = acc + _dot(yc_ref[...], w_ref[D_SSM:D_SSM + D_CONV, :])
    acc = acc + _dot(ya_ref[...], w_ref[D_SSM + D_CONV:D_MODEL, :])
    o_ref[...] = x_ref[...] + _rms(acc, g_ref[...])


def _out_proj(x, ys, yc, ya, w, g, *, tm):
    m = x.shape[0]
    return pl.pallas_call(
        _out_proj_kernel,
        out_shape=jax.ShapeDtypeStruct((m, D_MODEL), F32),
        grid=(m // tm,),
        in_specs=[pl.BlockSpec((tm, D_MODEL), lambda i: (i, 0)),
                  pl.BlockSpec((tm, D_SSM), lambda i: (i, 0)),
                  pl.BlockSpec((tm, D_CONV), lambda i: (i, 0)),
                  pl.BlockSpec((tm, D_ATT), lambda i: (i, 0)),
                  pl.BlockSpec((D_MODEL, D_MODEL), lambda i: (0, 0)),
                  pl.BlockSpec((1, D_MODEL), lambda i: (0, 0))],
        out_specs=pl.BlockSpec((tm, D_MODEL), lambda i: (i, 0)),
        compiler_params=pltpu.CompilerParams(dimension_semantics=("arbitrary",),
                                             vmem_limit_bytes=VMEM_LIMIT),
        name="out_proj",
    )(x, ys, yc, ya, w, g)


FFN_HALO = 8


def _ffn_kernel(x_ref, gpre_ref, wg_ref, wu_ref, wd_ref, cw_ref, cb_ref, gpost_ref, p1_ref, p2_ref,
                o_ref, aux_ref, h_ref, buf_ref, tail_ref, *, tm, tps, seq_len, chained):
    i = pl.program_id(0)
    j = pl.program_id(1)

    @pl.when(j == 0)
    def _():
        h_ref[...] = _rms(x_ref[...], gpre_ref[...]).astype(BF16)
        o_ref[...] = jnp.zeros(o_ref.shape, F32)

    h = h_ref[...]
    a_pre = _dot(h, wg_ref[...])
    buf_ref[FFN_HALO:FFN_HALO + tm, :] = a_pre
    if chained:
        seq_start = lax.rem(i, tps) == 0
        buf_ref[0:FFN_HALO, :] = jnp.where(seq_start, p1_ref[0], tail_ref[j])
        prev1 = buf_ref[FFN_HALO - 1:FFN_HALO - 1 + tm, :]
        prev2 = buf_ref[FFN_HALO - 2:FFN_HALO - 2 + tm, :]
        last = a_pre[tm - FFN_HALO:tm, :]
        tail_ref[j] = last
        aux_ref[0] = last
    else:
        buf_ref[0:FFN_HALO, :] = jnp.zeros((FFN_HALO, a_pre.shape[1]), F32)
        tpos = lax.rem(lax.broadcasted_iota(I32, a_pre.shape, 0), seq_len)
        prev1 = jnp.where(tpos >= 1, buf_ref[FFN_HALO - 1:FFN_HALO - 1 + tm, :], p1_ref[...])
        prev2 = jnp.where(tpos >= 2, buf_ref[FFN_HALO - 2:FFN_HALO - 2 + tm, :], p2_ref[...])
        aux_ref[...] = a_pre
    a = cw_ref[0:1, :] * prev2 + cw_ref[1:2, :] * prev1 + cw_ref[2:3, :] * a_pre + cb_ref[...]
    f = (_silu(a) * _dot(h, wu_ref[...])).astype(BF16)
    o_ref[...] += _dot(f, wd_ref[...])

    @pl.when(j == pl.num_programs(1) - 1)
    def _():
        o_ref[...] = x_ref[...] + _rms(o_ref[...], gpost_ref[...])


def _ffn(x, gpre, wg, wu, wd, cw, cb, gpost, p1, p2, *, tm, tf, seq_len, chained):
    m = x.shape[0]
    nf = D_FF // tf
    tps = max(seq_len // tm, 1)
    if chained:
        nseq = m // seq_len
        p_specs = [pl.BlockSpec((1, FFN_HALO, tf), lambda i, j: (i // tps, 0, j)),
                   pl.BlockSpec((1, FFN_HALO, tf), lambda i, j: (i // tps, 0, j))]
        aux_shape = jax.ShapeDtypeStruct((m // tm, FFN_HALO, D_FF), F32)
        aux_spec = pl.BlockSpec((1, FFN_HALO, tf), lambda i, j: (i, 0, j))
    else:
        p_specs = [pl.BlockSpec((tm, tf), lambda i, j: (i, j)), pl.BlockSpec((tm, tf), lambda i, j: (i, j))]
        aux_shape = jax.ShapeDtypeStruct((m, D_FF), F32)
        aux_spec = pl.BlockSpec((tm, tf), lambda i, j: (i, j))
    return pl.pallas_call(
        functools.partial(_ffn_kernel, tm=tm, tps=tps, seq_len=seq_len, chained=chained),
        out_shape=(jax.ShapeDtypeStruct((m, D_MODEL), F32), aux_shape),
        grid=(m // tm, nf),
        in_specs=[pl.BlockSpec((tm, D_MODEL), lambda i, j: (i, 0), pipeline_mode=pl.Buffered(1)),
                  pl.BlockSpec((1, D_MODEL), lambda i, j: (0, 0)),
                  pl.BlockSpec((D_MODEL, tf), lambda i, j: (0, j)),
                  pl.BlockSpec((D_MODEL, tf), lambda i, j: (0, j)),
                  pl.BlockSpec((tf, D_MODEL), lambda i, j: (j, 0)),
                  pl.BlockSpec((FFN_CONV, tf), lambda i, j: (0, j)),
                  pl.BlockSpec((1, tf), lambda i, j: (0, j)),
                  pl.BlockSpec((1, D_MODEL), lambda i, j: (0, 0))] + p_specs,
        out_specs=(pl.BlockSpec((tm, D_MODEL), lambda i, j: (i, 0)), aux_spec),
        scratch_shapes=[pltpu.VMEM((tm, D_MODEL), BF16),
                        pltpu.VMEM((FFN_HALO + tm, tf), F32),
                        pltpu.VMEM((nf, FFN_HALO, tf), F32)],
        compiler_params=pltpu.CompilerParams(dimension_semantics=("arbitrary", "arbitrary"),
                                             vmem_limit_bytes=VMEM_LIMIT),
        name="conv_ffn",
    )(x, gpre, wg, wu, wd, cw, cb, gpost, p1, p2)


def _prep_layer_weights(w):
    w_in = w["w_in"]
    o_dt = D_SSM + D_XBC
    o_glu = o_dt + H_SSM
    o_ki = o_glu + 2 * D_CONV + 4 * D_ATT
    o_wi = o_ki + D_IDX
    o_q = o_glu + 2 * D_CONV
    o_k = o_q + D_ATT
    o_qi = o_k + 2 * D_ATT
    pad = jnp.zeros((D_MODEL, LANES - H_SSM - H_IDX - D_IDX), w_in.dtype)
    w_r = jnp.concatenate([w_in[:, :o_dt], w_in[:, o_glu:o_k], w_in[:, o_qi:o_ki],
                           w_in[:, o_k:o_qi],
                           w_in[:, o_dt:o_glu], w_in[:, o_wi:o_wi + H_IDX], pad, w_in[:, o_ki:o_wi]],
                          axis=1).astype(BF16)
    assert w_r.shape[1] == D_PROJ and o_qi + H_IDX * D_IDX == o_ki
    row = lambda v: v.reshape(1, -1).astype(F32)
    padl = lambda v: jnp.pad(v.astype(F32), (0, LANES - v.shape[0])).reshape(1, LANES)
    return dict(
        w_in=w_r, g_mix_pre=row(w["g_mix_pre"]),
        ssm_conv_w=w["ssm_conv_w"].astype(F32), ssm_conv_b=row(w["ssm_conv_b"]),
        dt_bias=padl(w["ssm_dt_bias"]), a_log=padl(w["ssm_a_log"]),
        d_x=row(jnp.repeat(w["ssm_d"], SSM_HEAD_DIM)), ssm_norm_g=row(w["ssm_norm_g"]),
        cconv_w=w["cconv_w"].astype(F32), cconv_b=row(w["cconv_b"]),
        cconv_ln_g=row(w["cconv_ln_g"]), cconv_ln_b=row(w["cconv_ln_b"]),
        w_out=w["w_out"].astype(BF16), g_mix_post=row(w["g_mix_post"]), g_ffn_pre=row(w["g_ffn_pre"]),
        ffn_w_gate=w["ffn_w_gate"].astype(BF16), ffn_w_up=w["ffn_w_up"].astype(BF16),
        ffn_w_down=w["ffn_w_down"].astype(BF16), ffn_conv_w=w["ffn_conv_w"].astype(F32),
        ffn_conv_b=row(w["ffn_conv_b"]), g_ffn_post=row(w["g_ffn_post"]))


def _cast_kernel(w_ref, o_ref):
    o_ref[...] = w_ref[0].astype(o_ref.dtype)


CAST_ROWS = 128


def _layer_weight_bf16(w, l):
    _, rows, cols = w.shape
    assert rows % CAST_ROWS == 0
    return pl.pallas_call(
        _cast_kernel,
        out_shape=jax.ShapeDtypeStruct((rows, cols), BF16),
        grid=(rows // CAST_ROWS,),
        in_specs=[pl.BlockSpec((1, CAST_ROWS, cols), lambda i: (l, i, 0))],
        out_specs=pl.BlockSpec((CAST_ROWS, cols), lambda i: (i, 0)),
        name="weight_cast",
    )(w)


def _expand_matrix():
    e = np.zeros((LANES, D_SSM), np.float32)
    for h in range(H_SSM):
        e[h, h * SSM_HEAD_DIM:(h + 1) * SSM_HEAD_DIM] = 1.0
    return jnp.asarray(e, BF16)


def _front_pad(state, halo):
    return jnp.pad(state.astype(F32), ((0, 0), (halo - state.shape[1], 0), (0, 0)))


def _state_t(h):
    b = h.shape[0]
    return jnp.transpose(h.astype(F32), (0, 3, 1, 2)).reshape(b, SSM_STATE, D_SSM)


def _state_from_t(ht):
    b = ht.shape[0]
    return jnp.transpose(ht.reshape(b, SSM_STATE, H_SSM, SSM_HEAD_DIM), (0, 2, 3, 1))


def _mixer_common(x, lw, emat, ssm_conv_prev, ssm_h0, cconv_prev, *, bsz, t, tm_proj, kt, ssd_l, ssd_nv, cc_tm):
    u, k, v, small, *attn_ops = _in_proj(x, lw["g_mix_pre"], lw["w_in"], tm=tm_proj, kt=kt)
    y_ssm, ht, ctail = _ssd(u, small, _front_pad(ssm_conv_prev, SSD_HALO), _state_t(ssm_h0), lw["ssm_conv_w"],
                            lw["ssm_conv_b"], lw["dt_bias"], lw["a_log"], lw["d_x"], lw["ssm_norm_g"], emat,
                            bsz=bsz, t=t, L=ssd_l, nv=ssd_nv)
    y_conv, cctail = _cconv(u, _front_pad(cconv_prev, CC_HALO), lw["cconv_w"], lw["cconv_b"], lw["cconv_ln_g"],
                            lw["cconv_ln_b"], bsz=bsz, t=t, tm=cc_tm)
    ki = small[:, SM_KI:SM_KI + D_IDX]
    states = dict(k=k.reshape(bsz, t, H_ATT, HEAD_DIM), v=v.reshape(bsz, t, H_ATT, HEAD_DIM),
                  ki=ki.reshape(bsz, t, D_IDX), h=_state_from_t(ht),
                  ssm_conv=ctail[:, SSD_HALO - (SSM_CONV - 1):], cconv=cctail[:, CC_HALO - (CONV_WIDTH - 1):])
    return u, small, y_ssm, y_conv, k, v, ki, attn_ops, states


def _layer_prompt(x, lw, emat, btab, *, bsz, t, cfg):
    zeros = lambda *s: jnp.zeros(s, F32)
    kt = cfg["kt"]
    nk = t // kt
    u, small, y_ssm, y_conv, k, v, ki, (kb, vt, ki2), st = _mixer_common(
        x, lw, emat, zeros(bsz, SSM_CONV - 1, D_XBC), zeros(bsz, H_SSM, SSM_HEAD_DIM, SSM_STATE),
        zeros(bsz, CONV_WIDTH - 1, D_CONV), bsz=bsz, t=t, tm_proj=cfg["tm_proj"], kt=kt,
        ssd_l=cfg["ssd_l"], ssd_nv=cfg["ssd_l"], cc_tm=cfg["cc_tm"])
    tq = cfg["tq"]
    cfar = jnp.tile(btab[N_BIAS_NEAR - 1, :, 0:1, :], (1, 1, tq // LANES))
    y_att = _attn_prompt(u, small, kb.reshape(bsz, nk, kt, D_ATT), vt.reshape(bsz, nk, D_ATT, kt),
                         ki2.reshape(bsz, nk, kt, LANES), btab, cfar, bsz=bsz, t=t, tq=tq, kt=kt)
    x1 = _out_proj(x, y_ssm, y_conv, y_att, lw["w_out"], lw["g_mix_post"], tm=cfg["tm_out"])
    prev = zeros(bsz, FFN_HALO, D_FF)
    x2, ftail = _ffn(x1, lw["g_ffn_pre"], lw["ffn_w_gate"], lw["ffn_w_up"], lw["ffn_w_down"], lw["ffn_conv_w"],
                     lw["ffn_conv_b"], lw["g_ffn_post"], prev, prev, tm=cfg["tm_ffn"], tf=cfg["tf"], seq_len=t,
                     chained=True)
    ftail = ftail.reshape(bsz, t // cfg["tm_ffn"], FFN_HALO, D_FF)[:, -1]
    st["ffn_conv"] = ftail[:, FFN_HALO - (FFN_CONV - 1):]
    return x2, st


def _layer_sample(x, lw, emat, btab, past_k, past_v, past_ki, ssm_conv_prev, ssm_h0, cconv_prev, fconv_prev,
                  *, bsz, t, cfg):
    m = bsz * t
    u, small, y_ssm, y_conv, k, v, ki, _, st = _mixer_common(
        x, lw, emat, ssm_conv_prev, ssm_h0, cconv_prev, bsz=bsz, t=t, tm_proj=m, kt=None,
        ssd_l=LANES, ssd_nv=t, cc_tm=t)
    past = past_k.shape[1]
    n_keys = past + t
    ktp = cfg["kt_sample"]
    padk = lambda a: jnp.pad(a, ((0, 0), (0, ktp - n_keys), (0, 0)))
    k_all = padk(jnp.concatenate([past_k.reshape(bsz, past, D_ATT), k.reshape(bsz, t, D_ATT)], axis=1).astype(BF16))
    v_all = padk(jnp.concatenate([past_v.reshape(bsz, past, D_ATT), v.reshape(bsz, t, D_ATT)], axis=1).astype(BF16))
    ki_all = padk(jnp.concatenate([past_ki, ki.reshape(bsz, t, D_IDX)], axis=1).astype(BF16))
    kt_all = jnp.transpose(k_all, (0, 2, 1))
    kit = jnp.transpose(ki_all, (0, 2, 1))
    y_att = _attn_sample(u, small, kt_all, v_all, jnp.concatenate([kit, kit], axis=1), btab, bsz=bsz, tq=t, kt=ktp,
                         n_keys=n_keys)
    x1 = _out_proj(x, y_ssm, y_conv, y_att, lw["w_out"], lw["g_mix_post"], tm=m)
    fprev = fconv_prev.astype(F32)
    zrow = jnp.zeros((bsz, t - 1, D_FF), F32)
    p1 = jnp.concatenate([fprev[:, 1:2], zrow], axis=1).reshape(m, D_FF)
    p2 = jnp.concatenate([fprev, zrow[:, 1:]], axis=1).reshape(m, D_FF)
    x2, a_pre = _ffn(x1, lw["g_ffn_pre"], lw["ffn_w_gate"], lw["ffn_w_up"], lw["ffn_w_down"], lw["ffn_conv_w"],
                     lw["ffn_conv_b"], lw["g_ffn_post"], p1, p2, tm=m, tf=cfg["tf"], seq_len=t, chained=False)
    st["ffn_conv"] = a_pre.reshape(bsz, t, D_FF)[:, t - (FFN_CONV - 1):]
    return x2, st


_STATE_ORDER = ("k", "v", "ki", "h", "ssm_conv", "cconv", "ffn_conv")


def _prompt_cfg(t):
    big = t >= 4096
    return dict(tm_proj=1024 if big else 512, ssd_l=256, cc_tm=256, kt=512, tq=256,
                tm_out=512 if big else 256, tm_ffn=1024 if big else 256, tf=512)


def _forward(x_prompt, x_sample, cache_k, cache_v, cache_kidx, state_ssm, state_ssm_conv, state_cconv,
             state_ffn_conv, rel_bias, weights):
    bp, tp, _ = x_prompt.shape
    bs, ts, _ = x_sample.shape
    depth = weights["w_in"].shape[0]
    past = cache_k.shape[2]
    emat = _expand_matrix()
    cfg_p = _prompt_cfg(tp)
    kt_sample = -(-(past + ts) // LANES) * LANES
    cfg_s = dict(tf=512, kt_sample=kt_sample)
    rb = rel_bias.astype(F32)
    btab_p = _bias_table(rb, nd=N_BIAS_NEAR, rows=LANES, cols=LANES, off0=0, step=LANES, key_axis=0, scale=LOG2E)
    btab_s = _bias_table(rb, nd=1, rows=ts, cols=kt_sample, off0=-past, step=0)
    xp = x_prompt.reshape(bp * tp, D_MODEL)
    xs = x_sample.reshape(bs * ts, D_MODEL)
    p_states = {n: [] for n in _STATE_ORDER}
    s_states = {n: [] for n in _STATE_ORDER}
    for l in range(depth):
        lw = _prep_layer_weights({n: w[l] for n, w in weights.items()})
        for n in ("w_out", "ffn_w_gate", "ffn_w_up", "ffn_w_down"):
            lw[n] = _layer_weight_bf16(weights[n], l)
        xp, st_p = _layer_prompt(xp, lw, emat, btab_p, bsz=bp, t=tp, cfg=cfg_p)
        xs, st_s = _layer_sample(xs, lw, emat, btab_s, cache_k[l], cache_v[l], cache_kidx[l], state_ssm_conv[l],
                                 state_ssm[l], state_cconv[l], state_ffn_conv[l], bsz=bs, t=ts, cfg=cfg_s)
        for n in _STATE_ORDER:
            p_states[n].append(st_p[n])
            s_states[n].append(st_s[n])
    outs = [xp.reshape(bp, tp, D_MODEL), xs.reshape(bs, ts, D_MODEL)]
    outs += [jnp.stack(p_states[n]) for n in _STATE_ORDER]
    outs += [jnp.stack(s_states[n]) for n in _STATE_ORDER]
    return tuple(outs)


def kernel(x_prompt, x_sample, cache_k, cache_v, cache_kidx, state_ssm, state_ssm_conv, state_cconv, state_ffn_conv, rel_bias, g_mix_pre, w_in, ssm_conv_w, ssm_conv_b, ssm_dt_bias, ssm_a_log, ssm_d, ssm_norm_g, cconv_w, cconv_b, cconv_ln_g, cconv_ln_b, w_out, g_mix_post, g_ffn_pre, ffn_w_gate, ffn_w_up, ffn_conv_w, ffn_conv_b, ffn_w_down, g_ffn_post):
    weights = dict(g_mix_pre=g_mix_pre, w_in=w_in, ssm_conv_w=ssm_conv_w, ssm_conv_b=ssm_conv_b,
                   ssm_dt_bias=ssm_dt_bias, ssm_a_log=ssm_a_log, ssm_d=ssm_d, ssm_norm_g=ssm_norm_g,
                   cconv_w=cconv_w, cconv_b=cconv_b, cconv_ln_g=cconv_ln_g, cconv_ln_b=cconv_ln_b, w_out=w_out,
                   g_mix_post=g_mix_post, g_ffn_pre=g_ffn_pre, ffn_w_gate=ffn_w_gate, ffn_w_up=ffn_w_up,
                   ffn_conv_w=ffn_conv_w, ffn_conv_b=ffn_conv_b, ffn_w_down=ffn_w_down, g_ffn_post=g_ffn_post)
    return _forward(x_prompt, x_sample, cache_k, cache_v, cache_kidx, state_ssm, state_ssm_conv, state_cconv,
                    state_ffn_conv, rel_bias, weights)
```
